```python
import jax
import jax.numpy as jnp
from jax import lax
import numpy as np

D_MODEL = 1024
BATCH = 4
SEQ = 4096
DEPTH = 4

SSM_WIDTH = D_MODEL // 2
SSM_GROUP = 16
SSM_GROUPS = SSM_WIDTH // SSM_GROUP
SSM_STATE = 64
DT_MIN = 1e-3
DT_MAX = 1e-1
CONV_CH = D_MODEL // 2
CONV_K = 3
HEAD_DIM = 64
N_HEADS = (D_MODEL // 2) // HEAD_DIM
N_KV_HEADS = 2
GQA = N_HEADS // N_KV_HEADS
ATTN_WIDTH = N_HEADS * HEAD_DIM
KV_WIDTH = N_KV_HEADS * HEAD_DIM
CMP_BLOCK = 32
SEL_BLOCK = 64
N_SELECT = 16
WINDOW = 512
CMP_HIDDEN = 256
Q_CHUNK = 64
FORCE_SCORE = 1e4
NSA_BRANCHES = 3
MIX_BRANCHES = 3
D_FF = -(-8 * D_MODEL // (3 * 256)) * 256
RMS_EPS = 1e-6
IN_WIDTH = SSM_WIDTH + 3 * CONV_CH + ATTN_WIDTH + 6 * KV_WIDTH + N_HEADS * NSA_BRANCHES + MIX_BRANCHES * D_MODEL

kernel_name = 'hybrid_s5_conv_nsa_gated_trunk'


def rms_norm(x, g):
    xf = x.astype(jnp.float32)
    y = xf * lax.rsqrt(jnp.mean(xf * xf, axis=-1, keepdims=True) + RMS_EPS)
    return (y * g.astype(jnp.float32)).astype(x.dtype)


def masked_softmax(s, mask):
    s = jnp.where(mask, s, -jnp.inf)
    m = jnp.max(s, axis=-1, keepdims=True)
    m = jnp.where(jnp.isfinite(m), m, 0.0)
    e = jnp.exp(s - m)
    return e / jnp.maximum(jnp.sum(e, axis=-1, keepdims=True), 1e-30)


def split_projection(z):
    sizes = ([SSM_WIDTH, CONV_CH, CONV_CH, CONV_CH, ATTN_WIDTH] + [KV_WIDTH] * 6
             + [N_HEADS * NSA_BRANCHES, MIX_BRANCHES * D_MODEL])
    return jnp.split(z, np.cumsum(sizes)[:-1].tolist(), axis=-1)


def s5_mixer(u, lam_re, lam_im, b_re, b_im, c_re, c_im, d_skip, log_dt, w_glu):
    bsz, seq, _ = u.shape
    f32 = jnp.float32
    uf = u.astype(f32).reshape(bsz, seq, SSM_GROUPS, SSM_GROUP)
    lam = lax.complex(lam_re.astype(f32), lam_im.astype(f32))
    dt = jnp.exp(log_dt.astype(f32))[:, None]
    lam_bar = jnp.exp(lam * dt)
    b = lax.complex(b_re.astype(f32), b_im.astype(f32))
    b_bar = ((lam_bar - 1.0) / lam)[..., None] * b
    c = lax.complex(c_re.astype(f32), c_im.astype(f32))
    bu = jnp.einsum('bsgh,gph->bsgp', uf.astype(jnp.complex64), b_bar)
    a = jnp.broadcast_to(lam_bar, bu.shape)

    def combine(left, right):
        a_l, b_l = left
        a_r, b_r = right
        return a_r * a_l, a_r * b_l + b_r

    _, states = lax.associative_scan(combine, (a, bu), axis=1)
    y = jnp.einsum('bsgp,ghp->bsgh', states, c).real + d_skip.astype(f32).reshape(SSM_GROUPS, SSM_GROUP) * uf
    y = jax.nn.gelu(y.reshape(bsz, seq, SSM_WIDTH)).astype(u.dtype)
    y_lin, y_gate = jnp.split(y @ w_glu, 2, axis=-1)
    return y_lin * jax.nn.sigmoid(y_gate)


def short_conv_mixer(xin, gate_b, gate_c, conv_w, w_out):
    z = gate_c * xin
    z = lax.conv_general_dilated(z, conv_w[:, None, :], window_strides=(1,), padding=[(CONV_K - 1, 0)],
                                 dimension_numbers=('NWC', 'WIO', 'NWC'), feature_group_count=CONV_CH)
    return (gate_b * z) @ w_out


def nsa_mixer(q, k_cmp, v_cmp, k_sel, v_sel, k_win, v_win, gate_logits,
              q_norm_g, k_norm_g, cmp_pe, cmp_w1, cmp_w2, w_o):
    bsz, seq, _ = q.shape
    f32 = jnp.float32
    scale = HEAD_DIM ** -0.5
    pos = jnp.arange(seq)

    def heads(t, n):
        return t.reshape(bsz, seq, n, HEAD_DIM)

    q = rms_norm(heads(q, N_HEADS), q_norm_g).reshape(bsz, seq, N_KV_HEADS, GQA, HEAD_DIM)

    n_cmp = seq // CMP_BLOCK

    def compress(t, which):
        blocks = heads(t, N_KV_HEADS).reshape(bsz, n_cmp, CMP_BLOCK, N_KV_HEADS, HEAD_DIM) + cmp_pe[which][:, None, :]
        blocks = blocks.transpose(0, 1, 3, 2, 4).reshape(bsz, n_cmp, N_KV_HEADS, CMP_BLOCK * HEAD_DIM)
        return jax.nn.gelu(blocks @ cmp_w1[which]) @ cmp_w2[which]

    kc = rms_norm(compress(k_cmp, 0), k_norm_g[0])
    vc = compress(v_cmp, 1)
    s = jnp.einsum('bqhgd,bchd->bhgqc', q, kc).astype(f32) * scale
    cmp_end = (jnp.arange(n_cmp) + 1) * CMP_BLOCK - 1
    p_cmp = masked_softmax(s, cmp_end[None, :] <= pos[:, None])
    o_cmp = jnp.einsum('bhgqc,bchd->bqhgd', p_cmp.astype(vc.dtype), vc)

    n_blocks = seq // SEL_BLOCK
    n_top = min(N_SELECT, n_blocks)
    imp = jnp.sum(p_cmp, axis=2).reshape(bsz, N_KV_HEADS, seq, n_blocks, SEL_BLOCK // CMP_BLOCK).sum(-1)
    blk = jnp.arange(n_blocks)[None, :]
    cur = (pos // SEL_BLOCK)[:, None]
    forced = (blk == 0) | (blk == cur) | (blk == cur - 1)
    visible = blk * SEL_BLOCK <= pos[:, None]
    imp = jnp.where(forced, FORCE_SCORE, jnp.where(visible, imp, -jnp.inf))
    _, sel_idx = lax.top_k(imp, n_top)

    ks_blocks = rms_norm(heads(k_sel, N_KV_HEADS), k_norm_g[1]).reshape(
        bsz, n_blocks, SEL_BLOCK, N_KV_HEADS, HEAD_DIM).transpose(0, 3, 1, 2, 4)
    vs_blocks = heads(v_sel, N_KV_HEADS).reshape(
        bsz, n_blocks, SEL_BLOCK, N_KV_HEADS, HEAD_DIM).transpose(0, 3, 1, 2, 4)
    pad = ((0, 0), (WINDOW, 0), (0, 0), (0, 0))
    kw_pad = jnp.pad(rms_norm(heads(k_win, N_KV_HEADS), k_norm_g[2]), pad)
    vw_pad = jnp.pad(heads(v_win, N_KV_HEADS), pad)
    b_ix = jnp.arange(bsz)[:, None, None, None]
    h_ix = jnp.arange(N_KV_HEADS)[None, :, None, None]
    n_sel_keys = n_top * SEL_BLOCK

    def query_chunk(c):
        start = c * Q_CHUNK
        t = start + jnp.arange(Q_CHUNK)
        qc = lax.dynamic_slice_in_dim(q, start, Q_CHUNK, axis=1)
        idx = lax.dynamic_slice_in_dim(sel_idx, start, Q_CHUNK, axis=2)
        kg = ks_blocks[b_ix, h_ix, idx].reshape(bsz, N_KV_HEADS, Q_CHUNK, n_sel_keys, HEAD_DIM)
        vg = vs_blocks[b_ix, h_ix, idx].reshape(bsz, N_KV_HEADS, Q_CHUNK, n_sel_keys, HEAD_DIM)
        key_pos = (idx[..., None] * SEL_BLOCK + jnp.arange(SEL_BLOCK)).reshape(bsz, N_KV_HEADS, Q_CHUNK, n_sel_keys)
        s_sel = jnp.einsum('bqhgd,bhqkd->bhgqk', qc, kg).astype(f32) * scale
        p_sel = masked_softmax(s_sel, (key_pos <= t[:, None])[:, :, None])
        o_sel = jnp.einsum('bhgqk,bhqkd->bqhgd', p_sel.astype(vg.dtype), vg)
        kwc = lax.dynamic_slice_in_dim(kw_pad, start, Q_CHUNK + WINDOW, axis=1)
        vwc = lax.dynamic_slice_in_dim(vw_pad, start, Q_CHUNK + WINDOW, axis=1)
        wpos = (start - WINDOW + jnp.arange(Q_CHUNK + WINDOW))[None, :]
        wmask = (wpos <= t[:, None]) & (wpos > t[:, None] - WINDOW) & (wpos >= 0)
        s_win = jnp.einsum('bqhgd,bkhd->bhgqk', qc, kwc).astype(f32) * scale
        p_win = masked_softmax(s_win, wmask)
        o_win = jnp.einsum('bhgqk,bkhd->bqhgd', p_win.astype(vwc.dtype), vwc)
        return o_sel, o_win

    o_sel, o_win = lax.map(query_chunk, jnp.arange(seq // Q_CHUNK))

    def unchunk(o):
        return jnp.moveaxis(o, 0, 1).reshape(bsz, seq, N_KV_HEADS, GQA, HEAD_DIM)

    g = jax.nn.sigmoid(gate_logits.astype(f32)).reshape(bsz, seq, N_KV_HEADS, GQA, NSA_BRANCHES).astype(q.dtype)
    o = g[..., 0:1] * o_cmp + g[..., 1:2] * unchunk(o_sel) + g[..., 2:3] * unchunk(o_win)
    return o.reshape(bsz, seq, ATTN_WIDTH) @ w_o


def setup_inputs(seed: int = 0) -> dict:
    key = jax.random.key(seed)
    ks = jax.random.split(key, 24)
    f32 = jnp.float32
    nl, G, P, H = DEPTH, SSM_GROUPS, SSM_STATE, SSM_GROUP

    def nrm(k, shape, scale):
        return scale * jax.random.normal(k, shape, f32)

    return {
        'x': nrm(ks[0], (BATCH, SEQ, D_MODEL), 1.0),
        'mix_norm_g': 1.0 + nrm(ks[1], (nl, D_MODEL), 0.02),
        'w_in': nrm(ks[2], (nl, D_MODEL, IN_WIDTH), D_MODEL ** -0.5),
        'ssm_lam_re': -0.5 + nrm(ks[3], (nl, G, P), 0.01),
        'ssm_lam_im': jnp.pi * jnp.arange(P, dtype=f32) + nrm(ks[4], (nl, G, P), 0.01),
        'ssm_b_re': nrm(ks[5], (nl, G, P, H), (2 * H) ** -0.5),
        'ssm_b_im': nrm(ks[6], (nl, G, P, H), (2 * H) ** -0.5),
        'ssm_c_re': nrm(ks[7], (nl, G, H, P), 0.5 ** 0.5),
        'ssm_c_im': nrm(ks[8], (nl, G, H, P), 0.5 ** 0.5),
        'ssm_d': nrm(ks[9], (nl, SSM_WIDTH), 1.0),
        'ssm_log_dt': jax.random.uniform(ks[10], (nl, G), f32, float(np.log(DT_MIN)), float(np.log(DT_MAX))),
        'ssm_w_glu': nrm(ks[11], (nl, SSM_WIDTH, 2 * D_MODEL), SSM_WIDTH ** -0.5),
        'conv_w': nrm(ks[12], (nl, CONV_K, CONV_CH), CONV_K ** -0.5),
        'conv_w_out': nrm(ks[13], (nl, CONV_CH, D_MODEL), CONV_CH ** -0.5),
        'q_norm_g': 1.0 + nrm(ks[14], (nl, HEAD_DIM), 0.02),
        'k_norm_g': 1.0 + nrm(ks[15], (nl, NSA_BRANCHES, HEAD_DIM), 0.02),
        'cmp_pe': nrm(ks[16], (nl, 2, CMP_BLOCK, HEAD_DIM), 0.1),
        'cmp_w1': nrm(ks[17], (nl, 2, CMP_BLOCK * HEAD_DIM, CMP_HIDDEN), (CMP_BLOCK * HEAD_DIM) ** -0.5),
        'cmp_w2': nrm(ks[18], (nl, 2, CMP_HIDDEN, HEAD_DIM), CMP_HIDDEN ** -0.5),
        'nsa_w_o': nrm(ks[19], (nl, ATTN_WIDTH, D_MODEL), ATTN_WIDTH ** -0.5),
        'w_out': nrm(ks[20], (nl, D_MODEL, D_MODEL), D_MODEL ** -0.5),
        'ffn_norm_g': 1.0 + nrm(ks[21], (nl, D_MODEL), 0.02),
        'ffn_w_gate_up': nrm(ks[22], (nl, D_MODEL, 2 * D_FF), D_MODEL ** -0.5),
        'ffn_w_down': nrm(ks[23], (nl, D_FF, D_MODEL), D_FF ** -0.5),
    }


def reference(x, mix_norm_g, w_in, ssm_lam_re, ssm_lam_im, ssm_b_re, ssm_b_im, ssm_c_re, ssm_c_im,
              ssm_d, ssm_log_dt, ssm_w_glu, conv_w, conv_w_out, q_norm_g, k_norm_g, cmp_pe, cmp_w1,
              cmp_w2, nsa_w_o, w_out, ffn_norm_g, ffn_w_gate_up, ffn_w_down):
    bsz, seq, _ = x.shape
    for i in range(DEPTH):
        h = rms_norm(x, mix_norm_g[i])
        (u, c_b, c_c, c_x, q, k_c, v_c, k_s, v_s, k_w, v_w, nsa_gates, mix_gates) = split_projection(h @ w_in[i])
        y_ssm = s5_mixer(u, ssm_lam_re[i], ssm_lam_im[i], ssm_b_re[i], ssm_b_im[i], ssm_c_re[i], ssm_c_im[i],
                         ssm_d[i], ssm_log_dt[i], ssm_w_glu[i])
        y_conv = short_conv_mixer(c_x, c_b, c_c, conv_w[i], conv_w_out[i])
        y_attn = nsa_mixer(q, k_c, v_c, k_s, v_s, k_w, v_w, nsa_gates, q_norm_g[i], k_norm_g[i],
                           cmp_pe[i], cmp_w1[i], cmp_w2[i], nsa_w_o[i])
        gates = jax.nn.sigmoid(mix_gates.astype(jnp.float32)).astype(x.dtype).reshape(bsz, seq, MIX_BRANCHES, D_MODEL)
        mixed = gates[:, :, 0] * y_ssm + gates[:, :, 1] * y_conv + gates[:, :, 2] * y_attn
        x = x + mixed @ w_out[i]
        h = rms_norm(x, ffn_norm_g[i])
        f_gate, f_up = jnp.split(h @ ffn_w_gate_up[i], 2, axis=-1)
        x = x + (jax.nn.silu(f_gate) * f_up) @ ffn_w_down[i]
    return x
```

```python
import functools

import jax
import jax.numpy as jnp
import numpy as np
from jax import lax
from jax.experimental import pallas as pl
from jax.experimental.pallas import tpu as pltpu

F32 = jnp.float32
BF16 = jnp.bfloat16

D_MODEL = 1024
SSM_WIDTH = 512
SSM_GROUP = 16
SSM_GROUPS = SSM_WIDTH // SSM_GROUP
SSM_STATE = 64
CONV_CH = 512
CONV_K = 3
HEAD_DIM = 64
N_HEADS = 8
N_KV_HEADS = 2
GQA = N_HEADS // N_KV_HEADS
ATTN_WIDTH = N_HEADS * HEAD_DIM
KV_WIDTH = N_KV_HEADS * HEAD_DIM
CMP_BLOCK = 32
SEL_BLOCK = 64
SEL_SHIFT = 6
N_SELECT = 16
WINDOW = 512
CMP_HIDDEN = 256
FORCE_SCORE = 1e4
NSA_BRANCHES = 3
MIX_BRANCHES = 3
D_FF = 2816
RMS_EPS = 1e-6

LANES = 128
MASKED = -1e30
S5_CHUNK = 16
GATE_PAD = LANES

Z_MIX = 0
Z_U = Z_MIX + MIX_BRANCHES * D_MODEL
Z_CB = Z_U + SSM_WIDTH
Z_CC = Z_CB + CONV_CH
Z_CX = Z_CC + CONV_CH
Z_Q = Z_CX + CONV_CH
Z_KC = Z_Q + ATTN_WIDTH
Z_VC = Z_KC + KV_WIDTH
Z_KS = Z_VC + KV_WIDTH
Z_VS = Z_KS + KV_WIDTH
Z_KW = Z_VS + KV_WIDTH
Z_VW = Z_KW + KV_WIDTH
Z_GATE = Z_VW + KV_WIDTH
Z_WIDTH = Z_GATE + GATE_PAD


def _gelu_tanh(x):
    return 0.5 * x * (1.0 + jnp.tanh(np.sqrt(2.0 / np.pi).astype(np.float32) * (x + 0.044715 * (x * x * x))))


def _sigmoid(x):
    return 1.0 / (1.0 + jnp.exp(-x))


def _rms(x, g):
    return x * lax.rsqrt(jnp.mean(x * x, axis=-1, keepdims=True) + RMS_EPS) * g


def _dot(a, b):
    return jnp.dot(a, b, preferred_element_type=F32)


def _dot_nt(a, b):
    return lax.dot_general(a, b, (((1,), (1,)), ((), ())), preferred_element_type=F32)


def _inproj_kernel(x_ref, g_ref, w_ref, o_ref):
    h = _rms(x_ref[...], g_ref[...]).astype(BF16)
    o_ref[...] = _dot(h, w_ref[...])


def _inproj(x2, g, w, *, tm, tn):
    t, d = x2.shape
    n = w.shape[1]
    return pl.pallas_call(
        _inproj_kernel,
        grid=(n // tn, t // tm),
        in_specs=[pl.BlockSpec((tm, d), lambda j, i: (i, 0)),
                  pl.BlockSpec((1, d), lambda j, i: (0, 0)),
                  pl.BlockSpec((d, tn), lambda j, i: (0, j))],
        out_specs=pl.BlockSpec((tm, tn), lambda j, i: (i, j)),
        out_shape=jax.ShapeDtypeStruct((t, n), F32),
        name="inproj",
    )(x2, g.reshape(1, d), w)


def _permute_w_in(w):
    n_plain = SSM_WIDTH + 3 * CONV_CH + ATTN_WIDTH + 6 * KV_WIDTH
    n_gate = N_HEADS * NSA_BRANCHES
    gate = jnp.pad(w[:, n_plain:n_plain + n_gate], ((0, 0), (0, GATE_PAD - n_gate)))
    return jnp.concatenate([w[:, n_plain + n_gate:], w[:, :n_plain], gate], axis=1).astype(BF16)


def _s5_tables(lam_re, lam_im, b_re, b_im, c_re, c_im, d_skip, log_dt, n_chunks):
    hp = lax.Precision.HIGHEST
    g, p = lam_re.shape
    h, l = SSM_GROUP, S5_CHUNK
    lam = lax.complex(lam_re, lam_im)
    lam_dt = lam * jnp.exp(log_dt)[:, None]
    lam_bar = jnp.exp(lam_dt)
    b_bar = ((lam_bar - 1.0) / lam)[..., None] * lax.complex(b_re, b_im)
    c = lax.complex(c_re, c_im)
    pw = jnp.exp(lam_dt[None] * jnp.arange(l + 1, dtype=F32)[:, None, None])
    kern = jnp.einsum('gop,tgp,gpi->tgio', c, pw[:l], b_bar, precision=hp).real
    r = jnp.arange(l)
    lag = r[None, :] - r[:, None]
    toe = jnp.where((lag >= 0)[:, :, None, None, None], kern[jnp.clip(lag, 0, l - 1)], 0.0)
    toe = toe.transpose(2, 0, 3, 1, 4)
    eye = (r[:, None, None, None] == r[None, None, :, None]) & (jnp.arange(h)[None, :, None, None] == jnp.arange(h)[None, None, None, :])
    toe = toe + jnp.where(eye[None], d_skip.reshape(g, 1, h, 1, 1), 0.0)
    toe = toe.reshape(g, l * h, l * h)
    st = pw[l - 1 - r][:, :, :, None] * b_bar[None]
    st = st.transpose(1, 0, 3, 2).reshape(g, l * h, p)
    cp = c[None] * pw[1:l + 1][:, :, None, :]
    cp = cp.transpose(1, 3, 0, 2).reshape(g, p, l * h)
    n_steps = max(1, int(np.ceil(np.log2(n_chunks))))
    dec = jnp.exp(lam_dt[None] * (l * (2.0 ** jnp.arange(n_steps, dtype=F32)))[:, None, None])

    def pair_diag(m):
        m = m.reshape(g // 2, 2, *m.shape[1:])
        z = jnp.zeros_like(m[:, 0])
        return jnp.concatenate([jnp.concatenate([m[:, 0], z], axis=2), jnp.concatenate([z, m[:, 1]], axis=2)], axis=1)

    def pair_lanes(m):
        return m.reshape(m.shape[0], g // 2, 2 * p).transpose(1, 0, 2)

    return dict(toe=pair_diag(toe).astype(BF16),
                s_re=pair_diag(st.real).astype(BF16), s_im=pair_diag(st.imag).astype(BF16),
                c_re=pair_diag(cp.real).astype(BF16), c_im=pair_diag(-cp.imag).astype(BF16),
                d_re=pair_lanes(dec.real), d_im=pair_lanes(dec.imag))


def _s5_kernel(u_ref, toe_ref, sre_ref, sim_ref, cre_ref, cim_ref, dre_ref, dim_ref, y_ref):
    nb, _, nc, _ = u_ref.shape
    n_steps = dre_ref.shape[1]
    row = lax.broadcasted_iota(jnp.int32, (nc, 2 * SSM_STATE), 0)
    for b in range(nb):
        ub = u_ref[b, 0].astype(BF16)
        xr = _dot(ub, sre_ref[0])
        xi = _dot(ub, sim_ref[0])
        for k in range(n_steps):
            s = 1 << k
            if s >= nc:
                break
            dr = dre_ref[0, k:k + 1, :]
            di = dim_ref[0, k:k + 1, :]
            sr = jnp.where(row >= s, pltpu.roll(xr, s, 0), 0.0)
            si = jnp.where(row >= s, pltpu.roll(xi, s, 0), 0.0)
            xr, xi = xr + (dr * sr - di * si), xi + (dr * si + di * sr)
        pr = jnp.where(row >= 1, pltpu.roll(xr, 1, 0), 0.0).astype(BF16)
        pi = jnp.where(row >= 1, pltpu.roll(xi, 1, 0), 0.0).astype(BF16)
        y_ref[b, 0] = _dot(ub, toe_ref[0]) + _dot(pr, cre_ref[0]) + _dot(pi, cim_ref[0])


def _s5_scan(u, tabs):
    bsz, seq, _ = u.shape
    l, h, gp = S5_CHUNK, SSM_GROUP, SSM_GROUPS // 2
    nc = seq // l
    w = 2 * l * h
    uc = u.reshape(bsz, nc, l, gp, 2, h).transpose(0, 3, 1, 4, 2, 5).reshape(bsz, gp, nc, w)
    wspec = lambda a: pl.BlockSpec((1,) + a.shape[1:], lambda j: (j, 0, 0))
    names = ('toe', 's_re', 's_im', 'c_re', 'c_im', 'd_re', 'd_im')
    y = pl.pallas_call(
        _s5_kernel,
        grid=(gp,),
        in_specs=[pl.BlockSpec((bsz, 1, nc, w), lambda j: (0, j, 0, 0))] + [wspec(tabs[k]) for k in names],
        out_specs=pl.BlockSpec((bsz, 1, nc, w), lambda j: (0, j, 0, 0)),
        out_shape=jax.ShapeDtypeStruct((bsz, gp, nc, w), F32),
        name="s5_scan",
    )(uc, *[tabs[k] for k in names])
    return y.reshape(bsz, gp, nc, 2, l, h).transpose(0, 2, 4, 1, 3, 5).reshape(bsz, seq, SSM_WIDTH)


def _kv_prep_kernel(ks_ref, vs_ref, kw_ref, vw_ref, g_ref, kaug_ref, vsel_ref, kwin_ref, vwin_ref, *, seq):
    tp = ks_ref.shape[0]
    pos = (pl.program_id(0) * tp) % seq + lax.broadcasted_iota(jnp.int32, (tp, HEAD_DIM), 0)
    blk = jnp.right_shift(pos, SEL_SHIFT)
    onehot = jnp.where(lax.broadcasted_iota(jnp.int32, (tp, HEAD_DIM), 1) == blk, 1.0, 0.0).astype(BF16)
    zeros = jnp.zeros((tp, HEAD_DIM), BF16)
    for hh in range(N_KV_HEADS):
        sl = slice(hh * HEAD_DIM, (hh + 1) * HEAD_DIM)
        kn = _rms(ks_ref[:, sl], g_ref[0:1, :]).astype(BF16)
        kaug_ref[hh] = jnp.concatenate([kn, onehot], axis=1)
        kn = _rms(kw_ref[:, sl], g_ref[1:2, :]).astype(BF16)
        kwin_ref[hh] = jnp.concatenate([kn, zeros], axis=1)
        vsel_ref[hh] = vs_ref[:, sl].astype(BF16)
        vwin_ref[hh] = vw_ref[:, sl].astype(BF16)


def _kv_prep(z, k_norm_g, *, seq, tp):
    t = z.shape[0]
    col = lambda off: pl.BlockSpec((tp, KV_WIDTH), lambda i, o=off // KV_WIDTH: (i, o))
    aug = 2 * HEAD_DIM
    return pl.pallas_call(
        functools.partial(_kv_prep_kernel, seq=seq),
        grid=(t // tp,),
        in_specs=[col(Z_KS), col(Z_VS), col(Z_KW), col(Z_VW), pl.BlockSpec((2, HEAD_DIM), lambda i: (0, 0))],
        out_specs=[pl.BlockSpec((N_KV_HEADS, tp, aug), lambda i: (0, i, 0)),
                   pl.BlockSpec((N_KV_HEADS, tp, HEAD_DIM), lambda i: (0, i, 0)),
                   pl.BlockSpec((N_KV_HEADS, tp, aug), lambda i: (0, i, 0)),
                   pl.BlockSpec((N_KV_HEADS, tp, HEAD_DIM), lambda i: (0, i, 0))],
        out_shape=[jax.ShapeDtypeStruct((N_KV_HEADS, t, aug), BF16),
                   jax.ShapeDtypeStruct((N_KV_HEADS, t, HEAD_DIM), BF16),
                   jax.ShapeDtypeStruct((N_KV_HEADS, t, aug), BF16),
                   jax.ShapeDtypeStruct((N_KV_HEADS, t, HEAD_DIM), BF16)],
        name="kv_prep",
    )(z, z, z, z, k_norm_g[1:3])


def _compress_kernel(x_ref, pe_ref, w1_ref, w2_ref, o_ref):
    x = (x_ref[0] + pe_ref[0]).astype(BF16)
    hid = _gelu_tanh(_dot(x, w1_ref[0])).astype(BF16)
    o_ref[0] = _dot(hid, w2_ref[0])


def _compress(z, cmp_pe, w1, w2, *, bsz, seq):
    nc = seq // CMP_BLOCK
    half = nc // 2

    def blocks(off):
        t = z[:, off:off + KV_WIDTH].reshape(bsz, half, 2, CMP_BLOCK, N_KV_HEADS, HEAD_DIM)
        return t.transpose(0, 4, 2, 1, 3, 5).reshape(bsz * N_KV_HEADS * nc, CMP_BLOCK * HEAD_DIM)

    x = jnp.stack([blocks(Z_KC), blocks(Z_VC)])
    m, kdim = x.shape[1:]
    out = pl.pallas_call(
        _compress_kernel,
        grid=(2,),
        in_specs=[pl.BlockSpec((1, m, kdim), lambda w: (w, 0, 0)),
                  pl.BlockSpec((1, 1, kdim), lambda w: (w, 0, 0)),
                  pl.BlockSpec((1, kdim, CMP_HIDDEN), lambda w: (w, 0, 0)),
                  pl.BlockSpec((1, CMP_HIDDEN, HEAD_DIM), lambda w: (w, 0, 0))],
        out_specs=pl.BlockSpec((1, m, HEAD_DIM), lambda w: (w, 0, 0)),
        out_shape=jax.ShapeDtypeStruct((2, m, HEAD_DIM), F32),
        name="compress",
    )(x, cmp_pe.reshape(2, 1, kdim), w1, w2)
    return out.reshape(2, bsz * N_KV_HEADS, nc, HEAD_DIM)


def _head_gate(gl, kvh, g, branch):
    c0 = g * NSA_BRANCHES + branch
    c1 = (GQA + g) * NSA_BRANCHES + branch
    return _sigmoid(jnp.where(kvh == 0, gl[:, c0:c0 + 1], gl[:, c1:c1 + 1]))


def _cmp_select_kernel(q_ref, gl_ref, kc_ref, vc_ref, qg_ref, kg_ref, o_ref, qaug_ref, imp_ref):
    tq = q_ref.shape[0]
    nc = kc_ref.shape[1]
    nb = nc // 2
    kvh = pl.program_id(1)
    q0 = pl.program_id(2) * tq
    scale = HEAD_DIM ** -0.5
    qn = [(_rms(q_ref[:, g * HEAD_DIM:(g + 1) * HEAD_DIM], qg_ref[...]) * scale).astype(BF16) for g in range(GQA)]
    kc = _rms(kc_ref[0], kg_ref[...]).astype(BF16)
    vc = vc_ref[0].astype(BF16)
    s = _dot_nt(jnp.concatenate(qn, axis=0), kc)
    rows = GQA * tq
    t = q0 + (lax.broadcasted_iota(jnp.int32, (rows, nc), 0) & (tq - 1))
    lane = lax.broadcasted_iota(jnp.int32, (rows, nc), 1)
    blk = jnp.where(lane < nb, 2 * lane, 2 * (lane - nb) + 1)
    valid = (blk + 1) * CMP_BLOCK - 1 <= t
    sm = jnp.where(valid, s, MASKED)
    m = jnp.max(sm, axis=-1, keepdims=True)
    m = jnp.where(m > 0.5 * MASKED, m, 0.0)
    e = jnp.where(valid, jnp.exp(sm - m), 0.0)
    p = e / jnp.maximum(jnp.sum(e, axis=-1, keepdims=True), 1e-30)
    o = _dot(p.astype(BF16), vc)
    gl = gl_ref[...]
    o_ref[...] = jnp.concatenate(
        [o[g * tq:(g + 1) * tq] * _head_gate(gl, kvh, g, 0) for g in range(GQA)], axis=1)

    psum = p[0:tq]
    for g in range(1, GQA):
        psum = psum + p[g * tq:(g + 1) * tq]
    pt = psum.T
    imp = pt[:nb] + pt[nb:]
    j = lax.broadcasted_iota(jnp.int32, (nb, tq), 0)
    tt = q0 + lax.broadcasted_iota(jnp.int32, (nb, tq), 1)
    cur = jnp.right_shift(tt, SEL_SHIFT)
    forced = (j == 0) | (j == cur) | (j == cur - 1)
    visible = j <= cur
    imp = jnp.where(forced, FORCE_SCORE, jnp.where(visible, imp, -jnp.inf))
    imp_ref[...] = imp
    n_vis = jnp.minimum(nb, (q0 + tq - 1) // SEL_BLOCK + 1)

    def count(i, cnt):
        vi = imp_ref[pl.ds(i, 1), :]
        beats = (vi > imp) | ((vi == imp) & (i < j))
        return cnt + jnp.where(beats, 1.0, 0.0)

    cnt = lax.fori_loop(0, n_vis, count, jnp.zeros((nb, tq), F32))
    selneg_t = jnp.where(visible & (cnt < float(N_SELECT)), 0.0, MASKED)
    pad_rows = LANES - nb
    if pad_rows > 0:
        selneg_t = jnp.concatenate([selneg_t, jnp.zeros((pad_rows, tq), F32)], axis=0)
    selneg = selneg_t.T[:, :HEAD_DIM].astype(BF16)
    for g in range(GQA):
        qaug_ref[0, g] = jnp.concatenate([qn[g], selneg], axis=1)


def _cmp_select(z, cmp_kv, q_norm_g, k_norm_g, *, bsz, seq, tq):
    t = z.shape[0]
    nq = seq // tq
    nc = seq // CMP_BLOCK
    aug = 2 * HEAD_DIM
    qw = GQA * HEAD_DIM
    kc, vc = cmp_kv[0], cmp_kv[1]
    return pl.pallas_call(
        _cmp_select_kernel,
        grid=(bsz, N_KV_HEADS, nq),
        in_specs=[pl.BlockSpec((tq, qw), lambda b, h, i: (b * nq + i, Z_Q // qw + h)),
                  pl.BlockSpec((tq, GATE_PAD), lambda b, h, i: (b * nq + i, Z_GATE // GATE_PAD)),
                  pl.BlockSpec((1, nc, HEAD_DIM), lambda b, h, i: (b * N_KV_HEADS + h, 0, 0)),
                  pl.BlockSpec((1, nc, HEAD_DIM), lambda b, h, i: (b * N_KV_HEADS + h, 0, 0)),
                  pl.BlockSpec((1, HEAD_DIM), lambda b, h, i: (0, 0)),
                  pl.BlockSpec((1, HEAD_DIM), lambda b, h, i: (0, 0))],
        out_specs=[pl.BlockSpec((tq, qw), lambda b, h, i: (b * nq + i, h)),
                   pl.BlockSpec((1, GQA, tq, aug), lambda b, h, i: (b * N_KV_HEADS + h, 0, i, 0))],
        out_shape=[jax.ShapeDtypeStruct((t, ATTN_WIDTH), F32),
                   jax.ShapeDtypeStruct((bsz * N_KV_HEADS, GQA, seq, aug), BF16)],
        scratch_shapes=[pltpu.VMEM((nc // 2, tq), F32)],
        name="cmp_select",
    )(z, z, kc, vc, q_norm_g.reshape(1, HEAD_DIM), k_norm_g[0:1])


def _flash_kernel(q_ref, k_ref, v_ref, gl_ref, o_ref, m_ref, l_ref, acc_ref, *, tk, window, branch):
    tq = q_ref.shape[2]
    rows = GQA * tq
    kvh = pl.program_id(1)
    q0 = pl.program_id(2) * tq
    qa = q_ref[0].reshape(rows, q_ref.shape[3])
    m_ref[...] = jnp.full(m_ref.shape, MASKED, F32)
    l_ref[...] = jnp.zeros(l_ref.shape, F32)
    acc_ref[...] = jnp.zeros(acc_ref.shape, F32)
    t = q0 + (lax.broadcasted_iota(jnp.int32, (rows, tk), 0) & (tq - 1))
    lane = lax.broadcasted_iota(jnp.int32, (rows, tk), 1)
    lo = jnp.maximum(0, q0 - window + 1) // tk if window else 0
    hi = (q0 + tq + tk - 1) // tk

    def body(kb, carry):
        k0 = pl.multiple_of(kb * tk, tk)
        s = _dot_nt(qa, k_ref[0, pl.ds(k0, tk), :])
        key = k0 + lane
        valid = key <= t
        if window:
            valid = valid & (key > t - window)
        s = jnp.where(valid, s, MASKED)
        m_old = m_ref[...]
        m_new = jnp.maximum(m_old, jnp.max(s, axis=-1, keepdims=True))
        alpha = jnp.exp(m_old - m_new)
        p = jnp.exp(s - m_new)
        l_ref[...] = alpha * l_ref[...] + jnp.sum(p, axis=-1, keepdims=True)
        acc_ref[...] = alpha * acc_ref[...] + _dot(p.astype(BF16), v_ref[0, pl.ds(k0, tk), :])
        m_ref[...] = m_new
        return carry

    lax.fori_loop(lo, hi, body, 0)
    o = acc_ref[...] / l_ref[...]
    gl = gl_ref[...]
    o_ref[...] = jnp.concatenate(
        [o[g * tq:(g + 1) * tq] * _head_gate(gl, kvh, g, branch) for g in range(GQA)], axis=1)


def _flash(qaug, k, v, z, *, bsz, seq, tq, tk, window, branch, name):
    t = z.shape[0]
    nq = seq // tq
    aug = qaug.shape[3]
    qw = GQA * HEAD_DIM
    rows = GQA * tq
    return pl.pallas_call(
        functools.partial(_flash_kernel, tk=tk, window=window, branch=branch),
        grid=(bsz, N_KV_HEADS, nq),
        in_specs=[pl.BlockSpec((1, GQA, tq, aug), lambda b, h, i: (b * N_KV_HEADS + h, 0, i, 0)),
                  pl.BlockSpec((1, seq, aug), lambda b, h, i: (h, b, 0)),
                  pl.BlockSpec((1, seq, HEAD_DIM), lambda b, h, i: (h, b, 0)),
                  pl.BlockSpec((tq, GATE_PAD), lambda b, h, i: (b * nq + i, Z_GATE // GATE_PAD))],
        out_specs=pl.BlockSpec((tq, qw), lambda b, h, i: (b * nq + i, h)),
        out_shape=jax.ShapeDtypeStruct((t, ATTN_WIDTH), F32),
        scratch_shapes=[pltpu.VMEM((rows, 1), F32), pltpu.VMEM((rows, 1), F32), pltpu.VMEM((rows, HEAD_DIM), F32)],
        name=name,
    )(qaug, k, v, z)


def _merge_kernel(x_ref, g0_ref, g1_ref, g2_ref, cb_ref, cc_ref, cx_ref, pc_ref, px_ref, ys_ref,
                  oc_ref, os_ref, ow_ref, cw_ref, wc_ref, wglu_ref, wo_ref, wout_ref, out_ref, *, seq):
    tm = x_ref.shape[0]
    yg = _dot(_gelu_tanh(ys_ref[...]).astype(BF16), wglu_ref[...])
    y_ssm = yg[:, :D_MODEL] * _sigmoid(yg[:, D_MODEL:])
    zc = cc_ref[...] * cx_ref[...]
    keep = jnp.where((pl.program_id(0) * tm) % seq != 0, 1.0, 0.0)
    prev = pc_ref[...] * px_ref[...] * keep
    row = lax.broadcasted_iota(jnp.int32, zc.shape, 0)
    z1 = jnp.where(row >= 1, pltpu.roll(zc, 1, 0), prev[7:8, :])
    z2 = jnp.where(row >= 2, pltpu.roll(zc, 2, 0), jnp.where(row == 1, prev[7:8, :], prev[6:7, :]))
    conv = cw_ref[0:1, :] * z2 + cw_ref[1:2, :] * z1 + cw_ref[2:3, :] * zc
    y_conv = _dot((cb_ref[...] * conv).astype(BF16), wc_ref[...])
    y_attn = _dot((oc_ref[...] + os_ref[...] + ow_ref[...]).astype(BF16), wo_ref[...])
    mixed = _sigmoid(g0_ref[...]) * y_ssm + _sigmoid(g1_ref[...]) * y_conv + _sigmoid(g2_ref[...]) * y_attn
    out_ref[...] = x_ref[...] + _dot(mixed.astype(BF16), wout_ref[...])


def _merge(x2, z, ys, o_cmp, o_sel, o_win, conv_w, wc, wglu, wo, wout, *, seq, tm):
    t, d = x2.shape
    rb = tm // 8
    zc = lambda width, off: pl.BlockSpec((tm, width), lambda i, o=off // width: (i, o))
    zprev = lambda off: pl.BlockSpec((8, CONV_CH), lambda i, o=off // CONV_CH: (jnp.maximum(i * rb - 1, 0), o))
    row = lambda width: pl.BlockSpec((tm, width), lambda i: (i, 0))
    full = lambda a: pl.BlockSpec(a.shape, lambda i: (0, 0))
    return pl.pallas_call(
        functools.partial(_merge_kernel, seq=seq),
        grid=(t // tm,),
        in_specs=[row(d), zc(d, Z_MIX), zc(d, Z_MIX + d), zc(d, Z_MIX + 2 * d),
                  zc(CONV_CH, Z_CB), zc(CONV_CH, Z_CC), zc(CONV_CH, Z_CX), zprev(Z_CC), zprev(Z_CX),
                  row(SSM_WIDTH), row(ATTN_WIDTH), row(ATTN_WIDTH), row(ATTN_WIDTH),
                  full(conv_w), full(wc), full(wglu), full(wo), full(wout)],
        out_specs=row(d),
        out_shape=jax.ShapeDtypeStruct((t, d), F32),
        name="merge",
    )(x2, z, z, z, z, z, z, z, z, ys, o_cmp, o_sel, o_win, conv_w, wc, wglu, wo, wout)


def _ffn_kernel(x_ref, g_ref, wg_ref, wu_ref, wd_ref, o_ref, h_ref, acc_ref):
    f = pl.program_id(1)

    @pl.when(f == 0)
    def _():
        h_ref[...] = _rms(x_ref[...], g_ref[...]).astype(BF16)
        acc_ref[...] = jnp.zeros(acc_ref.shape, F32)

    h = h_ref[...]
    gate = _dot(h, wg_ref[...])
    up = _dot(h, wu_ref[...])
    act = (gate * _sigmoid(gate) * up).astype(BF16)
    acc_ref[...] += _dot(act, wd_ref[...])

    @pl.when(f == pl.num_programs(1) - 1)
    def _():
        o_ref[...] = x_ref[...] + acc_ref[...]


def _ffn(x2, g, w_gate_up, w_down, *, tm, tf):
    t, d = x2.shape
    nf = D_FF // tf
    return pl.pallas_call(
        _ffn_kernel,
        grid=(t // tm, nf),
        in_specs=[pl.BlockSpec((tm, d), lambda i, f: (i, 0)),
                  pl.BlockSpec((1, d), lambda i, f: (0, 0)),
                  pl.BlockSpec((d, tf), lambda i, f: (0, f)),
                  pl.BlockSpec((d, tf), lambda i, f: (0, nf + f)),
                  pl.BlockSpec((tf, d), lambda i, f: (f, 0))],
        out_specs=pl.BlockSpec((tm, d), lambda i, f: (i, 0)),
        out_shape=jax.ShapeDtypeStruct((t, d), F32),
        scratch_shapes=[pltpu.VMEM((tm, d), BF16), pltpu.VMEM((tm, d), F32)],
        compiler_params=pltpu.CompilerParams(dimension_semantics=("parallel", "arbitrary")),
        name="ffn",
    )(x2, g.reshape(1, d), w_gate_up, w_gate_up, w_down)


def _pick(n, pref):
    while n % pref:
        pref //= 2
    return pref


def _layer(x2, p, *, bsz, seq):
    t = x2.shape[0]
    z = _inproj(x2, p['mix_norm_g'], p['w_in'], tm=_pick(t, 512), tn=Z_WIDTH // 3)
    u = z[:, Z_U:Z_U + SSM_WIDTH].reshape(bsz, seq, SSM_WIDTH)
    ys = _s5_scan(u, p['s5']).reshape(t, SSM_WIDTH)
    kaug, vsel, kwin, vwin = _kv_prep(z, p['k_norm_g'], seq=seq, tp=_pick(seq, 512))
    cmp_kv = _compress(z, p['cmp_pe'], p['cmp_w1'], p['cmp_w2'], bsz=bsz, seq=seq)
    tq = 128
    o_cmp, qaug = _cmp_select(z, cmp_kv, p['q_norm_g'], p['k_norm_g'], bsz=bsz, seq=seq, tq=tq)
    o_sel = _flash(qaug, kaug, vsel, z, bsz=bsz, seq=seq, tq=tq, tk=_pick(seq, 256), window=0, branch=1, name="sel_attn")
    o_win = _flash(qaug, kwin, vwin, z, bsz=bsz, seq=seq, tq=tq, tk=128, window=WINDOW, branch=2, name="win_attn")
    x2 = _merge(x2, z, ys, o_cmp, o_sel, o_win, p['conv_w'], p['conv_w_out'], p['ssm_w_glu'], p['nsa_w_o'],
                p['w_out'], seq=seq, tm=_pick(seq, 256))
    return _ffn(x2, p['ffn_norm_g'], p['ffn_w_gate_up'], p['ffn_w_down'], tm=_pick(t, 1024), tf=256)


def kernel(x, mix_norm_g, w_in, ssm_lam_re, ssm_lam_im, ssm_b_re, ssm_b_im, ssm_c_re, ssm_c_im, ssm_d, ssm_log_dt, ssm_w_glu, conv_w, conv_w_out, q_norm_g, k_norm_g, cmp_pe, cmp_w1, cmp_w2, nsa_w_o, w_out, ffn_norm_g, ffn_w_gate_up, ffn_w_down):
    bsz, seq, d = x.shape
    x2 = x.reshape(bsz * seq, d)
    for i in range(w_in.shape[0]):
        p = dict(
            mix_norm_g=mix_norm_g[i], w_in=_permute_w_in(w_in[i]),
            s5=_s5_tables(ssm_lam_re[i], ssm_lam_im[i], ssm_b_re[i], ssm_b_im[i], ssm_c_re[i], ssm_c_im[i],
                          ssm_d[i], ssm_log_dt[i], seq // S5_CHUNK),
            ssm_w_glu=ssm_w_glu[i].astype(BF16), conv_w=conv_w[i], conv_w_out=conv_w_out[i].astype(BF16),
            q_norm_g=q_norm_g[i], k_norm_g=k_norm_g[i], cmp_pe=cmp_pe[i],
            cmp_w1=cmp_w1[i].astype(BF16), cmp_w2=cmp_w2[i].astype(BF16), nsa_w_o=nsa_w_o[i].astype(BF16),
            w_out=w_out[i].astype(BF16), ffn_norm_g=ffn_norm_g[i],
            ffn_w_gate_up=ffn_w_gate_up[i].astype(BF16), ffn_w_down=ffn_w_down[i].astype(BF16))
        x2 = _layer(x2, p, bsz=bsz, seq=seq)
    return x2.reshape(bsz, seq, d)
```

```python
import functools

import jax
import jax.numpy as jnp
import numpy as np
from jax import lax
from jax.experimental import pallas as pl
from jax.experimental.pallas import tpu as pltpu

F32 = jnp.float32
BF16 = jnp.bfloat16

D_MODEL = 1024
SSM_WIDTH = 512
SSM_GROUP = 16
SSM_GROUPS = SSM_WIDTH // SSM_GROUP
SSM_STATE = 64
CONV_CH = 512
CONV_K = 3
HEAD_DIM = 64
N_HEADS = 8
N_KV_HEADS = 2
GQA = N_HEADS // N_KV_HEADS
ATTN_WIDTH = N_HEADS * HEAD_DIM
KV_WIDTH = N_KV_HEADS * HEAD_DIM
CMP_BLOCK = 32
SEL_BLOCK = 64
SEL_SHIFT = 6
N_SELECT = 16
WINDOW = 512
CMP_HIDDEN = 256
FORCE_SCORE = 1e4
NSA_BRANCHES = 3
MIX_BRANCHES = 3
D_FF = 2816
RMS_EPS = 1e-6

LANES = 128
MASKED = -1e30
LOG2E = 1.4426950408889634
S5_CHUNK = 16
GATE_PAD = LANES

Z_MIX = 0
Z_U = Z_MIX + MIX_BRANCHES * D_MODEL
Z_CB = Z_U + SSM_WIDTH
Z_CC = Z_CB + CONV_CH
Z_CX = Z_CC + CONV_CH
Z_Q = Z_CX + CONV_CH
Z_KC = Z_Q + ATTN_WIDTH
Z_VC = Z_KC + KV_WIDTH
Z_KS = Z_VC + KV_WIDTH
Z_VS = Z_KS + KV_WIDTH
Z_KW = Z_VS + KV_WIDTH
Z_VW = Z_KW + KV_WIDTH
Z_GATE = Z_VW + KV_WIDTH
Z_WIDTH = Z_GATE + GATE_PAD


def _gelu_tanh(x):
    return 0.5 * x * (1.0 + jnp.tanh(np.sqrt(2.0 / np.pi).astype(np.float32) * (x + 0.044715 * (x * x * x))))


def _sigmoid(x):
    return 1.0 / (1.0 + jnp.exp(-x))


def _rms(x, g):
    return x * lax.rsqrt(jnp.mean(x * x, axis=-1, keepdims=True) + RMS_EPS) * g


def _dot(a, b):
    return jnp.dot(a, b, preferred_element_type=F32)


def _dot_nt(a, b):
    return lax.dot_general(a, b, (((1,), (1,)), ((), ())), preferred_element_type=F32)


def _inproj_kernel(x_ref, g_ref, w_ref, o_ref):
    h = _rms(x_ref[...], g_ref[...]).astype(BF16)
    o_ref[...] = _dot(h, w_ref[...])


def _inproj(x2, g, w, *, tm, tn):
    t, d = x2.shape
    n = w.shape[1]
    return pl.pallas_call(
        _inproj_kernel,
        grid=(n // tn, t // tm),
        in_specs=[pl.BlockSpec((tm, d), lambda j, i: (i, 0)),
                  pl.BlockSpec((1, d), lambda j, i: (0, 0)),
                  pl.BlockSpec((d, tn), lambda j, i: (0, j))],
        out_specs=pl.BlockSpec((tm, tn), lambda j, i: (i, j)),
        out_shape=jax.ShapeDtypeStruct((t, n), F32),
        name="inproj",
    )(x2, g.reshape(1, d), w)


def _permute_w_in(w):
    n_plain = SSM_WIDTH + 3 * CONV_CH + ATTN_WIDTH + 6 * KV_WIDTH
    n_gate = N_HEADS * NSA_BRANCHES
    gate = jnp.pad(w[:, n_plain:n_plain + n_gate], ((0, 0), (0, GATE_PAD - n_gate)))
    return jnp.concatenate([w[:, n_plain + n_gate:], w[:, :n_plain], gate], axis=1).astype(BF16)


def _s5_tables(lam_re, lam_im, b_re, b_im, c_re, c_im, d_skip, log_dt, n_chunks):
    hp = lax.Precision.HIGHEST
    g, p = lam_re.shape
    h, l = SSM_GROUP, S5_CHUNK
    lam = lax.complex(lam_re, lam_im)
    lam_dt = lam * jnp.exp(log_dt)[:, None]
    lam_bar = jnp.exp(lam_dt)
    b_bar = ((lam_bar - 1.0) / lam)[..., None] * lax.complex(b_re, b_im)
    c = lax.complex(c_re, c_im)
    pw = jnp.exp(lam_dt[None] * jnp.arange(l + 1, dtype=F32)[:, None, None])
    kern = jnp.einsum('gop,tgp,gpi->tgio', c, pw[:l], b_bar, precision=hp).real
    r = jnp.arange(l)
    lag = r[None, :] - r[:, None]
    toe = jnp.where((lag >= 0)[:, :, None, None, None], kern[jnp.clip(lag, 0, l - 1)], 0.0)
    toe = toe.transpose(2, 0, 3, 1, 4)
    eye = (r[:, None, None, None] == r[None, None, :, None]) & (jnp.arange(h)[None, :, None, None] == jnp.arange(h)[None, None, None, :])
    toe = toe + jnp.where(eye[None], d_skip.reshape(g, 1, h, 1, 1), 0.0)
    toe = toe.reshape(g, l * h, l * h)
    st = pw[l - 1 - r][:, :, :, None] * b_bar[None]
    st = st.transpose(1, 0, 3, 2).reshape(g, l * h, p)
    cp = c[None] * pw[1:l + 1][:, :, None, :]
    cp = cp.transpose(1, 3, 0, 2).reshape(g, p, l * h)
    n_steps = max(1, int(np.ceil(np.log2(n_chunks))))
    dec = jnp.exp(lam_dt[None] * (l * (2.0 ** jnp.arange(n_steps, dtype=F32)))[:, None, None])

    def pair_diag(m):
        m = m.reshape(g // 2, 2, *m.shape[1:])
        z = jnp.zeros_like(m[:, 0])
        return jnp.concatenate([jnp.concatenate([m[:, 0], z], axis=2), jnp.concatenate([z, m[:, 1]], axis=2)], axis=1)

    def pair_lanes(m):
        return m.reshape(m.shape[0], g // 2, 2 * p).transpose(1, 0, 2)

    return dict(toe=pair_diag(toe).astype(BF16),
                s_re=pair_diag(st.real).astype(BF16), s_im=pair_diag(st.imag).astype(BF16),
                c_re=pair_diag(cp.real).astype(BF16), c_im=pair_diag(-cp.imag).astype(BF16),
                d_re=pair_lanes(dec.real), d_im=pair_lanes(dec.imag))


def _s5_kernel(u_ref, toe_ref, sre_ref, sim_ref, cre_ref, cim_ref, dre_ref, dim_ref, y_ref):
    nb, _, nc, _ = u_ref.shape
    n_steps = dre_ref.shape[1]
    row = lax.broadcasted_iota(jnp.int32, (nc, 2 * SSM_STATE), 0)
    for b in range(nb):
        ub = u_ref[b, 0].astype(BF16)
        xr = _dot(ub, sre_ref[0])
        xi = _dot(ub, sim_ref[0])
        for k in range(n_steps):
            s = 1 << k
            if s >= nc:
                break
            dr = dre_ref[0, k:k + 1, :]
            di = dim_ref[0, k:k + 1, :]
            sr = jnp.where(row >= s, pltpu.roll(xr, s, 0), 0.0)
            si = jnp.where(row >= s, pltpu.roll(xi, s, 0), 0.0)
            xr, xi = xr + (dr * sr - di * si), xi + (dr * si + di * sr)
        pr = jnp.where(row >= 1, pltpu.roll(xr, 1, 0), 0.0).astype(BF16)
        pi = jnp.where(row >= 1, pltpu.roll(xi, 1, 0), 0.0).astype(BF16)
        y_ref[b, 0] = _dot(ub, toe_ref[0]) + _dot(pr, cre_ref[0]) + _dot(pi, cim_ref[0])


def _s5_scan(u, tabs):
    bsz, seq, _ = u.shape
    l, h, gp = S5_CHUNK, SSM_GROUP, SSM_GROUPS // 2
    nc = seq // l
    w = 2 * l * h
    uc = u.reshape(bsz, nc, l, gp, 2, h).transpose(0, 3, 1, 4, 2, 5).reshape(bsz, gp, nc, w)
    wspec = lambda a: pl.BlockSpec((1,) + a.shape[1:], lambda j: (j, 0, 0))
    names = ('toe', 's_re', 's_im', 'c_re', 'c_im', 'd_re', 'd_im')
    y = pl.pallas_call(
        _s5_kernel,
        grid=(gp,),
        in_specs=[pl.BlockSpec((bsz, 1, nc, w), lambda j: (0, j, 0, 0))] + [wspec(tabs[k]) for k in names],
        out_specs=pl.BlockSpec((bsz, 1, nc, w), lambda j: (0, j, 0, 0)),
        out_shape=jax.ShapeDtypeStruct((bsz, gp, nc, w), F32),
        name="s5_scan",
    )(uc, *[tabs[k] for k in names])
    return y.reshape(bsz, gp, nc, 2, l, h).transpose(0, 2, 4, 1, 3, 5).reshape(bsz, seq, SSM_WIDTH)


def _kv_prep_kernel(ks_ref, vs_ref, kw_ref, vw_ref, g_ref, kaug_ref, vsel_ref, kwin_ref, vwin_ref, *, seq):
    tp = ks_ref.shape[0]
    pos = (pl.program_id(0) * tp) % seq + lax.broadcasted_iota(jnp.int32, (tp, HEAD_DIM), 0)
    blk = jnp.right_shift(pos, SEL_SHIFT)
    onehot = jnp.where(lax.broadcasted_iota(jnp.int32, (tp, HEAD_DIM), 1) == blk, 1.0, 0.0).astype(BF16)
    zeros = jnp.zeros((tp, HEAD_DIM), BF16)
    for hh in range(N_KV_HEADS):
        sl = slice(hh * HEAD_DIM, (hh + 1) * HEAD_DIM)
        kn = _rms(ks_ref[:, sl], g_ref[0:1, :]).astype(BF16)
        kaug_ref[hh] = jnp.concatenate([kn, onehot], axis=1)
        kn = _rms(kw_ref[:, sl], g_ref[1:2, :]).astype(BF16)
        kwin_ref[hh] = jnp.concatenate([kn, zeros], axis=1)
    for v_ref, vt_ref in ((vs_ref, vsel_ref), (vw_ref, vwin_ref)):
        tk = vt_ref.shape[3]
        vt = v_ref[...].T.astype(BF16)
        for hh in range(N_KV_HEADS):
            for j in range(tp // tk):
                vt_ref[hh, j] = vt[hh * HEAD_DIM:(hh + 1) * HEAD_DIM, j * tk:(j + 1) * tk]


def _kv_prep(z, k_norm_g, *, seq, tp, tk_sel, tk_win):
    t = z.shape[0]
    col = lambda off: pl.BlockSpec((tp, KV_WIDTH), lambda i, o=off // KV_WIDTH: (i, o))
    aug = 2 * HEAD_DIM
    kspec = pl.BlockSpec((N_KV_HEADS, tp, aug), lambda i: (0, i, 0))
    vspec = lambda tk: pl.BlockSpec((N_KV_HEADS, tp // tk, HEAD_DIM, tk), lambda i: (0, i, 0, 0))
    vshape = lambda tk: jax.ShapeDtypeStruct((N_KV_HEADS, t // tk, HEAD_DIM, tk), BF16)
    return pl.pallas_call(
        functools.partial(_kv_prep_kernel, seq=seq),
        grid=(t // tp,),
        in_specs=[col(Z_KS), col(Z_VS), col(Z_KW), col(Z_VW), pl.BlockSpec((2, HEAD_DIM), lambda i: (0, 0))],
        out_specs=[kspec, vspec(tk_sel), kspec, vspec(tk_win)],
        out_shape=[jax.ShapeDtypeStruct((N_KV_HEADS, t, aug), BF16), vshape(tk_sel),
                   jax.ShapeDtypeStruct((N_KV_HEADS, t, aug), BF16), vshape(tk_win)],
        name="kv_prep",
    )(z, z, z, z, k_norm_g[1:3])


def _compress_kernel(x_ref, pe_ref, w1_ref, w2_ref, o_ref):
    x = (x_ref[0] + pe_ref[0]).astype(BF16)
    hid = _gelu_tanh(_dot(x, w1_ref[0])).astype(BF16)
    o_ref[0] = _dot(hid, w2_ref[0])


def _compress(z, cmp_pe, w1, w2, *, bsz, seq):
    nc = seq // CMP_BLOCK
    half = nc // 2

    def blocks(off):
        t = z[:, off:off + KV_WIDTH].reshape(bsz, half, 2, CMP_BLOCK, N_KV_HEADS, HEAD_DIM)
        return t.transpose(0, 4, 2, 1, 3, 5).reshape(bsz * N_KV_HEADS * nc, CMP_BLOCK * HEAD_DIM)

    x = jnp.stack([blocks(Z_KC), blocks(Z_VC)])
    m, kdim = x.shape[1:]
    out = pl.pallas_call(
        _compress_kernel,
        grid=(2,),
        in_specs=[pl.BlockSpec((1, m, kdim), lambda w: (w, 0, 0)),
                  pl.BlockSpec((1, 1, kdim), lambda w: (w, 0, 0)),
                  pl.BlockSpec((1, kdim, CMP_HIDDEN), lambda w: (w, 0, 0)),
                  pl.BlockSpec((1, CMP_HIDDEN, HEAD_DIM), lambda w: (w, 0, 0))],
        out_specs=pl.BlockSpec((1, m, HEAD_DIM), lambda w: (w, 0, 0)),
        out_shape=jax.ShapeDtypeStruct((2, m, HEAD_DIM), F32),
        name="compress",
    )(x, cmp_pe.reshape(2, 1, kdim), w1, w2)
    return out.reshape(2, bsz * N_KV_HEADS, nc, HEAD_DIM)


def _head_gate(gl, kvh, g, branch):
    c0 = g * NSA_BRANCHES + branch
    c1 = (GQA + g) * NSA_BRANCHES + branch
    return _sigmoid(jnp.where(kvh == 0, gl[:, c0:c0 + 1], gl[:, c1:c1 + 1]))


def _cmp_select_kernel(q_ref, gl_ref, kc_ref, vc_ref, qg_ref, kg_ref, o_ref, qaug_ref, imp_ref):
    tq = q_ref.shape[0]
    nc = kc_ref.shape[1]
    nb = nc // 2
    kvh = pl.program_id(1)
    q0 = pl.program_id(2) * tq
    scale = HEAD_DIM ** -0.5
    qf = [_rms(q_ref[:, g * HEAD_DIM:(g + 1) * HEAD_DIM], qg_ref[...]) * scale for g in range(GQA)]
    qn = [x.astype(BF16) for x in qf]
    q2 = [(x * LOG2E).astype(BF16) for x in qf]
    kc = _rms(kc_ref[0], kg_ref[...]).astype(BF16)
    vc = vc_ref[0].astype(BF16)
    s = _dot_nt(jnp.concatenate(qn, axis=0), kc)
    rows = GQA * tq
    t = q0 + (lax.broadcasted_iota(jnp.int32, (rows, nc), 0) & (tq - 1))
    lane = lax.broadcasted_iota(jnp.int32, (rows, nc), 1)
    blk = jnp.where(lane < nb, 2 * lane, 2 * (lane - nb) + 1)
    valid = (blk + 1) * CMP_BLOCK - 1 <= t
    sm = jnp.where(valid, s, MASKED)
    m = jnp.max(sm, axis=-1, keepdims=True)
    m = jnp.where(m > 0.5 * MASKED, m, 0.0)
    e = jnp.where(valid, jnp.exp(sm - m), 0.0)
    p = e / jnp.maximum(jnp.sum(e, axis=-1, keepdims=True), 1e-30)
    o = _dot(p.astype(BF16), vc)
    gl = gl_ref[...]
    o_ref[...] = jnp.concatenate(
        [o[g * tq:(g + 1) * tq] * _head_gate(gl, kvh, g, 0) for g in range(GQA)], axis=1)

    psum = p[0:tq]
    for g in range(1, GQA):
        psum = psum + p[g * tq:(g + 1) * tq]
    pt = psum.T
    imp = pt[:nb] + pt[nb:]
    j = lax.broadcasted_iota(jnp.int32, (nb, tq), 0)
    tt = q0 + lax.broadcasted_iota(jnp.int32, (nb, tq), 1)
    cur = jnp.right_shift(tt, SEL_SHIFT)
    forced = (j == 0) | (j == cur) | (j == cur - 1)
    visible = j <= cur
    imp = jnp.where(forced, FORCE_SCORE, jnp.where(visible, imp, -jnp.inf))
    imp_ref[...] = imp
    n_vis = jnp.minimum(nb, (q0 + tq - 1) // SEL_BLOCK + 1)

    def count(i, cnt):
        vi = imp_ref[pl.ds(i, 1), :]
        beats = (vi > imp) | ((vi == imp) & (i < j))
        return cnt + jnp.where(beats, 1.0, 0.0)

    cnt = lax.fori_loop(0, n_vis, count, jnp.zeros((nb, tq), F32))
    selneg_t = jnp.where(visible & (cnt < float(N_SELECT)), 0.0, MASKED)
    pad_rows = LANES - nb
    if pad_rows > 0:
        selneg_t = jnp.concatenate([selneg_t, jnp.zeros((pad_rows, tq), F32)], axis=0)
    selneg = selneg_t.T[:, :HEAD_DIM].astype(BF16)
    for g in range(GQA):
        qaug_ref[0, g] = jnp.concatenate([q2[g], selneg], axis=1)


def _cmp_select(z, cmp_kv, q_norm_g, k_norm_g, *, bsz, seq, tq):
    t = z.shape[0]
    nq = seq // tq
    nc = seq // CMP_BLOCK
    aug = 2 * HEAD_DIM
    qw = GQA * HEAD_DIM
    kc, vc = cmp_kv[0], cmp_kv[1]
    return pl.pallas_call(
        _cmp_select_kernel,
        grid=(bsz, N_KV_HEADS, nq),
        in_specs=[pl.BlockSpec((tq, qw), lambda b, h, i: (b * nq + i, Z_Q // qw + h)),
                  pl.BlockSpec((tq, GATE_PAD), lambda b, h, i: (b * nq + i, Z_GATE // GATE_PAD)),
                  pl.BlockSpec((1, nc, HEAD_DIM), lambda b, h, i: (b * N_KV_HEADS + h, 0, 0)),
                  pl.BlockSpec((1, nc, HEAD_DIM), lambda b, h, i: (b * N_KV_HEADS + h, 0, 0)),
                  pl.BlockSpec((1, HEAD_DIM), lambda b, h, i: (0, 0)),
                  pl.BlockSpec((1, HEAD_DIM), lambda b, h, i: (0, 0))],
        out_specs=[pl.BlockSpec((tq, qw), lambda b, h, i: (b * nq + i, h)),
                   pl.BlockSpec((1, GQA, tq, aug), lambda b, h, i: (b * N_KV_HEADS + h, 0, i, 0))],
        out_shape=[jax.ShapeDtypeStruct((t, ATTN_WIDTH), F32),
                   jax.ShapeDtypeStruct((bsz * N_KV_HEADS, GQA, seq, aug), BF16)],
        scratch_shapes=[pltpu.VMEM((nc // 2, tq), F32)],
        name="cmp_select",
    )(z, z, kc, vc, q_norm_g.reshape(1, HEAD_DIM), k_norm_g[0:1])


def _flash_kernel(q_ref, k_ref, vt_ref, gl_ref, o_ref, p_ref, m_ref, l_ref, a_ref, acc_ref, *, window, branch):
    tq = q_ref.shape[2]
    tk = vt_ref.shape[3]
    cols = GQA * tq
    kvh = pl.program_id(1)
    q0 = pl.program_id(2) * tq
    qa = q_ref[0].reshape(cols, q_ref.shape[3])
    t = q0 + (lax.broadcasted_iota(jnp.int32, (tk, cols), 1) & (tq - 1))
    key_row = lax.broadcasted_iota(jnp.int32, (tk, cols), 0)
    lo = jnp.maximum(0, q0 - window + 1) // tk if window else 0
    hi = (q0 + tq + tk - 1) // tk
    mid1 = jnp.maximum(lo, (q0 + tq - 1 - window) // tk + 1) if window else lo
    mid2 = q0 // tk

    def scores(kb):
        return _dot_nt(k_ref[0, pl.ds(pl.multiple_of(kb * tk, tk), tk), :], qa)

    m_ref[...] = jnp.full(m_ref.shape, MASKED, F32)
    l_ref[...] = jnp.zeros(l_ref.shape, F32)
    a_ref[...] = jnp.ones(a_ref.shape, F32)
    acc_ref[...] = jnp.zeros(acc_ref.shape, F32)
    p_ref[...] = jnp.zeros(p_ref.shape, BF16)

    def tile(kb, s, masked):
        pv = _dot(vt_ref[0, jnp.maximum(kb - 1, lo)], p_ref[...])
        s_next = scores(jnp.minimum(kb + 1, hi - 1))
        if masked:
            key = kb * tk + key_row
            valid = key <= t
            if window:
                valid = valid & (key > t - window)
            s = jnp.where(valid, s, MASKED)
        m_old = m_ref[...]
        m_new = jnp.maximum(m_old, jnp.max(s, axis=0, keepdims=True))
        alpha = jnp.exp2(m_old - m_new)
        p = jnp.exp2(s - m_new)
        l_ref[...] = alpha * l_ref[...] + jnp.sum(p, axis=0, keepdims=True)
        acc_ref[...] = a_ref[...] * acc_ref[...] + pv
        a_ref[...] = alpha
        p_ref[...] = p.astype(BF16)
        m_ref[...] = m_new
        return s_next

    def run(first, last, s, masked):
        return lax.fori_loop(first, last, lambda kb, s: tile(kb, s, masked), s)

    s = run(lo, mid1, scores(lo), True)
    s = run(mid1, mid2, s, False)
    run(mid2, hi, s, True)
    acc = a_ref[...] * acc_ref[...] + _dot(vt_ref[0, hi - 1], p_ref[...])
    o_t = acc / l_ref[...]
    gl = gl_ref[...]
    out = []
    for g in range(0, GQA, 2):
        pair = jnp.concatenate([o_t[:, g * tq:(g + 1) * tq], o_t[:, (g + 1) * tq:(g + 2) * tq]], axis=0).T
        out.append(pair[:, :HEAD_DIM] * _head_gate(gl, kvh, g, branch))
        out.append(pair[:, HEAD_DIM:] * _head_gate(gl, kvh, g + 1, branch))
    o_ref[...] = jnp.concatenate(out, axis=1)


def _flash(qaug, k, vt, z, *, bsz, seq, tq, window, branch, name):
    t = z.shape[0]
    nq = seq // tq
    aug = qaug.shape[3]
    tk = vt.shape[3]
    qw = GQA * HEAD_DIM
    cols = GQA * tq
    return pl.pallas_call(
        functools.partial(_flash_kernel, window=window, branch=branch),
        grid=(bsz, N_KV_HEADS, nq),
        in_specs=[pl.BlockSpec((1, GQA, tq, aug), lambda b, h, i: (b * N_KV_HEADS + h, 0, i, 0)),
                  pl.BlockSpec((1, seq, aug), lambda b, h, i: (h, b, 0)),
                  pl.BlockSpec((1, seq // tk, HEAD_DIM, tk), lambda b, h, i: (h, b, 0, 0)),
                  pl.BlockSpec((tq, GATE_PAD), lambda b, h, i: (b * nq + i, Z_GATE // GATE_PAD))],
        out_specs=pl.BlockSpec((tq, qw), lambda b, h, i: (b * nq + i, h)),
        out_shape=jax.ShapeDtypeStruct((t, ATTN_WIDTH), F32),
        scratch_shapes=[pltpu.VMEM((tk, cols), BF16),
                        pltpu.VMEM((1, cols), F32), pltpu.VMEM((1, cols), F32), pltpu.VMEM((1, cols), F32),
                        pltpu.VMEM((HEAD_DIM, cols), F32)],
        name=name,
    )(qaug, k, vt, z)


def _merge_kernel(x_ref, g0_ref, g1_ref, g2_ref, cb_ref, cc_ref, cx_ref, pc_ref, px_ref, ys_ref,
                  oc_ref, os_ref, ow_ref, cw_ref, wc_ref, wglu_ref, wo_ref, wout_ref, out_ref, *, seq):
    tm = x_ref.shape[0]
    yg = _dot(_gelu_tanh(ys_ref[...]).astype(BF16), wglu_ref[...])
    y_ssm = yg[:, :D_MODEL] * _sigmoid(yg[:, D_MODEL:])
    zc = cc_ref[...] * cx_ref[...]
    keep = jnp.where((pl.program_id(0) * tm) % seq != 0, 1.0, 0.0)
    prev = pc_ref[...] * px_ref[...] * keep
    row = lax.broadcasted_iota(jnp.int32, zc.shape, 0)
    z1 = jnp.where(row >= 1, pltpu.roll(zc, 1, 0), prev[7:8, :])
    z2 = jnp.where(row >= 2, pltpu.roll(zc, 2, 0), jnp.where(row == 1, prev[7:8, :], prev[6:7, :]))
    conv = cw_ref[0:1, :] * z2 + cw_ref[1:2, :] * z1 + cw_ref[2:3, :] * zc
    y_conv = _dot((cb_ref[...] * conv).astype(BF16), wc_ref[...])
    y_attn = _dot((oc_ref[...] + os_ref[...] + ow_ref[...]).astype(BF16), wo_ref[...])
    mixed = _sigmoid(g0_ref[...]) * y_ssm + _sigmoid(g1_ref[...]) * y_conv + _sigmoid(g2_ref[...]) * y_attn
    out_ref[...] = x_ref[...] + _dot(mixed.astype(BF16), wout_ref[...])


def _merge(x2, z, ys, o_cmp, o_sel, o_win, conv_w, wc, wglu, wo, wout, *, seq, tm):
    t, d = x2.shape
    rb = tm // 8
    zc = lambda width, off: pl.BlockSpec((tm, width), lambda i, o=off // width: (i, o))
    zprev = lambda off: pl.BlockSpec((8, CONV_CH), lambda i, o=off // CONV_CH: (jnp.maximum(i * rb - 1, 0), o))
    row = lambda width: pl.BlockSpec((tm, width), lambda i: (i, 0))
    full = lambda a: pl.BlockSpec(a.shape, lambda i: (0, 0))
    return pl.pallas_call(
        functools.partial(_merge_kernel, seq=seq),
        grid=(t // tm,),
        in_specs=[row(d), zc(d, Z_MIX), zc(d, Z_MIX + d), zc(d, Z_MIX + 2 * d),
                  zc(CONV_CH, Z_CB), zc(CONV_CH, Z_CC), zc(CONV_CH, Z_CX), zprev(Z_CC), zprev(Z_CX),
                  row(SSM_WIDTH), row(ATTN_WIDTH), row(ATTN_WIDTH), row(ATTN_WIDTH),
                  full(conv_w), full(wc), full(wglu), full(wo), full(wout)],
        out_specs=row(d),
        out_shape=jax.ShapeDtypeStruct((t, d), F32),
        name="merge",
    )(x2, z, z, z, z, z, z, z, z, ys, o_cmp, o_sel, o_win, conv_w, wc, wglu, wo, wout)


def _ffn_kernel(x_ref, g_ref, wg_ref, wu_ref, wd_ref, o_ref, h_ref, acc_ref):
    f = pl.program_id(1)

    @pl.when(f == 0)
    def _():
        h_ref[...] = _rms(x_ref[...], g_ref[...]).astype(BF16)
        acc_ref[...] = jnp.zeros(acc_ref.shape, F32)

    h = h_ref[...]
    gate = _dot(h, wg_ref[...])
    up = _dot(h, wu_ref[...])
    act = (gate * _sigmoid(gate) * up).astype(BF16)
    acc_ref[...] += _dot(act, wd_ref[...])

    @pl.when(f == pl.num_programs(1) - 1)
    def _():
        o_ref[...] = x_ref[...] + acc_ref[...]


def _ffn(x2, g, w_gate_up, w_down, *, tm, tf):
    t, d = x2.shape
    nf = D_FF // tf
    return pl.pallas_call(
        _ffn_kernel,
        grid=(t // tm, nf),
        in_specs=[pl.BlockSpec((tm, d), lambda i, f: (i, 0)),
                  pl.BlockSpec((1, d), lambda i, f: (0, 0)),
                  pl.BlockSpec((d, tf), lambda i, f: (0, f)),
                  pl.BlockSpec((d, tf), lambda i, f: (0, nf + f)),
                  pl.BlockSpec((tf, d), lambda i, f: (f, 0))],
        out_specs=pl.BlockSpec((tm, d), lambda i, f: (i, 0)),
        out_shape=jax.ShapeDtypeStruct((t, d), F32),
        scratch_shapes=[pltpu.VMEM((tm, d), BF16), pltpu.VMEM((tm, d), F32)],
        compiler_params=pltpu.CompilerParams(dimension_semantics=("parallel", "arbitrary")),
        name="ffn",
    )(x2, g.reshape(1, d), w_gate_up, w_gate_up, w_down)


def _pick(n, pref):
    while n % pref:
        pref //= 2
    return pref


def _layer(x2, p, *, bsz, seq):
    t = x2.shape[0]
    z = _inproj(x2, p['mix_norm_g'], p['w_in'], tm=_pick(t, 512), tn=Z_WIDTH // 3)
    u = z[:, Z_U:Z_U + SSM_WIDTH].reshape(bsz, seq, SSM_WIDTH)
    ys = _s5_scan(u, p['s5']).reshape(t, SSM_WIDTH)
    kaug, vsel, kwin, vwin = _kv_prep(z, p['k_norm_g'], seq=seq, tp=_pick(seq, 512), tk_sel=512, tk_win=256)
    cmp_kv = _compress(z, p['cmp_pe'], p['cmp_w1'], p['cmp_w2'], bsz=bsz, seq=seq)
    o_cmp, qaug = _cmp_select(z, cmp_kv, p['q_norm_g'], p['k_norm_g'], bsz=bsz, seq=seq, tq=128)
    o_sel = _flash(qaug, kaug, vsel, z, bsz=bsz, seq=seq, tq=256, window=0, branch=1, name="sel_attn")
    o_win = _flash(qaug, kwin, vwin, z, bsz=bsz, seq=seq, tq=256, window=WINDOW, branch=2, name="win_attn")
    x2 = _merge(x2, z, ys, o_cmp, o_sel, o_win, p['conv_w'], p['conv_w_out'], p['ssm_w_glu'], p['nsa_w_o'],
                p['w_out'], seq=seq, tm=_pick(seq, 256))
    return _ffn(x2, p['ffn_norm_g'], p['ffn_w_gate_up'], p['ffn_w_down'], tm=_pick(t, 1024), tf=256)


def kernel(x, mix_norm_g, w_in, ssm_lam_re, ssm_lam_im, ssm_b_re, ssm_b_im, ssm_c_re, ssm_c_im, ssm_d, ssm_log_dt, ssm_w_glu, conv_w, conv_w_out, q_norm_g, k_norm_g, cmp_pe, cmp_w1, cmp_w2, nsa_w_o, w_out, ffn_norm_g, ffn_w_gate_up, ffn_w_down):
    bsz, seq, d = x.shape
    x2 = x.reshape(bsz * seq, d)
    for i in range(w_in.shape[0]):
        p = dict(
            mix_norm_g=mix_norm_g[i], w_in=_permute_w_in(w_in[i]),
            s5=_s5_tables(ssm_lam_re[i], ssm_lam_im[i], ssm_b_re[i], ssm_b_im[i], ssm_c_re[i], ssm_c_im[i],
                          ssm_d[i], ssm_log_dt[i], seq // S5_CHUNK),
            ssm_w_glu=ssm_w_glu[i].astype(BF16), conv_w=conv_w[i], conv_w_out=conv_w_out[i].astype(BF16),
            q_norm_g=q_norm_g[i], k_norm_g=k_norm_g[i], cmp_pe=cmp_pe[i],
            cmp_w1=cmp_w1[i].astype(BF16), cmp_w2=cmp_w2[i].astype(BF16), nsa_w_o=nsa_w_o[i].astype(BF16),
            w_out=w_out[i].astype(BF16), ffn_norm_g=ffn_norm_g[i],
            ffn_w_gate_up=ffn_w_gate_up[i].astype(BF16), ffn_w_down=ffn_w_down[i].astype(BF16))
        x2 = _layer(x2, p, bsz=bsz, seq=seq)
    return x2.reshape(bsz, seq, d)
```

```python
import functools

import jax
import jax.numpy as jnp
import numpy as np
from jax import lax
from jax.experimental import pallas as pl
from jax.experimental.pallas import tpu as pltpu

F32 = jnp.float32
BF16 = jnp.bfloat16

D_MODEL = 1024
SSM_WIDTH = 512
SSM_GROUP = 16
SSM_GROUPS = SSM_WIDTH // SSM_GROUP
SSM_STATE = 64
CONV_CH = 512
CONV_K = 3
HEAD_DIM = 64
HEAD_SHIFT = 6
N_HEADS = 8
N_KV_HEADS = 2
GQA = N_HEADS // N_KV_HEADS
ATTN_WIDTH = N_HEADS * HEAD_DIM
KV_WIDTH = N_KV_HEADS * HEAD_DIM
CMP_BLOCK = 32
SEL_BLOCK = 64
SEL_SHIFT = 6
N_SELECT = 16
WINDOW = 512
CMP_HIDDEN = 256
FORCE_SCORE = 1e4
NSA_BRANCHES = 3
MIX_BRANCHES = 3
D_FF = 2816
RMS_EPS = 1e-6

LANES = 128
MASKED = -1e30
LOG2E = 1.4426950408889634
S5_CHUNK = 16
S5_TILE_GROUPS = LANES // SSM_GROUP
GATE_PAD = LANES

Z_MIX = 0
Z_U = Z_MIX + MIX_BRANCHES * D_MODEL
Z_CB = Z_U + SSM_WIDTH
Z_CC = Z_CB + CONV_CH
Z_CX = Z_CC + CONV_CH
Z_Q = Z_CX + CONV_CH
Z_KC = Z_Q + ATTN_WIDTH
Z_VC = Z_KC + KV_WIDTH
Z_KS = Z_VC + KV_WIDTH
Z_VS = Z_KS + KV_WIDTH
Z_KW = Z_VS + KV_WIDTH
Z_VW = Z_KW + KV_WIDTH
Z_GATE = Z_VW + KV_WIDTH
Z_WIDTH = Z_GATE + GATE_PAD


def _gelu_tanh(x):
    return 0.5 * x * (1.0 + jnp.tanh(np.sqrt(2.0 / np.pi).astype(np.float32) * (x + 0.044715 * (x * x * x))))


def _sigmoid(x):
    return 1.0 / (1.0 + jnp.exp(-x))


def _rms(x, g):
    return x * lax.rsqrt(jnp.mean(x * x, axis=-1, keepdims=True) + RMS_EPS) * g


def _dot(a, b):
    return jnp.dot(a, b, preferred_element_type=F32)


def _dot_nt(a, b):
    return lax.dot_general(a, b, (((1,), (1,)), ((), ())), preferred_element_type=F32)


def _inproj_kernel(x_ref, g_ref, w_ref, o_ref):
    h = _rms(x_ref[...], g_ref[...]).astype(BF16)
    o_ref[...] = _dot(h, w_ref[...])


def _inproj(x2, g, w, *, tm, tn):
    t, d = x2.shape
    n = w.shape[1]
    return pl.pallas_call(
        _inproj_kernel,
        grid=(n // tn, t // tm),
        in_specs=[pl.BlockSpec((tm, d), lambda j, i: (i, 0)),
                  pl.BlockSpec((1, d), lambda j, i: (0, 0)),
                  pl.BlockSpec((d, tn), lambda j, i: (0, j))],
        out_specs=pl.BlockSpec((tm, tn), lambda j, i: (i, j)),
        out_shape=jax.ShapeDtypeStruct((t, n), F32),
        name="inproj",
    )(x2, g.reshape(1, d), w)


def _permute_w_in(w):
    n_plain = SSM_WIDTH + 3 * CONV_CH + ATTN_WIDTH + 6 * KV_WIDTH
    n_gate = N_HEADS * NSA_BRANCHES
    gate = jnp.pad(w[:, n_plain:n_plain + n_gate], ((0, 0), (0, GATE_PAD - n_gate)))
    return jnp.concatenate([w[:, n_plain + n_gate:], w[:, :n_plain], gate], axis=1).astype(BF16)


def _s5_tables(lam_re, lam_im, b_re, b_im, c_re, c_im, d_skip, log_dt, n_chunks):
    hp = lax.Precision.HIGHEST
    g, p = lam_re.shape
    h, l, gt = SSM_GROUP, S5_CHUNK, S5_TILE_GROUPS
    nt = g // gt
    dt = jnp.exp(log_dt)[:, None]
    ar, ai = lam_re * dt, lam_im * dt

    def powers(k):
        mag = jnp.exp(ar[None] * k[:, None, None])
        ang = ai[None] * k[:, None, None]
        return mag * jnp.cos(ang), mag * jnp.sin(ang)

    pr, pi = powers(jnp.arange(l + 1, dtype=F32))
    nr, ni = pr[1] - 1.0, pi[1]
    den = lam_re * lam_re + lam_im * lam_im
    fr, fi = (nr * lam_re + ni * lam_im) / den, (ni * lam_re - nr * lam_im) / den
    bbr = fr[..., None] * b_re - fi[..., None] * b_im
    bbi = fr[..., None] * b_im + fi[..., None] * b_re
    wr = pr[:l, :, :, None] * bbr[None] - pi[:l, :, :, None] * bbi[None]
    wi = pr[:l, :, :, None] * bbi[None] + pi[:l, :, :, None] * bbr[None]
    kern = (jnp.einsum('gop,tgpi->tgio', c_re, wr, precision=hp)
            - jnp.einsum('gop,tgpi->tgio', c_im, wi, precision=hp))
    r = jnp.arange(l)
    lag = r[None, :] - r[:, None]
    toe = jnp.where((lag >= 0)[:, :, None, None, None], kern[jnp.clip(lag, 0, l - 1)], 0.0)
    toe = toe.transpose(2, 0, 3, 1, 4)
    eye_t = (r[:, None, None, None] == r[None, None, :, None]) & (jnp.arange(h)[None, :, None, None] == jnp.arange(h)[None, None, None, :])
    toe = toe + jnp.where(eye_t[None], d_skip.reshape(g, 1, h, 1, 1), 0.0)
    eye_g = jnp.eye(gt, dtype=F32)
    toe = toe.reshape(nt, gt, l, h, l, h).transpose(0, 2, 1, 3, 4, 5)
    toe = toe[:, :, :, :, :, None, :] * eye_g[None, None, :, None, None, :, None]
    toe = toe.reshape(nt, l * gt * h, l * gt * h)

    def state_in(xr, xi):
        def one(x):
            x = x.transpose(1, 0, 3, 2).reshape(nt, gt, l, h, p).transpose(0, 2, 1, 3, 4)
            x = x[:, :, :, :, None, :] * eye_g[None, None, :, None, :, None]
            return x.reshape(nt, l * gt * h, gt * p).astype(BF16)
        return one(xr), one(xi)

    qr, qi = pr[l - 1 - r], pi[l - 1 - r]
    s_re, s_im = state_in(qr[..., None] * bbr[None] - qi[..., None] * bbi[None],
                          qr[..., None] * bbi[None] + qi[..., None] * bbr[None])

    def state_out(x):
        x = x.transpose(1, 3, 0, 2).reshape(nt, gt, p, l, h)
        x = x[:, :, :, :, None, :] * eye_g[None, :, None, None, :, None]
        return x.reshape(nt, gt * p, l * gt * h).astype(BF16)

    er, ei = pr[1:l + 1][:, :, None, :], pi[1:l + 1][:, :, None, :]
    c_re_t = state_out(c_re[None] * er - c_im[None] * ei)
    c_im_t = state_out(-(c_re[None] * ei + c_im[None] * er))
    n_steps = max(1, int(np.ceil(np.log2(n_chunks))))
    dr, di = powers(l * (2.0 ** jnp.arange(n_steps, dtype=F32)))
    lanes = lambda m: m.reshape(n_steps, nt, gt * p).transpose(1, 0, 2)
    return dict(toe=toe.astype(BF16), s_re=s_re, s_im=s_im, c_re=c_re_t, c_im=c_im_t, d_re=lanes(dr), d_im=lanes(di))


def _s5_kernel(u_ref, toe_ref, sre_ref, sim_ref, cre_ref, cim_ref, dre_ref, dim_ref, y_ref):
    seq = u_ref.shape[0]
    l = S5_CHUNK
    nc = seq // l
    n_steps = dre_ref.shape[1]
    ucat = jnp.concatenate([u_ref[pl.ds(r, nc, stride=l), :] for r in range(l)], axis=1).astype(BF16)
    xr = _dot(ucat, sre_ref[0])
    xi = _dot(ucat, sim_ref[0])
    row = lax.broadcasted_iota(jnp.int32, xr.shape, 0)
    for k in range(n_steps):
        s = 1 << k
        if s >= nc:
            break
        dr = dre_ref[0, k:k + 1, :]
        di = dim_ref[0, k:k + 1, :]
        sr = jnp.where(row >= s, pltpu.roll(xr, s, 0), 0.0)
        si = jnp.where(row >= s, pltpu.roll(xi, s, 0), 0.0)
        xr, xi = xr + (dr * sr - di * si), xi + (dr * si + di * sr)
    pr = jnp.where(row >= 1, pltpu.roll(xr, 1, 0), 0.0).astype(BF16)
    pi = jnp.where(row >= 1, pltpu.roll(xi, 1, 0), 0.0).astype(BF16)
    y = _dot(ucat, toe_ref[0]) + _dot(pr, cre_ref[0]) + _dot(pi, cim_ref[0])
    for s in range(l):
        y_ref[pl.ds(s, nc, stride=l), :] = y[:, s * LANES:(s + 1) * LANES]


def _s5_scan(z, tabs, *, bsz, seq):
    t = z.shape[0]
    nt = SSM_WIDTH // LANES
    wspec = lambda a: pl.BlockSpec((1,) + a.shape[1:], lambda j, b: (j, 0, 0))
    names = ('toe', 's_re', 's_im', 'c_re', 'c_im', 'd_re', 'd_im')
    return pl.pallas_call(
        _s5_kernel,
        grid=(nt, bsz),
        in_specs=[pl.BlockSpec((seq, LANES), lambda j, b: (b, Z_U // LANES + j))] + [wspec(tabs[k]) for k in names],
        out_specs=pl.BlockSpec((seq, LANES), lambda j, b: (b, j)),
        out_shape=jax.ShapeDtypeStruct((t, SSM_WIDTH), F32),
        name="s5_scan",
    )(z, *[tabs[k] for k in names])


def _kv_prep_kernel(ks_ref, vs_ref, kw_ref, vw_ref, g_ref, kaug_ref, vsel_ref, kwin_ref, vwin_ref, *, seq):
    tp = ks_ref.shape[0]
    pos = (pl.program_id(0) * tp) % seq + lax.broadcasted_iota(jnp.int32, (tp, HEAD_DIM), 0)
    blk = jnp.right_shift(pos, SEL_SHIFT)
    onehot = jnp.where(lax.broadcasted_iota(jnp.int32, (tp, HEAD_DIM), 1) == blk, 1.0, 0.0).astype(BF16)
    zeros = jnp.zeros((tp, HEAD_DIM), BF16)
    for hh in range(N_KV_HEADS):
        sl = slice(hh * HEAD_DIM, (hh + 1) * HEAD_DIM)
        kn = _rms(ks_ref[:, sl], g_ref[0:1, :]).astype(BF16)
        kaug_ref[hh] = jnp.concatenate([kn, onehot], axis=1)
        kn = _rms(kw_ref[:, sl], g_ref[1:2, :]).astype(BF16)
        kwin_ref[hh] = jnp.concatenate([kn, zeros], axis=1)
    for v_ref, vt_ref in ((vs_ref, vsel_ref), (vw_ref, vwin_ref)):
        tk = vt_ref.shape[3]
        vt = v_ref[...].T.astype(BF16)
        for hh in range(N_KV_HEADS):
            for j in range(tp // tk):
                vt_ref[hh, j] = vt[hh * HEAD_DIM:(hh + 1) * HEAD_DIM, j * tk:(j + 1) * tk]


def _kv_prep(z, k_norm_g, *, seq, tp, tk_sel, tk_win):
    t = z.shape[0]
    col = lambda off: pl.BlockSpec((tp, KV_WIDTH), lambda i, o=off // KV_WIDTH: (i, o))
    aug = 2 * HEAD_DIM
    kspec = pl.BlockSpec((N_KV_HEADS, tp, aug), lambda i: (0, i, 0))
    vspec = lambda tk: pl.BlockSpec((N_KV_HEADS, tp // tk, HEAD_DIM, tk), lambda i: (0, i, 0, 0))
    vshape = lambda tk: jax.ShapeDtypeStruct((N_KV_HEADS, t // tk, HEAD_DIM, tk), BF16)
    return pl.pallas_call(
        functools.partial(_kv_prep_kernel, seq=seq),
        grid=(t // tp,),
        in_specs=[col(Z_KS), col(Z_VS), col(Z_KW), col(Z_VW), pl.BlockSpec((2, HEAD_DIM), lambda i: (0, 0))],
        out_specs=[kspec, vspec(tk_sel), kspec, vspec(tk_win)],
        out_shape=[jax.ShapeDtypeStruct((N_KV_HEADS, t, aug), BF16), vshape(tk_sel),
                   jax.ShapeDtypeStruct((N_KV_HEADS, t, aug), BF16), vshape(tk_win)],
        name="kv_prep",
    )(z, z, z, z, k_norm_g[1:3])


def _compress_kernel(x_ref, pe_ref, w1_ref, w2_ref, g_ref, o_ref):
    x = (x_ref[0] + pe_ref[0]).astype(BF16)
    hid = _gelu_tanh(_dot(x, w1_ref[0])).astype(BF16)
    y = _dot(hid, w2_ref[0])
    o_ref[0] = jnp.where(pl.program_id(0) == 0, _rms(y, g_ref[...]), y)


def _compress(z, cmp_pe, w1, w2, k_gain, *, bsz, seq):
    nc = seq // CMP_BLOCK
    half = nc // 2

    def blocks(off):
        t = z[:, off:off + KV_WIDTH].reshape(bsz, half, 2, CMP_BLOCK, N_KV_HEADS, HEAD_DIM)
        return t.transpose(0, 4, 2, 1, 3, 5).reshape(bsz * N_KV_HEADS * nc, CMP_BLOCK * HEAD_DIM)

    x = jnp.stack([blocks(Z_KC), blocks(Z_VC)])
    m, kdim = x.shape[1:]
    out = pl.pallas_call(
        _compress_kernel,
        grid=(2,),
        in_specs=[pl.BlockSpec((1, m, kdim), lambda w: (w, 0, 0)),
                  pl.BlockSpec((1, 1, kdim), lambda w: (w, 0, 0)),
                  pl.BlockSpec((1, kdim, CMP_HIDDEN), lambda w: (w, 0, 0)),
                  pl.BlockSpec((1, CMP_HIDDEN, HEAD_DIM), lambda w: (w, 0, 0)),
                  pl.BlockSpec((1, HEAD_DIM), lambda w: (0, 0))],
        out_specs=pl.BlockSpec((1, m, HEAD_DIM), lambda w: (w, 0, 0)),
        out_shape=jax.ShapeDtypeStruct((2, m, HEAD_DIM), F32),
        name="compress",
    )(x, cmp_pe.reshape(2, 1, kdim), w1, w2, k_gain.reshape(1, HEAD_DIM))
    return out.reshape(2, bsz * N_KV_HEADS, nc, HEAD_DIM)


def _head_gate(gl, kvh, g, branch):
    c0 = g * NSA_BRANCHES + branch
    c1 = (GQA + g) * NSA_BRANCHES + branch
    if isinstance(kvh, int):
        c = c1 if kvh else c0
        return _sigmoid(gl[:, c:c + 1])
    return _sigmoid(jnp.where(kvh == 0, gl[:, c0:c0 + 1], gl[:, c1:c1 + 1]))


def _cmp_select_kernel(q_ref, gl_ref, kc_ref, vc_ref, qg_ref, o_ref, qaug_ref, imp_ref):
    tq = q_ref.shape[0]
    nc = kc_ref.shape[1]
    nb = nc // 2
    pair_w = 2 * HEAD_DIM
    qw = GQA * HEAD_DIM
    q0 = pl.program_id(1) * tq
    q = q_ref[...]
    x2 = q * q
    hi = x2.astype(BF16)
    lo = (x2 - hi.astype(F32)).astype(BF16)
    wq = q.shape[1]
    seg = jnp.where(jnp.right_shift(lax.broadcasted_iota(jnp.int32, (wq, wq), 0), HEAD_SHIFT)
                    == jnp.right_shift(lax.broadcasted_iota(jnp.int32, (wq, wq), 1), HEAD_SHIFT), 1.0, 0.0).astype(BF16)
    ss = _dot(hi, seg) + _dot(lo, seg)
    qf = q * lax.rsqrt(ss * (1.0 / HEAD_DIM) + RMS_EPS) * qg_ref[...] * (HEAD_DIM ** -0.5)
    qn = qf.astype(BF16)
    q2 = qf * LOG2E

    row = lax.broadcasted_iota(jnp.int32, (nc, tq), 0)
    t = q0 + lax.broadcasted_iota(jnp.int32, (nc, tq), 1)
    blk = jnp.where(row < nb, 2 * row, 2 * (row - nb) + 1)
    valid = (blk + 1) * CMP_BLOCK - 1 <= t
    lane = lax.broadcasted_iota(jnp.int32, (tq, pair_w), 1)
    j = lax.broadcasted_iota(jnp.int32, (nb, tq), 0)
    cur = jnp.right_shift(q0 + lax.broadcasted_iota(jnp.int32, (nb, tq), 1), SEL_SHIFT)
    forced = (j == 0) | (j == cur) | (j == cur - 1)
    visible = j <= cur
    gl = gl_ref[...]
    out = []
    imps = []
    for kvh in range(N_KV_HEADS):
        kc = kc_ref[kvh].astype(BF16)
        vc = vc_ref[kvh].astype(BF16)
        zk = jnp.zeros_like(kc)
        k2 = jnp.concatenate([jnp.concatenate([kc, zk], axis=1), jnp.concatenate([zk, kc], axis=1)], axis=0)
        psum = jnp.zeros((nc, tq), F32)
        for pr in range(GQA // 2):
            c0 = kvh * qw + pr * pair_w
            st = _dot_nt(k2, qn[:, c0:c0 + pair_w])
            ot = []
            for hh in range(2):
                sm = jnp.where(valid, st[hh * nc:(hh + 1) * nc], MASKED)
                m = jnp.max(sm, axis=0, keepdims=True)
                m = jnp.where(m > 0.5 * MASKED, m, 0.0)
                e = jnp.where(valid, jnp.exp(sm - m), 0.0)
                p = e * (1.0 / jnp.maximum(jnp.sum(e, axis=0, keepdims=True), 1e-30))
                psum = psum + p
                ot.append(lax.dot_general(vc, p.astype(BF16), (((0,), (0,)), ((), ())), preferred_element_type=F32))
            o_pair = jnp.concatenate(ot, axis=0).T
            gate = jnp.where(lane < HEAD_DIM, _head_gate(gl, kvh, 2 * pr, 0), _head_gate(gl, kvh, 2 * pr + 1, 0))
            out.append(o_pair * gate)
        imp = psum[:nb] + psum[nb:]
        imp = jnp.where(forced, FORCE_SCORE, jnp.where(visible, imp, -jnp.inf))
        imp_ref[kvh] = imp
        imps.append(imp)
    o_ref[...] = jnp.concatenate(out, axis=1)

    n_vis = jnp.minimum(nb, (q0 + tq - 1) // SEL_BLOCK + 1)

    def count(i, cnts):
        res = []
        for kvh in range(N_KV_HEADS):
            vi = imp_ref[kvh, pl.ds(i, 1), :]
            beats = (vi > imps[kvh]) | ((vi == imps[kvh]) & (i < j))
            res.append(cnts[kvh] + jnp.where(beats, 1.0, 0.0))
        return tuple(res)

    cnts = lax.fori_loop(0, n_vis, count, tuple(jnp.zeros((nb, tq), F32) for _ in range(N_KV_HEADS)))
    for kvh in range(N_KV_HEADS):
        selneg_t = jnp.where(visible & (cnts[kvh] < float(N_SELECT)), 0.0, MASKED)
        if nb < HEAD_DIM:
            selneg_t = jnp.concatenate([selneg_t, jnp.zeros((HEAD_DIM - nb, tq), F32)], axis=0)
        selneg = jnp.concatenate([selneg_t, selneg_t], axis=0).T
        for pr in range(GQA // 2):
            c0 = kvh * qw + pr * pair_w
            qp = q2[:, c0:c0 + pair_w]
            qaug_ref[kvh, 2 * pr] = jnp.where(lane < HEAD_DIM, qp, selneg).astype(BF16)
            qaug_ref[kvh, 2 * pr + 1] = jnp.where(lane < HEAD_DIM, pltpu.roll(qp, HEAD_DIM, 1), selneg).astype(BF16)


def _cmp_select(z, cmp_kv, q_norm_g, *, bsz, seq, tq):
    t = z.shape[0]
    nq = seq // tq
    nc = seq // CMP_BLOCK
    aug = 2 * HEAD_DIM
    qw = GQA * HEAD_DIM
    kc, vc = cmp_kv[0], cmp_kv[1]
    kv_spec = pl.BlockSpec((N_KV_HEADS, nc, HEAD_DIM), lambda b, i: (b, 0, 0))
    return pl.pallas_call(
        _cmp_select_kernel,
        grid=(bsz, nq),
        in_specs=[pl.BlockSpec((tq, ATTN_WIDTH), lambda b, i: (b * nq + i, Z_Q // ATTN_WIDTH)),
                  pl.BlockSpec((tq, GATE_PAD), lambda b, i: (b * nq + i, Z_GATE // GATE_PAD)),
                  kv_spec, kv_spec,
                  pl.BlockSpec((1, ATTN_WIDTH), lambda b, i: (0, 0))],
        out_specs=[pl.BlockSpec((tq, ATTN_WIDTH), lambda b, i: (b * nq + i, 0)),
                   pl.BlockSpec((N_KV_HEADS, GQA, tq, aug), lambda b, i: (b, 0, i, 0))],
        out_shape=[jax.ShapeDtypeStruct((t, ATTN_WIDTH), F32),
                   jax.ShapeDtypeStruct((bsz * N_KV_HEADS, GQA, seq, aug), BF16)],
        scratch_shapes=[pltpu.VMEM((N_KV_HEADS, nc // 2, tq), F32)],
        name="cmp_select",
    )(z, z, kc, vc, jnp.tile(q_norm_g, N_HEADS).reshape(1, ATTN_WIDTH))


def _flash_kernel(q_ref, k_ref, vt_ref, gl_ref, o_ref, p_ref, m_ref, l_ref, a_ref, acc_ref, *, window, branch):
    tq = q_ref.shape[2]
    tk = vt_ref.shape[3]
    cols = GQA * tq
    kvh = pl.program_id(1)
    q0 = pl.program_id(2) * tq
    qa = q_ref[0].reshape(cols, q_ref.shape[3])
    t = q0 + (lax.broadcasted_iota(jnp.int32, (tk, cols), 1) & (tq - 1))
    key_row = lax.broadcasted_iota(jnp.int32, (tk, cols), 0)
    lo = jnp.maximum(0, q0 - window + 1) // tk if window else 0
    hi = (q0 + tq + tk - 1) // tk
    mid1 = jnp.maximum(lo, (q0 + tq - 1 - window) // tk + 1) if window else lo
    mid2 = q0 // tk

    def scores(kb):
        return _dot_nt(k_ref[0, pl.ds(pl.multiple_of(kb * tk, tk), tk), :], qa)

    m_ref[...] = jnp.full(m_ref.shape, MASKED, F32)
    l_ref[...] = jnp.zeros(l_ref.shape, F32)
    a_ref[...] = jnp.ones(a_ref.shape, F32)
    acc_ref[...] = jnp.zeros(acc_ref.shape, F32)
    p_ref[...] = jnp.zeros(p_ref.shape, BF16)

    def tile(kb, s, masked):
        pv = _dot(vt_ref[0, jnp.maximum(kb - 1, lo)], p_ref[...])
        s_next = scores(jnp.minimum(kb + 1, hi - 1))
        if masked:
            key = kb * tk + key_row
            valid = key <= t
            if window:
                valid = valid & (key > t - window)
            s = jnp.where(valid, s, MASKED)
        m_old = m_ref[...]
        m_new = jnp.maximum(m_old, jnp.max(s, axis=0, keepdims=True))
        alpha = jnp.exp2(m_old - m_new)
        p = jnp.exp2(s - m_new)
        l_ref[...] = alpha * l_ref[...] + jnp.sum(p, axis=0, keepdims=True)
        acc_ref[...] = a_ref[...] * acc_ref[...] + pv
        a_ref[...] = alpha
        p_ref[...] = p.astype(BF16)
        m_ref[...] = m_new
        return s_next

    def run(first, last, s, masked):
        return lax.fori_loop(first, last, lambda kb, s: tile(kb, s, masked), s)

    s = run(lo, mid1, scores(lo), True)
    s = run(mid1, mid2, s, False)
    run(mid2, hi, s, True)
    acc = a_ref[...] * acc_ref[...] + _dot(vt_ref[0, hi - 1], p_ref[...])
    o_t = acc / l_ref[...]
    gl = gl_ref[...]
    out = []
    for g in range(0, GQA, 2):
        pair = jnp.concatenate([o_t[:, g * tq:(g + 1) * tq], o_t[:, (g + 1) * tq:(g + 2) * tq]], axis=0).T
        out.append(pair[:, :HEAD_DIM] * _head_gate(gl, kvh, g, branch))
        out.append(pair[:, HEAD_DIM:] * _head_gate(gl, kvh, g + 1, branch))
    o_ref[...] = jnp.concatenate(out, axis=1)


def _flash(qaug, k, vt, z, *, bsz, seq, tq, window, branch, name):
    t = z.shape[0]
    nq = seq // tq
    aug = qaug.shape[3]
    tk = vt.shape[3]
    qw = GQA * HEAD_DIM
    cols = GQA * tq
    return pl.pallas_call(
        functools.partial(_flash_kernel, window=window, branch=branch),
        grid=(bsz, N_KV_HEADS, nq),
        in_specs=[pl.BlockSpec((1, GQA, tq, aug), lambda b, h, i: (b * N_KV_HEADS + h, 0, i, 0)),
                  pl.BlockSpec((1, seq, aug), lambda b, h, i: (h, b, 0)),
                  pl.BlockSpec((1, seq // tk, HEAD_DIM, tk), lambda b, h, i: (h, b, 0, 0)),
                  pl.BlockSpec((tq, GATE_PAD), lambda b, h, i: (b * nq + i, Z_GATE // GATE_PAD))],
        out_specs=pl.BlockSpec((tq, qw), lambda b, h, i: (b * nq + i, h)),
        out_shape=jax.ShapeDtypeStruct((t, ATTN_WIDTH), F32),
        scratch_shapes=[pltpu.VMEM((tk, cols), BF16),
                        pltpu.VMEM((1, cols), F32), pltpu.VMEM((1, cols), F32), pltpu.VMEM((1, cols), F32),
                        pltpu.VMEM((HEAD_DIM, cols), F32)],
        name=name,
    )(qaug, k, vt, z)


def _merge_kernel(x_ref, g0_ref, g1_ref, g2_ref, cb_ref, cc_ref, cx_ref, pc_ref, px_ref, ys_ref,
                  oc_ref, os_ref, ow_ref, cw_ref, wc_ref, wglu_ref, wo_ref, wout_ref, out_ref, *, seq):
    tm = x_ref.shape[0]
    yg = _dot(_gelu_tanh(ys_ref[...]).astype(BF16), wglu_ref[...])
    y_ssm = yg[:, :D_MODEL] * _sigmoid(yg[:, D_MODEL:])
    zc = cc_ref[...] * cx_ref[...]
    keep = jnp.where((pl.program_id(0) * tm) % seq != 0, 1.0, 0.0)
    prev = pc_ref[...] * px_ref[...] * keep
    row = lax.broadcasted_iota(jnp.int32, zc.shape, 0)
    z1 = jnp.where(row >= 1, pltpu.roll(zc, 1, 0), prev[7:8, :])
    z2 = jnp.where(row >= 2, pltpu.roll(zc, 2, 0), jnp.where(row == 1, prev[7:8, :], prev[6:7, :]))
    conv = cw_ref[0:1, :] * z2 + cw_ref[1:2, :] * z1 + cw_ref[2:3, :] * zc
    y_conv = _dot((cb_ref[...] * conv).astype(BF16), wc_ref[...])
    y_attn = _dot((oc_ref[...] + os_ref[...] + ow_ref[...]).astype(BF16), wo_ref[...])
    mixed = _sigmoid(g0_ref[...]) * y_ssm + _sigmoid(g1_ref[...]) * y_conv + _sigmoid(g2_ref[...]) * y_attn
    out_ref[...] = x_ref[...] + _dot(mixed.astype(BF16), wout_ref[...])


def _merge(x2, z, ys, o_cmp, o_sel, o_win, conv_w, wc, wglu, wo, wout, *, seq, tm):
    t, d = x2.shape
    rb = tm // 8
    zc = lambda width, off: pl.BlockSpec((tm, width), lambda i, o=off // width: (i, o))
    zprev = lambda off: pl.BlockSpec((8, CONV_CH), lambda i, o=off // CONV_CH: (jnp.maximum(i * rb - 1, 0), o))
    row = lambda width: pl.BlockSpec((tm, width), lambda i: (i, 0))
    full = lambda a: pl.BlockSpec(a.shape, lambda i: (0, 0))
    return pl.pallas_call(
        functools.partial(_merge_kernel, seq=seq),
        grid=(t // tm,),
        in_specs=[row(d), zc(d, Z_MIX), zc(d, Z_MIX + d), zc(d, Z_MIX + 2 * d),
                  zc(CONV_CH, Z_CB), zc(CONV_CH, Z_CC), zc(CONV_CH, Z_CX), zprev(Z_CC), zprev(Z_CX),
                  row(SSM_WIDTH), row(ATTN_WIDTH), row(ATTN_WIDTH), row(ATTN_WIDTH),
                  full(conv_w), full(wc), full(wglu), full(wo), full(wout)],
        out_specs=row(d),
        out_shape=jax.ShapeDtypeStruct((t, d), F32),
        name="merge",
    )(x2, z, z, z, z, z, z, z, z, ys, o_cmp, o_sel, o_win, conv_w, wc, wglu, wo, wout)


def _ffn_kernel(x_ref, g_ref, wg_ref, wu_ref, wd_ref, o_ref, h_ref, acc_ref):
    f = pl.program_id(1)

    @pl.when(f == 0)
    def _():
        h_ref[...] = _rms(x_ref[...], g_ref[...]).astype(BF16)
        acc_ref[...] = jnp.zeros(acc_ref.shape, F32)

    h = h_ref[...]
    gate = _dot(h, wg_ref[...])
    up = _dot(h, wu_ref[...])
    act = (gate * _sigmoid(gate) * up).astype(BF16)
    acc_ref[...] += _dot(act, wd_ref[...])

    @pl.when(f == pl.num_programs(1) - 1)
    def _():
        o_ref[...] = x_ref[...] + acc_ref[...]


def _ffn(x2, g, w_gate_up, w_down, *, tm, tf):
    t, d = x2.shape
    nf = D_FF // tf
    return pl.pallas_call(
        _ffn_kernel,
        grid=(t // tm, nf),
        in_specs=[pl.BlockSpec((tm, d), lambda i, f: (i, 0)),
                  pl.BlockSpec((1, d), lambda i, f: (0, 0)),
                  pl.BlockSpec((d, tf), lambda i, f: (0, f)),
                  pl.BlockSpec((d, tf), lambda i, f: (0, nf + f)),
                  pl.BlockSpec((tf, d), lambda i, f: (f, 0))],
        out_specs=pl.BlockSpec((tm, d), lambda i, f: (i, 0)),
        out_shape=jax.ShapeDtypeStruct((t, d), F32),
        scratch_shapes=[pltpu.VMEM((tm, d), BF16), pltpu.VMEM((tm, d), F32)],
        compiler_params=pltpu.CompilerParams(dimension_semantics=("parallel", "arbitrary")),
        name="ffn",
    )(x2, g.reshape(1, d), w_gate_up, w_gate_up, w_down)


def _pick(n, pref):
    while n % pref:
        pref //= 2
    return pref


def _layer(x2, p, *, bsz, seq):
    t = x2.shape[0]
    z = _inproj(x2, p['mix_norm_g'], p['w_in'], tm=_pick(t, 512), tn=Z_WIDTH // 3)
    ys = _s5_scan(z, p['s5'], bsz=bsz, seq=seq)
    kaug, vsel, kwin, vwin = _kv_prep(z, p['k_norm_g'], seq=seq, tp=_pick(seq, 512), tk_sel=512, tk_win=256)
    cmp_kv = _compress(z, p['cmp_pe'], p['cmp_w1'], p['cmp_w2'], p['k_norm_g'][0], bsz=bsz, seq=seq)
    o_cmp, qaug = _cmp_select(z, cmp_kv, p['q_norm_g'], bsz=bsz, seq=seq, tq=256)
    o_sel = _flash(qaug, kaug, vsel, z, bsz=bsz, seq=seq, tq=256, window=0, branch=1, name="sel_attn")
    o_win = _flash(qaug, kwin, vwin, z, bsz=bsz, seq=seq, tq=256, window=WINDOW, branch=2, name="win_attn")
    x2 = _merge(x2, z, ys, o_cmp, o_sel, o_win, p['conv_w'], p['conv_w_out'], p['ssm_w_glu'], p['nsa_w_o'],
                p['w_out'], seq=seq, tm=_pick(seq, 256))
    return _ffn(x2, p['ffn_norm_g'], p['ffn_w_gate_up'], p['ffn_w_down'], tm=_pick(t, 1024), tf=256)


def kernel(x, mix_norm_g, w_in, ssm_lam_re, ssm_lam_im, ssm_b_re, ssm_b_im, ssm_c_re, ssm_c_im, ssm_d, ssm_log_dt, ssm_w_glu, conv_w, conv_w_out, q_norm_g, k_norm_g, cmp_pe, cmp_w1, cmp_w2, nsa_w_o, w_out, ffn_norm_g, ffn_w_gate_up, ffn_w_down):
    bsz, seq, d = x.shape
    x2 = x.reshape(bsz * seq, d)
    for i in range(w_in.shape[0]):
        p = dict(
            mix_norm_g=mix_norm_g[i], w_in=_permute_w_in(w_in[i]),
            s5=_s5_tables(ssm_lam_re[i], ssm_lam_im[i], ssm_b_re[i], ssm_b_im[i], ssm_c_re[i], ssm_c_im[i],
                          ssm_d[i], ssm_log_dt[i], seq // S5_CHUNK),
            ssm_w_glu=ssm_w_glu[i].astype(BF16), conv_w=conv_w[i], conv_w_out=conv_w_out[i].astype(BF16),
            q_norm_g=q_norm_g[i], k_norm_g=k_norm_g[i], cmp_pe=cmp_pe[i],
            cmp_w1=cmp_w1[i].astype(BF16), cmp_w2=cmp_w2[i].astype(BF16), nsa_w_o=nsa_w_o[i].astype(BF16),
            w_out=w_out[i].astype(BF16), ffn_norm_g=ffn_norm_g[i],
            ffn_w_gate_up=ffn_w_gate_up[i].astype(BF16), ffn_w_down=ffn_w_down[i].astype(BF16))
        x2 = _layer(x2, p, bsz=bsz, seq=seq)
    return x2.reshape(bsz, seq, d)
```

```python
import functools

import jax
import jax.numpy as jnp
import numpy as np
from jax import lax
from jax.experimental import pallas as pl
from jax.experimental.pallas import tpu as pltpu

F32 = jnp.float32
BF16 = jnp.bfloat16

D_MODEL = 1024
SSM_WIDTH = 512
SSM_GROUP = 16
SSM_GROUPS = SSM_WIDTH // SSM_GROUP
SSM_STATE = 64
CONV_CH = 512
CONV_K = 3
HEAD_DIM = 64
HEAD_SHIFT = 6
N_HEADS = 8
N_KV_HEADS = 2
GQA = N_HEADS // N_KV_HEADS
ATTN_WIDTH = N_HEADS * HEAD_DIM
KV_WIDTH = N_KV_HEADS * HEAD_DIM
CMP_BLOCK = 32
SEL_BLOCK = 64
SEL_SHIFT = 6
N_SELECT = 16
WINDOW = 512
CMP_HIDDEN = 256
FORCE_SCORE = 1e4
NSA_BRANCHES = 3
MIX_BRANCHES = 3
D_FF = 2816
RMS_EPS = 1e-6

LANES = 128
MASKED = -1e30
LOG2E = 1.4426950408889634
S5_CHUNK = 16
S5_TILE_GROUPS = LANES // SSM_GROUP
GATE_PAD = LANES

Z_MIX = 0
Z_U = Z_MIX + MIX_BRANCHES * D_MODEL
Z_CB = Z_U + SSM_WIDTH
Z_CC = Z_CB + CONV_CH
Z_CX = Z_CC + CONV_CH
Z_Q = Z_CX + CONV_CH
Z_KC = Z_Q + ATTN_WIDTH
Z_VC = Z_KC + KV_WIDTH
Z_KS = Z_VC + KV_WIDTH
Z_VS = Z_KS + KV_WIDTH
Z_KW = Z_VS + KV_WIDTH
Z_VW = Z_KW + KV_WIDTH
Z_GATE = Z_VW + KV_WIDTH
Z_WIDTH = Z_GATE + GATE_PAD


def _gelu_tanh(x):
    return 0.5 * x * (1.0 + jnp.tanh(np.sqrt(2.0 / np.pi).astype(np.float32) * (x + 0.044715 * (x * x * x))))


def _sigmoid(x):
    return 1.0 / (1.0 + jnp.exp(-x))


def _rms(x, g):
    return x * lax.rsqrt(jnp.mean(x * x, axis=-1, keepdims=True) + RMS_EPS) * g


def _dot(a, b):
    return jnp.dot(a, b, preferred_element_type=F32)


def _dot_nt(a, b):
    return lax.dot_general(a, b, (((1,), (1,)), ((), ())), preferred_element_type=F32)


def _inproj_kernel(x_ref, g_ref, w_ref, o_ref):
    h = _rms(x_ref[...], g_ref[...]).astype(BF16)
    o_ref[...] = _dot(h, w_ref[...])


def _inproj(x2, g, w, *, tm, tn):
    t, d = x2.shape
    n = w.shape[1]
    return pl.pallas_call(
        _inproj_kernel,
        grid=(n // tn, t // tm),
        in_specs=[pl.BlockSpec((tm, d), lambda j, i: (i, 0)),
                  pl.BlockSpec((1, d), lambda j, i: (0, 0)),
                  pl.BlockSpec((d, tn), lambda j, i: (0, j))],
        out_specs=pl.BlockSpec((tm, tn), lambda j, i: (i, j)),
        out_shape=jax.ShapeDtypeStruct((t, n), F32),
        name="inproj",
    )(x2, g.reshape(1, d), w)


def _permute_w_in(w):
    n_plain = SSM_WIDTH + 3 * CONV_CH + ATTN_WIDTH + 6 * KV_WIDTH
    n_gate = N_HEADS * NSA_BRANCHES
    gate = jnp.pad(w[:, n_plain:n_plain + n_gate], ((0, 0), (0, GATE_PAD - n_gate)))
    return jnp.concatenate([w[:, n_plain + n_gate:], w[:, :n_plain], gate], axis=1).astype(BF16)


def _s5_tables(lam_re, lam_im, b_re, b_im, c_re, c_im, d_skip, log_dt, n_chunks):
    hp = lax.Precision.HIGHEST
    g, p = lam_re.shape
    h, l, gt = SSM_GROUP, S5_CHUNK, S5_TILE_GROUPS
    nt = g // gt
    dt = jnp.exp(log_dt)[:, None]
    ar, ai = lam_re * dt, lam_im * dt

    def powers(k):
        mag = jnp.exp(ar[None] * k[:, None, None])
        ang = ai[None] * k[:, None, None]
        return mag * jnp.cos(ang), mag * jnp.sin(ang)

    pr, pi = powers(jnp.arange(l + 1, dtype=F32))
    nr, ni = pr[1] - 1.0, pi[1]
    den = lam_re * lam_re + lam_im * lam_im
    fr, fi = (nr * lam_re + ni * lam_im) / den, (ni * lam_re - nr * lam_im) / den
    bbr = fr[..., None] * b_re - fi[..., None] * b_im
    bbi = fr[..., None] * b_im + fi[..., None] * b_re
    wr = pr[:l, :, :, None] * bbr[None] - pi[:l, :, :, None] * bbi[None]
    wi = pr[:l, :, :, None] * bbi[None] + pi[:l, :, :, None] * bbr[None]
    kern = (jnp.einsum('gop,tgpi->tgio', c_re, wr, precision=hp)
            - jnp.einsum('gop,tgpi->tgio', c_im, wi, precision=hp))
    r = jnp.arange(l)

    def group_diag(x, row_w, col_w):
        rows, w = x.shape[-2:]
        tiled = jnp.tile(x, (1,) * (x.ndim - 1) + (gt,))
        own = (jnp.arange(rows)[:, None] // row_w) % gt == jnp.arange(gt * w)[None, :] // col_w
        return jnp.where(own, tiled, 0.0)

    kblk = group_diag(kern.reshape(l, g * h, h), h, h)
    krev = kblk.reshape(l, nt, LANES, LANES)[::-1].transpose(1, 0, 2, 3).reshape(nt, l * LANES, LANES)
    qr, qi = pr[l - 1 - r], pi[l - 1 - r]
    st_re = qr[..., None] * bbr[None] - qi[..., None] * bbi[None]
    st_im = qr[..., None] * bbi[None] + qi[..., None] * bbr[None]
    rows_in = lambda x: group_diag(x.transpose(0, 1, 3, 2).reshape(l, g * h, p), h, p)
    s_tab = jnp.concatenate([rows_in(st_re), rows_in(st_im)], axis=-1)
    s_tab = s_tab.reshape(l, nt, LANES, 2 * gt * p).transpose(1, 0, 2, 3).reshape(nt, l * LANES, 2 * gt * p)
    er, ei = pr[1:l + 1][:, :, None, :], pi[1:l + 1][:, :, None, :]
    rows_out = lambda x: group_diag(x.transpose(0, 1, 3, 2).reshape(l, g * p, h), p, h).reshape(l, nt, gt * p, LANES)
    c_tab = jnp.concatenate([rows_out(c_re[None] * er - c_im[None] * ei),
                             rows_out(-(c_re[None] * ei + c_im[None] * er))], axis=2)
    n_steps = max(1, int(np.ceil(np.log2(n_chunks))))
    dr, di = powers(l * (2.0 ** jnp.arange(n_steps, dtype=F32)))
    lanes = lambda m: m.reshape(n_steps, nt, gt * p).transpose(1, 0, 2)
    return dict(krev=krev.astype(BF16), s_tab=s_tab.astype(BF16), c_tab=c_tab.astype(BF16),
                d_re=lanes(dr), d_im=lanes(di), d_skip=d_skip.reshape(1, g * h))


def _s5_kernel(u_ref, krev_ref, s_ref, c_ref, dre_ref, dim_ref, dsk_ref, y_ref):
    seq = u_ref.shape[0]
    l = S5_CHUNK
    nc = seq // l
    n_steps = dre_ref.shape[1]
    ns = dre_ref.shape[2]
    us = [u_ref[pl.ds(r, nc, stride=l), :] for r in range(l)]
    ucat = jnp.concatenate(us, axis=1).astype(BF16)
    e = _dot(ucat, s_ref[0])
    xr, xi = e[:, :ns], e[:, ns:]
    row = lax.broadcasted_iota(jnp.int32, xr.shape, 0)
    for k in range(n_steps):
        s = 1 << k
        if s >= nc:
            break
        dr = dre_ref[0, k:k + 1, :]
        di = dim_ref[0, k:k + 1, :]
        sr = jnp.where(row >= s, pltpu.roll(xr, s, 0), 0.0)
        si = jnp.where(row >= s, pltpu.roll(xi, s, 0), 0.0)
        xr, xi = xr + (dr * sr - di * si), xi + (dr * si + di * sr)
    xp = jnp.concatenate([jnp.where(row >= 1, pltpu.roll(xr, 1, 0), 0.0),
                          jnp.where(row >= 1, pltpu.roll(xi, 1, 0), 0.0)], axis=1).astype(BF16)
    dsk = dsk_ref[...]
    for s in range(l):
        y = (_dot(ucat[:, :(s + 1) * LANES], krev_ref[0, (l - 1 - s) * LANES:, :])
             + _dot(xp, c_ref[s, 0]) + dsk * us[s])
        y_ref[pl.ds(s, nc, stride=l), :] = y


def _s5_scan(z, tabs, *, bsz, seq):
    t = z.shape[0]
    nt = SSM_WIDTH // LANES
    l = S5_CHUNK
    tile3 = lambda a: pl.BlockSpec((1,) + a.shape[1:], lambda j, b: (j, 0, 0))
    c_tab = tabs['c_tab']
    return pl.pallas_call(
        _s5_kernel,
        grid=(nt, bsz),
        in_specs=[pl.BlockSpec((seq, LANES), lambda j, b: (b, Z_U // LANES + j)),
                  tile3(tabs['krev']), tile3(tabs['s_tab']),
                  pl.BlockSpec((l, 1) + c_tab.shape[2:], lambda j, b: (0, j, 0, 0)),
                  tile3(tabs['d_re']), tile3(tabs['d_im']),
                  pl.BlockSpec((1, LANES), lambda j, b: (0, j))],
        out_specs=pl.BlockSpec((seq, LANES), lambda j, b: (b, j)),
        out_shape=jax.ShapeDtypeStruct((t, SSM_WIDTH), F32),
        name="s5_scan",
    )(z, tabs['krev'], tabs['s_tab'], c_tab, tabs['d_re'], tabs['d_im'], tabs['d_skip'])


def _kv_prep_kernel(ks_ref, vs_ref, kw_ref, vw_ref, g_ref, kaug_ref, vsel_ref, kwin_ref, vwin_ref, *, seq):
    tp = ks_ref.shape[0]
    pos = (pl.program_id(0) * tp) % seq + lax.broadcasted_iota(jnp.int32, (tp, HEAD_DIM), 0)
    blk = jnp.right_shift(pos, SEL_SHIFT)
    onehot = jnp.where(lax.broadcasted_iota(jnp.int32, (tp, HEAD_DIM), 1) == blk, 1.0, 0.0).astype(BF16)
    zeros = jnp.zeros((tp, HEAD_DIM), BF16)
    for hh in range(N_KV_HEADS):
        sl = slice(hh * HEAD_DIM, (hh + 1) * HEAD_DIM)
        kn = _rms(ks_ref[:, sl], g_ref[0:1, :]).astype(BF16)
        kaug_ref[hh] = jnp.concatenate([kn, onehot], axis=1)
        kn = _rms(kw_ref[:, sl], g_ref[1:2, :]).astype(BF16)
        kwin_ref[hh] = jnp.concatenate([kn, zeros], axis=1)
    for v_ref, vt_ref in ((vs_ref, vsel_ref), (vw_ref, vwin_ref)):
        tk = vt_ref.shape[3]
        vt = v_ref[...].T.astype(BF16)
        for hh in range(N_KV_HEADS):
            for j in range(tp // tk):
                vt_ref[hh, j] = vt[hh * HEAD_DIM:(hh + 1) * HEAD_DIM, j * tk:(j + 1) * tk]


def _kv_prep(z, k_norm_g, *, seq, tp, tk_sel, tk_win):
    t = z.shape[0]
    col = lambda off: pl.BlockSpec((tp, KV_WIDTH), lambda i, o=off // KV_WIDTH: (i, o))
    aug = 2 * HEAD_DIM
    kspec = pl.BlockSpec((N_KV_HEADS, tp, aug), lambda i: (0, i, 0))
    vspec = lambda tk: pl.BlockSpec((N_KV_HEADS, tp // tk, HEAD_DIM, tk), lambda i: (0, i, 0, 0))
    vshape = lambda tk: jax.ShapeDtypeStruct((N_KV_HEADS, t // tk, HEAD_DIM, tk), BF16)
    return pl.pallas_call(
        functools.partial(_kv_prep_kernel, seq=seq),
        grid=(t // tp,),
        in_specs=[col(Z_KS), col(Z_VS), col(Z_KW), col(Z_VW), pl.BlockSpec((2, HEAD_DIM), lambda i: (0, 0))],
        out_specs=[kspec, vspec(tk_sel), kspec, vspec(tk_win)],
        out_shape=[jax.ShapeDtypeStruct((N_KV_HEADS, t, aug), BF16), vshape(tk_sel),
                   jax.ShapeDtypeStruct((N_KV_HEADS, t, aug), BF16), vshape(tk_win)],
        name="kv_prep",
    )(z, z, z, z, k_norm_g[1:3])


def _compress_kernel(x_ref, pe_ref, w1_ref, w2_ref, g_ref, o_ref):
    seq = x_ref.shape[0]
    nb = seq // (2 * CMP_BLOCK)
    lane = lax.broadcasted_iota(jnp.int32, (nb, 2 * HEAD_DIM), 1)
    rows = []
    for hh in range(N_KV_HEADS):
        for parity in range(2):
            cols = []
            for s in range(0, CMP_BLOCK, 2):
                a = x_ref[pl.ds(parity * CMP_BLOCK + s, nb, stride=2 * CMP_BLOCK), :]
                b = x_ref[pl.ds(parity * CMP_BLOCK + s + 1, nb, stride=2 * CMP_BLOCK), :]
                if hh == 0:
                    cols.append(jnp.where(lane < HEAD_DIM, a, pltpu.roll(b, HEAD_DIM, 1)))
                else:
                    cols.append(jnp.where(lane < HEAD_DIM, pltpu.roll(a, HEAD_DIM, 1), b))
            rows.append(jnp.concatenate(cols, axis=1))
    x = (jnp.concatenate(rows, axis=0) + pe_ref[0]).astype(BF16)
    hid = _gelu_tanh(_dot(x, w1_ref[0])).astype(BF16)
    y = _dot(hid, w2_ref[0])
    o_ref[0, 0] = jnp.where(pl.program_id(0) == 0, _rms(y, g_ref[...]), y)


def _compress(z, cmp_pe, w1, w2, k_gain, *, bsz, seq):
    nc = seq // CMP_BLOCK
    kdim = CMP_BLOCK * HEAD_DIM
    m = N_KV_HEADS * nc
    out = pl.pallas_call(
        _compress_kernel,
        grid=(2, bsz),
        in_specs=[pl.BlockSpec((seq, KV_WIDTH), lambda w, b: (b, Z_KC // KV_WIDTH + w)),
                  pl.BlockSpec((1, 1, kdim), lambda w, b: (w, 0, 0)),
                  pl.BlockSpec((1, kdim, CMP_HIDDEN), lambda w, b: (w, 0, 0)),
                  pl.BlockSpec((1, CMP_HIDDEN, HEAD_DIM), lambda w, b: (w, 0, 0)),
                  pl.BlockSpec((1, HEAD_DIM), lambda w, b: (0, 0))],
        out_specs=pl.BlockSpec((1, 1, m, HEAD_DIM), lambda w, b: (w, b, 0, 0)),
        out_shape=jax.ShapeDtypeStruct((2, bsz, m, HEAD_DIM), F32),
        name="compress",
    )(z, cmp_pe.reshape(2, 1, kdim), w1, w2, k_gain.reshape(1, HEAD_DIM))
    return out.reshape(2, bsz * N_KV_HEADS, nc, HEAD_DIM)


def _head_gate(gl, kvh, g, branch):
    c0 = g * NSA_BRANCHES + branch
    c1 = (GQA + g) * NSA_BRANCHES + branch
    if isinstance(kvh, int):
        c = c1 if kvh else c0
        return _sigmoid(gl[:, c:c + 1])
    return _sigmoid(jnp.where(kvh == 0, gl[:, c0:c0 + 1], gl[:, c1:c1 + 1]))


def _cmp_select_kernel(q_ref, gl_ref, kc_ref, vc_ref, qg_ref, o_ref, qaug_ref, imp_ref):
    tq = q_ref.shape[0]
    nc = kc_ref.shape[1]
    nb = nc // 2
    pair_w = 2 * HEAD_DIM
    qw = GQA * HEAD_DIM
    q0 = pl.program_id(1) * tq
    q = q_ref[...]
    x2 = q * q
    hi = x2.astype(BF16)
    lo = (x2 - hi.astype(F32)).astype(BF16)
    wq = q.shape[1]
    seg = jnp.where(jnp.right_shift(lax.broadcasted_iota(jnp.int32, (wq, wq), 0), HEAD_SHIFT)
                    == jnp.right_shift(lax.broadcasted_iota(jnp.int32, (wq, wq), 1), HEAD_SHIFT), 1.0, 0.0).astype(BF16)
    ss = _dot(hi, seg) + _dot(lo, seg)
    qf = q * lax.rsqrt(ss * (1.0 / HEAD_DIM) + RMS_EPS) * qg_ref[...] * (HEAD_DIM ** -0.5)
    qn = qf.astype(BF16)
    q2 = qf * LOG2E

    row = lax.broadcasted_iota(jnp.int32, (nc, tq), 0)
    t = q0 + lax.broadcasted_iota(jnp.int32, (nc, tq), 1)
    blk = jnp.where(row < nb, 2 * row, 2 * (row - nb) + 1)
    valid = (blk + 1) * CMP_BLOCK - 1 <= t
    lane = lax.broadcasted_iota(jnp.int32, (tq, pair_w), 1)
    j = lax.broadcasted_iota(jnp.int32, (nb, tq), 0)
    cur = jnp.right_shift(q0 + lax.broadcasted_iota(jnp.int32, (nb, tq), 1), SEL_SHIFT)
    forced = (j == 0) | (j == cur) | (j == cur - 1)
    visible = j <= cur
    gl = gl_ref[...]
    out = []
    imps = []
    for kvh in range(N_KV_HEADS):
        kc = kc_ref[kvh].astype(BF16)
        vc = vc_ref[kvh].astype(BF16)
        zk = jnp.zeros_like(kc)
        k2 = jnp.concatenate([jnp.concatenate([kc, zk], axis=1), jnp.concatenate([zk, kc], axis=1)], axis=0)
        psum = jnp.zeros((nc, tq), F32)
        for pr in range(GQA // 2):
            c0 = kvh * qw + pr * pair_w
            st = _dot_nt(k2, qn[:, c0:c0 + pair_w])
            ot = []
            for hh in range(2):
                sm = jnp.where(valid, st[hh * nc:(hh + 1) * nc], MASKED)
                m = jnp.max(sm, axis=0, keepdims=True)
                m = jnp.where(m > 0.5 * MASKED, m, 0.0)
                e = jnp.where(valid, jnp.exp(sm - m), 0.0)
                p = e * (1.0 / jnp.maximum(jnp.sum(e, axis=0, keepdims=True), 1e-30))
                psum = psum + p
                ot.append(lax.dot_general(vc, p.astype(BF16), (((0,), (0,)), ((), ())), preferred_element_type=F32))
            o_pair = jnp.concatenate(ot, axis=0).T
            gate = jnp.where(lane < HEAD_DIM, _head_gate(gl, kvh, 2 * pr, 0), _head_gate(gl, kvh, 2 * pr + 1, 0))
            out.append(o_pair * gate)
        imp = psum[:nb] + psum[nb:]
        imp = jnp.where(forced, FORCE_SCORE, jnp.where(visible, imp, -jnp.inf))
        imp_ref[kvh] = imp
        imps.append(imp)
    o_ref[...] = jnp.concatenate(out, axis=1)

    n_vis = jnp.minimum(nb, (q0 + tq - 1) // SEL_BLOCK + 1)

    def count(i, cnts):
        res = []
        for kvh in range(N_KV_HEADS):
            vi = imp_ref[kvh, pl.ds(i, 1), :]
            beats = (vi > imps[kvh]) | ((vi == imps[kvh]) & (i < j))
            res.append(cnts[kvh] + jnp.where(beats, 1.0, 0.0))
        return tuple(res)

    cnts = lax.fori_loop(0, n_vis, count, tuple(jnp.zeros((nb, tq), F32) for _ in range(N_KV_HEADS)))
    for kvh in range(N_KV_HEADS):
        selneg_t = jnp.where(visible & (cnts[kvh] < float(N_SELECT)), 0.0, MASKED)
        if nb < HEAD_DIM:
            selneg_t = jnp.concatenate([selneg_t, jnp.zeros((HEAD_DIM - nb, tq), F32)], axis=0)
        selneg = jnp.concatenate([selneg_t, selneg_t], axis=0).T
        for pr in range(GQA // 2):
            c0 = kvh * qw + pr * pair_w
            qp = q2[:, c0:c0 + pair_w]
            qaug_ref[kvh, 2 * pr] = jnp.where(lane < HEAD_DIM, qp, selneg).astype(BF16)
            qaug_ref[kvh, 2 * pr + 1] = jnp.where(lane < HEAD_DIM, pltpu.roll(qp, HEAD_DIM, 1), selneg).astype(BF16)


def _cmp_select(z, cmp_kv, q_norm_g, *, bsz, seq, tq):
    t = z.shape[0]
    nq = seq // tq
    nc = seq // CMP_BLOCK
    aug = 2 * HEAD_DIM
    qw = GQA * HEAD_DIM
    kc, vc = cmp_kv[0], cmp_kv[1]
    kv_spec = pl.BlockSpec((N_KV_HEADS, nc, HEAD_DIM), lambda b, i: (b, 0, 0))
    return pl.pallas_call(
        _cmp_select_kernel,
        grid=(bsz, nq),
        in_specs=[pl.BlockSpec((tq, ATTN_WIDTH), lambda b, i: (b * nq + i, Z_Q // ATTN_WIDTH)),
                  pl.BlockSpec((tq, GATE_PAD), lambda b, i: (b * nq + i, Z_GATE // GATE_PAD)),
                  kv_spec, kv_spec,
                  pl.BlockSpec((1, ATTN_WIDTH), lambda b, i: (0, 0))],
        out_specs=[pl.BlockSpec((tq, ATTN_WIDTH), lambda b, i: (b * nq + i, 0)),
                   pl.BlockSpec((N_KV_HEADS, GQA, tq, aug), lambda b, i: (b, 0, i, 0))],
        out_shape=[jax.ShapeDtypeStruct((t, ATTN_WIDTH), F32),
                   jax.ShapeDtypeStruct((bsz * N_KV_HEADS, GQA, seq, aug), BF16)],
        scratch_shapes=[pltpu.VMEM((N_KV_HEADS, nc // 2, tq), F32)],
        name="cmp_select",
    )(z, z, kc, vc, jnp.tile(q_norm_g, N_HEADS).reshape(1, ATTN_WIDTH))


def _flash_kernel(q_ref, k_ref, vt_ref, gl_ref, o_ref, p_ref, m_ref, l_ref, a_ref, acc_ref, *, window, branch):
    tq = q_ref.shape[2]
    tk = vt_ref.shape[3]
    cols = GQA * tq
    kvh = pl.program_id(1)
    q0 = pl.program_id(2) * tq
    qa = q_ref[0].reshape(cols, q_ref.shape[3])
    t = q0 + (lax.broadcasted_iota(jnp.int32, (tk, cols), 1) & (tq - 1))
    key_row = lax.broadcasted_iota(jnp.int32, (tk, cols), 0)
    lo = jnp.maximum(0, q0 - window + 1) // tk if window else 0
    hi = (q0 + tq + tk - 1) // tk
    mid1 = jnp.maximum(lo, (q0 + tq - 1 - window) // tk + 1) if window else lo
    mid2 = q0 // tk

    def scores(kb):
        return _dot_nt(k_ref[0, pl.ds(pl.multiple_of(kb * tk, tk), tk), :], qa)

    m_ref[...] = jnp.full(m_ref.shape, MASKED, F32)
    l_ref[...] = jnp.zeros(l_ref.shape, F32)
    a_ref[...] = jnp.ones(a_ref.shape, F32)
    acc_ref[...] = jnp.zeros(acc_ref.shape, F32)
    p_ref[...] = jnp.zeros(p_ref.shape, BF16)

    def tile(kb, s, masked):
        pv = _dot(vt_ref[0, jnp.maximum(kb - 1, lo)], p_ref[...])
        s_next = scores(jnp.minimum(kb + 1, hi - 1))
        if masked:
            key = kb * tk + key_row
            valid = key <= t
            if window:
                valid = valid & (key > t - window)
            s = jnp.where(valid, s, MASKED)
        m_old = m_ref[...]
        m_new = jnp.maximum(m_old, jnp.max(s, axis=0, keepdims=True))
        alpha = jnp.exp2(m_old - m_new)
        p = jnp.exp2(s - m_new)
        l_ref[...] = alpha * l_ref[...] + jnp.sum(p, axis=0, keepdims=True)
        acc_ref[...] = a_ref[...] * acc_ref[...] + pv
        a_ref[...] = alpha
        p_ref[...] = p.astype(BF16)
        m_ref[...] = m_new
        return s_next

    def run(first, last, s, masked):
        return lax.fori_loop(first, last, lambda kb, s: tile(kb, s, masked), s)

    s = run(lo, mid1, scores(lo), True)
    s = run(mid1, mid2, s, False)
    run(mid2, hi, s, True)
    acc = a_ref[...] * acc_ref[...] + _dot(vt_ref[0, hi - 1], p_ref[...])
    o_t = acc / l_ref[...]
    gl = gl_ref[...]
    out = []
    for g in range(0, GQA, 2):
        pair = jnp.concatenate([o_t[:, g * tq:(g + 1) * tq], o_t[:, (g + 1) * tq:(g + 2) * tq]], axis=0).T
        out.append(pair[:, :HEAD_DIM] * _head_gate(gl, kvh, g, branch))
        out.append(pair[:, HEAD_DIM:] * _head_gate(gl, kvh, g + 1, branch))
    o_ref[...] = jnp.concatenate(out, axis=1)


def _flash(qaug, k, vt, z, *, bsz, seq, tq, window, branch, name):
    t = z.shape[0]
    nq = seq // tq
    aug = qaug.shape[3]
    tk = vt.shape[3]
    qw = GQA * HEAD_DIM
    cols = GQA * tq
    return pl.pallas_call(
        functools.partial(_flash_kernel, window=window, branch=branch),
        grid=(bsz, N_KV_HEADS, nq),
        in_specs=[pl.BlockSpec((1, GQA, tq, aug), lambda b, h, i: (b * N_KV_HEADS + h, 0, i, 0)),
                  pl.BlockSpec((1, seq, aug), lambda b, h, i: (h, b, 0)),
                  pl.BlockSpec((1, seq // tk, HEAD_DIM, tk), lambda b, h, i: (h, b, 0, 0)),
                  pl.BlockSpec((tq, GATE_PAD), lambda b, h, i: (b * nq + i, Z_GATE // GATE_PAD))],
        out_specs=pl.BlockSpec((tq, qw), lambda b, h, i: (b * nq + i, h)),
        out_shape=jax.ShapeDtypeStruct((t, ATTN_WIDTH), F32),
        scratch_shapes=[pltpu.VMEM((tk, cols), BF16),
                        pltpu.VMEM((1, cols), F32), pltpu.VMEM((1, cols), F32), pltpu.VMEM((1, cols), F32),
                        pltpu.VMEM((HEAD_DIM, cols), F32)],
        name=name,
    )(qaug, k, vt, z)


def _merge_kernel(x_ref, g0_ref, g1_ref, g2_ref, cb_ref, cc_ref, cx_ref, pc_ref, px_ref, ys_ref,
                  oc_ref, os_ref, ow_ref, cw_ref, wc_ref, wglu_ref, wo_ref, wout_ref, out_ref, *, seq):
    tm = x_ref.shape[0]
    yg = _dot(_gelu_tanh(ys_ref[...]).astype(BF16), wglu_ref[...])
    y_ssm = yg[:, :D_MODEL] * _sigmoid(yg[:, D_MODEL:])
    zc = cc_ref[...] * cx_ref[...]
    keep = jnp.where((pl.program_id(0) * tm) % seq != 0, 1.0, 0.0)
    prev = pc_ref[...] * px_ref[...] * keep
    row = lax.broadcasted_iota(jnp.int32, zc.shape, 0)
    z1 = jnp.where(row >= 1, pltpu.roll(zc, 1, 0), prev[7:8, :])
    z2 = jnp.where(row >= 2, pltpu.roll(zc, 2, 0), jnp.where(row == 1, prev[7:8, :], prev[6:7, :]))
    conv = cw_ref[0:1, :] * z2 + cw_ref[1:2, :] * z1 + cw_ref[2:3, :] * zc
    y_conv = _dot((cb_ref[...] * conv).astype(BF16), wc_ref[...])
    y_attn = _dot((oc_ref[...] + os_ref[...] + ow_ref[...]).astype(BF16), wo_ref[...])
    mixed = _sigmoid(g0_ref[...]) * y_ssm + _sigmoid(g1_ref[...]) * y_conv + _sigmoid(g2_ref[...]) * y_attn
    out_ref[...] = x_ref[...] + _dot(mixed.astype(BF16), wout_ref[...])


def _merge(x2, z, ys, o_cmp, o_sel, o_win, conv_w, wc, wglu, wo, wout, *, seq, tm):
    t, d = x2.shape
    rb = tm // 8
    zc = lambda width, off: pl.BlockSpec((tm, width), lambda i, o=off // width: (i, o))
    zprev = lambda off: pl.BlockSpec((8, CONV_CH), lambda i, o=off // CONV_CH: (jnp.maximum(i * rb - 1, 0), o))
    row = lambda width: pl.BlockSpec((tm, width), lambda i: (i, 0))
    full = lambda a: pl.BlockSpec(a.shape, lambda i: (0, 0))
    return pl.pallas_call(
        functools.partial(_merge_kernel, seq=seq),
        grid=(t // tm,),
        in_specs=[row(d), zc(d, Z_MIX), zc(d, Z_MIX + d), zc(d, Z_MIX + 2 * d),
                  zc(CONV_CH, Z_CB), zc(CONV_CH, Z_CC), zc(CONV_CH, Z_CX), zprev(Z_CC), zprev(Z_CX),
                  row(SSM_WIDTH), row(ATTN_WIDTH), row(ATTN_WIDTH), row(ATTN_WIDTH),
                  full(conv_w), full(wc), full(wglu), full(wo), full(wout)],
        out_specs=row(d),
        out_shape=jax.ShapeDtypeStruct((t, d), F32),
        name="merge",
    )(x2, z, z, z, z, z, z, z, z, ys, o_cmp, o_sel, o_win, conv_w, wc, wglu, wo, wout)


def _ffn_kernel(x_ref, g_ref, wg_ref, wu_ref, wd_ref, o_ref, h_ref, acc_ref):
    f = pl.program_id(1)

    @pl.when(f == 0)
    def _():
        h_ref[...] = _rms(x_ref[...], g_ref[...]).astype(BF16)
        acc_ref[...] = jnp.zeros(acc_ref.shape, F32)

    h = h_ref[...]
    gate = _dot(h, wg_ref[...])
    up = _dot(h, wu_ref[...])
    act = (gate * _sigmoid(gate) * up).astype(BF16)
    acc_ref[...] += _dot(act, wd_ref[...])

    @pl.when(f == pl.num_programs(1) - 1)
    def _():
        o_ref[...] = x_ref[...] + acc_ref[...]


def _ffn(x2, g, w_gate_up, w_down, *, tm, tf):
    t, d = x2.shape
    nf = D_FF // tf
    return pl.pallas_call(
        _ffn_kernel,
        grid=(t // tm, nf),
        in_specs=[pl.BlockSpec((tm, d), lambda i, f: (i, 0)),
                  pl.BlockSpec((1, d), lambda i, f: (0, 0)),
                  pl.BlockSpec((d, tf), lambda i, f: (0, f)),
                  pl.BlockSpec((d, tf), lambda i, f: (0, nf + f)),
                  pl.BlockSpec((tf, d), lambda i, f: (f, 0))],
        out_specs=pl.BlockSpec((tm, d), lambda i, f: (i, 0)),
        out_shape=jax.ShapeDtypeStruct((t, d), F32),
        scratch_shapes=[pltpu.VMEM((tm, d), BF16), pltpu.VMEM((tm, d), F32)],
        compiler_params=pltpu.CompilerParams(dimension_semantics=("parallel", "arbitrary")),
        name="ffn",
    )(x2, g.reshape(1, d), w_gate_up, w_gate_up, w_down)


def _pick(n, pref):
    while n % pref:
        pref //= 2
    return pref


def _layer(x2, p, *, bsz, seq):
    t = x2.shape[0]
    z = _inproj(x2, p['mix_norm_g'], p['w_in'], tm=_pick(t, 512), tn=Z_WIDTH // 3)
    ys = _s5_scan(z, p['s5'], bsz=bsz, seq=seq)
    kaug, vsel, kwin, vwin = _kv_prep(z, p['k_norm_g'], seq=seq, tp=_pick(seq, 512), tk_sel=512, tk_win=256)
    cmp_kv = _compress(z, p['cmp_pe'], p['cmp_w1'], p['cmp_w2'], p['k_norm_g'][0], bsz=bsz, seq=seq)
    o_cmp, qaug = _cmp_select(z, cmp_kv, p['q_norm_g'], bsz=bsz, seq=seq, tq=256)
    o_sel = _flash(qaug, kaug, vsel, z, bsz=bsz, seq=seq, tq=256, window=0, branch=1, name="sel_attn")
    o_win = _flash(qaug, kwin, vwin, z, bsz=bsz, seq=seq, tq=256, window=WINDOW, branch=2, name="win_attn")
    x2 = _merge(x2, z, ys, o_cmp, o_sel, o_win, p['conv_w'], p['conv_w_out'], p['ssm_w_glu'], p['nsa_w_o'],
                p['w_out'], seq=seq, tm=_pick(seq, 256))
    return _ffn(x2, p['ffn_norm_g'], p['ffn_w_gate_up'], p['ffn_w_down'], tm=_pick(t, 1024), tf=256)


def kernel(x, mix_norm_g, w_in, ssm_lam_re, ssm_lam_im, ssm_b_re, ssm_b_im, ssm_c_re, ssm_c_im, ssm_d, ssm_log_dt, ssm_w_glu, conv_w, conv_w_out, q_norm_g, k_norm_g, cmp_pe, cmp_w1, cmp_w2, nsa_w_o, w_out, ffn_norm_g, ffn_w_gate_up, ffn_w_down):
    bsz, seq, d = x.shape
    x2 = x.reshape(bsz * seq, d)
    for i in range(w_in.shape[0]):
        p = dict(
            mix_norm_g=mix_norm_g[i], w_in=_permute_w_in(w_in[i]),
            s5=_s5_tables(ssm_lam_re[i], ssm_lam_im[i], ssm_b_re[i], ssm_b_im[i], ssm_c_re[i], ssm_c_im[i],
                          ssm_d[i], ssm_log_dt[i], seq // S5_CHUNK),
            ssm_w_glu=ssm_w_glu[i].astype(BF16), conv_w=conv_w[i], conv_w_out=conv_w_out[i].astype(BF16),
            q_norm_g=q_norm_g[i], k_norm_g=k_norm_g[i], cmp_pe=cmp_pe[i],
            cmp_w1=cmp_w1[i].astype(BF16), cmp_w2=cmp_w2[i].astype(BF16), nsa_w_o=nsa_w_o[i].astype(BF16),
            w_out=w_out[i].astype(BF16), ffn_norm_g=ffn_norm_g[i],
            ffn_w_gate_up=ffn_w_gate_up[i].astype(BF16), ffn_w_down=ffn_w_down[i].astype(BF16))
        x2 = _layer(x2, p, bsz=bsz, seq=seq)
    return x2.reshape(bsz, seq, d)
```

```python
import functools

import jax
import jax.numpy as jnp
import numpy as np
from jax import lax
from jax.experimental import pallas as pl
from jax.experimental.pallas import tpu as pltpu

F32 = jnp.float32
BF16 = jnp.bfloat16

D_MODEL = 1024
SSM_WIDTH = 512
SSM_GROUP = 16
SSM_GROUPS = SSM_WIDTH // SSM_GROUP
SSM_STATE = 64
CONV_CH = 512
CONV_K = 3
HEAD_DIM = 64
HEAD_SHIFT = 6
N_HEADS = 8
N_KV_HEADS = 2
GQA = N_HEADS // N_KV_HEADS
ATTN_WIDTH = N_HEADS * HEAD_DIM
KV_WIDTH = N_KV_HEADS * HEAD_DIM
CMP_BLOCK = 32
SEL_BLOCK = 64
SEL_SHIFT = 6
N_SELECT = 16
WINDOW = 512
CMP_HIDDEN = 256
FORCE_SCORE = 1e4
NSA_BRANCHES = 3
MIX_BRANCHES = 3
D_FF = 2816
RMS_EPS = 1e-6

LANES = 128
MASKED = -1e30
LOG2E = 1.4426950408889634
S5_CHUNK = 16
S5_TILE_GROUPS = LANES // SSM_GROUP
GATE_PAD = LANES

Z_MIX = 0
Z_U = Z_MIX + MIX_BRANCHES * D_MODEL
Z_CB = Z_U + SSM_WIDTH
Z_CC = Z_CB + CONV_CH
Z_CX = Z_CC + CONV_CH
Z_Q = Z_CX + CONV_CH
Z_KC = Z_Q + ATTN_WIDTH
Z_VC = Z_KC + KV_WIDTH
Z_KS = Z_VC + KV_WIDTH
Z_VS = Z_KS + KV_WIDTH
Z_KW = Z_VS + KV_WIDTH
Z_VW = Z_KW + KV_WIDTH
Z_GATE = Z_VW + KV_WIDTH
Z_WIDTH = Z_GATE + GATE_PAD


def _gelu_tanh(x):
    return 0.5 * x * (1.0 + jnp.tanh(np.sqrt(2.0 / np.pi).astype(np.float32) * (x + 0.044715 * (x * x * x))))


def _sigmoid(x):
    return 1.0 / (1.0 + jnp.exp(-x))


def _rms(x, g):
    return x * lax.rsqrt(jnp.mean(x * x, axis=-1, keepdims=True) + RMS_EPS) * g


def _dot(a, b):
    return jnp.dot(a, b, preferred_element_type=F32)


def _dot_nt(a, b):
    return lax.dot_general(a, b, (((1,), (1,)), ((), ())), preferred_element_type=F32)


def _inproj_kernel(x_ref, g_ref, w_ref, o_ref):
    h = _rms(x_ref[...], g_ref[...]).astype(BF16)
    o_ref[...] = _dot(h, w_ref[...])


def _inproj(x2, g, w, *, tm, tn):
    t, d = x2.shape
    n = w.shape[1]
    return pl.pallas_call(
        _inproj_kernel,
        grid=(n // tn, t // tm),
        in_specs=[pl.BlockSpec((tm, d), lambda j, i: (i, 0)),
                  pl.BlockSpec((1, d), lambda j, i: (0, 0)),
                  pl.BlockSpec((d, tn), lambda j, i: (0, j))],
        out_specs=pl.BlockSpec((tm, tn), lambda j, i: (i, j)),
        out_shape=jax.ShapeDtypeStruct((t, n), F32),
        name="inproj",
    )(x2, g.reshape(1, d), w)


def _permute_w_in(w):
    n_plain = SSM_WIDTH + 3 * CONV_CH + ATTN_WIDTH + 6 * KV_WIDTH
    n_gate = N_HEADS * NSA_BRANCHES
    gate = jnp.pad(w[:, n_plain:n_plain + n_gate], ((0, 0), (0, GATE_PAD - n_gate)))
    return jnp.concatenate([w[:, n_plain + n_gate:], w[:, :n_plain], gate], axis=1).astype(BF16)


def _s5_tables(lam_re, lam_im, b_re, b_im, c_re, c_im, d_skip, log_dt, n_chunks):
    hp = lax.Precision.HIGHEST
    g, p = lam_re.shape
    h, l, gt = SSM_GROUP, S5_CHUNK, S5_TILE_GROUPS
    nt = g // gt
    dt = jnp.exp(log_dt)[:, None]
    ar, ai = lam_re * dt, lam_im * dt

    def powers(k):
        mag = jnp.exp(ar[None] * k[:, None, None])
        ang = ai[None] * k[:, None, None]
        return mag * jnp.cos(ang), mag * jnp.sin(ang)

    pr, pi = powers(jnp.arange(l + 1, dtype=F32))
    nr, ni = pr[1] - 1.0, pi[1]
    den = lam_re * lam_re + lam_im * lam_im
    fr, fi = (nr * lam_re + ni * lam_im) / den, (ni * lam_re - nr * lam_im) / den
    bbr = fr[..., None] * b_re - fi[..., None] * b_im
    bbi = fr[..., None] * b_im + fi[..., None] * b_re
    wr = pr[:l, :, :, None] * bbr[None] - pi[:l, :, :, None] * bbi[None]
    wi = pr[:l, :, :, None] * bbi[None] + pi[:l, :, :, None] * bbr[None]
    kern = (jnp.einsum('gop,tgpi->tgio', c_re, wr, precision=hp)
            - jnp.einsum('gop,tgpi->tgio', c_im, wi, precision=hp))
    r = jnp.arange(l)

    def group_diag(x, row_w, col_w):
        rows, w = x.shape[-2:]
        tiled = jnp.tile(x, (1,) * (x.ndim - 1) + (gt,))
        own = (jnp.arange(rows)[:, None] // row_w) % gt == jnp.arange(gt * w)[None, :] // col_w
        return jnp.where(own, tiled, 0.0)

    kblk = group_diag(kern.reshape(l, g * h, h), h, h)
    krev = kblk.reshape(l, nt, LANES, LANES)[::-1].transpose(1, 0, 2, 3).reshape(nt, l * LANES, LANES)
    qr, qi = pr[l - 1 - r], pi[l - 1 - r]
    st_re = qr[..., None] * bbr[None] - qi[..., None] * bbi[None]
    st_im = qr[..., None] * bbi[None] + qi[..., None] * bbr[None]
    rows_in = lambda x: group_diag(x.transpose(0, 1, 3, 2).reshape(l, g * h, p), h, p)
    s_tab = jnp.concatenate([rows_in(st_re), rows_in(st_im)], axis=-1)
    s_tab = s_tab.reshape(l, nt, LANES, 2 * gt * p).transpose(1, 0, 2, 3).reshape(nt, l * LANES, 2 * gt * p)
    er, ei = pr[1:l + 1][:, :, None, :], pi[1:l + 1][:, :, None, :]
    rows_out = lambda x: group_diag(x.transpose(0, 1, 3, 2).reshape(l, g * p, h), p, h).reshape(l, nt, gt * p, LANES)
    c_tab = jnp.concatenate([rows_out(c_re[None] * er - c_im[None] * ei),
                             rows_out(-(c_re[None] * ei + c_im[None] * er))], axis=2)
    n_steps = max(1, int(np.ceil(np.log2(n_chunks))))
    dr, di = powers(l * (2.0 ** jnp.arange(n_steps, dtype=F32)))
    lanes = lambda m: m.reshape(n_steps, nt, gt * p).transpose(1, 0, 2)
    return dict(krev=krev.astype(BF16), s_tab=s_tab.astype(BF16), c_tab=c_tab.astype(BF16),
                d_re=lanes(dr), d_im=lanes(di), d_skip=d_skip.reshape(1, g * h))


def _s5_kernel(u_ref, krev_ref, s_ref, c_ref, dre_ref, dim_ref, dsk_ref, y_ref):
    seq = u_ref.shape[0]
    l = S5_CHUNK
    nc = seq // l
    n_steps = dre_ref.shape[1]
    ns = dre_ref.shape[2]
    us = [u_ref[pl.ds(r, nc, stride=l), :] for r in range(l)]
    ucat = jnp.concatenate(us, axis=1).astype(BF16)
    e = _dot(ucat, s_ref[0])
    xr, xi = e[:, :ns], e[:, ns:]
    row = lax.broadcasted_iota(jnp.int32, xr.shape, 0)
    for k in range(n_steps):
        s = 1 << k
        if s >= nc:
            break
        dr = dre_ref[0, k:k + 1, :]
        di = dim_ref[0, k:k + 1, :]
        sr = jnp.where(row >= s, pltpu.roll(xr, s, 0), 0.0)
        si = jnp.where(row >= s, pltpu.roll(xi, s, 0), 0.0)
        xr, xi = xr + (dr * sr - di * si), xi + (dr * si + di * sr)
    xp = jnp.concatenate([jnp.where(row >= 1, pltpu.roll(xr, 1, 0), 0.0),
                          jnp.where(row >= 1, pltpu.roll(xi, 1, 0), 0.0)], axis=1).astype(BF16)
    dsk = dsk_ref[...]
    for s in range(l):
        y = (_dot(ucat[:, :(s + 1) * LANES], krev_ref[0, (l - 1 - s) * LANES:, :])
             + _dot(xp, c_ref[s, 0]) + dsk * us[s])
        y_ref[pl.ds(s, nc, stride=l), :] = y


def _s5_scan(z, tabs, *, bsz, seq):
    t = z.shape[0]
    nt = SSM_WIDTH // LANES
    l = S5_CHUNK
    tile3 = lambda a: pl.BlockSpec((1,) + a.shape[1:], lambda j, b: (j, 0, 0))
    c_tab = tabs['c_tab']
    return pl.pallas_call(
        _s5_kernel,
        grid=(nt, bsz),
        in_specs=[pl.BlockSpec((seq, LANES), lambda j, b: (b, Z_U // LANES + j)),
                  tile3(tabs['krev']), tile3(tabs['s_tab']),
                  pl.BlockSpec((l, 1) + c_tab.shape[2:], lambda j, b: (0, j, 0, 0)),
                  tile3(tabs['d_re']), tile3(tabs['d_im']),
                  pl.BlockSpec((1, LANES), lambda j, b: (0, j))],
        out_specs=pl.BlockSpec((seq, LANES), lambda j, b: (b, j)),
        out_shape=jax.ShapeDtypeStruct((t, SSM_WIDTH), F32),
        name="s5_scan",
    )(z, tabs['krev'], tabs['s_tab'], c_tab, tabs['d_re'], tabs['d_im'], tabs['d_skip'])


def _kv_prep_kernel(ks_ref, vs_ref, kw_ref, vw_ref, g_ref, kaug_ref, vsel_ref, kwin_ref, vwin_ref, *, seq):
    tp = ks_ref.shape[0]
    pos = (pl.program_id(0) * tp) % seq + lax.broadcasted_iota(jnp.int32, (tp, HEAD_DIM), 0)
    blk = jnp.right_shift(pos, SEL_SHIFT)
    onehot = jnp.where(lax.broadcasted_iota(jnp.int32, (tp, HEAD_DIM), 1) == blk, 1.0, 0.0).astype(BF16)
    zeros = jnp.zeros((tp, HEAD_DIM), BF16)
    for hh in range(N_KV_HEADS):
        sl = slice(hh * HEAD_DIM, (hh + 1) * HEAD_DIM)
        kn = _rms(ks_ref[:, sl], g_ref[0:1, :]).astype(BF16)
        kaug_ref[hh] = jnp.concatenate([kn, onehot], axis=1)
        kn = _rms(kw_ref[:, sl], g_ref[1:2, :]).astype(BF16)
        kwin_ref[hh] = jnp.concatenate([kn, zeros], axis=1)
    for v_ref, vt_ref in ((vs_ref, vsel_ref), (vw_ref, vwin_ref)):
        tk = vt_ref.shape[3]
        vt = v_ref[...].T.astype(BF16)
        for hh in range(N_KV_HEADS):
            for j in range(tp // tk):
                vt_ref[hh, j] = vt[hh * HEAD_DIM:(hh + 1) * HEAD_DIM, j * tk:(j + 1) * tk]


def _kv_prep(z, k_norm_g, *, seq, tp, tk_sel, tk_win):
    t = z.shape[0]
    col = lambda off: pl.BlockSpec((tp, KV_WIDTH), lambda i, o=off // KV_WIDTH: (i, o))
    aug = 2 * HEAD_DIM
    kspec = pl.BlockSpec((N_KV_HEADS, tp, aug), lambda i: (0, i, 0))
    vspec = lambda tk: pl.BlockSpec((N_KV_HEADS, tp // tk, HEAD_DIM, tk), lambda i: (0, i, 0, 0))
    vshape = lambda tk: jax.ShapeDtypeStruct((N_KV_HEADS, t // tk, HEAD_DIM, tk), BF16)
    return pl.pallas_call(
        functools.partial(_kv_prep_kernel, seq=seq),
        grid=(t // tp,),
        in_specs=[col(Z_KS), col(Z_VS), col(Z_KW), col(Z_VW), pl.BlockSpec((2, HEAD_DIM), lambda i: (0, 0))],
        out_specs=[kspec, vspec(tk_sel), kspec, vspec(tk_win)],
        out_shape=[jax.ShapeDtypeStruct((N_KV_HEADS, t, aug), BF16), vshape(tk_sel),
                   jax.ShapeDtypeStruct((N_KV_HEADS, t, aug), BF16), vshape(tk_win)],
        name="kv_prep",
    )(z, z, z, z, k_norm_g[1:3])


def _compress_kernel(x_ref, pe_ref, w1_ref, w2_ref, g_ref, o_ref):
    seq = x_ref.shape[0]
    nb = seq // (2 * CMP_BLOCK)
    lane = lax.broadcasted_iota(jnp.int32, (nb, 2 * HEAD_DIM), 1)
    rows = []
    for hh in range(N_KV_HEADS):
        for parity in range(2):
            cols = []
            for s in range(0, CMP_BLOCK, 2):
                a = x_ref[pl.ds(parity * CMP_BLOCK + s, nb, stride=2 * CMP_BLOCK), :]
                b = x_ref[pl.ds(parity * CMP_BLOCK + s + 1, nb, stride=2 * CMP_BLOCK), :]
                if hh == 0:
                    cols.append(jnp.where(lane < HEAD_DIM, a, pltpu.roll(b, HEAD_DIM, 1)))
                else:
                    cols.append(jnp.where(lane < HEAD_DIM, pltpu.roll(a, HEAD_DIM, 1), b))
            rows.append(jnp.concatenate(cols, axis=1))
    x = (jnp.concatenate(rows, axis=0) + pe_ref[0]).astype(BF16)
    hid = _gelu_tanh(_dot(x, w1_ref[0])).astype(BF16)
    y = _dot(hid, w2_ref[0])
    o_ref[0, 0] = jnp.where(pl.program_id(0) == 0, _rms(y, g_ref[...]), y)


def _compress(z, cmp_pe, w1, w2, k_gain, *, bsz, seq):
    nc = seq // CMP_BLOCK
    kdim = CMP_BLOCK * HEAD_DIM
    m = N_KV_HEADS * nc
    out = pl.pallas_call(
        _compress_kernel,
        grid=(2, bsz),
        in_specs=[pl.BlockSpec((seq, KV_WIDTH), lambda w, b: (b, Z_KC // KV_WIDTH + w)),
                  pl.BlockSpec((1, 1, kdim), lambda w, b: (w, 0, 0)),
                  pl.BlockSpec((1, kdim, CMP_HIDDEN), lambda w, b: (w, 0, 0)),
                  pl.BlockSpec((1, CMP_HIDDEN, HEAD_DIM), lambda w, b: (w, 0, 0)),
                  pl.BlockSpec((1, HEAD_DIM), lambda w, b: (0, 0))],
        out_specs=pl.BlockSpec((1, 1, m, HEAD_DIM), lambda w, b: (w, b, 0, 0)),
        out_shape=jax.ShapeDtypeStruct((2, bsz, m, HEAD_DIM), F32),
        name="compress",
    )(z, cmp_pe.reshape(2, 1, kdim), w1, w2, k_gain.reshape(1, HEAD_DIM))
    return out.reshape(2, bsz * N_KV_HEADS, nc, HEAD_DIM)


def _head_gate(gl, kvh, g, branch):
    c0 = g * NSA_BRANCHES + branch
    c1 = (GQA + g) * NSA_BRANCHES + branch
    if isinstance(kvh, int):
        c = c1 if kvh else c0
        return _sigmoid(gl[:, c:c + 1])
    return _sigmoid(jnp.where(kvh == 0, gl[:, c0:c0 + 1], gl[:, c1:c1 + 1]))


def _cmp_select_kernel(q_ref, gl_ref, kc_ref, vc_ref, qg_ref, o_ref, qaug_ref, imp_ref):
    tq = q_ref.shape[0]
    nc = kc_ref.shape[1]
    nb = nc // 2
    pair_w = 2 * HEAD_DIM
    qw = GQA * HEAD_DIM
    q0 = pl.program_id(1) * tq
    q = q_ref[...]
    x2 = q * q
    hi = x2.astype(BF16)
    lo = (x2 - hi.astype(F32)).astype(BF16)
    wq = q.shape[1]
    seg = jnp.where(jnp.right_shift(lax.broadcasted_iota(jnp.int32, (wq, wq), 0), HEAD_SHIFT)
                    == jnp.right_shift(lax.broadcasted_iota(jnp.int32, (wq, wq), 1), HEAD_SHIFT), 1.0, 0.0).astype(BF16)
    ss = _dot(hi, seg) + _dot(lo, seg)
    qf = q * lax.rsqrt(ss * (1.0 / HEAD_DIM) + RMS_EPS) * qg_ref[...] * (HEAD_DIM ** -0.5)
    qn = qf.astype(BF16)
    q2 = qf * LOG2E

    row = lax.broadcasted_iota(jnp.int32, (nc, tq), 0)
    t = q0 + lax.broadcasted_iota(jnp.int32, (nc, tq), 1)
    blk = jnp.where(row < nb, 2 * row, 2 * (row - nb) + 1)
    valid = (blk + 1) * CMP_BLOCK - 1 <= t
    lane = lax.broadcasted_iota(jnp.int32, (tq, pair_w), 1)
    j = lax.broadcasted_iota(jnp.int32, (nb, tq), 0)
    cur = jnp.right_shift(q0 + lax.broadcasted_iota(jnp.int32, (nb, tq), 1), SEL_SHIFT)
    forced = (j == 0) | (j == cur) | (j == cur - 1)
    visible = j <= cur
    gl = gl_ref[...]
    out = []
    imps = []
    for kvh in range(N_KV_HEADS):
        kc = kc_ref[kvh].astype(BF16)
        vc = vc_ref[kvh].astype(BF16)
        zk = jnp.zeros_like(kc)
        k2 = jnp.concatenate([jnp.concatenate([kc, zk], axis=1), jnp.concatenate([zk, kc], axis=1)], axis=0)
        psum = jnp.zeros((nc, tq), F32)
        for pr in range(GQA // 2):
            c0 = kvh * qw + pr * pair_w
            st = _dot_nt(k2, qn[:, c0:c0 + pair_w])
            ot = []
            for hh in range(2):
                sm = jnp.where(valid, st[hh * nc:(hh + 1) * nc], MASKED)
                m = jnp.max(sm, axis=0, keepdims=True)
                m = jnp.where(m > 0.5 * MASKED, m, 0.0)
                e = jnp.where(valid, jnp.exp(sm - m), 0.0)
                p = e * (1.0 / jnp.maximum(jnp.sum(e, axis=0, keepdims=True), 1e-30))
                psum = psum + p
                ot.append(lax.dot_general(vc, p.astype(BF16), (((0,), (0,)), ((), ())), preferred_element_type=F32))
            o_pair = jnp.concatenate(ot, axis=0).T
            gate = jnp.where(lane < HEAD_DIM, _head_gate(gl, kvh, 2 * pr, 0), _head_gate(gl, kvh, 2 * pr + 1, 0))
            out.append(o_pair * gate)
        imp = psum[:nb] + psum[nb:]
        imp = jnp.where(forced, FORCE_SCORE, jnp.where(visible, imp, -jnp.inf))
        imp_ref[kvh] = imp
        imps.append(imp)
    o_ref[...] = jnp.concatenate(out, axis=1)

    n_vis = jnp.minimum(nb, (q0 + tq - 1) // SEL_BLOCK + 1)

    def count(i, cnts):
        res = []
        for kvh in range(N_KV_HEADS):
            vi = imp_ref[kvh, pl.ds(i, 1), :]
            beats = (vi > imps[kvh]) | ((vi == imps[kvh]) & (i < j))
            res.append(cnts[kvh] + jnp.where(beats, 1.0, 0.0))
        return tuple(res)

    cnts = lax.fori_loop(0, n_vis, count, tuple(jnp.zeros((nb, tq), F32) for _ in range(N_KV_HEADS)))
    for kvh in range(N_KV_HEADS):
        selneg_t = jnp.where(visible & (cnts[kvh] < float(N_SELECT)), 0.0, MASKED)
        if nb < HEAD_DIM:
            selneg_t = jnp.concatenate([selneg_t, jnp.zeros((HEAD_DIM - nb, tq), F32)], axis=0)
        selneg = jnp.concatenate([selneg_t, selneg_t], axis=0).T
        for pr in range(GQA // 2):
            c0 = kvh * qw + pr * pair_w
            qp = q2[:, c0:c0 + pair_w]
            qaug_ref[kvh, 2 * pr] = jnp.where(lane < HEAD_DIM, qp, selneg).astype(BF16)
            qaug_ref[kvh, 2 * pr + 1] = jnp.where(lane < HEAD_DIM, pltpu.roll(qp, HEAD_DIM, 1), selneg).astype(BF16)


def _cmp_select(z, cmp_kv, q_norm_g, *, bsz, seq, tq):
    t = z.shape[0]
    nq = seq // tq
    nc = seq // CMP_BLOCK
    aug = 2 * HEAD_DIM
    qw = GQA * HEAD_DIM
    kc, vc = cmp_kv[0], cmp_kv[1]
    kv_spec = pl.BlockSpec((N_KV_HEADS, nc, HEAD_DIM), lambda b, i: (b, 0, 0))
    return pl.pallas_call(
        _cmp_select_kernel,
        grid=(bsz, nq),
        in_specs=[pl.BlockSpec((tq, ATTN_WIDTH), lambda b, i: (b * nq + i, Z_Q // ATTN_WIDTH)),
                  pl.BlockSpec((tq, GATE_PAD), lambda b, i: (b * nq + i, Z_GATE // GATE_PAD)),
                  kv_spec, kv_spec,
                  pl.BlockSpec((1, ATTN_WIDTH), lambda b, i: (0, 0))],
        out_specs=[pl.BlockSpec((tq, ATTN_WIDTH), lambda b, i: (b * nq + i, 0)),
                   pl.BlockSpec((N_KV_HEADS, GQA, tq, aug), lambda b, i: (b, 0, i, 0))],
        out_shape=[jax.ShapeDtypeStruct((t, ATTN_WIDTH), F32),
                   jax.ShapeDtypeStruct((bsz * N_KV_HEADS, GQA, seq, aug), BF16)],
        scratch_shapes=[pltpu.VMEM((N_KV_HEADS, nc // 2, tq), F32)],
        name="cmp_select",
    )(z, z, kc, vc, jnp.tile(q_norm_g, N_HEADS).reshape(1, ATTN_WIDTH))


def _flash_kernel(q_ref, k_ref, vt_ref, gl_ref, o_ref, sa_ref, sb_ref, pa_ref, pb_ref, m_ref, l_ref, a_ref,
                  acc_ref, *, window, branch):
    tq = q_ref.shape[2]
    tk = vt_ref.shape[3]
    cols = GQA * tq
    kvh = pl.program_id(1)
    q0 = pl.program_id(2) * tq
    qa = q_ref[0].reshape(cols, q_ref.shape[3])
    t = q0 + (lax.broadcasted_iota(jnp.int32, (tk, cols), 1) & (tq - 1))
    key_row = lax.broadcasted_iota(jnp.int32, (tk, cols), 0)
    span = 2 * tk
    lo = jnp.maximum(0, q0 - window + 1) // span if window else 0
    hi = (q0 + tq + span - 1) // span
    mid1 = jnp.maximum(lo, (q0 + tq - 1 - window) // span + 1) if window else lo
    mid2 = q0 // span
    first = 2 * lo
    last = 2 * hi - 1

    def scores(kb):
        return _dot_nt(k_ref[0, pl.ds(pl.multiple_of(kb * tk, tk), tk), :], qa)

    m_ref[...] = jnp.full(m_ref.shape, MASKED, F32)
    l_ref[...] = jnp.zeros(l_ref.shape, F32)
    a_ref[...] = jnp.ones(a_ref.shape, F32)
    acc_ref[...] = jnp.zeros(acc_ref.shape, F32)
    pb_ref[...] = jnp.zeros(pb_ref.shape, BF16)
    sa_ref[...] = scores(first)

    def tile(kb, s_cur, s_nxt, p_prev, p_cur, masked):
        pv = _dot(vt_ref[0, jnp.maximum(kb - 1, first)], p_prev[...])
        s_nxt[...] = scores(jnp.minimum(kb + 1, last))
        s = s_cur[...]
        if masked:
            key = kb * tk + key_row
            valid = key <= t
            if window:
                valid = valid & (key > t - window)
            s = jnp.where(valid, s, MASKED)
        m_old = m_ref[...]
        m_new = jnp.maximum(m_old, jnp.max(s, axis=0, keepdims=True))
        alpha = jnp.exp2(m_old - m_new)
        p = jnp.exp2(s - m_new)
        l_ref[...] = alpha * l_ref[...] + jnp.sum(p, axis=0, keepdims=True)
        acc_ref[...] = a_ref[...] * acc_ref[...] + pv
        a_ref[...] = alpha
        p_cur[...] = p.astype(BF16)
        m_ref[...] = m_new

    def run(begin, end, masked):
        def body(pair, carry):
            tile(2 * pair, sa_ref, sb_ref, pb_ref, pa_ref, masked)
            tile(2 * pair + 1, sb_ref, sa_ref, pa_ref, pb_ref, masked)
            return carry
        lax.fori_loop(begin, end, body, 0)

    run(lo, mid1, True)
    run(mid1, mid2, False)
    run(mid2, hi, True)
    acc = a_ref[...] * acc_ref[...] + _dot(vt_ref[0, last], pb_ref[...])
    o_t = acc / l_ref[...]
    gl = gl_ref[...]
    out = []
    for g in range(0, GQA, 2):
        pair = jnp.concatenate([o_t[:, g * tq:(g + 1) * tq], o_t[:, (g + 1) * tq:(g + 2) * tq]], axis=0).T
        out.append(pair[:, :HEAD_DIM] * _head_gate(gl, kvh, g, branch))
        out.append(pair[:, HEAD_DIM:] * _head_gate(gl, kvh, g + 1, branch))
    o_ref[...] = jnp.concatenate(out, axis=1)


def _flash(qaug, k, vt, z, *, bsz, seq, tq, window, branch, name):
    t = z.shape[0]
    nq = seq // tq
    aug = qaug.shape[3]
    tk = vt.shape[3]
    qw = GQA * HEAD_DIM
    cols = GQA * tq
    return pl.pallas_call(
        functools.partial(_flash_kernel, window=window, branch=branch),
        grid=(bsz, N_KV_HEADS, nq),
        in_specs=[pl.BlockSpec((1, GQA, tq, aug), lambda b, h, i: (b * N_KV_HEADS + h, 0, i, 0)),
                  pl.BlockSpec((1, seq, aug), lambda b, h, i: (h, b, 0)),
                  pl.BlockSpec((1, seq // tk, HEAD_DIM, tk), lambda b, h, i: (h, b, 0, 0)),
                  pl.BlockSpec((tq, GATE_PAD), lambda b, h, i: (b * nq + i, Z_GATE // GATE_PAD))],
        out_specs=pl.BlockSpec((tq, qw), lambda b, h, i: (b * nq + i, h)),
        out_shape=jax.ShapeDtypeStruct((t, ATTN_WIDTH), F32),
        scratch_shapes=[pltpu.VMEM((tk, cols), F32), pltpu.VMEM((tk, cols), F32),
                        pltpu.VMEM((tk, cols), BF16), pltpu.VMEM((tk, cols), BF16),
                        pltpu.VMEM((1, cols), F32), pltpu.VMEM((1, cols), F32), pltpu.VMEM((1, cols), F32),
                        pltpu.VMEM((HEAD_DIM, cols), F32)],
        name=name,
    )(qaug, k, vt, z)


def _merge_kernel(x_ref, g0_ref, g1_ref, g2_ref, cb_ref, cc_ref, cx_ref, pc_ref, px_ref, ys_ref,
                  oc_ref, os_ref, ow_ref, cw_ref, wc_ref, wglu_ref, wo_ref, wout_ref, out_ref, *, seq):
    tm = x_ref.shape[0]
    yg = _dot(_gelu_tanh(ys_ref[...]).astype(BF16), wglu_ref[...])
    y_ssm = yg[:, :D_MODEL] * _sigmoid(yg[:, D_MODEL:])
    zc = cc_ref[...] * cx_ref[...]
    keep = jnp.where((pl.program_id(0) * tm) % seq != 0, 1.0, 0.0)
    prev = pc_ref[...] * px_ref[...] * keep
    row = lax.broadcasted_iota(jnp.int32, zc.shape, 0)
    z1 = jnp.where(row >= 1, pltpu.roll(zc, 1, 0), prev[7:8, :])
    z2 = jnp.where(row >= 2, pltpu.roll(zc, 2, 0), jnp.where(row == 1, prev[7:8, :], prev[6:7, :]))
    conv = cw_ref[0:1, :] * z2 + cw_ref[1:2, :] * z1 + cw_ref[2:3, :] * zc
    y_conv = _dot((cb_ref[...] * conv).astype(BF16), wc_ref[...])
    y_attn = _dot((oc_ref[...] + os_ref[...] + ow_ref[...]).astype(BF16), wo_ref[...])
    mixed = _sigmoid(g0_ref[...]) * y_ssm + _sigmoid(g1_ref[...]) * y_conv + _sigmoid(g2_ref[...]) * y_attn
    out_ref[...] = x_ref[...] + _dot(mixed.astype(BF16), wout_ref[...])


def _merge(x2, z, ys, o_cmp, o_sel, o_win, conv_w, wc, wglu, wo, wout, *, seq, tm):
    t, d = x2.shape
    rb = tm // 8
    zc = lambda width, off: pl.BlockSpec((tm, width), lambda i, o=off // width: (i, o))
    zprev = lambda off: pl.BlockSpec((8, CONV_CH), lambda i, o=off // CONV_CH: (jnp.maximum(i * rb - 1, 0), o))
    row = lambda width: pl.BlockSpec((tm, width), lambda i: (i, 0))
    full = lambda a: pl.BlockSpec(a.shape, lambda i: (0, 0))
    return pl.pallas_call(
        functools.partial(_merge_kernel, seq=seq),
        grid=(t // tm,),
        in_specs=[row(d), zc(d, Z_MIX), zc(d, Z_MIX + d), zc(d, Z_MIX + 2 * d),
                  zc(CONV_CH, Z_CB), zc(CONV_CH, Z_CC), zc(CONV_CH, Z_CX), zprev(Z_CC), zprev(Z_CX),
                  row(SSM_WIDTH), row(ATTN_WIDTH), row(ATTN_WIDTH), row(ATTN_WIDTH),
                  full(conv_w), full(wc), full(wglu), full(wo), full(wout)],
        out_specs=row(d),
        out_shape=jax.ShapeDtypeStruct((t, d), F32),
        name="merge",
    )(x2, z, z, z, z, z, z, z, z, ys, o_cmp, o_sel, o_win, conv_w, wc, wglu, wo, wout)


def _ffn_kernel(x_ref, g_ref, wg_ref, wu_ref, wd_ref, o_ref, h_ref, acc_ref):
    f = pl.program_id(1)

    @pl.when(f == 0)
    def _():
        h_ref[...] = _rms(x_ref[...], g_ref[...]).astype(BF16)
        acc_ref[...] = jnp.zeros(acc_ref.shape, F32)

    h = h_ref[...]
    gate = _dot(h, wg_ref[...])
    up = _dot(h, wu_ref[...])
    act = (gate * _sigmoid(gate) * up).astype(BF16)
    acc_ref[...] += _dot(act, wd_ref[...])

    @pl.when(f == pl.num_programs(1) - 1)
    def _():
        o_ref[...] = x_ref[...] + acc_ref[...]


def _ffn(x2, g, w_gate_up, w_down, *, tm, tf):
    t, d = x2.shape
    nf = D_FF // tf
    return pl.pallas_call(
        _ffn_kernel,
        grid=(t // tm, nf),
        in_specs=[pl.BlockSpec((tm, d), lambda i, f: (i, 0)),
                  pl.BlockSpec((1, d), lambda i, f: (0, 0)),
                  pl.BlockSpec((d, tf), lambda i, f: (0, f)),
                  pl.BlockSpec((d, tf), lambda i, f: (0, nf + f)),
                  pl.BlockSpec((tf, d), lambda i, f: (f, 0))],
        out_specs=pl.BlockSpec((tm, d), lambda i, f: (i, 0)),
        out_shape=jax.ShapeDtypeStruct((t, d), F32),
        scratch_shapes=[pltpu.VMEM((tm, d), BF16), pltpu.VMEM((tm, d), F32)],
        compiler_params=pltpu.CompilerParams(dimension_semantics=("parallel", "arbitrary")),
        name="ffn",
    )(x2, g.reshape(1, d), w_gate_up, w_gate_up, w_down)


def _pick(n, pref):
    while n % pref:
        pref //= 2
    return pref


def _layer(x2, p, *, bsz, seq):
    t = x2.shape[0]
    z = _inproj(x2, p['mix_norm_g'], p['w_in'], tm=_pick(t, 512), tn=Z_WIDTH // 3)
    ys = _s5_scan(z, p['s5'], bsz=bsz, seq=seq)
    kaug, vsel, kwin, vwin = _kv_prep(z, p['k_norm_g'], seq=seq, tp=_pick(seq, 512), tk_sel=128, tk_win=128)
    cmp_kv = _compress(z, p['cmp_pe'], p['cmp_w1'], p['cmp_w2'], p['k_norm_g'][0], bsz=bsz, seq=seq)
    o_cmp, qaug = _cmp_select(z, cmp_kv, p['q_norm_g'], bsz=bsz, seq=seq, tq=256)
    o_sel = _flash(qaug, kaug, vsel, z, bsz=bsz, seq=seq, tq=256, window=0, branch=1, name="sel_attn")
    o_win = _flash(qaug, kwin, vwin, z, bsz=bsz, seq=seq, tq=256, window=WINDOW, branch=2, name="win_attn")
    x2 = _merge(x2, z, ys, o_cmp, o_sel, o_win, p['conv_w'], p['conv_w_out'], p['ssm_w_glu'], p['nsa_w_o'],
                p['w_out'], seq=seq, tm=_pick(seq, 256))
    return _ffn(x2, p['ffn_norm_g'], p['ffn_w_gate_up'], p['ffn_w_down'], tm=_pick(t, 1024), tf=256)


def kernel(x, mix_norm_g, w_in, ssm_lam_re, ssm_lam_im, ssm_b_re, ssm_b_im, ssm_c_re, ssm_c_im, ssm_d, ssm_log_dt, ssm_w_glu, conv_w, conv_w_out, q_norm_g, k_norm_g, cmp_pe, cmp_w1, cmp_w2, nsa_w_o, w_out, ffn_norm_g, ffn_w_gate_up, ffn_w_down):
    bsz, seq, d = x.shape
    x2 = x.reshape(bsz * seq, d)
    for i in range(w_in.shape[0]):
        p = dict(
            mix_norm_g=mix_norm_g[i], w_in=_permute_w_in(w_in[i]),
            s5=_s5_tables(ssm_lam_re[i], ssm_lam_im[i], ssm_b_re[i], ssm_b_im[i], ssm_c_re[i], ssm_c_im[i],
                          ssm_d[i], ssm_log_dt[i], seq // S5_CHUNK),
            ssm_w_glu=ssm_w_glu[i].astype(BF16), conv_w=conv_w[i], conv_w_out=conv_w_out[i].astype(BF16),
            q_norm_g=q_norm_g[i], k_norm_g=k_norm_g[i], cmp_pe=cmp_pe[i],
            cmp_w1=cmp_w1[i].astype(BF16), cmp_w2=cmp_w2[i].astype(BF16), nsa_w_o=nsa_w_o[i].astype(BF16),
            w_out=w_out[i].astype(BF16), ffn_norm_g=ffn_norm_g[i],
            ffn_w_gate_up=ffn_w_gate_up[i].astype(BF16), ffn_w_down=ffn_w_down[i].astype(BF16))
        x2 = _layer(x2, p, bsz=bsz, seq=seq)
    return x2.reshape(bsz, seq, d)
```

```python
import functools

import jax
import jax.numpy as jnp
import numpy as np
from jax import lax
from jax.experimental import pallas as pl
from jax.experimental.pallas import tpu as pltpu

F32 = jnp.float32
BF16 = jnp.bfloat16

D_MODEL = 1024
SSM_WIDTH = 512
SSM_GROUP = 16
SSM_GROUPS = SSM_WIDTH // SSM_GROUP
SSM_STATE = 64
CONV_CH = 512
CONV_K = 3
HEAD_DIM = 64
HEAD_SHIFT = 6
N_HEADS = 8
N_KV_HEADS = 2
GQA = N_HEADS // N_KV_HEADS
ATTN_WIDTH = N_HEADS * HEAD_DIM
KV_WIDTH = N_KV_HEADS * HEAD_DIM
CMP_BLOCK = 32
SEL_BLOCK = 64
SEL_SHIFT = 6
N_SELECT = 16
WINDOW = 512
CMP_HIDDEN = 256
FORCE_SCORE = 1e4
NSA_BRANCHES = 3
MIX_BRANCHES = 3
D_FF = 2816
RMS_EPS = 1e-6

LANES = 128
MASKED = -1e30
LOG2E = 1.4426950408889634
S5_CHUNK = 16
S5_TILE_GROUPS = LANES // SSM_GROUP
GATE_PAD = LANES
BF16_SUBLANES = 16
V_ROWS = HEAD_DIM + BF16_SUBLANES

Z_MIX = 0
Z_U = Z_MIX + MIX_BRANCHES * D_MODEL
Z_CB = Z_U + SSM_WIDTH
Z_CC = Z_CB + CONV_CH
Z_CX = Z_CC + CONV_CH
Z_Q = Z_CX + CONV_CH
Z_KC = Z_Q + ATTN_WIDTH
Z_VC = Z_KC + KV_WIDTH
Z_KS = Z_VC + KV_WIDTH
Z_VS = Z_KS + KV_WIDTH
Z_KW = Z_VS + KV_WIDTH
Z_VW = Z_KW + KV_WIDTH
Z_GATE = Z_VW + KV_WIDTH
Z_WIDTH = Z_GATE + GATE_PAD


def _gelu_tanh(x):
    return 0.5 * x * (1.0 + jnp.tanh(np.sqrt(2.0 / np.pi).astype(np.float32) * (x + 0.044715 * (x * x * x))))


def _sigmoid(x):
    return 1.0 / (1.0 + jnp.exp(-x))


def _rms(x, g):
    return x * lax.rsqrt(jnp.mean(x * x, axis=-1, keepdims=True) + RMS_EPS) * g


def _dot(a, b):
    return jnp.dot(a, b, preferred_element_type=F32)


def _dot_nt(a, b):
    return lax.dot_general(a, b, (((1,), (1,)), ((), ())), preferred_element_type=F32)


def _inproj_kernel(x_ref, g_ref, w_ref, o_ref):
    h = _rms(x_ref[...], g_ref[...]).astype(BF16)
    o_ref[...] = _dot(h, w_ref[...])


def _inproj(x2, g, w, *, tm, tn):
    t, d = x2.shape
    n = w.shape[1]
    return pl.pallas_call(
        _inproj_kernel,
        grid=(n // tn, t // tm),
        in_specs=[pl.BlockSpec((tm, d), lambda j, i: (i, 0)),
                  pl.BlockSpec((1, d), lambda j, i: (0, 0)),
                  pl.BlockSpec((d, tn), lambda j, i: (0, j))],
        out_specs=pl.BlockSpec((tm, tn), lambda j, i: (i, j)),
        out_shape=jax.ShapeDtypeStruct((t, n), F32),
        name="inproj",
    )(x2, g.reshape(1, d), w)


def _permute_w_in(w):
    n_plain = SSM_WIDTH + 3 * CONV_CH + ATTN_WIDTH + 6 * KV_WIDTH
    n_gate = N_HEADS * NSA_BRANCHES
    gate = jnp.pad(w[:, n_plain:n_plain + n_gate], ((0, 0), (0, GATE_PAD - n_gate)))
    return jnp.concatenate([w[:, n_plain + n_gate:], w[:, :n_plain], gate], axis=1).astype(BF16)


def _s5_tables(lam_re, lam_im, b_re, b_im, c_re, c_im, d_skip, log_dt, n_chunks):
    hp = lax.Precision.HIGHEST
    g, p = lam_re.shape
    h, l, gt = SSM_GROUP, S5_CHUNK, S5_TILE_GROUPS
    nt = g // gt
    dt = jnp.exp(log_dt)[:, None]
    ar, ai = lam_re * dt, lam_im * dt

    def powers(k):
        mag = jnp.exp(ar[None] * k[:, None, None])
        ang = ai[None] * k[:, None, None]
        return mag * jnp.cos(ang), mag * jnp.sin(ang)

    pr, pi = powers(jnp.arange(l + 1, dtype=F32))
    nr, ni = pr[1] - 1.0, pi[1]
    den = lam_re * lam_re + lam_im * lam_im
    fr, fi = (nr * lam_re + ni * lam_im) / den, (ni * lam_re - nr * lam_im) / den
    bbr = fr[..., None] * b_re - fi[..., None] * b_im
    bbi = fr[..., None] * b_im + fi[..., None] * b_re
    wr = pr[:l, :, :, None] * bbr[None] - pi[:l, :, :, None] * bbi[None]
    wi = pr[:l, :, :, None] * bbi[None] + pi[:l, :, :, None] * bbr[None]
    kern = (jnp.einsum('gop,tgpi->tgio', c_re, wr, precision=hp)
            - jnp.einsum('gop,tgpi->tgio', c_im, wi, precision=hp))
    r = jnp.arange(l)

    def group_diag(x, row_w, col_w):
        rows, w = x.shape[-2:]
        expand = (jnp.arange(w)[:, None] == jnp.arange(gt * w)[None, :] % w).astype(F32)
        tiled = jnp.einsum('...w,wv->...v', x, expand, precision=hp)
        own = (jnp.arange(rows)[:, None] // row_w) % gt == jnp.arange(gt * w)[None, :] // col_w
        return jnp.where(own, tiled, 0.0)

    kblk = group_diag(kern.reshape(l, g * h, h), h, h)
    krev = kblk.reshape(l, nt, LANES, LANES)[::-1].transpose(1, 0, 2, 3).reshape(nt, l * LANES, LANES)
    qr, qi = pr[l - 1 - r], pi[l - 1 - r]
    st_re = qr[..., None] * bbr[None] - qi[..., None] * bbi[None]
    st_im = qr[..., None] * bbi[None] + qi[..., None] * bbr[None]
    rows_in = lambda x: group_diag(x.transpose(0, 1, 3, 2).reshape(l, g * h, p), h, p)
    s_tab = jnp.concatenate([rows_in(st_re), rows_in(st_im)], axis=-1)
    s_tab = s_tab.reshape(l, nt, LANES, 2 * gt * p).transpose(1, 0, 2, 3).reshape(nt, l * LANES, 2 * gt * p)
    er, ei = pr[1:l + 1][:, :, None, :], pi[1:l + 1][:, :, None, :]
    rows_out = lambda x: group_diag(x.transpose(0, 1, 3, 2).reshape(l, g * p, h), p, h).reshape(l, nt, gt * p, LANES)
    c_tab = jnp.concatenate([rows_out(c_re[None] * er - c_im[None] * ei),
                             rows_out(-(c_re[None] * ei + c_im[None] * er))], axis=2)
    n_steps = max(1, int(np.ceil(np.log2(n_chunks))))
    dr, di = powers(l * (2.0 ** jnp.arange(n_steps, dtype=F32)))
    lanes = lambda m: m.reshape(n_steps, nt, gt * p).transpose(1, 0, 2)
    return dict(krev=krev.astype(BF16), s_tab=s_tab.astype(BF16), c_tab=c_tab.astype(BF16),
                d_re=lanes(dr), d_im=lanes(di), d_skip=d_skip.reshape(1, g * h))


def _s5_kernel(u_ref, krev_ref, s_ref, c_ref, dre_ref, dim_ref, dsk_ref, y_ref):
    seq = u_ref.shape[0]
    l = S5_CHUNK
    nc = seq // l
    n_steps = dre_ref.shape[1]
    ns = dre_ref.shape[2]
    us = [u_ref[pl.ds(r, nc, stride=l), :] for r in range(l)]
    ucat = jnp.concatenate(us, axis=1).astype(BF16)
    e = _dot(ucat, s_ref[0])
    xr, xi = e[:, :ns], e[:, ns:]
    row = lax.broadcasted_iota(jnp.int32, xr.shape, 0)
    for k in range(n_steps):
        s = 1 << k
        if s >= nc:
            break
        dr = dre_ref[0, k:k + 1, :]
        di = dim_ref[0, k:k + 1, :]
        sr = jnp.where(row >= s, pltpu.roll(xr, s, 0), 0.0)
        si = jnp.where(row >= s, pltpu.roll(xi, s, 0), 0.0)
        xr, xi = xr + (dr * sr - di * si), xi + (dr * si + di * sr)
    xp = jnp.concatenate([jnp.where(row >= 1, pltpu.roll(xr, 1, 0), 0.0),
                          jnp.where(row >= 1, pltpu.roll(xi, 1, 0), 0.0)], axis=1).astype(BF16)
    dsk = dsk_ref[...]
    for s in range(l):
        y = (_dot(ucat[:, :(s + 1) * LANES], krev_ref[0, (l - 1 - s) * LANES:, :])
             + _dot(xp, c_ref[s, 0]) + dsk * us[s])
        y_ref[pl.ds(s, nc, stride=l), :] = y


def _s5_scan(z, tabs, *, bsz, seq):
    t = z.shape[0]
    nt = SSM_WIDTH // LANES
    l = S5_CHUNK
    tile3 = lambda a: pl.BlockSpec((1,) + a.shape[1:], lambda j, b: (j, 0, 0))
    c_tab = tabs['c_tab']
    return pl.pallas_call(
        _s5_kernel,
        grid=(nt, bsz),
        in_specs=[pl.BlockSpec((seq, LANES), lambda j, b: (b, Z_U // LANES + j)),
                  tile3(tabs['krev']), tile3(tabs['s_tab']),
                  pl.BlockSpec((l, 1) + c_tab.shape[2:], lambda j, b: (0, j, 0, 0)),
                  tile3(tabs['d_re']), tile3(tabs['d_im']),
                  pl.BlockSpec((1, LANES), lambda j, b: (0, j))],
        out_specs=pl.BlockSpec((seq, LANES), lambda j, b: (b, j)),
        out_shape=jax.ShapeDtypeStruct((t, SSM_WIDTH), F32),
        name="s5_scan",
    )(z, tabs['krev'], tabs['s_tab'], c_tab, tabs['d_re'], tabs['d_im'], tabs['d_skip'])


def _kv_prep_kernel(ks_ref, vs_ref, kw_ref, vw_ref, g_ref, kaug_ref, vsel_ref, kwin_ref, vwin_ref, *, seq):
    tp = ks_ref.shape[0]
    pos = (pl.program_id(0) * tp) % seq + lax.broadcasted_iota(jnp.int32, (tp, HEAD_DIM), 0)
    blk = jnp.right_shift(pos, SEL_SHIFT)
    onehot = jnp.where(lax.broadcasted_iota(jnp.int32, (tp, HEAD_DIM), 1) == blk, 1.0, 0.0).astype(BF16)
    zeros = jnp.zeros((tp, HEAD_DIM), BF16)
    for hh in range(N_KV_HEADS):
        sl = slice(hh * HEAD_DIM, (hh + 1) * HEAD_DIM)
        kn = _rms(ks_ref[:, sl], g_ref[0:1, :]).astype(BF16)
        kaug_ref[hh] = jnp.concatenate([kn, onehot], axis=1)
        kn = _rms(kw_ref[:, sl], g_ref[1:2, :]).astype(BF16)
        kwin_ref[hh] = jnp.concatenate([kn, zeros], axis=1)
    for v_ref, vt_ref in ((vs_ref, vsel_ref), (vw_ref, vwin_ref)):
        tk = vt_ref.shape[3]
        vt = v_ref[...].T.astype(BF16)
        ones = jnp.where(lax.broadcasted_iota(jnp.int32, (V_ROWS - HEAD_DIM, tp), 0) == 0, 1.0, 0.0).astype(BF16)
        for hh in range(N_KV_HEADS):
            vh = jnp.concatenate([vt[hh * HEAD_DIM:(hh + 1) * HEAD_DIM], ones], axis=0)
            for j in range(tp // tk):
                vt_ref[hh, j] = vh[:, j * tk:(j + 1) * tk]


def _kv_prep(z, k_norm_g, *, seq, tp, tk_sel, tk_win):
    t = z.shape[0]
    col = lambda off: pl.BlockSpec((tp, KV_WIDTH), lambda i, o=off // KV_WIDTH: (i, o))
    aug = 2 * HEAD_DIM
    kspec = pl.BlockSpec((N_KV_HEADS, tp, aug), lambda i: (0, i, 0))
    vspec = lambda tk: pl.BlockSpec((N_KV_HEADS, tp // tk, V_ROWS, tk), lambda i: (0, i, 0, 0))
    vshape = lambda tk: jax.ShapeDtypeStruct((N_KV_HEADS, t // tk, V_ROWS, tk), BF16)
    return pl.pallas_call(
        functools.partial(_kv_prep_kernel, seq=seq),
        grid=(t // tp,),
        in_specs=[col(Z_KS), col(Z_VS), col(Z_KW), col(Z_VW), pl.BlockSpec((2, HEAD_DIM), lambda i: (0, 0))],
        out_specs=[kspec, vspec(tk_sel), kspec, vspec(tk_win)],
        out_shape=[jax.ShapeDtypeStruct((N_KV_HEADS, t, aug), BF16), vshape(tk_sel),
                   jax.ShapeDtypeStruct((N_KV_HEADS, t, aug), BF16), vshape(tk_win)],
        name="kv_prep",
    )(z, z, z, z, k_norm_g[1:3])


def _compress_kernel(x_ref, pe_ref, w1_ref, w2_ref, g_ref, o_ref):
    seq = x_ref.shape[0]
    nb = seq // (2 * CMP_BLOCK)
    lane = lax.broadcasted_iota(jnp.int32, (nb, 2 * HEAD_DIM), 1)
    rows = []
    for hh in range(N_KV_HEADS):
        for parity in range(2):
            cols = []
            for s in range(0, CMP_BLOCK, 2):
                a = x_ref[pl.ds(parity * CMP_BLOCK + s, nb, stride=2 * CMP_BLOCK), :]
                b = x_ref[pl.ds(parity * CMP_BLOCK + s + 1, nb, stride=2 * CMP_BLOCK), :]
                if hh == 0:
                    cols.append(jnp.where(lane < HEAD_DIM, a, pltpu.roll(b, HEAD_DIM, 1)))
                else:
                    cols.append(jnp.where(lane < HEAD_DIM, pltpu.roll(a, HEAD_DIM, 1), b))
            rows.append(jnp.concatenate(cols, axis=1))
    x = (jnp.concatenate(rows, axis=0) + pe_ref[0]).astype(BF16)
    hid = _gelu_tanh(_dot(x, w1_ref[0])).astype(BF16)
    y = _dot(hid, w2_ref[0])
    o_ref[0, 0] = jnp.where(pl.program_id(0) == 0, _rms(y, g_ref[...]), y)


def _compress(z, cmp_pe, w1, w2, k_gain, *, bsz, seq):
    nc = seq // CMP_BLOCK
    kdim = CMP_BLOCK * HEAD_DIM
    m = N_KV_HEADS * nc
    out = pl.pallas_call(
        _compress_kernel,
        grid=(2, bsz),
        in_specs=[pl.BlockSpec((seq, KV_WIDTH), lambda w, b: (b, Z_KC // KV_WIDTH + w)),
                  pl.BlockSpec((1, 1, kdim), lambda w, b: (w, 0, 0)),
                  pl.BlockSpec((1, kdim, CMP_HIDDEN), lambda w, b: (w, 0, 0)),
                  pl.BlockSpec((1, CMP_HIDDEN, HEAD_DIM), lambda w, b: (w, 0, 0)),
                  pl.BlockSpec((1, HEAD_DIM), lambda w, b: (0, 0))],
        out_specs=pl.BlockSpec((1, 1, m, HEAD_DIM), lambda w, b: (w, b, 0, 0)),
        out_shape=jax.ShapeDtypeStruct((2, bsz, m, HEAD_DIM), F32),
        name="compress",
    )(z, cmp_pe.reshape(2, 1, kdim), w1, w2, k_gain.reshape(1, HEAD_DIM))
    return out.reshape(2, bsz * N_KV_HEADS, nc, HEAD_DIM)


def _head_gate(gl, kvh, g, branch):
    c0 = g * NSA_BRANCHES + branch
    c1 = (GQA + g) * NSA_BRANCHES + branch
    if isinstance(kvh, int):
        c = c1 if kvh else c0
        return _sigmoid(gl[:, c:c + 1])
    return _sigmoid(jnp.where(kvh == 0, gl[:, c0:c0 + 1], gl[:, c1:c1 + 1]))


def _cmp_select_kernel(q_ref, gl_ref, kc_ref, vc_ref, qg_ref, o_ref, qaug_ref, imp_ref):
    tq = q_ref.shape[0]
    nc = kc_ref.shape[1]
    nb = nc // 2
    pair_w = 2 * HEAD_DIM
    qw = GQA * HEAD_DIM
    q0 = pl.program_id(1) * tq
    q = q_ref[...]
    x2 = q * q
    hi = x2.astype(BF16)
    lo = (x2 - hi.astype(F32)).astype(BF16)
    wq = q.shape[1]
    seg = jnp.where(jnp.right_shift(lax.broadcasted_iota(jnp.int32, (wq, wq), 0), HEAD_SHIFT)
                    == jnp.right_shift(lax.broadcasted_iota(jnp.int32, (wq, wq), 1), HEAD_SHIFT), 1.0, 0.0).astype(BF16)
    ss = _dot(hi, seg) + _dot(lo, seg)
    qf = q * lax.rsqrt(ss * (1.0 / HEAD_DIM) + RMS_EPS) * qg_ref[...] * (HEAD_DIM ** -0.5)
    qn = qf.astype(BF16)
    q2 = qf * LOG2E

    row = lax.broadcasted_iota(jnp.int32, (nc, tq), 0)
    t = q0 + lax.broadcasted_iota(jnp.int32, (nc, tq), 1)
    blk = jnp.where(row < nb, 2 * row, 2 * (row - nb) + 1)
    valid = (blk + 1) * CMP_BLOCK - 1 <= t
    lane = lax.broadcasted_iota(jnp.int32, (tq, pair_w), 1)
    j = lax.broadcasted_iota(jnp.int32, (nb, tq), 0)
    cur = jnp.right_shift(q0 + lax.broadcasted_iota(jnp.int32, (nb, tq), 1), SEL_SHIFT)
    forced = (j == 0) | (j == cur) | (j == cur - 1)
    visible = j <= cur
    gl = gl_ref[...]
    out = []
    imps = []
    for kvh in range(N_KV_HEADS):
        kc = kc_ref[kvh].astype(BF16)
        vc = vc_ref[kvh].astype(BF16)
        zk = jnp.zeros_like(kc)
        k2 = jnp.concatenate([jnp.concatenate([kc, zk], axis=1), jnp.concatenate([zk, kc], axis=1)], axis=0)
        psum = jnp.zeros((nc, tq), F32)
        for pr in range(GQA // 2):
            c0 = kvh * qw + pr * pair_w
            st = _dot_nt(k2, qn[:, c0:c0 + pair_w])
            ot = []
            for hh in range(2):
                sm = jnp.where(valid, st[hh * nc:(hh + 1) * nc], MASKED)
                m = jnp.max(sm, axis=0, keepdims=True)
                m = jnp.where(m > 0.5 * MASKED, m, 0.0)
                e = jnp.where(valid, jnp.exp(sm - m), 0.0)
                p = e * (1.0 / jnp.maximum(jnp.sum(e, axis=0, keepdims=True), 1e-30))
                psum = psum + p
                ot.append(lax.dot_general(vc, p.astype(BF16), (((0,), (0,)), ((), ())), preferred_element_type=F32))
            o_pair = jnp.concatenate(ot, axis=0).T
            gate = jnp.where(lane < HEAD_DIM, _head_gate(gl, kvh, 2 * pr, 0), _head_gate(gl, kvh, 2 * pr + 1, 0))
            out.append(o_pair * gate)
        imp = psum[:nb] + psum[nb:]
        imp = jnp.where(forced, FORCE_SCORE, jnp.where(visible, imp, -jnp.inf))
        imp_ref[kvh] = imp
        imps.append(imp)
    o_ref[...] = jnp.concatenate(out, axis=1)

    n_vis = jnp.minimum(nb, (q0 + tq - 1) // SEL_BLOCK + 1)

    def count(i, cnts):
        res = []
        for kvh in range(N_KV_HEADS):
            vi = imp_ref[kvh, pl.ds(i, 1), :]
            beats = (vi > imps[kvh]) | ((vi == imps[kvh]) & (i < j))
            res.append(cnts[kvh] + jnp.where(beats, 1.0, 0.0))
        return tuple(res)

    cnts = lax.fori_loop(0, n_vis, count, tuple(jnp.zeros((nb, tq), F32) for _ in range(N_KV_HEADS)))
    for kvh in range(N_KV_HEADS):
        selneg_t = jnp.where(visible & (cnts[kvh] < float(N_SELECT)), 0.0, MASKED)
        if nb < HEAD_DIM:
            selneg_t = jnp.concatenate([selneg_t, jnp.zeros((HEAD_DIM - nb, tq), F32)], axis=0)
        selneg = jnp.concatenate([selneg_t, selneg_t], axis=0).T
        for pr in range(GQA // 2):
            c0 = kvh * qw + pr * pair_w
            qp = q2[:, c0:c0 + pair_w]
            qaug_ref[kvh, 2 * pr] = jnp.where(lane < HEAD_DIM, qp, selneg).astype(BF16)
            qaug_ref[kvh, 2 * pr + 1] = jnp.where(lane < HEAD_DIM, pltpu.roll(qp, HEAD_DIM, 1), selneg).astype(BF16)


def _cmp_select(z, cmp_kv, q_norm_g, *, bsz, seq, tq):
    t = z.shape[0]
    nq = seq // tq
    nc = seq // CMP_BLOCK
    aug = 2 * HEAD_DIM
    qw = GQA * HEAD_DIM
    kc, vc = cmp_kv[0], cmp_kv[1]
    kv_spec = pl.BlockSpec((N_KV_HEADS, nc, HEAD_DIM), lambda b, i: (b, 0, 0))
    return pl.pallas_call(
        _cmp_select_kernel,
        grid=(bsz, nq),
        in_specs=[pl.BlockSpec((tq, ATTN_WIDTH), lambda b, i: (b * nq + i, Z_Q // ATTN_WIDTH)),
                  pl.BlockSpec((tq, GATE_PAD), lambda b, i: (b * nq + i, Z_GATE // GATE_PAD)),
                  kv_spec, kv_spec,
                  pl.BlockSpec((1, ATTN_WIDTH), lambda b, i: (0, 0))],
        out_specs=[pl.BlockSpec((tq, ATTN_WIDTH), lambda b, i: (b * nq + i, 0)),
                   pl.BlockSpec((N_KV_HEADS, GQA, tq, aug), lambda b, i: (b, 0, i, 0))],
        out_shape=[jax.ShapeDtypeStruct((t, ATTN_WIDTH), F32),
                   jax.ShapeDtypeStruct((bsz * N_KV_HEADS, GQA, seq, aug), BF16)],
        scratch_shapes=[pltpu.VMEM((N_KV_HEADS, nc // 2, tq), F32)],
        name="cmp_select",
    )(z, z, kc, vc, jnp.tile(q_norm_g, N_HEADS).reshape(1, ATTN_WIDTH))


def _flash_kernel(q_ref, k_ref, vt_ref, gl_ref, o_ref, sa_ref, sb_ref, pa_ref, pb_ref, m_ref, a_ref, acc_ref,
                  *, window, branch):
    tq = q_ref.shape[2]
    tk = vt_ref.shape[3]
    cols = GQA * tq
    kvh = pl.program_id(1)
    q0 = pl.program_id(2) * tq
    qa = q_ref[0].reshape(cols, q_ref.shape[3])
    t = q0 + (lax.broadcasted_iota(jnp.int32, (tk, cols), 1) & (tq - 1))
    key_row = lax.broadcasted_iota(jnp.int32, (tk, cols), 0)
    span = 2 * tk
    lo = jnp.maximum(0, q0 - window + 1) // span if window else 0
    hi = (q0 + tq + span - 1) // span
    mid1 = jnp.maximum(lo, (q0 + tq - 1 - window) // span + 1) if window else lo
    mid2 = q0 // span
    first = 2 * lo
    last = 2 * hi - 1

    def scores(kb):
        return _dot_nt(k_ref[0, pl.ds(pl.multiple_of(kb * tk, tk), tk), :], qa)

    m_ref[...] = jnp.full(m_ref.shape, MASKED, F32)
    a_ref[...] = jnp.ones(a_ref.shape, F32)
    acc_ref[...] = jnp.zeros(acc_ref.shape, F32)
    pb_ref[...] = jnp.zeros(pb_ref.shape, BF16)
    sa_ref[...] = scores(first)

    def tile(kb, s_cur, s_nxt, p_prev, p_cur, mask):
        pv = _dot(vt_ref[0, jnp.maximum(kb - 1, first)], p_prev[...])
        s_nxt[...] = scores(jnp.minimum(kb + 1, last))
        s = s_cur[...]
        if mask == 'causal':
            s = jnp.where(kb * tk + key_row <= t, s, MASKED)
        elif mask == 'window':
            s = jnp.where(kb * tk + key_row > t - window, s, MASKED)
        m_old = m_ref[...]
        m_new = jnp.maximum(m_old, jnp.max(s, axis=0, keepdims=True))
        acc_ref[...] = a_ref[...] * acc_ref[...] + pv
        a_ref[...] = jnp.exp2(m_old - m_new)
        p_cur[...] = jnp.exp2(s - m_new).astype(BF16)
        m_ref[...] = m_new

    def run(begin, end, mask):
        def body(pair, carry):
            tile(2 * pair, sa_ref, sb_ref, pb_ref, pa_ref, mask)
            tile(2 * pair + 1, sb_ref, sa_ref, pa_ref, pb_ref, mask)
            return carry
        lax.fori_loop(begin, end, body, 0)

    run(lo, mid1, 'window')
    run(mid1, mid2, None)
    run(mid2, hi, 'causal')
    acc = a_ref[...] * acc_ref[...] + _dot(vt_ref[0, last], pb_ref[...])
    o_t = acc[:HEAD_DIM] / acc[HEAD_DIM:HEAD_DIM + 1]
    gl = gl_ref[...]
    out = []
    for g in range(0, GQA, 2):
        pair = jnp.concatenate([o_t[:, g * tq:(g + 1) * tq], o_t[:, (g + 1) * tq:(g + 2) * tq]], axis=0).T
        out.append(pair[:, :HEAD_DIM] * _head_gate(gl, kvh, g, branch))
        out.append(pair[:, HEAD_DIM:] * _head_gate(gl, kvh, g + 1, branch))
    o_ref[...] = jnp.concatenate(out, axis=1)


def _flash(qaug, k, vt, z, *, bsz, seq, tq, window, branch, name):
    t = z.shape[0]
    nq = seq // tq
    aug = qaug.shape[3]
    tk = vt.shape[3]
    qw = GQA * HEAD_DIM
    cols = GQA * tq
    return pl.pallas_call(
        functools.partial(_flash_kernel, window=window, branch=branch),
        grid=(bsz, N_KV_HEADS, nq),
        in_specs=[pl.BlockSpec((1, GQA, tq, aug), lambda b, h, i: (b * N_KV_HEADS + h, 0, i, 0)),
                  pl.BlockSpec((1, seq, aug), lambda b, h, i: (h, b, 0)),
                  pl.BlockSpec((1, seq // tk, V_ROWS, tk), lambda b, h, i: (h, b, 0, 0)),
                  pl.BlockSpec((tq, GATE_PAD), lambda b, h, i: (b * nq + i, Z_GATE // GATE_PAD))],
        out_specs=pl.BlockSpec((tq, qw), lambda b, h, i: (b * nq + i, h)),
        out_shape=jax.ShapeDtypeStruct((t, ATTN_WIDTH), F32),
        scratch_shapes=[pltpu.VMEM((tk, cols), F32), pltpu.VMEM((tk, cols), F32),
                        pltpu.VMEM((tk, cols), BF16), pltpu.VMEM((tk, cols), BF16),
                        pltpu.VMEM((1, cols), F32), pltpu.VMEM((1, cols), F32),
                        pltpu.VMEM((V_ROWS, cols), F32)],
        name=name,
    )(qaug, k, vt, z)


def _merge_kernel(x_ref, g0_ref, g1_ref, g2_ref, cb_ref, cc_ref, cx_ref, pc_ref, px_ref, ys_ref,
                  oc_ref, os_ref, ow_ref, cw_ref, wc_ref, wglu_ref, wo_ref, wout_ref, out_ref, *, seq):
    tm = x_ref.shape[0]
    yg = _dot(_gelu_tanh(ys_ref[...]).astype(BF16), wglu_ref[...])
    y_ssm = yg[:, :D_MODEL] * _sigmoid(yg[:, D_MODEL:])
    zc = cc_ref[...] * cx_ref[...]
    keep = jnp.where((pl.program_id(0) * tm) % seq != 0, 1.0, 0.0)
    prev = pc_ref[...] * px_ref[...] * keep
    row = lax.broadcasted_iota(jnp.int32, zc.shape, 0)
    z1 = jnp.where(row >= 1, pltpu.roll(zc, 1, 0), prev[7:8, :])
    z2 = jnp.where(row >= 2, pltpu.roll(zc, 2, 0), jnp.where(row == 1, prev[7:8, :], prev[6:7, :]))
    conv = cw_ref[0:1, :] * z2 + cw_ref[1:2, :] * z1 + cw_ref[2:3, :] * zc
    y_conv = _dot((cb_ref[...] * conv).astype(BF16), wc_ref[...])
    y_attn = _dot((oc_ref[...] + os_ref[...] + ow_ref[...]).astype(BF16), wo_ref[...])
    mixed = _sigmoid(g0_ref[...]) * y_ssm + _sigmoid(g1_ref[...]) * y_conv + _sigmoid(g2_ref[...]) * y_attn
    out_ref[...] = x_ref[...] + _dot(mixed.astype(BF16), wout_ref[...])


def _merge(x2, z, ys, o_cmp, o_sel, o_win, conv_w, wc, wglu, wo, wout, *, seq, tm):
    t, d = x2.shape
    rb = tm // 8
    zc = lambda width, off: pl.BlockSpec((tm, width), lambda i, o=off // width: (i, o))
    zprev = lambda off: pl.BlockSpec((8, CONV_CH), lambda i, o=off // CONV_CH: (jnp.maximum(i * rb - 1, 0), o))
    row = lambda width: pl.BlockSpec((tm, width), lambda i: (i, 0))
    full = lambda a: pl.BlockSpec(a.shape, lambda i: (0, 0))
    return pl.pallas_call(
        functools.partial(_merge_kernel, seq=seq),
        grid=(t // tm,),
        in_specs=[row(d), zc(d, Z_MIX), zc(d, Z_MIX + d), zc(d, Z_MIX + 2 * d),
                  zc(CONV_CH, Z_CB), zc(CONV_CH, Z_CC), zc(CONV_CH, Z_CX), zprev(Z_CC), zprev(Z_CX),
                  row(SSM_WIDTH), row(ATTN_WIDTH), row(ATTN_WIDTH), row(ATTN_WIDTH),
                  full(conv_w), full(wc), full(wglu), full(wo), full(wout)],
        out_specs=row(d),
        out_shape=jax.ShapeDtypeStruct((t, d), F32),
        name="merge",
    )(x2, z, z, z, z, z, z, z, z, ys, o_cmp, o_sel, o_win, conv_w, wc, wglu, wo, wout)


def _ffn_kernel(x_ref, g_ref, wg_ref, wu_ref, wd_ref, o_ref, h_ref, acc_ref):
    f = pl.program_id(1)

    @pl.when(f == 0)
    def _():
        h_ref[...] = _rms(x_ref[...], g_ref[...]).astype(BF16)
        acc_ref[...] = jnp.zeros(acc_ref.shape, F32)

    h = h_ref[...]
    gate = _dot(h, wg_ref[...])
    up = _dot(h, wu_ref[...])
    act = (gate * _sigmoid(gate) * up).astype(BF16)
    acc_ref[...] += _dot(act, wd_ref[...])

    @pl.when(f == pl.num_programs(1) - 1)
    def _():
        o_ref[...] = x_ref[...] + acc_ref[...]


def _ffn(x2, g, w_gate_up, w_down, *, tm, tf):
    t, d = x2.shape
    nf = D_FF // tf
    return pl.pallas_call(
        _ffn_kernel,
        grid=(t // tm, nf),
        in_specs=[pl.BlockSpec((tm, d), lambda i, f: (i, 0)),
                  pl.BlockSpec((1, d), lambda i, f: (0, 0)),
                  pl.BlockSpec((d, tf), lambda i, f: (0, f)),
                  pl.BlockSpec((d, tf), lambda i, f: (0, nf + f)),
                  pl.BlockSpec((tf, d), lambda i, f: (f, 0))],
        out_specs=pl.BlockSpec((tm, d), lambda i, f: (i, 0)),
        out_shape=jax.ShapeDtypeStruct((t, d), F32),
        scratch_shapes=[pltpu.VMEM((tm, d), BF16), pltpu.VMEM((tm, d), F32)],
        compiler_params=pltpu.CompilerParams(dimension_semantics=("parallel", "arbitrary")),
        name="ffn",
    )(x2, g.reshape(1, d), w_gate_up, w_gate_up, w_down)


def _pick(n, pref):
    while n % pref:
        pref //= 2
    return pref


def _layer(x2, p, *, bsz, seq):
    t = x2.shape[0]
    z = _inproj(x2, p['mix_norm_g'], p['w_in'], tm=_pick(t, 512), tn=Z_WIDTH // 3)
    ys = _s5_scan(z, p['s5'], bsz=bsz, seq=seq)
    kaug, vsel, kwin, vwin = _kv_prep(z, p['k_norm_g'], seq=seq, tp=_pick(seq, 512), tk_sel=128, tk_win=128)
    cmp_kv = _compress(z, p['cmp_pe'], p['cmp_w1'], p['cmp_w2'], p['k_norm_g'][0], bsz=bsz, seq=seq)
    o_cmp, qaug = _cmp_select(z, cmp_kv, p['q_norm_g'], bsz=bsz, seq=seq, tq=256)
    o_sel = _flash(qaug, kaug, vsel, z, bsz=bsz, seq=seq, tq=256, window=0, branch=1, name="sel_attn")
    o_win = _flash(qaug, kwin, vwin, z, bsz=bsz, seq=seq, tq=256, window=WINDOW, branch=2, name="win_attn")
    x2 = _merge(x2, z, ys, o_cmp, o_sel, o_win, p['conv_w'], p['conv_w_out'], p['ssm_w_glu'], p['nsa_w_o'],
                p['w_out'], seq=seq, tm=_pick(seq, 256))
    return _ffn(x2, p['ffn_norm_g'], p['ffn_w_gate_up'], p['ffn_w_down'], tm=_pick(t, 1024), tf=256)


def kernel(x, mix_norm_g, w_in, ssm_lam_re, ssm_lam_im, ssm_b_re, ssm_b_im, ssm_c_re, ssm_c_im, ssm_d, ssm_log_dt, ssm_w_glu, conv_w, conv_w_out, q_norm_g, k_norm_g, cmp_pe, cmp_w1, cmp_w2, nsa_w_o, w_out, ffn_norm_g, ffn_w_gate_up, ffn_w_down):
    bsz, seq, d = x.shape
    x2 = x.reshape(bsz * seq, d)
    for i in range(w_in.shape[0]):
        p = dict(
            mix_norm_g=mix_norm_g[i], w_in=_permute_w_in(w_in[i]),
            s5=_s5_tables(ssm_lam_re[i], ssm_lam_im[i], ssm_b_re[i], ssm_b_im[i], ssm_c_re[i], ssm_c_im[i],
                          ssm_d[i], ssm_log_dt[i], seq // S5_CHUNK),
            ssm_w_glu=ssm_w_glu[i].astype(BF16), conv_w=conv_w[i], conv_w_out=conv_w_out[i].astype(BF16),
            q_norm_g=q_norm_g[i], k_norm_g=k_norm_g[i], cmp_pe=cmp_pe[i],
            cmp_w1=cmp_w1[i].astype(BF16), cmp_w2=cmp_w2[i].astype(BF16), nsa_w_o=nsa_w_o[i].astype(BF16),
            w_out=w_out[i].astype(BF16), ffn_norm_g=ffn_norm_g[i],
            ffn_w_gate_up=ffn_w_gate_up[i].astype(BF16), ffn_w_down=ffn_w_down[i].astype(BF16))
        x2 = _layer(x2, p, bsz=bsz, seq=seq)
    return x2.reshape(bsz, seq, d)
```

```python
import functools

import jax
import jax.numpy as jnp
import numpy as np
from jax import lax
from jax.experimental import pallas as pl
from jax.experimental.pallas import tpu as pltpu

F32 = jnp.float32
BF16 = jnp.bfloat16

D_MODEL = 1024
SSM_WIDTH = 512
SSM_GROUP = 16
SSM_GROUPS = SSM_WIDTH // SSM_GROUP
SSM_STATE = 64
CONV_CH = 512
CONV_K = 3
HEAD_DIM = 64
HEAD_SHIFT = 6
N_HEADS = 8
N_KV_HEADS = 2
GQA = N_HEADS // N_KV_HEADS
ATTN_WIDTH = N_HEADS * HEAD_DIM
KV_WIDTH = N_KV_HEADS * HEAD_DIM
CMP_BLOCK = 32
SEL_BLOCK = 64
SEL_SHIFT = 6
N_SELECT = 16
WINDOW = 512
CMP_HIDDEN = 256
FORCE_SCORE = 1e4
NSA_BRANCHES = 3
MIX_BRANCHES = 3
D_FF = 2816
RMS_EPS = 1e-6

LANES = 128
MASKED = -1e30
LOG2E = 1.4426950408889634
FLASH_UNROLL = 4
S5_CHUNK = 16
S5_TILE_GROUPS = LANES // SSM_GROUP
GATE_PAD = LANES
BF16_SUBLANES = 16
V_ROWS = HEAD_DIM + BF16_SUBLANES

Z_MIX = 0
Z_U = Z_MIX + MIX_BRANCHES * D_MODEL
Z_CB = Z_U + SSM_WIDTH
Z_CC = Z_CB + CONV_CH
Z_CX = Z_CC + CONV_CH
Z_Q = Z_CX + CONV_CH
Z_KC = Z_Q + ATTN_WIDTH
Z_VC = Z_KC + KV_WIDTH
Z_KS = Z_VC + KV_WIDTH
Z_VS = Z_KS + KV_WIDTH
Z_KW = Z_VS + KV_WIDTH
Z_VW = Z_KW + KV_WIDTH
Z_GATE = Z_VW + KV_WIDTH
Z_WIDTH = Z_GATE + GATE_PAD


def _gelu_tanh(x):
    return 0.5 * x * (1.0 + jnp.tanh(np.sqrt(2.0 / np.pi).astype(np.float32) * (x + 0.044715 * (x * x * x))))


def _sigmoid(x):
    return 1.0 / (1.0 + jnp.exp(-x))


def _rms(x, g):
    return x * lax.rsqrt(jnp.mean(x * x, axis=-1, keepdims=True) + RMS_EPS) * g


def _dot(a, b):
    return jnp.dot(a, b, preferred_element_type=F32)


def _dot_nt(a, b):
    return lax.dot_general(a, b, (((1,), (1,)), ((), ())), preferred_element_type=F32)


def _inproj_kernel(x_ref, g_ref, w_ref, o_ref):
    h = _rms(x_ref[...], g_ref[...]).astype(BF16)
    o_ref[...] = _dot(h, w_ref[...])


def _inproj(x2, g, w, *, tm, tn):
    t, d = x2.shape
    n = w.shape[1]
    return pl.pallas_call(
        _inproj_kernel,
        grid=(n // tn, t // tm),
        in_specs=[pl.BlockSpec((tm, d), lambda j, i: (i, 0)),
                  pl.BlockSpec((1, d), lambda j, i: (0, 0)),
                  pl.BlockSpec((d, tn), lambda j, i: (0, j))],
        out_specs=pl.BlockSpec((tm, tn), lambda j, i: (i, j)),
        out_shape=jax.ShapeDtypeStruct((t, n), F32),
        name="inproj",
    )(x2, g.reshape(1, d), w)


def _permute_w_in(w):
    n_plain = SSM_WIDTH + 3 * CONV_CH + ATTN_WIDTH + 6 * KV_WIDTH
    n_gate = N_HEADS * NSA_BRANCHES
    gate = jnp.pad(w[:, n_plain:n_plain + n_gate], ((0, 0), (0, GATE_PAD - n_gate)))
    return jnp.concatenate([w[:, n_plain + n_gate:], w[:, :n_plain], gate], axis=1).astype(BF16)


def _s5_tables(lam_re, lam_im, b_re, b_im, c_re, c_im, d_skip, log_dt, n_chunks):
    hp = lax.Precision.HIGHEST
    g, p = lam_re.shape
    h, l, gt = SSM_GROUP, S5_CHUNK, S5_TILE_GROUPS
    nt = g // gt
    dt = jnp.exp(log_dt)[:, None]
    ar, ai = lam_re * dt, lam_im * dt

    def powers(k):
        mag = jnp.exp(ar[None] * k[:, None, None])
        ang = ai[None] * k[:, None, None]
        return mag * jnp.cos(ang), mag * jnp.sin(ang)

    pr, pi = powers(jnp.arange(l + 1, dtype=F32))
    nr, ni = pr[1] - 1.0, pi[1]
    den = lam_re * lam_re + lam_im * lam_im
    fr, fi = (nr * lam_re + ni * lam_im) / den, (ni * lam_re - nr * lam_im) / den
    bbr = fr[..., None] * b_re - fi[..., None] * b_im
    bbi = fr[..., None] * b_im + fi[..., None] * b_re
    wr = pr[:l, :, :, None] * bbr[None] - pi[:l, :, :, None] * bbi[None]
    wi = pr[:l, :, :, None] * bbi[None] + pi[:l, :, :, None] * bbr[None]
    kern = (jnp.einsum('gop,tgpi->tgio', c_re, wr, precision=hp)
            - jnp.einsum('gop,tgpi->tgio', c_im, wi, precision=hp))
    r = jnp.arange(l)

    def group_diag(x, row_w, col_w):
        rows, w = x.shape[-2:]
        expand = (jnp.arange(w)[:, None] == jnp.arange(gt * w)[None, :] % w).astype(BF16)
        tiled = jnp.einsum('...w,wv->...v', x.astype(BF16), expand, preferred_element_type=BF16)
        own = (jnp.arange(rows)[:, None] // row_w) % gt == jnp.arange(gt * w)[None, :] // col_w
        return jnp.where(own, tiled, jnp.zeros((), BF16))

    kblk = group_diag(kern.reshape(l, g * h, h), h, h)
    krev = kblk.reshape(l, nt, LANES, LANES)[::-1].transpose(1, 0, 2, 3).reshape(nt, l * LANES, LANES)
    qr, qi = pr[l - 1 - r], pi[l - 1 - r]
    st_re = qr[..., None] * bbr[None] - qi[..., None] * bbi[None]
    st_im = qr[..., None] * bbi[None] + qi[..., None] * bbr[None]
    rows_in = lambda x: group_diag(x.transpose(0, 1, 3, 2).reshape(l, g * h, p), h, p)
    s_tab = jnp.concatenate([rows_in(st_re), rows_in(st_im)], axis=-1)
    s_tab = s_tab.reshape(l, nt, LANES, 2 * gt * p).transpose(1, 0, 2, 3).reshape(nt, l * LANES, 2 * gt * p)
    er, ei = pr[1:l + 1][:, :, None, :], pi[1:l + 1][:, :, None, :]
    rows_out = lambda x: group_diag(x.transpose(0, 1, 3, 2).reshape(l, g * p, h), p, h).reshape(l, nt, gt * p, LANES)
    c_tab = jnp.concatenate([rows_out(c_re[None] * er - c_im[None] * ei),
                             rows_out(-(c_re[None] * ei + c_im[None] * er))], axis=2)
    n_steps = max(1, int(np.ceil(np.log2(n_chunks))))
    dr, di = powers(l * (2.0 ** jnp.arange(n_steps, dtype=F32)))
    lanes = lambda m: m.reshape(n_steps, nt, gt * p).transpose(1, 0, 2)
    return dict(krev=krev, s_tab=s_tab, c_tab=c_tab, d_re=lanes(dr), d_im=lanes(di), d_skip=d_skip.reshape(1, g * h))


def _s5_kernel(u_ref, krev_ref, s_ref, c_ref, dre_ref, dim_ref, dsk_ref, y_ref):
    seq = u_ref.shape[0]
    l = S5_CHUNK
    nc = seq // l
    n_steps = dre_ref.shape[1]
    ns = dre_ref.shape[2]
    us = [u_ref[pl.ds(r, nc, stride=l), :] for r in range(l)]
    ucat = jnp.concatenate(us, axis=1).astype(BF16)
    e = _dot(ucat, s_ref[0])
    xr, xi = e[:, :ns], e[:, ns:]
    row = lax.broadcasted_iota(jnp.int32, xr.shape, 0)
    for k in range(n_steps):
        s = 1 << k
        if s >= nc:
            break
        dr = dre_ref[0, k:k + 1, :]
        di = dim_ref[0, k:k + 1, :]
        sr = jnp.where(row >= s, pltpu.roll(xr, s, 0), 0.0)
        si = jnp.where(row >= s, pltpu.roll(xi, s, 0), 0.0)
        xr, xi = xr + (dr * sr - di * si), xi + (dr * si + di * sr)
    xp = jnp.concatenate([jnp.where(row >= 1, pltpu.roll(xr, 1, 0), 0.0),
                          jnp.where(row >= 1, pltpu.roll(xi, 1, 0), 0.0)], axis=1).astype(BF16)
    dsk = dsk_ref[...]
    for s in range(l):
        y = (_dot(ucat[:, :(s + 1) * LANES], krev_ref[0, (l - 1 - s) * LANES:, :])
             + _dot(xp, c_ref[s, 0]) + dsk * us[s])
        y_ref[pl.ds(s, nc, stride=l), :] = y


def _s5_scan(z, tabs, *, bsz, seq):
    t = z.shape[0]
    nt = SSM_WIDTH // LANES
    l = S5_CHUNK
    tile3 = lambda a: pl.BlockSpec((1,) + a.shape[1:], lambda j, b: (j, 0, 0))
    c_tab = tabs['c_tab']
    return pl.pallas_call(
        _s5_kernel,
        grid=(nt, bsz),
        in_specs=[pl.BlockSpec((seq, LANES), lambda j, b: (b, Z_U // LANES + j)),
                  tile3(tabs['krev']), tile3(tabs['s_tab']),
                  pl.BlockSpec((l, 1) + c_tab.shape[2:], lambda j, b: (0, j, 0, 0)),
                  tile3(tabs['d_re']), tile3(tabs['d_im']),
                  pl.BlockSpec((1, LANES), lambda j, b: (0, j))],
        out_specs=pl.BlockSpec((seq, LANES), lambda j, b: (b, j)),
        out_shape=jax.ShapeDtypeStruct((t, SSM_WIDTH), F32),
        name="s5_scan",
    )(z, tabs['krev'], tabs['s_tab'], c_tab, tabs['d_re'], tabs['d_im'], tabs['d_skip'])


def _kv_prep_kernel(ks_ref, vs_ref, kw_ref, vw_ref, g_ref, kaug_ref, vsel_ref, kwin_ref, vwin_ref, *, seq):
    tp = ks_ref.shape[0]
    pos = (pl.program_id(0) * tp) % seq + lax.broadcasted_iota(jnp.int32, (tp, HEAD_DIM), 0)
    blk = jnp.right_shift(pos, SEL_SHIFT)
    onehot = jnp.where(lax.broadcasted_iota(jnp.int32, (tp, HEAD_DIM), 1) == blk, 1.0, 0.0).astype(BF16)
    zeros = jnp.zeros((tp, HEAD_DIM), BF16)
    for hh in range(N_KV_HEADS):
        sl = slice(hh * HEAD_DIM, (hh + 1) * HEAD_DIM)
        kn = _rms(ks_ref[:, sl], g_ref[0:1, :]).astype(BF16)
        kaug_ref[hh] = jnp.concatenate([kn, onehot], axis=1)
        kn = _rms(kw_ref[:, sl], g_ref[1:2, :]).astype(BF16)
        kwin_ref[hh] = jnp.concatenate([kn, zeros], axis=1)
    for v_ref, vt_ref in ((vs_ref, vsel_ref), (vw_ref, vwin_ref)):
        tk = vt_ref.shape[3]
        vt = v_ref[...].T.astype(BF16)
        ones = jnp.where(lax.broadcasted_iota(jnp.int32, (V_ROWS - HEAD_DIM, tp), 0) == 0, 1.0, 0.0).astype(BF16)
        for hh in range(N_KV_HEADS):
            vh = jnp.concatenate([vt[hh * HEAD_DIM:(hh + 1) * HEAD_DIM], ones], axis=0)
            for j in range(tp // tk):
                vt_ref[hh, j] = vh[:, j * tk:(j + 1) * tk]


def _kv_prep(z, k_norm_g, *, seq, tp, tk_sel, tk_win):
    t = z.shape[0]
    col = lambda off: pl.BlockSpec((tp, KV_WIDTH), lambda i, o=off // KV_WIDTH: (i, o))
    aug = 2 * HEAD_DIM
    kspec = pl.BlockSpec((N_KV_HEADS, tp, aug), lambda i: (0, i, 0))
    vspec = lambda tk: pl.BlockSpec((N_KV_HEADS, tp // tk, V_ROWS, tk), lambda i: (0, i, 0, 0))
    vshape = lambda tk: jax.ShapeDtypeStruct((N_KV_HEADS, t // tk, V_ROWS, tk), BF16)
    return pl.pallas_call(
        functools.partial(_kv_prep_kernel, seq=seq),
        grid=(t // tp,),
        in_specs=[col(Z_KS), col(Z_VS), col(Z_KW), col(Z_VW), pl.BlockSpec((2, HEAD_DIM), lambda i: (0, 0))],
        out_specs=[kspec, vspec(tk_sel), kspec, vspec(tk_win)],
        out_shape=[jax.ShapeDtypeStruct((N_KV_HEADS, t, aug), BF16), vshape(tk_sel),
                   jax.ShapeDtypeStruct((N_KV_HEADS, t, aug), BF16), vshape(tk_win)],
        name="kv_prep",
    )(z, z, z, z, k_norm_g[1:3])


def _compress_kernel(x_ref, pe_ref, w1_ref, w2_ref, g_ref, o_ref):
    seq = x_ref.shape[0]
    nb = seq // (2 * CMP_BLOCK)
    lane = lax.broadcasted_iota(jnp.int32, (nb, 2 * HEAD_DIM), 1)
    rows = []
    for hh in range(N_KV_HEADS):
        for parity in range(2):
            cols = []
            for s in range(0, CMP_BLOCK, 2):
                a = x_ref[pl.ds(parity * CMP_BLOCK + s, nb, stride=2 * CMP_BLOCK), :]
                b = x_ref[pl.ds(parity * CMP_BLOCK + s + 1, nb, stride=2 * CMP_BLOCK), :]
                if hh == 0:
                    cols.append(jnp.where(lane < HEAD_DIM, a, pltpu.roll(b, HEAD_DIM, 1)))
                else:
                    cols.append(jnp.where(lane < HEAD_DIM, pltpu.roll(a, HEAD_DIM, 1), b))
            rows.append(jnp.concatenate(cols, axis=1))
    x = (jnp.concatenate(rows, axis=0) + pe_ref[0]).astype(BF16)
    hid = _gelu_tanh(_dot(x, w1_ref[0])).astype(BF16)
    y = _dot(hid, w2_ref[0])
    o_ref[0, 0] = jnp.where(pl.program_id(0) == 0, _rms(y, g_ref[...]), y)


def _compress(z, cmp_pe, w1, w2, k_gain, *, bsz, seq):
    nc = seq // CMP_BLOCK
    kdim = CMP_BLOCK * HEAD_DIM
    m = N_KV_HEADS * nc
    out = pl.pallas_call(
        _compress_kernel,
        grid=(2, bsz),
        in_specs=[pl.BlockSpec((seq, KV_WIDTH), lambda w, b: (b, Z_KC // KV_WIDTH + w)),
                  pl.BlockSpec((1, 1, kdim), lambda w, b: (w, 0, 0)),
                  pl.BlockSpec((1, kdim, CMP_HIDDEN), lambda w, b: (w, 0, 0)),
                  pl.BlockSpec((1, CMP_HIDDEN, HEAD_DIM), lambda w, b: (w, 0, 0)),
                  pl.BlockSpec((1, HEAD_DIM), lambda w, b: (0, 0))],
        out_specs=pl.BlockSpec((1, 1, m, HEAD_DIM), lambda w, b: (w, b, 0, 0)),
        out_shape=jax.ShapeDtypeStruct((2, bsz, m, HEAD_DIM), F32),
        name="compress",
    )(z, cmp_pe.reshape(2, 1, kdim), w1, w2, k_gain.reshape(1, HEAD_DIM))
    return out.reshape(2, bsz * N_KV_HEADS, nc, HEAD_DIM)


def _head_gate(gl, kvh, g, branch):
    c0 = g * NSA_BRANCHES + branch
    c1 = (GQA + g) * NSA_BRANCHES + branch
    if isinstance(kvh, int):
        c = c1 if kvh else c0
        return _sigmoid(gl[:, c:c + 1])
    return _sigmoid(jnp.where(kvh == 0, gl[:, c0:c0 + 1], gl[:, c1:c1 + 1]))


def _cmp_select_kernel(q_ref, gl_ref, kc_ref, vc_ref, qg_ref, o_ref, qaug_ref, imp_ref):
    tq = q_ref.shape[0]
    nc = kc_ref.shape[1]
    nb = nc // 2
    pair_w = 2 * HEAD_DIM
    qw = GQA * HEAD_DIM
    q0 = pl.program_id(1) * tq
    q = q_ref[...]
    x2 = q * q
    hi = x2.astype(BF16)
    lo = (x2 - hi.astype(F32)).astype(BF16)
    wq = q.shape[1]
    seg = jnp.where(jnp.right_shift(lax.broadcasted_iota(jnp.int32, (wq, wq), 0), HEAD_SHIFT)
                    == jnp.right_shift(lax.broadcasted_iota(jnp.int32, (wq, wq), 1), HEAD_SHIFT), 1.0, 0.0).astype(BF16)
    ss = _dot(hi, seg) + _dot(lo, seg)
    qf = q * lax.rsqrt(ss * (1.0 / HEAD_DIM) + RMS_EPS) * qg_ref[...] * (HEAD_DIM ** -0.5)
    qn = qf.astype(BF16)
    q2 = qf * LOG2E

    row = lax.broadcasted_iota(jnp.int32, (nc, tq), 0)
    t = q0 + lax.broadcasted_iota(jnp.int32, (nc, tq), 1)
    blk = jnp.where(row < nb, 2 * row, 2 * (row - nb) + 1)
    valid = (blk + 1) * CMP_BLOCK - 1 <= t
    lane = lax.broadcasted_iota(jnp.int32, (tq, pair_w), 1)
    j = lax.broadcasted_iota(jnp.int32, (nb, tq), 0)
    cur = jnp.right_shift(q0 + lax.broadcasted_iota(jnp.int32, (nb, tq), 1), SEL_SHIFT)
    forced = (j == 0) | (j == cur) | (j == cur - 1)
    visible = j <= cur
    gl = gl_ref[...]
    out = []
    imps = []
    for kvh in range(N_KV_HEADS):
        kc = kc_ref[kvh].astype(BF16)
        vc = vc_ref[kvh].astype(BF16)
        zk = jnp.zeros_like(kc)
        k2 = jnp.concatenate([jnp.concatenate([kc, zk], axis=1), jnp.concatenate([zk, kc], axis=1)], axis=0)
        psum = jnp.zeros((nc, tq), F32)
        for pr in range(GQA // 2):
            c0 = kvh * qw + pr * pair_w
            st = _dot_nt(k2, qn[:, c0:c0 + pair_w])
            ot = []
            for hh in range(2):
                sm = jnp.where(valid, st[hh * nc:(hh + 1) * nc], MASKED)
                m = jnp.max(sm, axis=0, keepdims=True)
                m = jnp.where(m > 0.5 * MASKED, m, 0.0)
                e = jnp.exp(sm - m)
                p = e * (1.0 / jnp.maximum(jnp.sum(e, axis=0, keepdims=True), 1e-30))
                psum = psum + p
                ot.append(lax.dot_general(vc, p.astype(BF16), (((0,), (0,)), ((), ())), preferred_element_type=F32))
            o_pair = jnp.concatenate(ot, axis=0).T
            gate = jnp.where(lane < HEAD_DIM, _head_gate(gl, kvh, 2 * pr, 0), _head_gate(gl, kvh, 2 * pr + 1, 0))
            out.append(o_pair * gate)
        imp = psum[:nb] + psum[nb:]
        imp = jnp.where(forced, FORCE_SCORE, jnp.where(visible, imp, -jnp.inf))
        imp_ref[kvh] = imp
        imps.append(imp)
    o_ref[...] = jnp.concatenate(out, axis=1)

    n_vis = jnp.minimum(nb, (q0 + tq - 1) // SEL_BLOCK + 1)

    def count(i, cnts):
        res = []
        for kvh in range(N_KV_HEADS):
            vi = imp_ref[kvh, pl.ds(i, 1), :]
            beats = (vi > imps[kvh]) | ((vi == imps[kvh]) & (i < j))
            res.append(cnts[kvh] + jnp.where(beats, 1.0, 0.0))
        return tuple(res)

    cnts = lax.fori_loop(0, n_vis, count, tuple(jnp.zeros((nb, tq), F32) for _ in range(N_KV_HEADS)))
    for kvh in range(N_KV_HEADS):
        selneg_t = jnp.where(visible & (cnts[kvh] < float(N_SELECT)), 0.0, MASKED)
        if nb < HEAD_DIM:
            selneg_t = jnp.concatenate([selneg_t, jnp.zeros((HEAD_DIM - nb, tq), F32)], axis=0)
        selneg = jnp.concatenate([selneg_t, selneg_t], axis=0).T
        for pr in range(GQA // 2):
            c0 = kvh * qw + pr * pair_w
            qp = q2[:, c0:c0 + pair_w]
            qaug_ref[kvh, 2 * pr] = jnp.where(lane < HEAD_DIM, qp, selneg).astype(BF16)
            qaug_ref[kvh, 2 * pr + 1] = jnp.where(lane < HEAD_DIM, pltpu.roll(qp, HEAD_DIM, 1), selneg).astype(BF16)


def _cmp_select(z, cmp_kv, q_norm_g, *, bsz, seq, tq):
    t = z.shape[0]
    nq = seq // tq
    nc = seq // CMP_BLOCK
    aug = 2 * HEAD_DIM
    qw = GQA * HEAD_DIM
    kc, vc = cmp_kv[0], cmp_kv[1]
    kv_spec = pl.BlockSpec((N_KV_HEADS, nc, HEAD_DIM), lambda b, i: (b, 0, 0))
    return pl.pallas_call(
        _cmp_select_kernel,
        grid=(bsz, nq),
        in_specs=[pl.BlockSpec((tq, ATTN_WIDTH), lambda b, i: (b * nq + i, Z_Q // ATTN_WIDTH)),
                  pl.BlockSpec((tq, GATE_PAD), lambda b, i: (b * nq + i, Z_GATE // GATE_PAD)),
                  kv_spec, kv_spec,
                  pl.BlockSpec((1, ATTN_WIDTH), lambda b, i: (0, 0))],
        out_specs=[pl.BlockSpec((tq, ATTN_WIDTH), lambda b, i: (b * nq + i, 0)),
                   pl.BlockSpec((N_KV_HEADS, GQA, tq, aug), lambda b, i: (b, 0, i, 0))],
        out_shape=[jax.ShapeDtypeStruct((t, ATTN_WIDTH), F32),
                   jax.ShapeDtypeStruct((bsz * N_KV_HEADS, GQA, seq, aug), BF16)],
        scratch_shapes=[pltpu.VMEM((N_KV_HEADS, nc // 2, tq), F32)],
        name="cmp_select",
    )(z, z, kc, vc, jnp.tile(q_norm_g, N_HEADS).reshape(1, ATTN_WIDTH))


def _flash_kernel(q_ref, k_ref, vt_ref, gl_ref, o_ref, sa_ref, sb_ref, pa_ref, pb_ref, m_ref, a_ref, acc_ref,
                  *, window, branch):
    tq = q_ref.shape[2]
    tk = vt_ref.shape[3]
    cols = GQA * tq
    kvh = pl.program_id(1)
    q0 = pl.program_id(2) * tq
    qa = q_ref[0].reshape(cols, q_ref.shape[3])
    t = q0 + (lax.broadcasted_iota(jnp.int32, (tk, cols), 1) & (tq - 1))
    key_row = lax.broadcasted_iota(jnp.int32, (tk, cols), 0)
    span = 2 * tk
    lo = jnp.maximum(0, q0 - window + 1) // span if window else 0
    hi = (q0 + tq + span - 1) // span
    mid1 = jnp.maximum(lo, (q0 + tq - 1 - window) // span + 1) if window else lo
    mid2 = q0 // span
    first = 2 * lo
    last = 2 * hi - 1

    def scores(kb):
        return _dot_nt(k_ref[0, pl.ds(pl.multiple_of(kb * tk, tk), tk), :], qa)

    m_ref[...] = jnp.full(m_ref.shape, MASKED, F32)
    a_ref[...] = jnp.ones(a_ref.shape, F32)
    acc_ref[...] = jnp.zeros(acc_ref.shape, F32)
    pb_ref[...] = jnp.zeros(pb_ref.shape, BF16)
    sa_ref[...] = scores(first)

    def tile(kb, s_cur, s_nxt, p_prev, p_cur, mask):
        pv = _dot(vt_ref[0, jnp.maximum(kb - 1, first)], p_prev[...])
        s_nxt[...] = scores(jnp.minimum(kb + 1, last))
        s = s_cur[...]
        if mask == 'causal':
            s = jnp.where(kb * tk + key_row <= t, s, MASKED)
        elif mask == 'window':
            s = jnp.where(kb * tk + key_row > t - window, s, MASKED)
        m_old = m_ref[...]
        m_new = jnp.maximum(m_old, jnp.max(s, axis=0, keepdims=True))
        acc_ref[...] = a_ref[...] * acc_ref[...] + pv
        a_ref[...] = jnp.exp2(m_old - m_new)
        p_cur[...] = jnp.exp2(s - m_new).astype(BF16)
        m_ref[...] = m_new

    def pair_step(pair, mask):
        tile(2 * pair, sa_ref, sb_ref, pb_ref, pa_ref, mask)
        tile(2 * pair + 1, sb_ref, sa_ref, pa_ref, pb_ref, mask)

    def run(begin, end, mask, unroll=1):
        n_main = (end - begin) // unroll

        def main(i, carry):
            for u in range(unroll):
                pair_step(begin + i * unroll + u, mask)
            return carry

        def rest(pair, carry):
            pair_step(pair, mask)
            return carry

        if unroll > 1:
            lax.fori_loop(0, n_main, main, 0)
        lax.fori_loop(begin + n_main * unroll if unroll > 1 else begin, end, rest, 0)

    if window:
        interior = (mid1 - lo == 1) & (mid2 - mid1 == 1) & (hi - mid2 == 1)

        @pl.when(interior)
        def _():
            pair_step(lo, 'window')
            pair_step(lo + 1, None)
            pair_step(lo + 2, 'causal')

        @pl.when(jnp.logical_not(interior))
        def _():
            run(lo, mid1, 'window')
            run(mid1, mid2, None)
            run(mid2, hi, 'causal')
    else:
        run(mid1, mid2, None, unroll=FLASH_UNROLL)
        run(mid2, hi, 'causal')
    acc = a_ref[...] * acc_ref[...] + _dot(vt_ref[0, last], pb_ref[...])
    o_t = acc[:HEAD_DIM] / acc[HEAD_DIM:HEAD_DIM + 1]
    gl = gl_ref[...]
    out = []
    for g in range(0, GQA, 2):
        pair = jnp.concatenate([o_t[:, g * tq:(g + 1) * tq], o_t[:, (g + 1) * tq:(g + 2) * tq]], axis=0).T
        out.append(pair[:, :HEAD_DIM] * _head_gate(gl, kvh, g, branch))
        out.append(pair[:, HEAD_DIM:] * _head_gate(gl, kvh, g + 1, branch))
    o_ref[...] = jnp.concatenate(out, axis=1)


def _flash(qaug, k, vt, z, *, bsz, seq, tq, window, branch, name):
    t = z.shape[0]
    nq = seq // tq
    aug = qaug.shape[3]
    tk = vt.shape[3]
    qw = GQA * HEAD_DIM
    cols = GQA * tq
    return pl.pallas_call(
        functools.partial(_flash_kernel, window=window, branch=branch),
        grid=(bsz, N_KV_HEADS, nq),
        in_specs=[pl.BlockSpec((1, GQA, tq, aug), lambda b, h, i: (b * N_KV_HEADS + h, 0, i, 0)),
                  pl.BlockSpec((1, seq, aug), lambda b, h, i: (h, b, 0)),
                  pl.BlockSpec((1, seq // tk, V_ROWS, tk), lambda b, h, i: (h, b, 0, 0)),
                  pl.BlockSpec((tq, GATE_PAD), lambda b, h, i: (b * nq + i, Z_GATE // GATE_PAD))],
        out_specs=pl.BlockSpec((tq, qw), lambda b, h, i: (b * nq + i, h)),
        out_shape=jax.ShapeDtypeStruct((t, ATTN_WIDTH), F32),
        scratch_shapes=[pltpu.VMEM((tk, cols), F32), pltpu.VMEM((tk, cols), F32),
                        pltpu.VMEM((tk, cols), BF16), pltpu.VMEM((tk, cols), BF16),
                        pltpu.VMEM((1, cols), F32), pltpu.VMEM((1, cols), F32),
                        pltpu.VMEM((V_ROWS, cols), F32)],
        name=name,
    )(qaug, k, vt, z)


def _merge_kernel(x_ref, g0_ref, g1_ref, g2_ref, cb_ref, cc_ref, cx_ref, pc_ref, px_ref, ys_ref,
                  oc_ref, os_ref, ow_ref, cw_ref, wc_ref, wglu_ref, wo_ref, wout_ref, out_ref, *, seq):
    tm = x_ref.shape[0]
    yg = _dot(_gelu_tanh(ys_ref[...]).astype(BF16), wglu_ref[...])
    y_ssm = yg[:, :D_MODEL] * _sigmoid(yg[:, D_MODEL:])
    zc = cc_ref[...] * cx_ref[...]
    keep = jnp.where((pl.program_id(0) * tm) % seq != 0, 1.0, 0.0)
    prev = pc_ref[...] * px_ref[...] * keep
    row = lax.broadcasted_iota(jnp.int32, zc.shape, 0)
    z1 = jnp.where(row >= 1, pltpu.roll(zc, 1, 0), prev[7:8, :])
    z2 = jnp.where(row >= 2, pltpu.roll(zc, 2, 0), jnp.where(row == 1, prev[7:8, :], prev[6:7, :]))
    conv = cw_ref[0:1, :] * z2 + cw_ref[1:2, :] * z1 + cw_ref[2:3, :] * zc
    y_conv = _dot((cb_ref[...] * conv).astype(BF16), wc_ref[...])
    y_attn = _dot((oc_ref[...] + os_ref[...] + ow_ref[...]).astype(BF16), wo_ref[...])
    mixed = _sigmoid(g0_ref[...]) * y_ssm + _sigmoid(g1_ref[...]) * y_conv + _sigmoid(g2_ref[...]) * y_attn
    out_ref[...] = x_ref[...] + _dot(mixed.astype(BF16), wout_ref[...])


def _merge(x2, z, ys, o_cmp, o_sel, o_win, conv_w, wc, wglu, wo, wout, *, seq, tm):
    t, d = x2.shape
    rb = tm // 8
    zc = lambda width, off: pl.BlockSpec((tm, width), lambda i, o=off // width: (i, o))
    zprev = lambda off: pl.BlockSpec((8, CONV_CH), lambda i, o=off // CONV_CH: (jnp.maximum(i * rb - 1, 0), o))
    row = lambda width: pl.BlockSpec((tm, width), lambda i: (i, 0))
    full = lambda a: pl.BlockSpec(a.shape, lambda i: (0, 0))
    return pl.pallas_call(
        functools.partial(_merge_kernel, seq=seq),
        grid=(t // tm,),
        in_specs=[row(d), zc(d, Z_MIX), zc(d, Z_MIX + d), zc(d, Z_MIX + 2 * d),
                  zc(CONV_CH, Z_CB), zc(CONV_CH, Z_CC), zc(CONV_CH, Z_CX), zprev(Z_CC), zprev(Z_CX),
                  row(SSM_WIDTH), row(ATTN_WIDTH), row(ATTN_WIDTH), row(ATTN_WIDTH),
                  full(conv_w), full(wc), full(wglu), full(wo), full(wout)],
        out_specs=row(d),
        out_shape=jax.ShapeDtypeStruct((t, d), F32),
        name="merge",
    )(x2, z, z, z, z, z, z, z, z, ys, o_cmp, o_sel, o_win, conv_w, wc, wglu, wo, wout)


def _ffn_kernel(x_ref, g_ref, wg_ref, wu_ref, wd_ref, o_ref, h_ref, acc_ref):
    f = pl.program_id(1)

    @pl.when(f == 0)
    def _():
        h_ref[...] = _rms(x_ref[...], g_ref[...]).astype(BF16)
        acc_ref[...] = jnp.zeros(acc_ref.shape, F32)

    h = h_ref[...]
    gate = _dot(h, wg_ref[...])
    up = _dot(h, wu_ref[...])
    act = (gate * _sigmoid(gate) * up).astype(BF16)
    acc_ref[...] += _dot(act, wd_ref[...])

    @pl.when(f == pl.num_programs(1) - 1)
    def _():
        o_ref[...] = x_ref[...] + acc_ref[...]


def _ffn(x2, g, w_gate_up, w_down, *, tm, tf):
    t, d = x2.shape
    nf = D_FF // tf
    return pl.pallas_call(
        _ffn_kernel,
        grid=(t // tm, nf),
        in_specs=[pl.BlockSpec((tm, d), lambda i, f: (i, 0)),
                  pl.BlockSpec((1, d), lambda i, f: (0, 0)),
                  pl.BlockSpec((d, tf), lambda i, f: (0, f)),
                  pl.BlockSpec((d, tf), lambda i, f: (0, nf + f)),
                  pl.BlockSpec((tf, d), lambda i, f: (f, 0))],
        out_specs=pl.BlockSpec((tm, d), lambda i, f: (i, 0)),
        out_shape=jax.ShapeDtypeStruct((t, d), F32),
        scratch_shapes=[pltpu.VMEM((tm, d), BF16), pltpu.VMEM((tm, d), F32)],
        compiler_params=pltpu.CompilerParams(dimension_semantics=("parallel", "arbitrary")),
        name="ffn",
    )(x2, g.reshape(1, d), w_gate_up, w_gate_up, w_down)


def _pick(n, pref):
    while n % pref:
        pref //= 2
    return pref


def _layer(x2, p, *, bsz, seq):
    t = x2.shape[0]
    z = _inproj(x2, p['mix_norm_g'], p['w_in'], tm=_pick(t, 512), tn=Z_WIDTH // 3)
    ys = _s5_scan(z, p['s5'], bsz=bsz, seq=seq)
    kaug, vsel, kwin, vwin = _kv_prep(z, p['k_norm_g'], seq=seq, tp=_pick(seq, 512), tk_sel=128, tk_win=128)
    cmp_kv = _compress(z, p['cmp_pe'], p['cmp_w1'], p['cmp_w2'], p['k_norm_g'][0], bsz=bsz, seq=seq)
    o_cmp, qaug = _cmp_select(z, cmp_kv, p['q_norm_g'], bsz=bsz, seq=seq, tq=256)
    tq = _pick(seq, 256)
    o_sel = _flash(qaug, kaug, vsel, z, bsz=bsz, seq=seq, tq=tq, window=0, branch=1, name="sel_attn")
    o_win = _flash(qaug, kwin, vwin, z, bsz=bsz, seq=seq, tq=tq, window=WINDOW, branch=2, name="win_attn")
    x2 = _merge(x2, z, ys, o_cmp, o_sel, o_win, p['conv_w'], p['conv_w_out'], p['ssm_w_glu'], p['nsa_w_o'],
                p['w_out'], seq=seq, tm=_pick(seq, 256))
    return _ffn(x2, p['ffn_norm_g'], p['ffn_w_gate_up'], p['ffn_w_down'], tm=_pick(t, 1024), tf=256)


def kernel(x, mix_norm_g, w_in, ssm_lam_re, ssm_lam_im, ssm_b_re, ssm_b_im, ssm_c_re, ssm_c_im, ssm_d, ssm_log_dt, ssm_w_glu, conv_w, conv_w_out, q_norm_g, k_norm_g, cmp_pe, cmp_w1, cmp_w2, nsa_w_o, w_out, ffn_norm_g, ffn_w_gate_up, ffn_w_down):
    bsz, seq, d = x.shape
    x2 = x.reshape(bsz * seq, d)
    for i in range(w_in.shape[0]):
        p = dict(
            mix_norm_g=mix_norm_g[i], w_in=_permute_w_in(w_in[i]),
            s5=_s5_tables(ssm_lam_re[i], ssm_lam_im[i], ssm_b_re[i], ssm_b_im[i], ssm_c_re[i], ssm_c_im[i],
                          ssm_d[i], ssm_log_dt[i], seq // S5_CHUNK),
            ssm_w_glu=ssm_w_glu[i].astype(BF16), conv_w=conv_w[i], conv_w_out=conv_w_out[i].astype(BF16),
            q_norm_g=q_norm_g[i], k_norm_g=k_norm_g[i], cmp_pe=cmp_pe[i],
            cmp_w1=cmp_w1[i].astype(BF16), cmp_w2=cmp_w2[i].astype(BF16), nsa_w_o=nsa_w_o[i].astype(BF16),
            w_out=w_out[i].astype(BF16), ffn_norm_g=ffn_norm_g[i],
            ffn_w_gate_up=ffn_w_gate_up[i].astype(BF16), ffn_w_down=ffn_w_down[i].astype(BF16))
        x2 = _layer(x2, p, bsz=bsz, seq=seq)
    return x2.reshape(bsz, seq, d)
```

```python
import functools

import jax
import jax.numpy as jnp
import numpy as np
from jax import lax
from jax.experimental import pallas as pl
from jax.experimental.pallas import tpu as pltpu

F32 = jnp.float32
BF16 = jnp.bfloat16

D_MODEL = 1024
SSM_WIDTH = 512
SSM_GROUP = 16
SSM_GROUPS = SSM_WIDTH // SSM_GROUP
SSM_STATE = 64
CONV_CH = 512
CONV_K = 3
HEAD_DIM = 64
HEAD_SHIFT = 6
N_HEADS = 8
N_KV_HEADS = 2
GQA = N_HEADS // N_KV_HEADS
ATTN_WIDTH = N_HEADS * HEAD_DIM
KV_WIDTH = N_KV_HEADS * HEAD_DIM
CMP_BLOCK = 32
SEL_BLOCK = 64
SEL_SHIFT = 6
N_SELECT = 16
WINDOW = 512
CMP_HIDDEN = 256
FORCE_SCORE = 1e4
NSA_BRANCHES = 3
MIX_BRANCHES = 3
D_FF = 2816
RMS_EPS = 1e-6

LANES = 128
MASKED = -1e30
LOG2E = 1.4426950408889634
FLASH_UNROLL = 4
S5_CHUNK = 16
S5_TILE_GROUPS = LANES // SSM_GROUP
GATE_PAD = LANES
BF16_SUBLANES = 16
V_ROWS = HEAD_DIM + BF16_SUBLANES

Z_MIX = 0
Z_U = Z_MIX + MIX_BRANCHES * D_MODEL
Z_CB = Z_U + SSM_WIDTH
Z_CC = Z_CB + CONV_CH
Z_CX = Z_CC + CONV_CH
Z_Q = Z_CX + CONV_CH
Z_KC = Z_Q + ATTN_WIDTH
Z_VC = Z_KC + KV_WIDTH
Z_KS = Z_VC + KV_WIDTH
Z_VS = Z_KS + KV_WIDTH
Z_KW = Z_VS + KV_WIDTH
Z_VW = Z_KW + KV_WIDTH
Z_GATE = Z_VW + KV_WIDTH
Z_WIDTH = Z_GATE + GATE_PAD


def _gelu_tanh(x):
    return 0.5 * x * (1.0 + jnp.tanh(np.sqrt(2.0 / np.pi).astype(np.float32) * (x + 0.044715 * (x * x * x))))


def _sigmoid(x):
    return 1.0 / (1.0 + jnp.exp(-x))


def _rms(x, g):
    return x * lax.rsqrt(jnp.mean(x * x, axis=-1, keepdims=True) + RMS_EPS) * g


def _dot(a, b):
    return jnp.dot(a, b, preferred_element_type=F32)


def _dot_nt(a, b):
    return lax.dot_general(a, b, (((1,), (1,)), ((), ())), preferred_element_type=F32)


def _inproj_kernel(x_ref, g_ref, w_ref, o_ref):
    h = _rms(x_ref[...], g_ref[...]).astype(BF16)
    o_ref[...] = _dot(h, w_ref[...])


def _inproj(x2, g, w, *, tm, tn):
    t, d = x2.shape
    n = w.shape[1]
    return pl.pallas_call(
        _inproj_kernel,
        grid=(n // tn, t // tm),
        in_specs=[pl.BlockSpec((tm, d), lambda j, i: (i, 0)),
                  pl.BlockSpec((1, d), lambda j, i: (0, 0)),
                  pl.BlockSpec((d, tn), lambda j, i: (0, j))],
        out_specs=pl.BlockSpec((tm, tn), lambda j, i: (i, j)),
        out_shape=jax.ShapeDtypeStruct((t, n), F32),
        name="inproj",
    )(x2, g.reshape(1, d), w)


def _permute_w_in(w):
    n_plain = SSM_WIDTH + 3 * CONV_CH + ATTN_WIDTH + 6 * KV_WIDTH
    n_gate = N_HEADS * NSA_BRANCHES
    gate = jnp.pad(w[:, n_plain:n_plain + n_gate], ((0, 0), (0, GATE_PAD - n_gate)))
    return jnp.concatenate([w[:, n_plain + n_gate:], w[:, :n_plain], gate], axis=1).astype(BF16)


def _s5_tables(lam_re, lam_im, b_re, b_im, c_re, c_im, d_skip, log_dt, n_chunks):
    hp = lax.Precision.HIGHEST
    g, p = lam_re.shape
    h, l, gt = SSM_GROUP, S5_CHUNK, S5_TILE_GROUPS
    nt = g // gt
    dt = jnp.exp(log_dt)[:, None]
    ar, ai = lam_re * dt, lam_im * dt

    def powers(k):
        mag = jnp.exp(ar[None] * k[:, None, None])
        ang = ai[None] * k[:, None, None]
        return mag * jnp.cos(ang), mag * jnp.sin(ang)

    pr, pi = powers(jnp.arange(l + 1, dtype=F32))
    nr, ni = pr[1] - 1.0, pi[1]
    den = lam_re * lam_re + lam_im * lam_im
    fr, fi = (nr * lam_re + ni * lam_im) / den, (ni * lam_re - nr * lam_im) / den
    bbr = fr[..., None] * b_re - fi[..., None] * b_im
    bbi = fr[..., None] * b_im + fi[..., None] * b_re
    wr = pr[:l, :, :, None] * bbr[None] - pi[:l, :, :, None] * bbi[None]
    wi = pr[:l, :, :, None] * bbi[None] + pi[:l, :, :, None] * bbr[None]
    kern = (jnp.einsum('gop,tgpi->tgio', c_re, wr, precision=hp)
            - jnp.einsum('gop,tgpi->tgio', c_im, wi, precision=hp))
    r = jnp.arange(l)

    def group_diag(x, row_w, col_w):
        rows, w = x.shape[-2:]
        expand = (jnp.arange(w)[:, None] == jnp.arange(gt * w)[None, :] % w).astype(BF16)
        tiled = jnp.einsum('...w,wv->...v', x.astype(BF16), expand, preferred_element_type=BF16)
        own = (jnp.arange(rows)[:, None] // row_w) % gt == jnp.arange(gt * w)[None, :] // col_w
        return jnp.where(own, tiled, jnp.zeros((), BF16))

    kblk = group_diag(kern.reshape(l, g * h, h), h, h)
    krev = kblk.reshape(l, nt, LANES, LANES)[::-1].transpose(1, 0, 2, 3).reshape(nt, l * LANES, LANES)
    qr, qi = pr[l - 1 - r], pi[l - 1 - r]
    st_re = qr[..., None] * bbr[None] - qi[..., None] * bbi[None]
    st_im = qr[..., None] * bbi[None] + qi[..., None] * bbr[None]
    rows_in = lambda x: group_diag(x.transpose(0, 1, 3, 2).reshape(l, g * h, p), h, p)
    s_tab = jnp.concatenate([rows_in(st_re), rows_in(st_im)], axis=-1)
    s_tab = s_tab.reshape(l, nt, LANES, 2 * gt * p).transpose(1, 0, 2, 3).reshape(nt, l * LANES, 2 * gt * p)
    er, ei = pr[1:l + 1][:, :, None, :], pi[1:l + 1][:, :, None, :]
    rows_out = lambda x: group_diag(x.transpose(0, 1, 3, 2).reshape(l, g * p, h), p, h).reshape(l, nt, gt * p, LANES)
    c_tab = jnp.concatenate([rows_out(c_re[None] * er - c_im[None] * ei),
                             rows_out(-(c_re[None] * ei + c_im[None] * er))], axis=2)
    n_steps = max(1, int(np.ceil(np.log2(n_chunks))))
    dr, di = powers(l * (2.0 ** jnp.arange(n_steps, dtype=F32)))
    lanes = lambda m: m.reshape(n_steps, nt, gt * p).transpose(1, 0, 2)
    return dict(krev=krev, s_tab=s_tab, c_tab=c_tab, d_re=lanes(dr), d_im=lanes(di), d_skip=d_skip.reshape(1, g * h))


def _s5_kernel(u_ref, krev_ref, s_ref, c_ref, dre_ref, dim_ref, dsk_ref, y_ref):
    seq = u_ref.shape[0]
    l = S5_CHUNK
    nc = seq // l
    n_steps = dre_ref.shape[1]
    ns = dre_ref.shape[2]
    us = [u_ref[pl.ds(r, nc, stride=l), :] for r in range(l)]
    ucat = jnp.concatenate(us, axis=1).astype(BF16)
    e = _dot(ucat, s_ref[0])
    xr, xi = e[:, :ns], e[:, ns:]
    row = lax.broadcasted_iota(jnp.int32, xr.shape, 0)
    for k in range(n_steps):
        s = 1 << k
        if s >= nc:
            break
        dr = dre_ref[0, k:k + 1, :]
        di = dim_ref[0, k:k + 1, :]
        sr = jnp.where(row >= s, pltpu.roll(xr, s, 0), 0.0)
        si = jnp.where(row >= s, pltpu.roll(xi, s, 0), 0.0)
        xr, xi = xr + (dr * sr - di * si), xi + (dr * si + di * sr)
    xp = jnp.concatenate([jnp.where(row >= 1, pltpu.roll(xr, 1, 0), 0.0),
                          jnp.where(row >= 1, pltpu.roll(xi, 1, 0), 0.0)], axis=1).astype(BF16)
    dsk = dsk_ref[...]
    for s in range(l):
        y = (_dot(ucat[:, :(s + 1) * LANES], krev_ref[0, (l - 1 - s) * LANES:, :])
             + _dot(xp, c_ref[s, 0]) + dsk * us[s])
        y_ref[pl.ds(s, nc, stride=l), :] = y


def _s5_scan(z, tabs, *, bsz, seq):
    t = z.shape[0]
    nt = SSM_WIDTH // LANES
    l = S5_CHUNK
    tile3 = lambda a: pl.BlockSpec((1,) + a.shape[1:], lambda j, b: (j, 0, 0))
    c_tab = tabs['c_tab']
    return pl.pallas_call(
        _s5_kernel,
        grid=(nt, bsz),
        in_specs=[pl.BlockSpec((seq, LANES), lambda j, b: (b, Z_U // LANES + j)),
                  tile3(tabs['krev']), tile3(tabs['s_tab']),
                  pl.BlockSpec((l, 1) + c_tab.shape[2:], lambda j, b: (0, j, 0, 0)),
                  tile3(tabs['d_re']), tile3(tabs['d_im']),
                  pl.BlockSpec((1, LANES), lambda j, b: (0, j))],
        out_specs=pl.BlockSpec((seq, LANES), lambda j, b: (b, j)),
        out_shape=jax.ShapeDtypeStruct((t, SSM_WIDTH), F32),
        name="s5_scan",
    )(z, tabs['krev'], tabs['s_tab'], c_tab, tabs['d_re'], tabs['d_im'], tabs['d_skip'])


def _kv_prep_kernel(ks_ref, vs_ref, kw_ref, vw_ref, g_ref, kaug_ref, vsel_ref, kwin_ref, vwin_ref, *, seq):
    tp = ks_ref.shape[0]
    pos = (pl.program_id(0) * tp) % seq + lax.broadcasted_iota(jnp.int32, (tp, HEAD_DIM), 0)
    blk = jnp.right_shift(pos, SEL_SHIFT)
    onehot = jnp.where(lax.broadcasted_iota(jnp.int32, (tp, HEAD_DIM), 1) == blk, 1.0, 0.0).astype(BF16)
    zeros = jnp.zeros((tp, HEAD_DIM), BF16)
    for hh in range(N_KV_HEADS):
        sl = slice(hh * HEAD_DIM, (hh + 1) * HEAD_DIM)
        kn = _rms(ks_ref[:, sl], g_ref[0:1, :]).astype(BF16)
        kaug_ref[hh] = jnp.concatenate([kn, onehot], axis=1)
        kn = _rms(kw_ref[:, sl], g_ref[1:2, :]).astype(BF16)
        kwin_ref[hh] = jnp.concatenate([kn, zeros], axis=1)
    for v_ref, vt_ref in ((vs_ref, vsel_ref), (vw_ref, vwin_ref)):
        tk = vt_ref.shape[3]
        vt = v_ref[...].T.astype(BF16)
        ones = jnp.where(lax.broadcasted_iota(jnp.int32, (V_ROWS - HEAD_DIM, tp), 0) == 0, 1.0, 0.0).astype(BF16)
        for hh in range(N_KV_HEADS):
            vh = jnp.concatenate([vt[hh * HEAD_DIM:(hh + 1) * HEAD_DIM], ones], axis=0)
            for j in range(tp // tk):
                vt_ref[hh, j] = vh[:, j * tk:(j + 1) * tk]


def _kv_prep(z, k_norm_g, *, seq, tp, tk_sel, tk_win):
    t = z.shape[0]
    col = lambda off: pl.BlockSpec((tp, KV_WIDTH), lambda i, o=off // KV_WIDTH: (i, o))
    aug = 2 * HEAD_DIM
    kspec = pl.BlockSpec((N_KV_HEADS, tp, aug), lambda i: (0, i, 0))
    vspec = lambda tk: pl.BlockSpec((N_KV_HEADS, tp // tk, V_ROWS, tk), lambda i: (0, i, 0, 0))
    vshape = lambda tk: jax.ShapeDtypeStruct((N_KV_HEADS, t // tk, V_ROWS, tk), BF16)
    return pl.pallas_call(
        functools.partial(_kv_prep_kernel, seq=seq),
        grid=(t // tp,),
        in_specs=[col(Z_KS), col(Z_VS), col(Z_KW), col(Z_VW), pl.BlockSpec((2, HEAD_DIM), lambda i: (0, 0))],
        out_specs=[kspec, vspec(tk_sel), kspec, vspec(tk_win)],
        out_shape=[jax.ShapeDtypeStruct((N_KV_HEADS, t, aug), BF16), vshape(tk_sel),
                   jax.ShapeDtypeStruct((N_KV_HEADS, t, aug), BF16), vshape(tk_win)],
        name="kv_prep",
    )(z, z, z, z, k_norm_g[1:3])


def _compress_kernel(x_ref, pe_ref, w1_ref, w2_ref, g_ref, o_ref):
    seq = x_ref.shape[0]
    nb = seq // (2 * CMP_BLOCK)
    lane = lax.broadcasted_iota(jnp.int32, (nb, 2 * HEAD_DIM), 1)
    rows = []
    for hh in range(N_KV_HEADS):
        for parity in range(2):
            cols = []
            for s in range(0, CMP_BLOCK, 2):
                a = x_ref[pl.ds(parity * CMP_BLOCK + s, nb, stride=2 * CMP_BLOCK), :]
                b = x_ref[pl.ds(parity * CMP_BLOCK + s + 1, nb, stride=2 * CMP_BLOCK), :]
                if hh == 0:
                    cols.append(jnp.where(lane < HEAD_DIM, a, pltpu.roll(b, HEAD_DIM, 1)))
                else:
                    cols.append(jnp.where(lane < HEAD_DIM, pltpu.roll(a, HEAD_DIM, 1), b))
            rows.append(jnp.concatenate(cols, axis=1))
    x = (jnp.concatenate(rows, axis=0) + pe_ref[0]).astype(BF16)
    hid = _gelu_tanh(_dot(x, w1_ref[0])).astype(BF16)
    y = _dot(hid, w2_ref[0])
    o_ref[0, 0] = jnp.where(pl.program_id(0) == 0, _rms(y, g_ref[...]), y)


def _compress(z, cmp_pe, w1, w2, k_gain, *, bsz, seq):
    nc = seq // CMP_BLOCK
    kdim = CMP_BLOCK * HEAD_DIM
    m = N_KV_HEADS * nc
    out = pl.pallas_call(
        _compress_kernel,
        grid=(2, bsz),
        in_specs=[pl.BlockSpec((seq, KV_WIDTH), lambda w, b: (b, Z_KC // KV_WIDTH + w)),
                  pl.BlockSpec((1, 1, kdim), lambda w, b: (w, 0, 0)),
                  pl.BlockSpec((1, kdim, CMP_HIDDEN), lambda w, b: (w, 0, 0)),
                  pl.BlockSpec((1, CMP_HIDDEN, HEAD_DIM), lambda w, b: (w, 0, 0)),
                  pl.BlockSpec((1, HEAD_DIM), lambda w, b: (0, 0))],
        out_specs=pl.BlockSpec((1, 1, m, HEAD_DIM), lambda w, b: (w, b, 0, 0)),
        out_shape=jax.ShapeDtypeStruct((2, bsz, m, HEAD_DIM), F32),
        name="compress",
    )(z, cmp_pe.reshape(2, 1, kdim), w1, w2, k_gain.reshape(1, HEAD_DIM))
    return out.reshape(2, bsz * N_KV_HEADS, nc, HEAD_DIM)


def _head_gate(gates, kvh, g, branch):
    c0 = g * NSA_BRANCHES + branch
    c1 = (GQA + g) * NSA_BRANCHES + branch
    if isinstance(kvh, int):
        c = c1 if kvh else c0
        return gates[:, c:c + 1]
    return jnp.where(kvh == 0, gates[:, c0:c0 + 1], gates[:, c1:c1 + 1])


def _cmp_select_kernel(q_ref, gl_ref, kc_ref, vc_ref, qg_ref, o_ref, qaug_ref, imp_ref):
    tq = q_ref.shape[0]
    nc = kc_ref.shape[1]
    nb = nc // 2
    pair_w = 2 * HEAD_DIM
    qw = GQA * HEAD_DIM
    q0 = pl.program_id(1) * tq
    q = q_ref[...]
    x2 = q * q
    hi = x2.astype(BF16)
    lo = (x2 - hi.astype(F32)).astype(BF16)
    wq = q.shape[1]
    seg = jnp.where(jnp.right_shift(lax.broadcasted_iota(jnp.int32, (wq, wq), 0), HEAD_SHIFT)
                    == jnp.right_shift(lax.broadcasted_iota(jnp.int32, (wq, wq), 1), HEAD_SHIFT), 1.0, 0.0).astype(BF16)
    ss = _dot(hi, seg) + _dot(lo, seg)
    qf = q * lax.rsqrt(ss * (1.0 / HEAD_DIM) + RMS_EPS) * qg_ref[...] * (HEAD_DIM ** -0.5)
    qn = qf.astype(BF16)
    q2 = qf * LOG2E

    row = lax.broadcasted_iota(jnp.int32, (nc, tq), 0)
    t = q0 + lax.broadcasted_iota(jnp.int32, (nc, tq), 1)
    blk = jnp.where(row < nb, 2 * row, 2 * (row - nb) + 1)
    valid = (blk + 1) * CMP_BLOCK - 1 <= t
    lane = lax.broadcasted_iota(jnp.int32, (tq, pair_w), 1)
    j = lax.broadcasted_iota(jnp.int32, (nb, tq), 0)
    cur = jnp.right_shift(q0 + lax.broadcasted_iota(jnp.int32, (nb, tq), 1), SEL_SHIFT)
    forced = (j == 0) | (j == cur) | (j == cur - 1)
    visible = j <= cur
    gl = _sigmoid(gl_ref[...])

    def two_heads(x):
        zx = jnp.zeros_like(x)
        return jnp.concatenate([jnp.concatenate([x, zx], axis=1), jnp.concatenate([zx, x], axis=1)], axis=0)

    pairs = [(kvh, pr) for kvh in range(N_KV_HEADS) for pr in range(GQA // 2)]
    k2 = [two_heads(kc_ref[kvh].astype(BF16)) for kvh in range(N_KV_HEADS)]
    v2 = [two_heads(vc_ref[kvh].astype(BF16)) for kvh in range(N_KV_HEADS)]
    st = [_dot_nt(k2[kvh], qn[:, kvh * qw + pr * pair_w:kvh * qw + (pr + 1) * pair_w]) for kvh, pr in pairs]
    psum = [jnp.zeros((nc, tq), F32) for _ in range(N_KV_HEADS)]
    probs = []
    for (kvh, pr), s2 in zip(pairs, st):
        ph = []
        for hh in range(2):
            sm = jnp.where(valid, s2[hh * nc:(hh + 1) * nc], MASKED)
            m = jnp.max(sm, axis=0, keepdims=True)
            m = jnp.where(m > 0.5 * MASKED, m, 0.0)
            e = jnp.exp(sm - m)
            p = e * (1.0 / jnp.maximum(jnp.sum(e, axis=0, keepdims=True), 1e-30))
            psum[kvh] = psum[kvh] + p
            ph.append(p.astype(BF16))
        probs.append(jnp.concatenate(ph, axis=0))
    imps = []
    for kvh in range(N_KV_HEADS):
        imp = psum[kvh][:nb] + psum[kvh][nb:]
        imp = jnp.where(forced, FORCE_SCORE, jnp.where(visible, imp, -jnp.inf))
        imp_ref[kvh] = imp
        imps.append(imp)
    out = []
    for (kvh, pr), p2 in zip(pairs, probs):
        o_pair = lax.dot_general(v2[kvh], p2, (((0,), (0,)), ((), ())), preferred_element_type=F32).T
        gate = jnp.where(lane < HEAD_DIM, _head_gate(gl, kvh, 2 * pr, 0), _head_gate(gl, kvh, 2 * pr + 1, 0))
        out.append(o_pair * gate)
    o_ref[...] = jnp.concatenate(out, axis=1)

    n_vis = jnp.minimum(nb, (q0 + tq - 1) // SEL_BLOCK + 1)

    def count(i, cnts):
        res = []
        for kvh in range(N_KV_HEADS):
            vi = imp_ref[kvh, pl.ds(i, 1), :]
            beats = (vi > imps[kvh]) | ((vi == imps[kvh]) & (i < j))
            res.append(cnts[kvh] + jnp.where(beats, 1.0, 0.0))
        return tuple(res)

    cnts = lax.fori_loop(0, n_vis, count, tuple(jnp.zeros((nb, tq), F32) for _ in range(N_KV_HEADS)))
    for kvh in range(N_KV_HEADS):
        selneg_t = jnp.where(visible & (cnts[kvh] < float(N_SELECT)), 0.0, MASKED)
        if nb < HEAD_DIM:
            selneg_t = jnp.concatenate([selneg_t, jnp.zeros((HEAD_DIM - nb, tq), F32)], axis=0)
        selneg = jnp.concatenate([selneg_t, selneg_t], axis=0).T
        for pr in range(GQA // 2):
            c0 = kvh * qw + pr * pair_w
            qp = q2[:, c0:c0 + pair_w]
            qaug_ref[kvh, 2 * pr] = jnp.where(lane < HEAD_DIM, qp, selneg).astype(BF16)
            qaug_ref[kvh, 2 * pr + 1] = jnp.where(lane < HEAD_DIM, pltpu.roll(qp, HEAD_DIM, 1), selneg).astype(BF16)


def _cmp_select(z, cmp_kv, q_norm_g, *, bsz, seq, tq):
    t = z.shape[0]
    nq = seq // tq
    nc = seq // CMP_BLOCK
    aug = 2 * HEAD_DIM
    qw = GQA * HEAD_DIM
    kc, vc = cmp_kv[0], cmp_kv[1]
    kv_spec = pl.BlockSpec((N_KV_HEADS, nc, HEAD_DIM), lambda b, i: (b, 0, 0))
    return pl.pallas_call(
        _cmp_select_kernel,
        grid=(bsz, nq),
        in_specs=[pl.BlockSpec((tq, ATTN_WIDTH), lambda b, i: (b * nq + i, Z_Q // ATTN_WIDTH)),
                  pl.BlockSpec((tq, GATE_PAD), lambda b, i: (b * nq + i, Z_GATE // GATE_PAD)),
                  kv_spec, kv_spec,
                  pl.BlockSpec((1, ATTN_WIDTH), lambda b, i: (0, 0))],
        out_specs=[pl.BlockSpec((tq, ATTN_WIDTH), lambda b, i: (b * nq + i, 0)),
                   pl.BlockSpec((N_KV_HEADS, GQA, tq, aug), lambda b, i: (b, 0, i, 0))],
        out_shape=[jax.ShapeDtypeStruct((t, ATTN_WIDTH), F32),
                   jax.ShapeDtypeStruct((bsz * N_KV_HEADS, GQA, seq, aug), BF16)],
        scratch_shapes=[pltpu.VMEM((N_KV_HEADS, nc // 2, tq), F32)],
        name="cmp_select",
    )(z, z, kc, vc, jnp.tile(q_norm_g, N_HEADS).reshape(1, ATTN_WIDTH))


def _flash_kernel(q_ref, k_ref, vt_ref, gl_ref, o_ref, sa_ref, sb_ref, pa_ref, pb_ref, m_ref, a_ref, acc_ref,
                  *, window, branch):
    tq = q_ref.shape[2]
    tk = vt_ref.shape[3]
    cols = GQA * tq
    kvh = pl.program_id(1)
    q0 = pl.program_id(2) * tq
    qa = q_ref[0].reshape(cols, q_ref.shape[3])
    t = q0 + (lax.broadcasted_iota(jnp.int32, (tk, cols), 1) & (tq - 1))
    key_row = lax.broadcasted_iota(jnp.int32, (tk, cols), 0)
    span = 2 * tk
    lo = jnp.maximum(0, q0 - window + 1) // span if window else 0
    hi = (q0 + tq + span - 1) // span
    mid1 = jnp.maximum(lo, (q0 + tq - 1 - window) // span + 1) if window else lo
    mid2 = q0 // span
    first = 2 * lo
    last = 2 * hi - 1

    def scores(kb):
        return _dot_nt(k_ref[0, pl.ds(pl.multiple_of(kb * tk, tk), tk), :], qa)

    m_ref[...] = jnp.full(m_ref.shape, MASKED, F32)
    a_ref[...] = jnp.ones(a_ref.shape, F32)
    acc_ref[...] = jnp.zeros(acc_ref.shape, F32)
    pb_ref[...] = jnp.zeros(pb_ref.shape, BF16)
    sa_ref[...] = scores(first)

    def tile(kb, s_cur, s_nxt, p_prev, p_cur, mask):
        pv = _dot(vt_ref[0, jnp.maximum(kb - 1, first)], p_prev[...])
        s_nxt[...] = scores(jnp.minimum(kb + 1, last))
        s = s_cur[...]
        if mask == 'causal':
            s = jnp.where(kb * tk + key_row <= t, s, MASKED)
        elif mask == 'window':
            s = jnp.where(kb * tk + key_row > t - window, s, MASKED)
        m_old = m_ref[...]
        m_new = jnp.maximum(m_old, jnp.max(s, axis=0, keepdims=True))
        acc_ref[...] = a_ref[...] * acc_ref[...] + pv
        a_ref[...] = jnp.exp2(m_old - m_new)
        p_cur[...] = jnp.exp2(s - m_new).astype(BF16)
        m_ref[...] = m_new

    def pair_step(pair, mask):
        tile(2 * pair, sa_ref, sb_ref, pb_ref, pa_ref, mask)
        tile(2 * pair + 1, sb_ref, sa_ref, pa_ref, pb_ref, mask)

    def run(begin, end, mask, unroll=1):
        n_main = (end - begin) // unroll

        def main(i, carry):
            for u in range(unroll):
                pair_step(begin + i * unroll + u, mask)
            return carry

        def rest(pair, carry):
            pair_step(pair, mask)
            return carry

        if unroll > 1:
            lax.fori_loop(0, n_main, main, 0)
        lax.fori_loop(begin + n_main * unroll if unroll > 1 else begin, end, rest, 0)

    if window:
        interior = (mid1 - lo == 1) & (mid2 - mid1 == 1) & (hi - mid2 == 1)

        @pl.when(interior)
        def _():
            pair_step(lo, 'window')
            pair_step(lo + 1, None)
            pair_step(lo + 2, 'causal')

        @pl.when(jnp.logical_not(interior))
        def _():
            run(lo, mid1, 'window')
            run(mid1, mid2, None)
            run(mid2, hi, 'causal')
    else:
        run(mid1, mid2, None, unroll=FLASH_UNROLL)
        run(mid2, hi, 'causal')
    acc = a_ref[...] * acc_ref[...] + _dot(vt_ref[0, last], pb_ref[...])
    o_t = acc[:HEAD_DIM] / acc[HEAD_DIM:HEAD_DIM + 1]
    gl = _sigmoid(gl_ref[...])
    out = []
    for g in range(0, GQA, 2):
        pair = jnp.concatenate([o_t[:, g * tq:(g + 1) * tq], o_t[:, (g + 1) * tq:(g + 2) * tq]], axis=0).T
        out.append(pair[:, :HEAD_DIM] * _head_gate(gl, kvh, g, branch))
        out.append(pair[:, HEAD_DIM:] * _head_gate(gl, kvh, g + 1, branch))
    o_ref[...] = jnp.concatenate(out, axis=1)


def _flash(qaug, k, vt, z, *, bsz, seq, tq, window, branch, name):
    t = z.shape[0]
    nq = seq // tq
    aug = qaug.shape[3]
    tk = vt.shape[3]
    qw = GQA * HEAD_DIM
    cols = GQA * tq
    return pl.pallas_call(
        functools.partial(_flash_kernel, window=window, branch=branch),
        grid=(bsz, N_KV_HEADS, nq),
        in_specs=[pl.BlockSpec((1, GQA, tq, aug), lambda b, h, i: (b * N_KV_HEADS + h, 0, i, 0)),
                  pl.BlockSpec((1, seq, aug), lambda b, h, i: (h, b, 0)),
                  pl.BlockSpec((1, seq // tk, V_ROWS, tk), lambda b, h, i: (h, b, 0, 0)),
                  pl.BlockSpec((tq, GATE_PAD), lambda b, h, i: (b * nq + i, Z_GATE // GATE_PAD))],
        out_specs=pl.BlockSpec((tq, qw), lambda b, h, i: (b * nq + i, h)),
        out_shape=jax.ShapeDtypeStruct((t, ATTN_WIDTH), F32),
        scratch_shapes=[pltpu.VMEM((tk, cols), F32), pltpu.VMEM((tk, cols), F32),
                        pltpu.VMEM((tk, cols), BF16), pltpu.VMEM((tk, cols), BF16),
                        pltpu.VMEM((1, cols), F32), pltpu.VMEM((1, cols), F32),
                        pltpu.VMEM((V_ROWS, cols), F32)],
        name=name,
    )(qaug, k, vt, z)


def _merge_kernel(x_ref, g0_ref, g1_ref, g2_ref, cb_ref, cc_ref, cx_ref, pc_ref, px_ref, ys_ref,
                  oc_ref, os_ref, ow_ref, cw_ref, wc_ref, wglu_ref, wo_ref, wout_ref, out_ref, *, seq):
    tm = x_ref.shape[0]
    yg = _dot(_gelu_tanh(ys_ref[...]).astype(BF16), wglu_ref[...])
    y_ssm = yg[:, :D_MODEL] * _sigmoid(yg[:, D_MODEL:])
    zc = cc_ref[...] * cx_ref[...]
    keep = jnp.where((pl.program_id(0) * tm) % seq != 0, 1.0, 0.0)
    prev = pc_ref[...] * px_ref[...] * keep
    row = lax.broadcasted_iota(jnp.int32, zc.shape, 0)
    z1 = jnp.where(row >= 1, pltpu.roll(zc, 1, 0), prev[7:8, :])
    z2 = jnp.where(row >= 2, pltpu.roll(zc, 2, 0), jnp.where(row == 1, prev[7:8, :], prev[6:7, :]))
    conv = cw_ref[0:1, :] * z2 + cw_ref[1:2, :] * z1 + cw_ref[2:3, :] * zc
    y_conv = _dot((cb_ref[...] * conv).astype(BF16), wc_ref[...])
    y_attn = _dot((oc_ref[...] + os_ref[...] + ow_ref[...]).astype(BF16), wo_ref[...])
    mixed = _sigmoid(g0_ref[...]) * y_ssm + _sigmoid(g1_ref[...]) * y_conv + _sigmoid(g2_ref[...]) * y_attn
    out_ref[...] = x_ref[...] + _dot(mixed.astype(BF16), wout_ref[...])


def _merge(x2, z, ys, o_cmp, o_sel, o_win, conv_w, wc, wglu, wo, wout, *, seq, tm):
    t, d = x2.shape
    rb = tm // 8
    zc = lambda width, off: pl.BlockSpec((tm, width), lambda i, o=off // width: (i, o))
    zprev = lambda off: pl.BlockSpec((8, CONV_CH), lambda i, o=off // CONV_CH: (jnp.maximum(i * rb - 1, 0), o))
    row = lambda width: pl.BlockSpec((tm, width), lambda i: (i, 0))
    full = lambda a: pl.BlockSpec(a.shape, lambda i: (0, 0))
    return pl.pallas_call(
        functools.partial(_merge_kernel, seq=seq),
        grid=(t // tm,),
        in_specs=[row(d), zc(d, Z_MIX), zc(d, Z_MIX + d), zc(d, Z_MIX + 2 * d),
                  zc(CONV_CH, Z_CB), zc(CONV_CH, Z_CC), zc(CONV_CH, Z_CX), zprev(Z_CC), zprev(Z_CX),
                  row(SSM_WIDTH), row(ATTN_WIDTH), row(ATTN_WIDTH), row(ATTN_WIDTH),
                  full(conv_w), full(wc), full(wglu), full(wo), full(wout)],
        out_specs=row(d),
        out_shape=jax.ShapeDtypeStruct((t, d), F32),
        name="merge",
    )(x2, z, z, z, z, z, z, z, z, ys, o_cmp, o_sel, o_win, conv_w, wc, wglu, wo, wout)


def _ffn_kernel(x_ref, g_ref, wg_ref, wu_ref, wd_ref, o_ref, h_ref, acc_ref):
    f = pl.program_id(1)

    @pl.when(f == 0)
    def _():
        h_ref[...] = _rms(x_ref[...], g_ref[...]).astype(BF16)
        acc_ref[...] = jnp.zeros(acc_ref.shape, F32)

    h = h_ref[...]
    gate = _dot(h, wg_ref[...])
    up = _dot(h, wu_ref[...])
    act = (gate * _sigmoid(gate) * up).astype(BF16)
    acc_ref[...] += _dot(act, wd_ref[...])

    @pl.when(f == pl.num_programs(1) - 1)
    def _():
        o_ref[...] = x_ref[...] + acc_ref[...]


def _ffn(x2, g, w_gate_up, w_down, *, tm, tf):
    t, d = x2.shape
    nf = D_FF // tf
    return pl.pallas_call(
        _ffn_kernel,
        grid=(t // tm, nf),
        in_specs=[pl.BlockSpec((tm, d), lambda i, f: (i, 0)),
                  pl.BlockSpec((1, d), lambda i, f: (0, 0)),
                  pl.BlockSpec((d, tf), lambda i, f: (0, f)),
                  pl.BlockSpec((d, tf), lambda i, f: (0, nf + f)),
                  pl.BlockSpec((tf, d), lambda i, f: (f, 0))],
        out_specs=pl.BlockSpec((tm, d), lambda i, f: (i, 0)),
        out_shape=jax.ShapeDtypeStruct((t, d), F32),
        scratch_shapes=[pltpu.VMEM((tm, d), BF16), pltpu.VMEM((tm, d), F32)],
        compiler_params=pltpu.CompilerParams(dimension_semantics=("parallel", "arbitrary")),
        name="ffn",
    )(x2, g.reshape(1, d), w_gate_up, w_gate_up, w_down)


def _pick(n, pref):
    while n % pref:
        pref //= 2
    return pref


def _layer(x2, p, *, bsz, seq):
    t = x2.shape[0]
    z = _inproj(x2, p['mix_norm_g'], p['w_in'], tm=_pick(t, 1024), tn=Z_WIDTH // 3)
    ys = _s5_scan(z, p['s5'], bsz=bsz, seq=seq)
    kaug, vsel, kwin, vwin = _kv_prep(z, p['k_norm_g'], seq=seq, tp=_pick(seq, 512), tk_sel=128, tk_win=128)
    cmp_kv = _compress(z, p['cmp_pe'], p['cmp_w1'], p['cmp_w2'], p['k_norm_g'][0], bsz=bsz, seq=seq)
    o_cmp, qaug = _cmp_select(z, cmp_kv, p['q_norm_g'], bsz=bsz, seq=seq, tq=256)
    tq = _pick(seq, 256)
    o_sel = _flash(qaug, kaug, vsel, z, bsz=bsz, seq=seq, tq=tq, window=0, branch=1, name="sel_attn")
    o_win = _flash(qaug, kwin, vwin, z, bsz=bsz, seq=seq, tq=tq, window=WINDOW, branch=2, name="win_attn")
    x2 = _merge(x2, z, ys, o_cmp, o_sel, o_win, p['conv_w'], p['conv_w_out'], p['ssm_w_glu'], p['nsa_w_o'],
                p['w_out'], seq=seq, tm=_pick(seq, 256))
    return _ffn(x2, p['ffn_norm_g'], p['ffn_w_gate_up'], p['ffn_w_down'], tm=_pick(t, 1024), tf=256)


def kernel(x, mix_norm_g, w_in, ssm_lam_re, ssm_lam_im, ssm_b_re, ssm_b_im, ssm_c_re, ssm_c_im, ssm_d, ssm_log_dt, ssm_w_glu, conv_w, conv_w_out, q_norm_g, k_norm_g, cmp_pe, cmp_w1, cmp_w2, nsa_w_o, w_out, ffn_norm_g, ffn_w_gate_up, ffn_w_down):
    bsz, seq, d = x.shape
    x2 = x.reshape(bsz * seq, d)
    for i in range(w_in.shape[0]):
        p = dict(
            mix_norm_g=mix_norm_g[i], w_in=_permute_w_in(w_in[i]),
            s5=_s5_tables(ssm_lam_re[i], ssm_lam_im[i], ssm_b_re[i], ssm_b_im[i], ssm_c_re[i], ssm_c_im[i],
                          ssm_d[i], ssm_log_dt[i], seq // S5_CHUNK),
            ssm_w_glu=ssm_w_glu[i].astype(BF16), conv_w=conv_w[i], conv_w_out=conv_w_out[i].astype(BF16),
            q_norm_g=q_norm_g[i], k_norm_g=k_norm_g[i], cmp_pe=cmp_pe[i],
            cmp_w1=cmp_w1[i].astype(BF16), cmp_w2=cmp_w2[i].astype(BF16), nsa_w_o=nsa_w_o[i].astype(BF16),
            w_out=w_out[i].astype(BF16), ffn_norm_g=ffn_norm_g[i],
            ffn_w_gate_up=ffn_w_gate_up[i].astype(BF16), ffn_w_down=ffn_w_down[i].astype(BF16))
        x2 = _layer(x2, p, bsz=bsz, seq=seq)
    return x2.reshape(bsz, seq, d)
```

```python
import functools

import jax
import jax.numpy as jnp
import numpy as np
from jax import lax
from jax.experimental import pallas as pl
from jax.experimental.pallas import tpu as pltpu

F32 = jnp.float32
BF16 = jnp.bfloat16

D_MODEL = 1024
SSM_WIDTH = 512
SSM_GROUP = 16
SSM_GROUPS = SSM_WIDTH // SSM_GROUP
SSM_STATE = 64
CONV_CH = 512
CONV_K = 3
HEAD_DIM = 64
HEAD_SHIFT = 6
N_HEADS = 8
N_KV_HEADS = 2
GQA = N_HEADS // N_KV_HEADS
ATTN_WIDTH = N_HEADS * HEAD_DIM
KV_WIDTH = N_KV_HEADS * HEAD_DIM
CMP_BLOCK = 32
SEL_BLOCK = 64
SEL_SHIFT = 6
N_SELECT = 16
WINDOW = 512
CMP_HIDDEN = 256
FORCE_SCORE = 1e4
NSA_BRANCHES = 3
MIX_BRANCHES = 3
D_FF = 2816
RMS_EPS = 1e-6

LANES = 128
MASKED = -1e30
LOG2E = 1.4426950408889634
FLASH_UNROLL = 4
S5_CHUNK = 16
S5_TILE_GROUPS = LANES // SSM_GROUP
GATE_PAD = LANES
BF16_SUBLANES = 16
V_ROWS = HEAD_DIM + BF16_SUBLANES

Z_MIX = 0
Z_U = Z_MIX + MIX_BRANCHES * D_MODEL
Z_CB = Z_U + SSM_WIDTH
Z_CC = Z_CB + CONV_CH
Z_CX = Z_CC + CONV_CH
Z_Q = Z_CX + CONV_CH
Z_KC = Z_Q + ATTN_WIDTH
Z_VC = Z_KC + KV_WIDTH
Z_KS = Z_VC + KV_WIDTH
Z_VS = Z_KS + KV_WIDTH
Z_KW = Z_VS + KV_WIDTH
Z_VW = Z_KW + KV_WIDTH
Z_GATE = Z_VW + KV_WIDTH
Z_WIDTH = Z_GATE + GATE_PAD


def _gelu_tanh(x):
    return 0.5 * x * (1.0 + jnp.tanh(np.sqrt(2.0 / np.pi).astype(np.float32) * (x + 0.044715 * (x * x * x))))


def _sigmoid(x):
    return 1.0 / (1.0 + jnp.exp(-x))


def _rms(x, g):
    return x * lax.rsqrt(jnp.mean(x * x, axis=-1, keepdims=True) + RMS_EPS) * g


def _dot(a, b):
    return jnp.dot(a, b, preferred_element_type=F32)


def _dot_nt(a, b):
    return lax.dot_general(a, b, (((1,), (1,)), ((), ())), preferred_element_type=F32)


def _inproj_kernel(x_ref, g_ref, w_ref, o_ref):
    h = _rms(x_ref[...], g_ref[...]).astype(BF16)
    o_ref[...] = _dot(h, w_ref[...])


def _inproj(x2, g, w, *, tm, tn):
    t, d = x2.shape
    n = w.shape[1]
    return pl.pallas_call(
        _inproj_kernel,
        grid=(n // tn, t // tm),
        in_specs=[pl.BlockSpec((tm, d), lambda j, i: (i, 0)),
                  pl.BlockSpec((1, d), lambda j, i: (0, 0)),
                  pl.BlockSpec((d, tn), lambda j, i: (0, j))],
        out_specs=pl.BlockSpec((tm, tn), lambda j, i: (i, j)),
        out_shape=jax.ShapeDtypeStruct((t, n), F32),
        name="inproj",
    )(x2, g.reshape(1, d), w)


def _permute_w_in(w):
    n_plain = SSM_WIDTH + 3 * CONV_CH + ATTN_WIDTH + 6 * KV_WIDTH
    n_gate = N_HEADS * NSA_BRANCHES
    gate = jnp.pad(w[:, n_plain:n_plain + n_gate], ((0, 0), (0, GATE_PAD - n_gate)))
    return jnp.concatenate([w[:, n_plain + n_gate:], w[:, :n_plain], gate], axis=1).astype(BF16)


def _s5_tables(lam_re, lam_im, b_re, b_im, c_re, c_im, d_skip, log_dt, n_chunks):
    hp = lax.Precision.HIGHEST
    g, p = lam_re.shape
    h, l, gt = SSM_GROUP, S5_CHUNK, S5_TILE_GROUPS
    nt = g // gt
    dt = jnp.exp(log_dt)[:, None]
    ar, ai = lam_re * dt, lam_im * dt

    def powers(k):
        mag = jnp.exp(ar[None] * k[:, None, None])
        ang = ai[None] * k[:, None, None]
        return mag * jnp.cos(ang), mag * jnp.sin(ang)

    pr, pi = powers(jnp.arange(l + 1, dtype=F32))
    nr, ni = pr[1] - 1.0, pi[1]
    den = lam_re * lam_re + lam_im * lam_im
    fr, fi = (nr * lam_re + ni * lam_im) / den, (ni * lam_re - nr * lam_im) / den
    bbr = fr[..., None] * b_re - fi[..., None] * b_im
    bbi = fr[..., None] * b_im + fi[..., None] * b_re
    wr = pr[:l, :, :, None] * bbr[None] - pi[:l, :, :, None] * bbi[None]
    wi = pr[:l, :, :, None] * bbi[None] + pi[:l, :, :, None] * bbr[None]
    kern = (jnp.einsum('gop,tgpi->tgio', c_re, wr, precision=hp)
            - jnp.einsum('gop,tgpi->tgio', c_im, wi, precision=hp))
    r = jnp.arange(l)

    def group_diag(x, row_w, col_w):
        rows, w = x.shape[-2:]
        expand = (jnp.arange(w)[:, None] == jnp.arange(gt * w)[None, :] % w).astype(BF16)
        tiled = jnp.einsum('...w,wv->...v', x.astype(BF16), expand, preferred_element_type=BF16)
        own = (jnp.arange(rows)[:, None] // row_w) % gt == jnp.arange(gt * w)[None, :] // col_w
        return jnp.where(own, tiled, jnp.zeros((), BF16))

    kblk = group_diag(kern.reshape(l, g * h, h), h, h)
    krev = kblk.reshape(l, nt, LANES, LANES)[::-1].transpose(1, 0, 2, 3).reshape(nt, l * LANES, LANES)
    qr, qi = pr[l - 1 - r], pi[l - 1 - r]
    st_re = qr[..., None] * bbr[None] - qi[..., None] * bbi[None]
    st_im = qr[..., None] * bbi[None] + qi[..., None] * bbr[None]
    rows_in = lambda x: group_diag(x.transpose(0, 1, 3, 2).reshape(l, g * h, p), h, p)
    s_tab = jnp.concatenate([rows_in(st_re), rows_in(st_im)], axis=-1)
    s_tab = s_tab.reshape(l, nt, LANES, 2 * gt * p).transpose(1, 0, 2, 3).reshape(nt, l * LANES, 2 * gt * p)
    er, ei = pr[1:l + 1][:, :, None, :], pi[1:l + 1][:, :, None, :]
    rows_out = lambda x: group_diag(x.transpose(0, 1, 3, 2).reshape(l, g * p, h), p, h).reshape(l, nt, gt * p, LANES)
    c_tab = jnp.concatenate([rows_out(c_re[None] * er - c_im[None] * ei),
                             rows_out(-(c_re[None] * ei + c_im[None] * er))], axis=2)
    n_steps = max(1, int(np.ceil(np.log2(n_chunks))))
    dr, di = powers(l * (2.0 ** jnp.arange(n_steps, dtype=F32)))
    lanes = lambda m: m.reshape(n_steps, nt, gt * p).transpose(1, 0, 2)
    return dict(krev=krev, s_tab=s_tab, c_tab=c_tab, d_re=lanes(dr), d_im=lanes(di), d_skip=d_skip.reshape(1, g * h))


def _s5_kernel(u_ref, krev_ref, s_ref, c_ref, dre_ref, dim_ref, dsk_ref, y_ref):
    seq = u_ref.shape[0]
    l = S5_CHUNK
    nc = seq // l
    n_steps = dre_ref.shape[1]
    ns = dre_ref.shape[2]
    us = [u_ref[pl.ds(r, nc, stride=l), :] for r in range(l)]
    ucat = jnp.concatenate(us, axis=1).astype(BF16)
    e = _dot(ucat, s_ref[0])
    xr, xi = e[:, :ns], e[:, ns:]
    row = lax.broadcasted_iota(jnp.int32, xr.shape, 0)
    for k in range(n_steps):
        s = 1 << k
        if s >= nc:
            break
        dr = dre_ref[0, k:k + 1, :]
        di = dim_ref[0, k:k + 1, :]
        sr = jnp.where(row >= s, pltpu.roll(xr, s, 0), 0.0)
        si = jnp.where(row >= s, pltpu.roll(xi, s, 0), 0.0)
        xr, xi = xr + (dr * sr - di * si), xi + (dr * si + di * sr)
    xp = jnp.concatenate([jnp.where(row >= 1, pltpu.roll(xr, 1, 0), 0.0),
                          jnp.where(row >= 1, pltpu.roll(xi, 1, 0), 0.0)], axis=1).astype(BF16)
    dsk = dsk_ref[...]
    for s in range(l):
        y = (_dot(ucat[:, :(s + 1) * LANES], krev_ref[0, (l - 1 - s) * LANES:, :])
             + _dot(xp, c_ref[s, 0]) + dsk * us[s])
        y_ref[pl.ds(s, nc, stride=l), :] = y


def _s5_scan(z, tabs, *, bsz, seq):
    t = z.shape[0]
    nt = SSM_WIDTH // LANES
    l = S5_CHUNK
    tile3 = lambda a: pl.BlockSpec((1,) + a.shape[1:], lambda j, b: (j, 0, 0))
    c_tab = tabs['c_tab']
    return pl.pallas_call(
        _s5_kernel,
        grid=(nt, bsz),
        in_specs=[pl.BlockSpec((seq, LANES), lambda j, b: (b, Z_U // LANES + j)),
                  tile3(tabs['krev']), tile3(tabs['s_tab']),
                  pl.BlockSpec((l, 1) + c_tab.shape[2:], lambda j, b: (0, j, 0, 0)),
                  tile3(tabs['d_re']), tile3(tabs['d_im']),
                  pl.BlockSpec((1, LANES), lambda j, b: (0, j))],
        out_specs=pl.BlockSpec((seq, LANES), lambda j, b: (b, j)),
        out_shape=jax.ShapeDtypeStruct((t, SSM_WIDTH), F32),
        name="s5_scan",
    )(z, tabs['krev'], tabs['s_tab'], c_tab, tabs['d_re'], tabs['d_im'], tabs['d_skip'])


def _kv_prep_kernel(ks_ref, vs_ref, kw_ref, vw_ref, g_ref, kaug_ref, vsel_ref, kwin_ref, vwin_ref, *, seq):
    tp = ks_ref.shape[0]
    pos = (pl.program_id(0) * tp) % seq + lax.broadcasted_iota(jnp.int32, (tp, HEAD_DIM), 0)
    blk = jnp.right_shift(pos, SEL_SHIFT)
    onehot = jnp.where(lax.broadcasted_iota(jnp.int32, (tp, HEAD_DIM), 1) == blk, 1.0, 0.0).astype(BF16)
    zeros = jnp.zeros((tp, HEAD_DIM), BF16)
    for hh in range(N_KV_HEADS):
        sl = slice(hh * HEAD_DIM, (hh + 1) * HEAD_DIM)
        kn = _rms(ks_ref[:, sl], g_ref[0:1, :]).astype(BF16)
        kaug_ref[hh] = jnp.concatenate([kn, onehot], axis=1)
        kn = _rms(kw_ref[:, sl], g_ref[1:2, :]).astype(BF16)
        kwin_ref[hh] = jnp.concatenate([kn, zeros], axis=1)
    for v_ref, vt_ref in ((vs_ref, vsel_ref), (vw_ref, vwin_ref)):
        tk = vt_ref.shape[3]
        vt = v_ref[...].T.astype(BF16)
        ones = jnp.where(lax.broadcasted_iota(jnp.int32, (V_ROWS - HEAD_DIM, tp), 0) == 0, 1.0, 0.0).astype(BF16)
        for hh in range(N_KV_HEADS):
            vh = jnp.concatenate([vt[hh * HEAD_DIM:(hh + 1) * HEAD_DIM], ones], axis=0)
            for j in range(tp // tk):
                vt_ref[hh, j] = vh[:, j * tk:(j + 1) * tk]


def _kv_prep(z, k_norm_g, *, seq, tp, tk_sel, tk_win):
    t = z.shape[0]
    col = lambda off: pl.BlockSpec((tp, KV_WIDTH), lambda i, o=off // KV_WIDTH: (i, o))
    aug = 2 * HEAD_DIM
    kspec = pl.BlockSpec((N_KV_HEADS, tp, aug), lambda i: (0, i, 0))
    vspec = lambda tk: pl.BlockSpec((N_KV_HEADS, tp // tk, V_ROWS, tk), lambda i: (0, i, 0, 0))
    vshape = lambda tk: jax.ShapeDtypeStruct((N_KV_HEADS, t // tk, V_ROWS, tk), BF16)
    return pl.pallas_call(
        functools.partial(_kv_prep_kernel, seq=seq),
        grid=(t // tp,),
        in_specs=[col(Z_KS), col(Z_VS), col(Z_KW), col(Z_VW), pl.BlockSpec((2, HEAD_DIM), lambda i: (0, 0))],
        out_specs=[kspec, vspec(tk_sel), kspec, vspec(tk_win)],
        out_shape=[jax.ShapeDtypeStruct((N_KV_HEADS, t, aug), BF16), vshape(tk_sel),
                   jax.ShapeDtypeStruct((N_KV_HEADS, t, aug), BF16), vshape(tk_win)],
        name="kv_prep",
    )(z, z, z, z, k_norm_g[1:3])


def _compress_kernel(x_ref, pe_ref, w1_ref, w2_ref, g_ref, o_ref):
    seq = x_ref.shape[0]
    nb = seq // (2 * CMP_BLOCK)
    lane = lax.broadcasted_iota(jnp.int32, (nb, 2 * HEAD_DIM), 1)
    rows = []
    for hh in range(N_KV_HEADS):
        for parity in range(2):
            cols = []
            for s in range(0, CMP_BLOCK, 2):
                a = x_ref[pl.ds(parity * CMP_BLOCK + s, nb, stride=2 * CMP_BLOCK), :]
                b = x_ref[pl.ds(parity * CMP_BLOCK + s + 1, nb, stride=2 * CMP_BLOCK), :]
                if hh == 0:
                    cols.append(jnp.where(lane < HEAD_DIM, a, pltpu.roll(b, HEAD_DIM, 1)))
                else:
                    cols.append(jnp.where(lane < HEAD_DIM, pltpu.roll(a, HEAD_DIM, 1), b))
            rows.append(jnp.concatenate(cols, axis=1))
    x = (jnp.concatenate(rows, axis=0) + pe_ref[0]).astype(BF16)
    hid = _gelu_tanh(_dot(x, w1_ref[0])).astype(BF16)
    y = _dot(hid, w2_ref[0])
    o_ref[0, 0] = jnp.where(pl.program_id(0) == 0, _rms(y, g_ref[...]), y)


def _compress(z, cmp_pe, w1, w2, k_gain, *, bsz, seq):
    nc = seq // CMP_BLOCK
    kdim = CMP_BLOCK * HEAD_DIM
    m = N_KV_HEADS * nc
    out = pl.pallas_call(
        _compress_kernel,
        grid=(2, bsz),
        in_specs=[pl.BlockSpec((seq, KV_WIDTH), lambda w, b: (b, Z_KC // KV_WIDTH + w)),
                  pl.BlockSpec((1, 1, kdim), lambda w, b: (w, 0, 0)),
                  pl.BlockSpec((1, kdim, CMP_HIDDEN), lambda w, b: (w, 0, 0)),
                  pl.BlockSpec((1, CMP_HIDDEN, HEAD_DIM), lambda w, b: (w, 0, 0)),
                  pl.BlockSpec((1, HEAD_DIM), lambda w, b: (0, 0))],
        out_specs=pl.BlockSpec((1, 1, m, HEAD_DIM), lambda w, b: (w, b, 0, 0)),
        out_shape=jax.ShapeDtypeStruct((2, bsz, m, HEAD_DIM), F32),
        name="compress",
    )(z, cmp_pe.reshape(2, 1, kdim), w1, w2, k_gain.reshape(1, HEAD_DIM))
    return out.reshape(2, bsz * N_KV_HEADS, nc, HEAD_DIM)


def _head_gate(gates, kvh, g, branch):
    c0 = g * NSA_BRANCHES + branch
    c1 = (GQA + g) * NSA_BRANCHES + branch
    if isinstance(kvh, int):
        c = c1 if kvh else c0
        return gates[:, c:c + 1]
    return jnp.where(kvh == 0, gates[:, c0:c0 + 1], gates[:, c1:c1 + 1])


def _cmp_select_kernel(q_ref, kc_ref, vc_ref, qg_ref, o_ref, qaug_ref, imp_ref):
    tq = q_ref.shape[0]
    nc = kc_ref.shape[1]
    nb = nc // 2
    pair_w = 2 * HEAD_DIM
    qw = GQA * HEAD_DIM
    q0 = pl.program_id(1) * tq
    q = q_ref[...]
    x2 = q * q
    hi = x2.astype(BF16)
    lo = (x2 - hi.astype(F32)).astype(BF16)
    wq = q.shape[1]
    seg = jnp.where(jnp.right_shift(lax.broadcasted_iota(jnp.int32, (wq, wq), 0), HEAD_SHIFT)
                    == jnp.right_shift(lax.broadcasted_iota(jnp.int32, (wq, wq), 1), HEAD_SHIFT), 1.0, 0.0).astype(BF16)
    ss = _dot(hi, seg) + _dot(lo, seg)
    qf = q * lax.rsqrt(ss * (1.0 / HEAD_DIM) + RMS_EPS) * qg_ref[...] * (HEAD_DIM ** -0.5)
    qn = qf.astype(BF16)
    q2 = qf * LOG2E

    row = lax.broadcasted_iota(jnp.int32, (nc, tq), 0)
    t = q0 + lax.broadcasted_iota(jnp.int32, (nc, tq), 1)
    blk = jnp.where(row < nb, 2 * row, 2 * (row - nb) + 1)
    valid = (blk + 1) * CMP_BLOCK - 1 <= t
    lane = lax.broadcasted_iota(jnp.int32, (tq, pair_w), 1)
    j = lax.broadcasted_iota(jnp.int32, (nb, tq), 0)
    cur = jnp.right_shift(q0 + lax.broadcasted_iota(jnp.int32, (nb, tq), 1), SEL_SHIFT)
    forced = (j == 0) | (j == cur) | (j == cur - 1)
    visible = j <= cur

    def two_heads(x):
        zx = jnp.zeros_like(x)
        return jnp.concatenate([jnp.concatenate([x, zx], axis=1), jnp.concatenate([zx, x], axis=1)], axis=0)

    pairs = [(kvh, pr) for kvh in range(N_KV_HEADS) for pr in range(GQA // 2)]
    k2 = [two_heads(kc_ref[kvh].astype(BF16)) for kvh in range(N_KV_HEADS)]
    v2 = [two_heads(vc_ref[kvh].astype(BF16)) for kvh in range(N_KV_HEADS)]
    st = [_dot_nt(k2[kvh], qn[:, kvh * qw + pr * pair_w:kvh * qw + (pr + 1) * pair_w]) for kvh, pr in pairs]
    psum = [jnp.zeros((nc, tq), F32) for _ in range(N_KV_HEADS)]
    probs = []
    for (kvh, pr), s2 in zip(pairs, st):
        ph = []
        for hh in range(2):
            sm = jnp.where(valid, s2[hh * nc:(hh + 1) * nc], MASKED)
            m = jnp.max(sm, axis=0, keepdims=True)
            m = jnp.where(m > 0.5 * MASKED, m, 0.0)
            e = jnp.exp(sm - m)
            p = e * (1.0 / jnp.maximum(jnp.sum(e, axis=0, keepdims=True), 1e-30))
            psum[kvh] = psum[kvh] + p
            ph.append(p.astype(BF16))
        probs.append(jnp.concatenate(ph, axis=0))
    imps = []
    for kvh in range(N_KV_HEADS):
        imp = psum[kvh][:nb] + psum[kvh][nb:]
        imp = jnp.where(forced, FORCE_SCORE, jnp.where(visible, imp, -jnp.inf))
        imp_ref[kvh] = imp
        imps.append(imp)
    for n, ((kvh, pr), p2) in enumerate(zip(pairs, probs)):
        o_ref[n * pair_w:(n + 1) * pair_w, :] = lax.dot_general(
            v2[kvh], p2, (((0,), (0,)), ((), ())), preferred_element_type=F32)

    n_vis = jnp.minimum(nb, (q0 + tq - 1) // SEL_BLOCK + 1)

    def count(i, cnts):
        res = []
        for kvh in range(N_KV_HEADS):
            vi = imp_ref[kvh, pl.ds(i, 1), :]
            beats = (vi > imps[kvh]) | ((vi == imps[kvh]) & (i < j))
            res.append(cnts[kvh] + jnp.where(beats, 1.0, 0.0))
        return tuple(res)

    cnts = lax.fori_loop(0, n_vis, count, tuple(jnp.zeros((nb, tq), F32) for _ in range(N_KV_HEADS)))
    for kvh in range(N_KV_HEADS):
        selneg_t = jnp.where(visible & (cnts[kvh] < float(N_SELECT)), 0.0, MASKED)
        if nb < HEAD_DIM:
            selneg_t = jnp.concatenate([selneg_t, jnp.zeros((HEAD_DIM - nb, tq), F32)], axis=0)
        selneg = jnp.concatenate([selneg_t, selneg_t], axis=0).T
        for pr in range(GQA // 2):
            c0 = kvh * qw + pr * pair_w
            qp = q2[:, c0:c0 + pair_w]
            qaug_ref[kvh, 2 * pr] = jnp.where(lane < HEAD_DIM, qp, selneg).astype(BF16)
            qaug_ref[kvh, 2 * pr + 1] = jnp.where(lane < HEAD_DIM, pltpu.roll(qp, HEAD_DIM, 1), selneg).astype(BF16)


def _cmp_select(z, cmp_kv, q_norm_g, *, bsz, seq, tq):
    t = z.shape[0]
    nq = seq // tq
    nc = seq // CMP_BLOCK
    aug = 2 * HEAD_DIM
    qw = GQA * HEAD_DIM
    kc, vc = cmp_kv[0], cmp_kv[1]
    kv_spec = pl.BlockSpec((N_KV_HEADS, nc, HEAD_DIM), lambda b, i: (b, 0, 0))
    return pl.pallas_call(
        _cmp_select_kernel,
        grid=(bsz, nq),
        in_specs=[pl.BlockSpec((tq, ATTN_WIDTH), lambda b, i: (b * nq + i, Z_Q // ATTN_WIDTH)),
                  kv_spec, kv_spec,
                  pl.BlockSpec((1, ATTN_WIDTH), lambda b, i: (0, 0))],
        out_specs=[pl.BlockSpec((ATTN_WIDTH, tq), lambda b, i: (0, b * nq + i)),
                   pl.BlockSpec((N_KV_HEADS, GQA, tq, aug), lambda b, i: (b, 0, i, 0))],
        out_shape=[jax.ShapeDtypeStruct((ATTN_WIDTH, t), F32),
                   jax.ShapeDtypeStruct((bsz * N_KV_HEADS, GQA, seq, aug), BF16)],
        scratch_shapes=[pltpu.VMEM((N_KV_HEADS, nc // 2, tq), F32)],
        name="cmp_select",
    )(z, kc, vc, jnp.tile(q_norm_g, N_HEADS).reshape(1, ATTN_WIDTH))


def _flash_kernel(q_ref, k_ref, vt_ref, o_ref, sa_ref, sb_ref, pa_ref, pb_ref, m_ref, a_ref, acc_ref, *, window):
    tq = q_ref.shape[2]
    tk = vt_ref.shape[3]
    cols = GQA * tq
    q0 = pl.program_id(2) * tq
    qa = q_ref[0].reshape(cols, q_ref.shape[3])
    t = q0 + (lax.broadcasted_iota(jnp.int32, (tk, cols), 1) & (tq - 1))
    key_row = lax.broadcasted_iota(jnp.int32, (tk, cols), 0)
    span = 2 * tk
    lo = jnp.maximum(0, q0 - window + 1) // span if window else 0
    hi = (q0 + tq + span - 1) // span
    mid1 = jnp.maximum(lo, (q0 + tq - 1 - window) // span + 1) if window else lo
    mid2 = q0 // span
    first = 2 * lo
    last = 2 * hi - 1

    def scores(kb):
        return _dot_nt(k_ref[0, pl.ds(pl.multiple_of(kb * tk, tk), tk), :], qa)

    m_ref[...] = jnp.full(m_ref.shape, MASKED, F32)
    a_ref[...] = jnp.ones(a_ref.shape, F32)
    acc_ref[...] = jnp.zeros(acc_ref.shape, F32)
    pb_ref[...] = jnp.zeros(pb_ref.shape, BF16)
    sa_ref[...] = scores(first)

    def tile(kb, s_cur, s_nxt, p_prev, p_cur, mask):
        pv = _dot(vt_ref[0, jnp.maximum(kb - 1, first)], p_prev[...])
        s_nxt[...] = scores(jnp.minimum(kb + 1, last))
        s = s_cur[...]
        if mask == 'causal':
            s = jnp.where(kb * tk + key_row <= t, s, MASKED)
        elif mask == 'window':
            s = jnp.where(kb * tk + key_row > t - window, s, MASKED)
        m_old = m_ref[...]
        m_new = jnp.maximum(m_old, jnp.max(s, axis=0, keepdims=True))
        acc_ref[...] = a_ref[...] * acc_ref[...] + pv
        a_ref[...] = jnp.exp2(m_old - m_new)
        p_cur[...] = jnp.exp2(s - m_new).astype(BF16)
        m_ref[...] = m_new

    def pair_step(pair, mask):
        tile(2 * pair, sa_ref, sb_ref, pb_ref, pa_ref, mask)
        tile(2 * pair + 1, sb_ref, sa_ref, pa_ref, pb_ref, mask)

    def run(begin, end, mask, unroll=1):
        n_main = (end - begin) // unroll

        def main(i, carry):
            for u in range(unroll):
                pair_step(begin + i * unroll + u, mask)
            return carry

        def rest(pair, carry):
            pair_step(pair, mask)
            return carry

        if unroll > 1:
            lax.fori_loop(0, n_main, main, 0)
        lax.fori_loop(begin + n_main * unroll if unroll > 1 else begin, end, rest, 0)

    if window:
        interior = (mid1 - lo == 1) & (mid2 - mid1 == 1) & (hi - mid2 == 1)

        @pl.when(interior)
        def _():
            pair_step(lo, 'window')
            pair_step(lo + 1, None)
            pair_step(lo + 2, 'causal')

        @pl.when(jnp.logical_not(interior))
        def _():
            run(lo, mid1, 'window')
            run(mid1, mid2, None)
            run(mid2, hi, 'causal')
    else:
        run(mid1, mid2, None, unroll=FLASH_UNROLL)
        run(mid2, hi, 'causal')
    acc = a_ref[...] * acc_ref[...] + _dot(vt_ref[0, last], pb_ref[...])
    o_t = acc[:HEAD_DIM] / acc[HEAD_DIM:HEAD_DIM + 1]
    for g in range(GQA):
        o_ref[g * HEAD_DIM:(g + 1) * HEAD_DIM, :] = o_t[:, g * tq:(g + 1) * tq]


def _flash(qaug, k, vt, *, bsz, seq, tq, window, name):
    t = bsz * seq
    nq = seq // tq
    aug = qaug.shape[3]
    tk = vt.shape[3]
    qw = GQA * HEAD_DIM
    cols = GQA * tq
    return pl.pallas_call(
        functools.partial(_flash_kernel, window=window),
        grid=(bsz, N_KV_HEADS, nq),
        in_specs=[pl.BlockSpec((1, GQA, tq, aug), lambda b, h, i: (b * N_KV_HEADS + h, 0, i, 0)),
                  pl.BlockSpec((1, seq, aug), lambda b, h, i: (h, b, 0)),
                  pl.BlockSpec((1, seq // tk, V_ROWS, tk), lambda b, h, i: (h, b, 0, 0))],
        out_specs=pl.BlockSpec((qw, tq), lambda b, h, i: (h, b * nq + i)),
        out_shape=jax.ShapeDtypeStruct((ATTN_WIDTH, t), F32),
        scratch_shapes=[pltpu.VMEM((tk, cols), F32), pltpu.VMEM((tk, cols), F32),
                        pltpu.VMEM((tk, cols), BF16), pltpu.VMEM((tk, cols), BF16),
                        pltpu.VMEM((1, cols), F32), pltpu.VMEM((1, cols), F32),
                        pltpu.VMEM((V_ROWS, cols), F32)],
        name=name,
    )(qaug, k, vt)


def _merge_kernel(x_ref, g0_ref, g1_ref, g2_ref, cb_ref, cc_ref, cx_ref, pc_ref, px_ref, gl_ref, ys_ref,
                  oc_ref, os_ref, ow_ref, cw_ref, wc_ref, wglu_ref, wo_ref, wout_ref, out_ref, *, seq):
    tm = x_ref.shape[0]
    yg = _dot(_gelu_tanh(ys_ref[...]).astype(BF16), wglu_ref[...])
    y_ssm = yg[:, :D_MODEL] * _sigmoid(yg[:, D_MODEL:])
    zc = cc_ref[...] * cx_ref[...]
    keep = jnp.where((pl.program_id(0) * tm) % seq != 0, 1.0, 0.0)
    prev = pc_ref[...] * px_ref[...] * keep
    row = lax.broadcasted_iota(jnp.int32, zc.shape, 0)
    z1 = jnp.where(row >= 1, pltpu.roll(zc, 1, 0), prev[7:8, :])
    z2 = jnp.where(row >= 2, pltpu.roll(zc, 2, 0), jnp.where(row == 1, prev[7:8, :], prev[6:7, :]))
    conv = cw_ref[0:1, :] * z2 + cw_ref[1:2, :] * z1 + cw_ref[2:3, :] * zc
    y_conv = _dot((cb_ref[...] * conv).astype(BF16), wc_ref[...])
    gates_t = _sigmoid(gl_ref[...]).T
    heads = []
    for hd in range(N_HEADS):
        rows = slice(hd * HEAD_DIM, (hd + 1) * HEAD_DIM)
        c = hd * NSA_BRANCHES
        heads.append(oc_ref[rows, :] * gates_t[c:c + 1] + os_ref[rows, :] * gates_t[c + 1:c + 2]
                     + ow_ref[rows, :] * gates_t[c + 2:c + 3])
    o_t = jnp.concatenate(heads, axis=0).astype(BF16)
    y_attn = lax.dot_general(o_t, wo_ref[...], (((0,), (0,)), ((), ())), preferred_element_type=F32)
    mixed =_sigmoid(g0_ref[...]) * y_ssm + _sigmoid(g1_ref[...]) * y_conv + _sigmoid(g2_ref[...]) * y_attn
    out_ref[...] = x_ref[...] + _dot(mixed.astype(BF16), wout_ref[...])


def _merge(x2, z, ys, o_cmp, o_sel, o_win, conv_w, wc, wglu, wo, wout, *, seq, tm):
    t, d = x2.shape
    rb = tm // 8
    zc = lambda width, off: pl.BlockSpec((tm, width), lambda i, o=off // width: (i, o))
    zprev = lambda off: pl.BlockSpec((8, CONV_CH), lambda i, o=off // CONV_CH: (jnp.maximum(i * rb - 1, 0), o))
    row = lambda width: pl.BlockSpec((tm, width), lambda i: (i, 0))
    full = lambda a: pl.BlockSpec(a.shape, lambda i: (0, 0))
    branch_t = pl.BlockSpec((ATTN_WIDTH, tm), lambda i: (0, i))
    return pl.pallas_call(
        functools.partial(_merge_kernel, seq=seq),
        grid=(t // tm,),
        in_specs=[row(d), zc(d, Z_MIX), zc(d, Z_MIX + d), zc(d, Z_MIX + 2 * d),
                  zc(CONV_CH, Z_CB), zc(CONV_CH, Z_CC), zc(CONV_CH, Z_CX), zprev(Z_CC), zprev(Z_CX),
                  zc(GATE_PAD, Z_GATE), row(SSM_WIDTH), branch_t, branch_t, branch_t,
                  full(conv_w), full(wc), full(wglu), full(wo), full(wout)],
        out_specs=row(d),
        out_shape=jax.ShapeDtypeStruct((t, d), F32),
        name="merge",
    )(x2, z, z, z, z, z, z, z, z, z, ys, o_cmp, o_sel, o_win, conv_w, wc, wglu, wo, wout)


def _ffn_kernel(x_ref, g_ref, wg_ref, wu_ref, wd_ref, o_ref, h_ref, acc_ref):
    f = pl.program_id(1)

    @pl.when(f == 0)
    def _():
        h_ref[...] = _rms(x_ref[...], g_ref[...]).astype(BF16)
        acc_ref[...] = jnp.zeros(acc_ref.shape, F32)

    h = h_ref[...]
    gate = _dot(h, wg_ref[...])
    up = _dot(h, wu_ref[...])
    act = (gate * _sigmoid(gate) * up).astype(BF16)
    acc_ref[...] += _dot(act, wd_ref[...])

    @pl.when(f == pl.num_programs(1) - 1)
    def _():
        o_ref[...] = x_ref[...] + acc_ref[...]


def _ffn(x2, g, w_gate_up, w_down, *, tm, tf):
    t, d = x2.shape
    nf = D_FF // tf
    return pl.pallas_call(
        _ffn_kernel,
        grid=(t // tm, nf),
        in_specs=[pl.BlockSpec((tm, d), lambda i, f: (i, 0)),
                  pl.BlockSpec((1, d), lambda i, f: (0, 0)),
                  pl.BlockSpec((d, tf), lambda i, f: (0, f)),
                  pl.BlockSpec((d, tf), lambda i, f: (0, nf + f)),
                  pl.BlockSpec((tf, d), lambda i, f: (f, 0))],
        out_specs=pl.BlockSpec((tm, d), lambda i, f: (i, 0)),
        out_shape=jax.ShapeDtypeStruct((t, d), F32),
        scratch_shapes=[pltpu.VMEM((tm, d), BF16), pltpu.VMEM((tm, d), F32)],
        compiler_params=pltpu.CompilerParams(dimension_semantics=("parallel", "arbitrary")),
        name="ffn",
    )(x2, g.reshape(1, d), w_gate_up, w_gate_up, w_down)


def _pick(n, pref):
    while n % pref:
        pref //= 2
    return pref


def _layer(x2, p, *, bsz, seq):
    t = x2.shape[0]
    z = _inproj(x2, p['mix_norm_g'], p['w_in'], tm=_pick(t, 1024), tn=Z_WIDTH // 3)
    ys = _s5_scan(z, p['s5'], bsz=bsz, seq=seq)
    kaug, vsel, kwin, vwin = _kv_prep(z, p['k_norm_g'], seq=seq, tp=_pick(seq, 512), tk_sel=128, tk_win=128)
    cmp_kv = _compress(z, p['cmp_pe'], p['cmp_w1'], p['cmp_w2'], p['k_norm_g'][0], bsz=bsz, seq=seq)
    o_cmp, qaug = _cmp_select(z, cmp_kv, p['q_norm_g'], bsz=bsz, seq=seq, tq=256)
    tq = _pick(seq, 256)
    o_sel = _flash(qaug, kaug, vsel, bsz=bsz, seq=seq, tq=tq, window=0, name="sel_attn")
    o_win = _flash(qaug, kwin, vwin, bsz=bsz, seq=seq, tq=tq, window=WINDOW, name="win_attn")
    x2 = _merge(x2, z, ys, o_cmp, o_sel, o_win, p['conv_w'], p['conv_w_out'], p['ssm_w_glu'], p['nsa_w_o'],
                p['w_out'], seq=seq, tm=_pick(seq, 256))
    return _ffn(x2, p['ffn_norm_g'], p['ffn_w_gate_up'], p['ffn_w_down'], tm=_pick(t, 1024), tf=256)


def kernel(x, mix_norm_g, w_in, ssm_lam_re, ssm_lam_im, ssm_b_re, ssm_b_im, ssm_c_re, ssm_c_im, ssm_d, ssm_log_dt, ssm_w_glu, conv_w, conv_w_out, q_norm_g, k_norm_g, cmp_pe, cmp_w1, cmp_w2, nsa_w_o, w_out, ffn_norm_g, ffn_w_gate_up, ffn_w_down):
    bsz, seq, d = x.shape
    x2 = x.reshape(bsz * seq, d)
    for i in range(w_in.shape[0]):
        p = dict(
            mix_norm_g=mix_norm_g[i], w_in=_permute_w_in(w_in[i]),
            s5=_s5_tables(ssm_lam_re[i], ssm_lam_im[i], ssm_b_re[i], ssm_b_im[i], ssm_c_re[i], ssm_c_im[i],
                          ssm_d[i], ssm_log_dt[i], seq // S5_CHUNK),
            ssm_w_glu=ssm_w_glu[i].astype(BF16), conv_w=conv_w[i], conv_w_out=conv_w_out[i].astype(BF16),
            q_norm_g=q_norm_g[i], k_norm_g=k_norm_g[i], cmp_pe=cmp_pe[i],
            cmp_w1=cmp_w1[i].astype(BF16), cmp_w2=cmp_w2[i].astype(BF16), nsa_w_o=nsa_w_o[i].astype(BF16),
            w_out=w_out[i].astype(BF16), ffn_norm_g=ffn_norm_g[i],
            ffn_w_gate_up=ffn_w_gate_up[i].astype(BF16), ffn_w_down=ffn_w_down[i].astype(BF16))
        x2 = _layer(x2, p, bsz=bsz, seq=seq)
    return x2.reshape(bsz, seq, d)
```

```python
import functools

import jax
import jax.numpy as jnp
import numpy as np
from jax import lax
from jax.experimental import pallas as pl
from jax.experimental.pallas import tpu as pltpu

F32 = jnp.float32
BF16 = jnp.bfloat16

D_MODEL = 1024
SSM_WIDTH = 512
SSM_GROUP = 16
SSM_GROUPS = SSM_WIDTH // SSM_GROUP
SSM_STATE = 64
CONV_CH = 512
CONV_K = 3
HEAD_DIM = 64
HEAD_SHIFT = 6
N_HEADS = 8
N_KV_HEADS = 2
GQA = N_HEADS // N_KV_HEADS
ATTN_WIDTH = N_HEADS * HEAD_DIM
KV_WIDTH = N_KV_HEADS * HEAD_DIM
CMP_BLOCK = 32
SEL_BLOCK = 64
SEL_SHIFT = 6
N_SELECT = 16
WINDOW = 512
CMP_HIDDEN = 256
FORCE_SCORE = 1e4
NSA_BRANCHES = 3
MIX_BRANCHES = 3
D_FF = 2816
RMS_EPS = 1e-6

LANES = 128
SUBLANES = 8
MASKED = -1e30
LOG2E = 1.4426950408889634
FLASH_UNROLL = 4
S5_CHUNK = 16
S5_TILE_GROUPS = LANES // SSM_GROUP
GATE_PAD = LANES
BF16_SUBLANES = 16
V_ROWS = HEAD_DIM + BF16_SUBLANES

Z_MIX = 0
Z_U = Z_MIX + MIX_BRANCHES * D_MODEL
Z_CB = Z_U + SSM_WIDTH
Z_CC = Z_CB + CONV_CH
Z_CX = Z_CC + CONV_CH
Z_Q = Z_CX + CONV_CH
Z_KC = Z_Q + ATTN_WIDTH
Z_VC = Z_KC + KV_WIDTH
Z_KS = Z_VC + KV_WIDTH
Z_VS = Z_KS + KV_WIDTH
Z_KW = Z_VS + KV_WIDTH
Z_VW = Z_KW + KV_WIDTH
Z_GATE = Z_VW + KV_WIDTH
Z_WIDTH = Z_GATE + GATE_PAD


def _gelu_tanh(x):
    return 0.5 * x * (1.0 + jnp.tanh(np.sqrt(2.0 / np.pi).astype(np.float32) * (x + 0.044715 * (x * x * x))))


def _sigmoid(x):
    return 0.5 * jnp.tanh(0.5 * x) + 0.5


def _rms(x, g):
    return x * lax.rsqrt(jnp.mean(x * x, axis=-1, keepdims=True) + RMS_EPS) * g


def _dot(a, b):
    return jnp.dot(a, b, preferred_element_type=F32)


def _dot_nt(a, b):
    return lax.dot_general(a, b, (((1,), (1,)), ((), ())), preferred_element_type=F32)


def _inproj_kernel(x_ref, g_ref, w_ref, o_ref):
    h = _rms(x_ref[...], g_ref[...]).astype(BF16)
    o_ref[...] = _dot(h, w_ref[...])


def _inproj(x2, g, w, *, tm, tn):
    t, d = x2.shape
    n = w.shape[1]
    return pl.pallas_call(
        _inproj_kernel,
        grid=(n // tn, t // tm),
        in_specs=[pl.BlockSpec((tm, d), lambda j, i: (i, 0)),
                  pl.BlockSpec((1, d), lambda j, i: (0, 0)),
                  pl.BlockSpec((d, tn), lambda j, i: (0, j))],
        out_specs=pl.BlockSpec((tm, tn), lambda j, i: (i, j)),
        out_shape=jax.ShapeDtypeStruct((t, n), F32),
        name="inproj",
    )(x2, g.reshape(1, d), w)


def _permute_w_in(w):
    n_plain = SSM_WIDTH + 3 * CONV_CH + ATTN_WIDTH + 6 * KV_WIDTH
    n_gate = N_HEADS * NSA_BRANCHES
    gate = jnp.pad(w[:, n_plain:n_plain + n_gate], ((0, 0), (0, GATE_PAD - n_gate)))
    return jnp.concatenate([w[:, n_plain + n_gate:], w[:, :n_plain], gate], axis=1).astype(BF16)


def _s5_tables(lam_re, lam_im, b_re, b_im, c_re, c_im, d_skip, log_dt, n_chunks):
    hp = lax.Precision.HIGHEST
    g, p = lam_re.shape
    h, l, gt = SSM_GROUP, S5_CHUNK, S5_TILE_GROUPS
    nt = g // gt
    dt = jnp.exp(log_dt)[:, None]
    ar, ai = lam_re * dt, lam_im * dt

    def powers(k):
        mag = jnp.exp(ar[None] * k[:, None, None])
        ang = ai[None] * k[:, None, None]
        return mag * jnp.cos(ang), mag * jnp.sin(ang)

    pr, pi = powers(jnp.arange(l + 1, dtype=F32))
    nr, ni = pr[1] - 1.0, pi[1]
    den = lam_re * lam_re + lam_im * lam_im
    fr, fi = (nr * lam_re + ni * lam_im) / den, (ni * lam_re - nr * lam_im) / den
    bbr = fr[..., None] * b_re - fi[..., None] * b_im
    bbi = fr[..., None] * b_im + fi[..., None] * b_re
    wr = pr[:l, :, :, None] * bbr[None] - pi[:l, :, :, None] * bbi[None]
    wi = pr[:l, :, :, None] * bbi[None] + pi[:l, :, :, None] * bbr[None]
    kern = (jnp.einsum('gop,tgpi->tgio', c_re, wr, precision=hp)
            - jnp.einsum('gop,tgpi->tgio', c_im, wi, precision=hp))
    r = jnp.arange(l)

    def group_diag(x, row_w, col_w):
        rows, w = x.shape[-2:]
        expand = (jnp.arange(w)[:, None] == jnp.arange(gt * w)[None, :] % w).astype(BF16)
        tiled = jnp.einsum('...w,wv->...v', x.astype(BF16), expand, preferred_element_type=BF16)
        own = (jnp.arange(rows)[:, None] // row_w) % gt == jnp.arange(gt * w)[None, :] // col_w
        return jnp.where(own, tiled, jnp.zeros((), BF16))

    kblk = group_diag(kern.reshape(l, g * h, h), h, h)
    krev = kblk.reshape(l, nt, LANES, LANES)[::-1].transpose(1, 0, 2, 3).reshape(nt, l * LANES, LANES)
    qr, qi = pr[l - 1 - r], pi[l - 1 - r]
    st_re = qr[..., None] * bbr[None] - qi[..., None] * bbi[None]
    st_im = qr[..., None] * bbi[None] + qi[..., None] * bbr[None]
    rows_in = lambda x: group_diag(x.transpose(0, 1, 3, 2).reshape(l, g * h, p), h, p)
    s_tab = jnp.concatenate([rows_in(st_re), rows_in(st_im)], axis=-1)
    s_tab = s_tab.reshape(l, nt, LANES, 2 * gt * p).transpose(1, 0, 2, 3).reshape(nt, l * LANES, 2 * gt * p)
    er, ei = pr[1:l + 1][:, :, None, :], pi[1:l + 1][:, :, None, :]
    rows_out = lambda x: group_diag(x.transpose(0, 1, 3, 2).reshape(l, g * p, h), p, h).reshape(l, nt, gt * p, LANES)
    c_tab = jnp.concatenate([rows_out(c_re[None] * er - c_im[None] * ei),
                             rows_out(-(c_re[None] * ei + c_im[None] * er))], axis=2)
    n_steps = max(1, int(np.ceil(np.log2(n_chunks))))
    dr, di = powers(l * (2.0 ** jnp.arange(n_steps, dtype=F32)))
    lanes = lambda m: m.reshape(n_steps, nt, gt * p).transpose(1, 0, 2)
    return dict(krev=krev, s_tab=s_tab, c_tab=c_tab, d_re=lanes(dr), d_im=lanes(di), d_skip=d_skip.reshape(1, g * h))


def _s5_kernel(u_ref, krev_ref, s_ref, c_ref, dre_ref, dim_ref, dsk_ref, y_ref):
    seq = u_ref.shape[0]
    l = S5_CHUNK
    nc = seq // l
    n_steps = dre_ref.shape[1]
    ns = dre_ref.shape[2]
    us = [u_ref[pl.ds(r, nc, stride=l), :] for r in range(l)]
    ucat = jnp.concatenate(us, axis=1).astype(BF16)
    e = _dot(ucat, s_ref[0])
    xr, xi = e[:, :ns], e[:, ns:]
    row = lax.broadcasted_iota(jnp.int32, xr.shape, 0)
    for k in range(n_steps):
        s = 1 << k
        if s >= nc:
            break
        dr = dre_ref[0, k:k + 1, :]
        di = dim_ref[0, k:k + 1, :]
        sr = jnp.where(row >= s, pltpu.roll(xr, s, 0), 0.0)
        si = jnp.where(row >= s, pltpu.roll(xi, s, 0), 0.0)
        xr, xi = xr + (dr * sr - di * si), xi + (dr * si + di * sr)
    xp = jnp.concatenate([jnp.where(row >= 1, pltpu.roll(xr, 1, 0), 0.0),
                          jnp.where(row >= 1, pltpu.roll(xi, 1, 0), 0.0)], axis=1).astype(BF16)
    dsk = dsk_ref[...]
    for s in range(l):
        y = (_dot(ucat[:, :(s + 1) * LANES], krev_ref[0, (l - 1 - s) * LANES:, :])
             + _dot(xp, c_ref[s, 0]) + dsk * us[s])
        y_ref[pl.ds(s, nc, stride=l), :] = y


def _s5_scan(z, tabs, *, bsz, seq):
    t = z.shape[0]
    nt = SSM_WIDTH // LANES
    l = S5_CHUNK
    tile3 = lambda a: pl.BlockSpec((1,) + a.shape[1:], lambda j, b: (j, 0, 0))
    c_tab = tabs['c_tab']
    return pl.pallas_call(
        _s5_kernel,
        grid=(nt, bsz),
        in_specs=[pl.BlockSpec((seq, LANES), lambda j, b: (b, Z_U // LANES + j)),
                  tile3(tabs['krev']), tile3(tabs['s_tab']),
                  pl.BlockSpec((l, 1) + c_tab.shape[2:], lambda j, b: (0, j, 0, 0)),
                  tile3(tabs['d_re']), tile3(tabs['d_im']),
                  pl.BlockSpec((1, LANES), lambda j, b: (0, j))],
        out_specs=pl.BlockSpec((seq, LANES), lambda j, b: (b, j)),
        out_shape=jax.ShapeDtypeStruct((t, SSM_WIDTH), F32),
        name="s5_scan",
    )(z, tabs['krev'], tabs['s_tab'], c_tab, tabs['d_re'], tabs['d_im'], tabs['d_skip'])


def _kv_prep_kernel(ks_ref, vs_ref, kw_ref, vw_ref, g_ref, kaug_ref, vsel_ref, kwin_ref, vwin_ref, *, seq):
    tp = ks_ref.shape[0]
    pos = (pl.program_id(0) * tp) % seq + lax.broadcasted_iota(jnp.int32, (tp, HEAD_DIM), 0)
    blk = jnp.right_shift(pos, SEL_SHIFT)
    onehot = jnp.where(lax.broadcasted_iota(jnp.int32, (tp, HEAD_DIM), 1) == blk, 1.0, 0.0).astype(BF16)
    zeros = jnp.zeros((tp, HEAD_DIM), BF16)
    for hh in range(N_KV_HEADS):
        sl = slice(hh * HEAD_DIM, (hh + 1) * HEAD_DIM)
        kn = _rms(ks_ref[:, sl], g_ref[0:1, :]).astype(BF16)
        kaug_ref[hh] = jnp.concatenate([kn, onehot], axis=1)
        kn = _rms(kw_ref[:, sl], g_ref[1:2, :]).astype(BF16)
        kwin_ref[hh] = jnp.concatenate([kn, zeros], axis=1)
    for v_ref, vt_ref in ((vs_ref, vsel_ref), (vw_ref, vwin_ref)):
        tk = vt_ref.shape[3]
        vt = v_ref[...].T.astype(BF16)
        ones = jnp.where(lax.broadcasted_iota(jnp.int32, (V_ROWS - HEAD_DIM, tp), 0) == 0, 1.0, 0.0).astype(BF16)
        for hh in range(N_KV_HEADS):
            vh = jnp.concatenate([vt[hh * HEAD_DIM:(hh + 1) * HEAD_DIM], ones], axis=0)
            for j in range(tp // tk):
                vt_ref[hh, j] = vh[:, j * tk:(j + 1) * tk]


def _kv_prep(z, k_norm_g, *, seq, tp, tk_sel, tk_win):
    t = z.shape[0]
    col = lambda off: pl.BlockSpec((tp, KV_WIDTH), lambda i, o=off // KV_WIDTH: (i, o))
    aug = 2 * HEAD_DIM
    kspec = pl.BlockSpec((N_KV_HEADS, tp, aug), lambda i: (0, i, 0))
    vspec = lambda tk: pl.BlockSpec((N_KV_HEADS, tp // tk, V_ROWS, tk), lambda i: (0, i, 0, 0))
    vshape = lambda tk: jax.ShapeDtypeStruct((N_KV_HEADS, t // tk, V_ROWS, tk), BF16)
    return pl.pallas_call(
        functools.partial(_kv_prep_kernel, seq=seq),
        grid=(t // tp,),
        in_specs=[col(Z_KS), col(Z_VS), col(Z_KW), col(Z_VW), pl.BlockSpec((2, HEAD_DIM), lambda i: (0, 0))],
        out_specs=[kspec, vspec(tk_sel), kspec, vspec(tk_win)],
        out_shape=[jax.ShapeDtypeStruct((N_KV_HEADS, t, aug), BF16), vshape(tk_sel),
                   jax.ShapeDtypeStruct((N_KV_HEADS, t, aug), BF16), vshape(tk_win)],
        name="kv_prep",
    )(z, z, z, z, k_norm_g[1:3])


def _compress_kernel(x_ref, pe_ref, w1_ref, w2_ref, g_ref, o_ref):
    seq = x_ref.shape[0]
    nb = seq // (2 * CMP_BLOCK)
    lane = lax.broadcasted_iota(jnp.int32, (nb, 2 * HEAD_DIM), 1)
    rows = []
    for hh in range(N_KV_HEADS):
        for parity in range(2):
            cols = []
            for s in range(0, CMP_BLOCK, 2):
                a = x_ref[pl.ds(parity * CMP_BLOCK + s, nb, stride=2 * CMP_BLOCK), :]
                b = x_ref[pl.ds(parity * CMP_BLOCK + s + 1, nb, stride=2 * CMP_BLOCK), :]
                if hh == 0:
                    cols.append(jnp.where(lane < HEAD_DIM, a, pltpu.roll(b, HEAD_DIM, 1)))
                else:
                    cols.append(jnp.where(lane < HEAD_DIM, pltpu.roll(a, HEAD_DIM, 1), b))
            rows.append(jnp.concatenate(cols, axis=1))
    x = (jnp.concatenate(rows, axis=0) + pe_ref[0]).astype(BF16)
    hid = _gelu_tanh(_dot(x, w1_ref[0])).astype(BF16)
    y = _dot(hid, w2_ref[0])
    o_ref[0, 0] = jnp.where(pl.program_id(0) == 0, _rms(y, g_ref[...]), y)


def _compress(z, cmp_pe, w1, w2, k_gain, *, bsz, seq):
    nc = seq // CMP_BLOCK
    kdim = CMP_BLOCK * HEAD_DIM
    m = N_KV_HEADS * nc
    out = pl.pallas_call(
        _compress_kernel,
        grid=(2, bsz),
        in_specs=[pl.BlockSpec((seq, KV_WIDTH), lambda w, b: (b, Z_KC // KV_WIDTH + w)),
                  pl.BlockSpec((1, 1, kdim), lambda w, b: (w, 0, 0)),
                  pl.BlockSpec((1, kdim, CMP_HIDDEN), lambda w, b: (w, 0, 0)),
                  pl.BlockSpec((1, CMP_HIDDEN, HEAD_DIM), lambda w, b: (w, 0, 0)),
                  pl.BlockSpec((1, HEAD_DIM), lambda w, b: (0, 0))],
        out_specs=pl.BlockSpec((1, 1, m, HEAD_DIM), lambda w, b: (w, b, 0, 0)),
        out_shape=jax.ShapeDtypeStruct((2, bsz, m, HEAD_DIM), F32),
        name="compress",
    )(z, cmp_pe.reshape(2, 1, kdim), w1, w2, k_gain.reshape(1, HEAD_DIM))
    return out.reshape(2, bsz * N_KV_HEADS, nc, HEAD_DIM)


def _head_gate(gates, kvh, g, branch):
    c0 = g * NSA_BRANCHES + branch
    c1 = (GQA + g) * NSA_BRANCHES + branch
    if isinstance(kvh, int):
        c = c1 if kvh else c0
        return gates[:, c:c + 1]
    return jnp.where(kvh == 0, gates[:, c0:c0 + 1], gates[:, c1:c1 + 1])


def _cmp_select_kernel(q_ref, kc_ref, vc_ref, qg_ref, o_ref, qaug_ref, imp_ref, cnt_ref):
    tq = q_ref.shape[0]
    nc = kc_ref.shape[1]
    nb = nc // 2
    pair_w = 2 * HEAD_DIM
    qw = GQA * HEAD_DIM
    q0 = pl.program_id(1) * tq
    q = q_ref[...]
    x2 = q * q
    hi = x2.astype(BF16)
    lo = (x2 - hi.astype(F32)).astype(BF16)
    wq = q.shape[1]
    seg = jnp.where(jnp.right_shift(lax.broadcasted_iota(jnp.int32, (wq, wq), 0), HEAD_SHIFT)
                    == jnp.right_shift(lax.broadcasted_iota(jnp.int32, (wq, wq), 1), HEAD_SHIFT), 1.0, 0.0).astype(BF16)
    ss = _dot(hi, seg) + _dot(lo, seg)
    qf = q * lax.rsqrt(ss * (1.0 / HEAD_DIM) + RMS_EPS) * qg_ref[...] * (HEAD_DIM ** -0.5)
    qn = qf.astype(BF16)
    q2 = qf * LOG2E

    row = lax.broadcasted_iota(jnp.int32, (nc, tq), 0)
    t = q0 + lax.broadcasted_iota(jnp.int32, (nc, tq), 1)
    blk = jnp.where(row < nb, 2 * row, 2 * (row - nb) + 1)
    valid = (blk + 1) * CMP_BLOCK - 1 <= t
    lane = lax.broadcasted_iota(jnp.int32, (tq, pair_w), 1)
    j = lax.broadcasted_iota(jnp.int32, (nb, tq), 0)
    cur = jnp.right_shift(q0 + lax.broadcasted_iota(jnp.int32, (nb, tq), 1), SEL_SHIFT)
    forced = (j == 0) | (j == cur) | (j == cur - 1)
    visible = j <= cur

    def two_heads(x):
        zx = jnp.zeros_like(x)
        return jnp.concatenate([jnp.concatenate([x, zx], axis=1), jnp.concatenate([zx, x], axis=1)], axis=0)

    pairs = [(kvh, pr) for kvh in range(N_KV_HEADS) for pr in range(GQA // 2)]
    k2 = [two_heads(kc_ref[kvh].astype(BF16)) for kvh in range(N_KV_HEADS)]
    v2 = [two_heads(vc_ref[kvh].astype(BF16)) for kvh in range(N_KV_HEADS)]
    st = [_dot_nt(k2[kvh], qn[:, kvh * qw + pr * pair_w:kvh * qw + (pr + 1) * pair_w]) for kvh, pr in pairs]
    psum = [jnp.zeros((nc, tq), F32) for _ in range(N_KV_HEADS)]
    probs = []
    for (kvh, pr), s2 in zip(pairs, st):
        ph = []
        for hh in range(2):
            sm = jnp.where(valid, s2[hh * nc:(hh + 1) * nc], MASKED)
            m = jnp.max(sm, axis=0, keepdims=True)
            m = jnp.where(m > 0.5 * MASKED, m, 0.0)
            e = jnp.exp(sm - m)
            p = e * (1.0 / jnp.maximum(jnp.sum(e, axis=0, keepdims=True), 1e-30))
            psum[kvh] = psum[kvh] + p
            ph.append(p.astype(BF16))
        probs.append(jnp.concatenate(ph, axis=0))
    imps = []
    for kvh in range(N_KV_HEADS):
        imp = psum[kvh][:nb] + psum[kvh][nb:]
        imp = jnp.where(forced, FORCE_SCORE, jnp.where(visible, imp, -jnp.inf))
        imp_ref[kvh] = imp
        imps.append(imp)
    for n, ((kvh, pr), p2) in enumerate(zip(pairs, probs)):
        o_ref[n * pair_w:(n + 1) * pair_w, :] = lax.dot_general(
            v2[kvh], p2, (((0,), (0,)), ((), ())), preferred_element_type=F32)

    n_vis = jnp.minimum(nb, (q0 + tq - 1) // SEL_BLOCK + 1)
    n_slabs = nb // SUBLANES
    row_in_slab = lax.broadcasted_iota(jnp.int32, (SUBLANES, tq), 0)
    cnt_ref[...] = jnp.zeros(cnt_ref.shape, F32)
    for grp in range(n_slabs):
        @pl.when(grp * SUBLANES < n_vis)
        def _(grp=grp):
            for kvh in range(N_KV_HEADS):
                slabs = [imp_ref[kvh, s * SUBLANES:(s + 1) * SUBLANES, :] for s in range(n_slabs)]
                cnt = [jnp.zeros((SUBLANES, tq), F32) for _ in range(n_slabs)]
                for i in range(grp * SUBLANES, (grp + 1) * SUBLANES):
                    vi = imp_ref[kvh, i:i + 1, :]
                    for s in range(n_slabs):
                        if s > grp:
                            beats = vi >= slabs[s]
                        elif s < grp:
                            beats = vi > slabs[s]
                        else:
                            beats = (vi > slabs[s]) | ((vi == slabs[s]) & (i - s * SUBLANES < row_in_slab))
                        cnt[s] = cnt[s] + jnp.where(beats, 1.0, 0.0)
                cnt_ref[kvh] += jnp.concatenate(cnt, axis=0)
    for kvh in range(N_KV_HEADS):
        selneg_t = jnp.where(visible & (cnt_ref[kvh] < float(N_SELECT)), 0.0, MASKED)
        if nb < HEAD_DIM:
            selneg_t = jnp.concatenate([selneg_t, jnp.zeros((HEAD_DIM - nb, tq), F32)], axis=0)
        selneg = jnp.concatenate([selneg_t, selneg_t], axis=0).T
        for pr in range(GQA // 2):
            c0 = kvh * qw + pr * pair_w
            qp = q2[:, c0:c0 + pair_w]
            qaug_ref[kvh, 2 * pr] = jnp.where(lane < HEAD_DIM, qp, selneg).astype(BF16)
            qaug_ref[kvh, 2 * pr + 1] = jnp.where(lane < HEAD_DIM, pltpu.roll(qp, HEAD_DIM, 1), selneg).astype(BF16)


def _cmp_select(z, cmp_kv, q_norm_g, *, bsz, seq, tq):
    t = z.shape[0]
    nq = seq // tq
    nc = seq // CMP_BLOCK
    aug = 2 * HEAD_DIM
    qw = GQA * HEAD_DIM
    kc, vc = cmp_kv[0], cmp_kv[1]
    kv_spec = pl.BlockSpec((N_KV_HEADS, nc, HEAD_DIM), lambda b, i: (b, 0, 0))
    return pl.pallas_call(
        _cmp_select_kernel,
        grid=(bsz, nq),
        in_specs=[pl.BlockSpec((tq, ATTN_WIDTH), lambda b, i: (b * nq + i, Z_Q // ATTN_WIDTH)),
                  kv_spec, kv_spec,
                  pl.BlockSpec((1, ATTN_WIDTH), lambda b, i: (0, 0))],
        out_specs=[pl.BlockSpec((ATTN_WIDTH, tq), lambda b, i: (0, b * nq + i)),
                   pl.BlockSpec((N_KV_HEADS, GQA, tq, aug), lambda b, i: (b, 0, i, 0))],
        out_shape=[jax.ShapeDtypeStruct((ATTN_WIDTH, t), F32),
                   jax.ShapeDtypeStruct((bsz * N_KV_HEADS, GQA, seq, aug), BF16)],
        scratch_shapes=[pltpu.VMEM((N_KV_HEADS, nc // 2, tq), F32), pltpu.VMEM((N_KV_HEADS, nc // 2, tq), F32)],
        name="cmp_select",
    )(z, kc, vc, jnp.tile(q_norm_g, N_HEADS).reshape(1, ATTN_WIDTH))


def _flash_kernel(q_ref, k_ref, vt_ref, o_ref, sa_ref, sb_ref, pa_ref, pb_ref, m_ref, a_ref, acc_ref, *, window):
    tq = q_ref.shape[2]
    tk = vt_ref.shape[3]
    cols = GQA * tq
    q0 = pl.program_id(2) * tq
    qa = q_ref[0].reshape(cols, q_ref.shape[3])
    t = q0 + (lax.broadcasted_iota(jnp.int32, (tk, cols), 1) & (tq - 1))
    key_row = lax.broadcasted_iota(jnp.int32, (tk, cols), 0)
    span = 2 * tk
    lo = jnp.maximum(0, q0 - window + 1) // span if window else 0
    hi = (q0 + tq + span - 1) // span
    mid1 = jnp.maximum(lo, (q0 + tq - 1 - window) // span + 1) if window else lo
    mid2 = q0 // span
    first = 2 * lo
    last = 2 * hi - 1

    def scores(kb):
        return _dot_nt(k_ref[0, pl.ds(pl.multiple_of(kb * tk, tk), tk), :], qa)

    m_ref[...] = jnp.full(m_ref.shape, MASKED, F32)
    a_ref[...] = jnp.ones(a_ref.shape, F32)
    acc_ref[...] = jnp.zeros(acc_ref.shape, F32)
    pb_ref[...] = jnp.zeros(pb_ref.shape, BF16)
    sa_ref[...] = scores(first)

    def tile(kb, s_cur, s_nxt, p_prev, p_cur, mask):
        pv = _dot(vt_ref[0, jnp.maximum(kb - 1, first)], p_prev[...])
        s_nxt[...] = scores(jnp.minimum(kb + 1, last))
        s = s_cur[...]
        if mask == 'causal':
            s = jnp.where(kb * tk + key_row <= t, s, MASKED)
        elif mask == 'window':
            s = jnp.where(kb * tk + key_row > t - window, s, MASKED)
        m_old = m_ref[...]
        m_new = jnp.maximum(m_old, jnp.max(s, axis=0, keepdims=True))
        acc_ref[...] = a_ref[...] * acc_ref[...] + pv
        a_ref[...] = jnp.exp2(m_old - m_new)
        p_cur[...] = jnp.exp2(s - m_new).astype(BF16)
        m_ref[...] = m_new

    def pair_step(pair, mask):
        tile(2 * pair, sa_ref, sb_ref, pb_ref, pa_ref, mask)
        tile(2 * pair + 1, sb_ref, sa_ref, pa_ref, pb_ref, mask)

    def run(begin, end, mask, unroll=1):
        n_main = (end - begin) // unroll

        def main(i, carry):
            for u in range(unroll):
                pair_step(begin + i * unroll + u, mask)
            return carry

        def rest(pair, carry):
            pair_step(pair, mask)
            return carry

        if unroll > 1:
            lax.fori_loop(0, n_main, main, 0)
        lax.fori_loop(begin + n_main * unroll if unroll > 1 else begin, end, rest, 0)

    if window:
        interior = (mid1 - lo == 1) & (mid2 - mid1 == 1) & (hi - mid2 == 1)

        @pl.when(interior)
        def _():
            pair_step(lo, 'window')
            pair_step(lo + 1, None)
            pair_step(lo + 2, 'causal')

        @pl.when(jnp.logical_not(interior))
        def _():
            run(lo, mid1, 'window')
            run(mid1, mid2, None)
            run(mid2, hi, 'causal')
    else:
        run(mid1, mid2, None, unroll=FLASH_UNROLL)
        run(mid2, hi, 'causal')
    acc = a_ref[...] * acc_ref[...] + _dot(vt_ref[0, last], pb_ref[...])
    o_t = acc[:HEAD_DIM] / acc[HEAD_DIM:HEAD_DIM + 1]
    for g in range(GQA):
        o_ref[g * HEAD_DIM:(g + 1) * HEAD_DIM, :] = o_t[:, g * tq:(g + 1) * tq]


def _flash(qaug, k, vt, *, bsz, seq, tq, window, name):
    t = bsz * seq
    nq = seq // tq
    aug = qaug.shape[3]
    tk = vt.shape[3]
    qw = GQA * HEAD_DIM
    cols = GQA * tq
    return pl.pallas_call(
        functools.partial(_flash_kernel, window=window),
        grid=(bsz, N_KV_HEADS, nq),
        in_specs=[pl.BlockSpec((1, GQA, tq, aug), lambda b, h, i: (b * N_KV_HEADS + h, 0, i, 0)),
                  pl.BlockSpec((1, seq, aug), lambda b, h, i: (h, b, 0)),
                  pl.BlockSpec((1, seq // tk, V_ROWS, tk), lambda b, h, i: (h, b, 0, 0))],
        out_specs=pl.BlockSpec((qw, tq), lambda b, h, i: (h, b * nq + i)),
        out_shape=jax.ShapeDtypeStruct((ATTN_WIDTH, t), F32),
        scratch_shapes=[pltpu.VMEM((tk, cols), F32), pltpu.VMEM((tk, cols), F32),
                        pltpu.VMEM((tk, cols), BF16), pltpu.VMEM((tk, cols), BF16),
                        pltpu.VMEM((1, cols), F32), pltpu.VMEM((1, cols), F32),
                        pltpu.VMEM((V_ROWS, cols), F32)],
        name=name,
    )(qaug, k, vt)


def _merge_kernel(x_ref, g0_ref, g1_ref, g2_ref, cb_ref, cc_ref, cx_ref, pc_ref, px_ref, gl_ref, ys_ref,
                  oc_ref, os_ref, ow_ref, cw_ref, wc_ref, wglu_ref, wo_ref, wout_ref, out_ref, *, seq):
    tm = x_ref.shape[0]
    yg = _dot(_gelu_tanh(ys_ref[...]).astype(BF16), wglu_ref[...])
    y_ssm = yg[:, :D_MODEL] * _sigmoid(yg[:, D_MODEL:])
    zc = cc_ref[...] * cx_ref[...]
    keep = jnp.where((pl.program_id(0) * tm) % seq != 0, 1.0, 0.0)
    prev = pc_ref[...] * px_ref[...] * keep
    row = lax.broadcasted_iota(jnp.int32, zc.shape, 0)
    z1 = jnp.where(row >= 1, pltpu.roll(zc, 1, 0), prev[7:8, :])
    z2 = jnp.where(row >= 2, pltpu.roll(zc, 2, 0), jnp.where(row == 1, prev[7:8, :], prev[6:7, :]))
    conv = cw_ref[0:1, :] * z2 + cw_ref[1:2, :] * z1 + cw_ref[2:3, :] * zc
    y_conv = _dot((cb_ref[...] * conv).astype(BF16), wc_ref[...])
    gates_t = _sigmoid(gl_ref[...]).T
    heads = []
    for hd in range(N_HEADS):
        rows = slice(hd * HEAD_DIM, (hd + 1) * HEAD_DIM)
        c = hd * NSA_BRANCHES
        heads.append(oc_ref[rows, :] * gates_t[c:c + 1] + os_ref[rows, :] * gates_t[c + 1:c + 2]
                     + ow_ref[rows, :] * gates_t[c + 2:c + 3])
    o_t = jnp.concatenate(heads, axis=0).astype(BF16)
    y_attn = lax.dot_general(o_t, wo_ref[...], (((0,), (0,)), ((), ())), preferred_element_type=F32)
    mixed =_sigmoid(g0_ref[...]) * y_ssm + _sigmoid(g1_ref[...]) * y_conv + _sigmoid(g2_ref[...]) * y_attn
    out_ref[...] = x_ref[...] + _dot(mixed.astype(BF16), wout_ref[...])


def _merge(x2, z, ys, o_cmp, o_sel, o_win, conv_w, wc, wglu, wo, wout, *, seq, tm):
    t, d = x2.shape
    rb = tm // 8
    zc = lambda width, off: pl.BlockSpec((tm, width), lambda i, o=off // width: (i, o))
    zprev = lambda off: pl.BlockSpec((8, CONV_CH), lambda i, o=off // CONV_CH: (jnp.maximum(i * rb - 1, 0), o))
    row = lambda width: pl.BlockSpec((tm, width), lambda i: (i, 0))
    full = lambda a: pl.BlockSpec(a.shape, lambda i: (0, 0))
    branch_t = pl.BlockSpec((ATTN_WIDTH, tm), lambda i: (0, i))
    return pl.pallas_call(
        functools.partial(_merge_kernel, seq=seq),
        grid=(t // tm,),
        in_specs=[row(d), zc(d, Z_MIX), zc(d, Z_MIX + d), zc(d, Z_MIX + 2 * d),
                  zc(CONV_CH, Z_CB), zc(CONV_CH, Z_CC), zc(CONV_CH, Z_CX), zprev(Z_CC), zprev(Z_CX),
                  zc(GATE_PAD, Z_GATE), row(SSM_WIDTH), branch_t, branch_t, branch_t,
                  full(conv_w), full(wc), full(wglu), full(wo), full(wout)],
        out_specs=row(d),
        out_shape=jax.ShapeDtypeStruct((t, d), F32),
        name="merge",
    )(x2, z, z, z, z, z, z, z, z, z, ys, o_cmp, o_sel, o_win, conv_w, wc, wglu, wo, wout)


def _ffn_kernel(x_ref, g_ref, wg_ref, wu_ref, wd_ref, o_ref, h_ref, acc_ref):
    f = pl.program_id(1)

    @pl.when(f == 0)
    def _():
        h_ref[...] = _rms(x_ref[...], g_ref[...]).astype(BF16)
        acc_ref[...] = jnp.zeros(acc_ref.shape, F32)

    h = h_ref[...]
    gate = _dot(h, wg_ref[...])
    up = _dot(h, wu_ref[...])
    act = (gate * _sigmoid(gate) * up).astype(BF16)
    acc_ref[...] += _dot(act, wd_ref[...])

    @pl.when(f == pl.num_programs(1) - 1)
    def _():
        o_ref[...] = x_ref[...] + acc_ref[...]


def _ffn(x2, g, w_gate_up, w_down, *, tm, tf):
    t, d = x2.shape
    nf = D_FF // tf
    return pl.pallas_call(
        _ffn_kernel,
        grid=(t // tm, nf),
        in_specs=[pl.BlockSpec((tm, d), lambda i, f: (i, 0)),
                  pl.BlockSpec((1, d), lambda i, f: (0, 0)),
                  pl.BlockSpec((d, tf), lambda i, f: (0, f)),
                  pl.BlockSpec((d, tf), lambda i, f: (0, nf + f)),
                  pl.BlockSpec((tf, d), lambda i, f: (f, 0))],
        out_specs=pl.BlockSpec((tm, d), lambda i, f: (i, 0)),
        out_shape=jax.ShapeDtypeStruct((t, d), F32),
        scratch_shapes=[pltpu.VMEM((tm, d), BF16), pltpu.VMEM((tm, d), F32)],
        compiler_params=pltpu.CompilerParams(dimension_semantics=("parallel", "arbitrary")),
        name="ffn",
    )(x2, g.reshape(1, d), w_gate_up, w_gate_up, w_down)


def _pick(n, pref):
    while n % pref:
        pref //= 2
    return pref


def _layer(x2, p, *, bsz, seq):
    t = x2.shape[0]
    z = _inproj(x2, p['mix_norm_g'], p['w_in'], tm=_pick(t, 1024), tn=Z_WIDTH // 3)
    ys = _s5_scan(z, p['s5'], bsz=bsz, seq=seq)
    kaug, vsel, kwin, vwin = _kv_prep(z, p['k_norm_g'], seq=seq, tp=_pick(seq, 512), tk_sel=128, tk_win=128)
    cmp_kv = _compress(z, p['cmp_pe'], p['cmp_w1'], p['cmp_w2'], p['k_norm_g'][0], bsz=bsz, seq=seq)
    o_cmp, qaug = _cmp_select(z, cmp_kv, p['q_norm_g'], bsz=bsz, seq=seq, tq=256)
    tq = _pick(seq, 256)
    o_sel = _flash(qaug, kaug, vsel, bsz=bsz, seq=seq, tq=tq, window=0, name="sel_attn")
    o_win = _flash(qaug, kwin, vwin, bsz=bsz, seq=seq, tq=tq, window=WINDOW, name="win_attn")
    x2 = _merge(x2, z, ys, o_cmp, o_sel, o_win, p['conv_w'], p['conv_w_out'], p['ssm_w_glu'], p['nsa_w_o'],
                p['w_out'], seq=seq, tm=_pick(seq, 256))
    return _ffn(x2, p['ffn_norm_g'], p['ffn_w_gate_up'], p['ffn_w_down'], tm=_pick(t, 1024), tf=256)


def kernel(x, mix_norm_g, w_in, ssm_lam_re, ssm_lam_im, ssm_b_re, ssm_b_im, ssm_c_re, ssm_c_im, ssm_d, ssm_log_dt, ssm_w_glu, conv_w, conv_w_out, q_norm_g, k_norm_g, cmp_pe, cmp_w1, cmp_w2, nsa_w_o, w_out, ffn_norm_g, ffn_w_gate_up, ffn_w_down):
    bsz, seq, d = x.shape
    x2 = x.reshape(bsz * seq, d)
    for i in range(w_in.shape[0]):
        p = dict(
            mix_norm_g=mix_norm_g[i], w_in=_permute_w_in(w_in[i]),
            s5=_s5_tables(ssm_lam_re[i], ssm_lam_im[i], ssm_b_re[i], ssm_b_im[i], ssm_c_re[i], ssm_c_im[i],
                          ssm_d[i], ssm_log_dt[i], seq // S5_CHUNK),
            ssm_w_glu=ssm_w_glu[i].astype(BF16), conv_w=conv_w[i], conv_w_out=conv_w_out[i].astype(BF16),
            q_norm_g=q_norm_g[i], k_norm_g=k_norm_g[i], cmp_pe=cmp_pe[i],
            cmp_w1=cmp_w1[i].astype(BF16), cmp_w2=cmp_w2[i].astype(BF16), nsa_w_o=nsa_w_o[i].astype(BF16),
            w_out=w_out[i].astype(BF16), ffn_norm_g=ffn_norm_g[i],
            ffn_w_gate_up=ffn_w_gate_up[i].astype(BF16), ffn_w_down=ffn_w_down[i].astype(BF16))
        x2 = _layer(x2, p, bsz=bsz, seq=seq)
    return x2.reshape(bsz, seq, d)
```

```python
import functools

import jax
import jax.numpy as jnp
import numpy as np
from jax import lax
from jax.experimental import pallas as pl
from jax.experimental.pallas import tpu as pltpu

F32 = jnp.float32
BF16 = jnp.bfloat16

D_MODEL = 1024
SSM_WIDTH = 512
SSM_GROUP = 16
SSM_GROUPS = SSM_WIDTH // SSM_GROUP
SSM_STATE = 64
CONV_CH = 512
CONV_K = 3
HEAD_DIM = 64
HEAD_SHIFT = 6
N_HEADS = 8
N_KV_HEADS = 2
GQA = N_HEADS // N_KV_HEADS
ATTN_WIDTH = N_HEADS * HEAD_DIM
KV_WIDTH = N_KV_HEADS * HEAD_DIM
CMP_BLOCK = 32
SEL_BLOCK = 64
SEL_SHIFT = 6
N_SELECT = 16
WINDOW = 512
CMP_HIDDEN = 256
FORCE_SCORE = 1e4
NSA_BRANCHES = 3
MIX_BRANCHES = 3
D_FF = 2816
RMS_EPS = 1e-6

LANES = 128
SUBLANES = 8
MASKED = -1e30
LOG2E = 1.4426950408889634
FLASH_UNROLL = 4
S5_CHUNK = 16
S5_TILE_GROUPS = LANES // SSM_GROUP
GATE_PAD = LANES
BF16_SUBLANES = 16
V_ROWS = HEAD_DIM + BF16_SUBLANES

Z_MIX = 0
Z_U = Z_MIX + MIX_BRANCHES * D_MODEL
Z_CB = Z_U + SSM_WIDTH
Z_CC = Z_CB + CONV_CH
Z_CX = Z_CC + CONV_CH
Z_Q = Z_CX + CONV_CH
Z_KC = Z_Q + ATTN_WIDTH
Z_VC = Z_KC + KV_WIDTH
Z_KS = Z_VC + KV_WIDTH
Z_VS = Z_KS + KV_WIDTH
Z_KW = Z_VS + KV_WIDTH
Z_VW = Z_KW + KV_WIDTH
Z_GATE = Z_VW + KV_WIDTH
Z_WIDTH = Z_GATE + GATE_PAD


def _gelu_tanh(x):
    return 0.5 * x * (1.0 + jnp.tanh(np.sqrt(2.0 / np.pi).astype(np.float32) * (x + 0.044715 * (x * x * x))))


def _sigmoid(x):
    return 0.5 * jnp.tanh(0.5 * x) + 0.5


def _rms(x, g):
    return x * lax.rsqrt(jnp.mean(x * x, axis=-1, keepdims=True) + RMS_EPS) * g


def _dot(a, b):
    return jnp.dot(a, b, preferred_element_type=F32)


def _dot_nt(a, b):
    return lax.dot_general(a, b, (((1,), (1,)), ((), ())), preferred_element_type=F32)


def _inproj_kernel(x_ref, g_ref, w_ref, o_ref):
    h = _rms(x_ref[...], g_ref[...]).astype(BF16)
    o_ref[...] = _dot(h, w_ref[...])


def _inproj(x2, g, w, *, tm, tn):
    t, d = x2.shape
    n = w.shape[1]
    return pl.pallas_call(
        _inproj_kernel,
        grid=(n // tn, t // tm),
        in_specs=[pl.BlockSpec((tm, d), lambda j, i: (i, 0)),
                  pl.BlockSpec((1, d), lambda j, i: (0, 0)),
                  pl.BlockSpec((d, tn), lambda j, i: (0, j))],
        out_specs=pl.BlockSpec((tm, tn), lambda j, i: (i, j)),
        out_shape=jax.ShapeDtypeStruct((t, n), F32),
        name="inproj",
    )(x2, g.reshape(1, d), w)


def _permute_w_in(w):
    n_plain = SSM_WIDTH + 3 * CONV_CH + ATTN_WIDTH + 6 * KV_WIDTH
    n_gate = N_HEADS * NSA_BRANCHES
    gate = jnp.pad(w[:, n_plain:n_plain + n_gate], ((0, 0), (0, GATE_PAD - n_gate)))
    return jnp.concatenate([w[:, n_plain + n_gate:], w[:, :n_plain], gate], axis=1).astype(BF16)


def _s5_tables(lam_re, lam_im, b_re, b_im, c_re, c_im, d_skip, log_dt, n_chunks):
    hp = lax.Precision.HIGHEST
    g, p = lam_re.shape
    h, l, gt = SSM_GROUP, S5_CHUNK, S5_TILE_GROUPS
    nt = g // gt
    dt = jnp.exp(log_dt)[:, None]
    ar, ai = lam_re * dt, lam_im * dt

    def powers(k):
        mag = jnp.exp(ar[None] * k[:, None, None])
        ang = ai[None] * k[:, None, None]
        return mag * jnp.cos(ang), mag * jnp.sin(ang)

    pr, pi = powers(jnp.arange(l + 1, dtype=F32))
    nr, ni = pr[1] - 1.0, pi[1]
    den = lam_re * lam_re + lam_im * lam_im
    fr, fi = (nr * lam_re + ni * lam_im) / den, (ni * lam_re - nr * lam_im) / den
    bbr = fr[..., None] * b_re - fi[..., None] * b_im
    bbi = fr[..., None] * b_im + fi[..., None] * b_re
    wr = pr[:l, :, :, None] * bbr[None] - pi[:l, :, :, None] * bbi[None]
    wi = pr[:l, :, :, None] * bbi[None] + pi[:l, :, :, None] * bbr[None]
    kern = (jnp.einsum('gop,tgpi->tgio', c_re, wr, precision=hp)
            - jnp.einsum('gop,tgpi->tgio', c_im, wi, precision=hp))
    r = jnp.arange(l)

    def group_diag(x, row_w, col_w):
        rows, w = x.shape[-2:]
        expand = (jnp.arange(w)[:, None] == jnp.arange(gt * w)[None, :] % w).astype(BF16)
        tiled = jnp.einsum('...w,wv->...v', x.astype(BF16), expand, preferred_element_type=BF16)
        own = (jnp.arange(rows)[:, None] // row_w) % gt == jnp.arange(gt * w)[None, :] // col_w
        return jnp.where(own, tiled, jnp.zeros((), BF16))

    kblk = group_diag(kern.reshape(l, g * h, h), h, h)
    krev = kblk.reshape(l, nt, LANES, LANES)[::-1].transpose(1, 0, 2, 3).reshape(nt, l * LANES, LANES)
    krev = jnp.concatenate([krev, jnp.zeros((nt, LANES, LANES), krev.dtype)], axis=1)
    t_pair = jnp.concatenate(
        [jnp.concatenate([krev[:, (l - 1 - s) * LANES:(l + 1) * LANES], krev[:, (l - 2 - s) * LANES:l * LANES]], axis=2)
         for s in range(0, l, 2)], axis=1)
    qr, qi = pr[l - 1 - r], pi[l - 1 - r]
    st_re = qr[..., None] * bbr[None] - qi[..., None] * bbi[None]
    st_im = qr[..., None] * bbi[None] + qi[..., None] * bbr[None]
    rows_in = lambda x: group_diag(x.transpose(0, 1, 3, 2).reshape(l, g * h, p), h, p)
    s_tab = jnp.concatenate([rows_in(st_re), rows_in(st_im)], axis=-1)
    s_tab = s_tab.reshape(l, nt, LANES, 2 * gt * p)
    er, ei = pr[1:l + 1][:, :, None, :], pi[1:l + 1][:, :, None, :]
    rows_out = lambda x: group_diag(x.transpose(0, 1, 3, 2).reshape(l, g * p, h), p, h).reshape(l, nt, gt * p, LANES)
    c_tab = jnp.concatenate([rows_out(c_re[None] * er - c_im[None] * ei),
                             rows_out(-(c_re[None] * ei + c_im[None] * er))], axis=2)
    c_tab = c_tab.reshape(l // 2, 2, nt, 2 * gt * p, LANES).transpose(0, 2, 3, 1, 4).reshape(
        l // 2, nt, 2 * gt * p, 2 * LANES)
    n_steps = max(1, int(np.ceil(np.log2(n_chunks))))
    dr, di = powers(l * (2.0 ** jnp.arange(n_steps, dtype=F32)))
    lanes = lambda m: m.reshape(n_steps, nt, gt * p).transpose(1, 0, 2)
    return dict(t_pair=t_pair, s_tab=s_tab, c_tab=c_tab, d_re=lanes(dr), d_im=lanes(di),
                d_skip=d_skip.reshape(1, g * h))


def _s5_kernel(u_ref, tp_ref, s_ref, c_ref, dre_ref, dim_ref, dsk_ref, y_ref):
    seq = u_ref.shape[0]
    l = S5_CHUNK
    nc = seq // l
    n_steps = dre_ref.shape[1]
    ns = dre_ref.shape[2]
    us = [u_ref[pl.ds(r, nc, stride=l), :] for r in range(l)]
    ucat = jnp.concatenate(us, axis=1).astype(BF16)
    e = _dot(ucat, s_ref[:, 0].reshape(l * LANES, 2 * ns))
    xr, xi = e[:, :ns], e[:, ns:]
    row = lax.broadcasted_iota(jnp.int32, xr.shape, 0)
    for k in range(n_steps):
        s = 1 << k
        if s >= nc:
            break
        dr = dre_ref[0, k:k + 1, :]
        di = dim_ref[0, k:k + 1, :]
        sr = jnp.where(row >= s, pltpu.roll(xr, s, 0), 0.0)
        si = jnp.where(row >= s, pltpu.roll(xi, s, 0), 0.0)
        xr, xi = xr + (dr * sr - di * si), xi + (dr * si + di * sr)
    xp = jnp.concatenate([jnp.where(row >= 1, pltpu.roll(xr, 1, 0), 0.0),
                          jnp.where(row >= 1, pltpu.roll(xi, 1, 0), 0.0)], axis=1).astype(BF16)
    dsk = dsk_ref[...]
    off = 0
    for s in range(0, l, 2):
        rows = (s + 2) * LANES
        y2 = _dot(ucat[:, :rows], tp_ref[0, off:off + rows, :]) + _dot(xp, c_ref[s // 2, 0])
        y_ref[pl.ds(s, nc, stride=l), :] = y2[:, :LANES] + dsk * us[s]
        y_ref[pl.ds(s + 1, nc, stride=l), :] = y2[:, LANES:] + dsk * us[s + 1]
        off += rows


def _s5_scan(z, tabs, *, bsz, seq):
    t = z.shape[0]
    nt = SSM_WIDTH // LANES
    tile3 = lambda a: pl.BlockSpec((1,) + a.shape[1:], lambda j, b: (j, 0, 0))
    step4 = lambda a: pl.BlockSpec((a.shape[0], 1) + a.shape[2:], lambda j, b: (0, j, 0, 0))
    return pl.pallas_call(
        _s5_kernel,
        grid=(nt, bsz),
        in_specs=[pl.BlockSpec((seq, LANES), lambda j, b: (b, Z_U // LANES + j)),
                  tile3(tabs['t_pair']), step4(tabs['s_tab']), step4(tabs['c_tab']),
                  tile3(tabs['d_re']), tile3(tabs['d_im']),
                  pl.BlockSpec((1, LANES), lambda j, b: (0, j))],
        out_specs=pl.BlockSpec((seq, LANES), lambda j, b: (b, j)),
        out_shape=jax.ShapeDtypeStruct((t, SSM_WIDTH), F32),
        name="s5_scan",
    )(z, tabs['t_pair'], tabs['s_tab'], tabs['c_tab'], tabs['d_re'], tabs['d_im'], tabs['d_skip'])


def _kv_prep_kernel(ks_ref, vs_ref, kw_ref, vw_ref, g_ref, kaug_ref, vsel_ref, kwin_ref, vwin_ref, *, seq):
    tp = ks_ref.shape[0]
    pos = (pl.program_id(0) * tp) % seq + lax.broadcasted_iota(jnp.int32, (tp, HEAD_DIM), 0)
    blk = jnp.right_shift(pos, SEL_SHIFT)
    onehot = jnp.where(lax.broadcasted_iota(jnp.int32, (tp, HEAD_DIM), 1) == blk, 1.0, 0.0).astype(BF16)
    zeros = jnp.zeros((tp, HEAD_DIM), BF16)
    for hh in range(N_KV_HEADS):
        sl = slice(hh * HEAD_DIM, (hh + 1) * HEAD_DIM)
        kn = _rms(ks_ref[:, sl], g_ref[0:1, :]).astype(BF16)
        kaug_ref[hh] = jnp.concatenate([kn, onehot], axis=1)
        kn = _rms(kw_ref[:, sl], g_ref[1:2, :]).astype(BF16)
        kwin_ref[hh] = jnp.concatenate([kn, zeros], axis=1)
    for v_ref, vt_ref in ((vs_ref, vsel_ref), (vw_ref, vwin_ref)):
        tk = vt_ref.shape[3]
        vt = v_ref[...].T.astype(BF16)
        ones = jnp.where(lax.broadcasted_iota(jnp.int32, (V_ROWS - HEAD_DIM, tp), 0) == 0, 1.0, 0.0).astype(BF16)
        for hh in range(N_KV_HEADS):
            vh = jnp.concatenate([vt[hh * HEAD_DIM:(hh + 1) * HEAD_DIM], ones], axis=0)
            for j in range(tp // tk):
                vt_ref[hh, j] = vh[:, j * tk:(j + 1) * tk]


def _kv_prep(z, k_norm_g, *, seq, tp, tk_sel, tk_win):
    t = z.shape[0]
    col = lambda off: pl.BlockSpec((tp, KV_WIDTH), lambda i, o=off // KV_WIDTH: (i, o))
    aug = 2 * HEAD_DIM
    kspec = pl.BlockSpec((N_KV_HEADS, tp, aug), lambda i: (0, i, 0))
    vspec = lambda tk: pl.BlockSpec((N_KV_HEADS, tp // tk, V_ROWS, tk), lambda i: (0, i, 0, 0))
    vshape = lambda tk: jax.ShapeDtypeStruct((N_KV_HEADS, t // tk, V_ROWS, tk), BF16)
    return pl.pallas_call(
        functools.partial(_kv_prep_kernel, seq=seq),
        grid=(t // tp,),
        in_specs=[col(Z_KS), col(Z_VS), col(Z_KW), col(Z_VW), pl.BlockSpec((2, HEAD_DIM), lambda i: (0, 0))],
        out_specs=[kspec, vspec(tk_sel), kspec, vspec(tk_win)],
        out_shape=[jax.ShapeDtypeStruct((N_KV_HEADS, t, aug), BF16), vshape(tk_sel),
                   jax.ShapeDtypeStruct((N_KV_HEADS, t, aug), BF16), vshape(tk_win)],
        name="kv_prep",
    )(z, z, z, z, k_norm_g[1:3])


def _compress_kernel(x_ref, pe_ref, w1_ref, w2_ref, g_ref, o_ref):
    seq = x_ref.shape[0]
    nb = seq // (2 * CMP_BLOCK)
    lane = lax.broadcasted_iota(jnp.int32, (nb, 2 * HEAD_DIM), 1)
    rows = []
    for hh in range(N_KV_HEADS):
        for parity in range(2):
            cols = []
            for s in range(0, CMP_BLOCK, 2):
                a = x_ref[pl.ds(parity * CMP_BLOCK + s, nb, stride=2 * CMP_BLOCK), :]
                b = x_ref[pl.ds(parity * CMP_BLOCK + s + 1, nb, stride=2 * CMP_BLOCK), :]
                if hh == 0:
                    cols.append(jnp.where(lane < HEAD_DIM, a, pltpu.roll(b, HEAD_DIM, 1)))
                else:
                    cols.append(jnp.where(lane < HEAD_DIM, pltpu.roll(a, HEAD_DIM, 1), b))
            rows.append(jnp.concatenate(cols, axis=1))
    x = (jnp.concatenate(rows, axis=0) + pe_ref[0]).astype(BF16)
    hid = _gelu_tanh(_dot(x, w1_ref[0])).astype(BF16)
    y = _dot(hid, w2_ref[0])
    o_ref[0, 0] = jnp.where(pl.program_id(0) == 0, _rms(y, g_ref[...]), y)


def _compress(z, cmp_pe, w1, w2, k_gain, *, bsz, seq):
    nc = seq // CMP_BLOCK
    kdim = CMP_BLOCK * HEAD_DIM
    m = N_KV_HEADS * nc
    out = pl.pallas_call(
        _compress_kernel,
        grid=(2, bsz),
        in_specs=[pl.BlockSpec((seq, KV_WIDTH), lambda w, b: (b, Z_KC // KV_WIDTH + w)),
                  pl.BlockSpec((1, 1, kdim), lambda w, b: (w, 0, 0)),
                  pl.BlockSpec((1, kdim, CMP_HIDDEN), lambda w, b: (w, 0, 0)),
                  pl.BlockSpec((1, CMP_HIDDEN, HEAD_DIM), lambda w, b: (w, 0, 0)),
                  pl.BlockSpec((1, HEAD_DIM), lambda w, b: (0, 0))],
        out_specs=pl.BlockSpec((1, 1, m, HEAD_DIM), lambda w, b: (w, b, 0, 0)),
        out_shape=jax.ShapeDtypeStruct((2, bsz, m, HEAD_DIM), F32),
        name="compress",
    )(z, cmp_pe.reshape(2, 1, kdim), w1, w2, k_gain.reshape(1, HEAD_DIM))
    return out.reshape(2, bsz * N_KV_HEADS, nc, HEAD_DIM)


def _head_gate(gates, kvh, g, branch):
    c0 = g * NSA_BRANCHES + branch
    c1 = (GQA + g) * NSA_BRANCHES + branch
    if isinstance(kvh, int):
        c = c1 if kvh else c0
        return gates[:, c:c + 1]
    return jnp.where(kvh == 0, gates[:, c0:c0 + 1], gates[:, c1:c1 + 1])


def _cmp_select_kernel(q_ref, kc_ref, vc_ref, qg_ref, o_ref, qaug_ref, imp_ref, cnt_ref):
    tq = q_ref.shape[0]
    nc = kc_ref.shape[1]
    nb = nc // 2
    pair_w = 2 * HEAD_DIM
    qw = GQA * HEAD_DIM
    q0 = pl.program_id(1) * tq
    q = q_ref[...]
    x2 = q * q
    hi = x2.astype(BF16)
    lo = (x2 - hi.astype(F32)).astype(BF16)
    wq = q.shape[1]
    seg = jnp.where(jnp.right_shift(lax.broadcasted_iota(jnp.int32, (wq, wq), 0), HEAD_SHIFT)
                    == jnp.right_shift(lax.broadcasted_iota(jnp.int32, (wq, wq), 1), HEAD_SHIFT), 1.0, 0.0).astype(BF16)
    ss = _dot(hi, seg) + _dot(lo, seg)
    qf = q * lax.rsqrt(ss * (1.0 / HEAD_DIM) + RMS_EPS) * qg_ref[...] * (HEAD_DIM ** -0.5)
    qn = qf.astype(BF16)
    q2 = qf * LOG2E

    row = lax.broadcasted_iota(jnp.int32, (nc, tq), 0)
    t = q0 + lax.broadcasted_iota(jnp.int32, (nc, tq), 1)
    blk = jnp.where(row < nb, 2 * row, 2 * (row - nb) + 1)
    valid = (blk + 1) * CMP_BLOCK - 1 <= t
    lane = lax.broadcasted_iota(jnp.int32, (tq, pair_w), 1)
    j = lax.broadcasted_iota(jnp.int32, (nb, tq), 0)
    cur = jnp.right_shift(q0 + lax.broadcasted_iota(jnp.int32, (nb, tq), 1), SEL_SHIFT)
    forced = (j == 0) | (j == cur) | (j == cur - 1)
    visible = j <= cur

    def two_heads(x):
        zx = jnp.zeros_like(x)
        return jnp.concatenate([jnp.concatenate([x, zx], axis=1), jnp.concatenate([zx, x], axis=1)], axis=0)

    pairs = [(kvh, pr) for kvh in range(N_KV_HEADS) for pr in range(GQA // 2)]
    k2 = [two_heads(kc_ref[kvh].astype(BF16)) for kvh in range(N_KV_HEADS)]
    v2 = [two_heads(vc_ref[kvh].astype(BF16)) for kvh in range(N_KV_HEADS)]
    st = [_dot_nt(k2[kvh], qn[:, kvh * qw + pr * pair_w:kvh * qw + (pr + 1) * pair_w]) for kvh, pr in pairs]
    psum = [jnp.zeros((nc, tq), F32) for _ in range(N_KV_HEADS)]
    probs = []
    for (kvh, pr), s2 in zip(pairs, st):
        ph = []
        for hh in range(2):
            sm = jnp.where(valid, s2[hh * nc:(hh + 1) * nc], MASKED)
            m = jnp.max(sm, axis=0, keepdims=True)
            m = jnp.where(m > 0.5 * MASKED, m, 0.0)
            e = jnp.exp(sm - m)
            p = e * (1.0 / jnp.maximum(jnp.sum(e, axis=0, keepdims=True), 1e-30))
            psum[kvh] = psum[kvh] + p
            ph.append(p.astype(BF16))
        probs.append(jnp.concatenate(ph, axis=0))
    imps = []
    for kvh in range(N_KV_HEADS):
        imp = psum[kvh][:nb] + psum[kvh][nb:]
        imp = jnp.where(forced, FORCE_SCORE, jnp.where(visible, imp, -jnp.inf))
        imp_ref[kvh] = imp
        imps.append(imp)
    for n, ((kvh, pr), p2) in enumerate(zip(pairs, probs)):
        o_ref[n * pair_w:(n + 1) * pair_w, :] = lax.dot_general(
            v2[kvh], p2, (((0,), (0,)), ((), ())), preferred_element_type=F32)

    n_vis = jnp.minimum(nb, (q0 + tq - 1) // SEL_BLOCK + 1)
    n_slabs = nb // SUBLANES
    row_in_slab = lax.broadcasted_iota(jnp.int32, (SUBLANES, tq), 0)
    cnt_ref[...] = jnp.zeros(cnt_ref.shape, F32)
    for grp in range(n_slabs):
        @pl.when(grp * SUBLANES < n_vis)
        def _(grp=grp):
            for kvh in range(N_KV_HEADS):
                slabs = [imp_ref[kvh, s * SUBLANES:(s + 1) * SUBLANES, :] for s in range(n_slabs)]
                cnt = [jnp.zeros((SUBLANES, tq), F32) for _ in range(n_slabs)]
                for i in range(grp * SUBLANES, (grp + 1) * SUBLANES):
                    vi = imp_ref[kvh, i:i + 1, :]
                    for s in range(n_slabs):
                        if s > grp:
                            beats = vi >= slabs[s]
                        elif s < grp:
                            beats = vi > slabs[s]
                        else:
                            beats = (vi > slabs[s]) | ((vi == slabs[s]) & (i - s * SUBLANES < row_in_slab))
                        cnt[s] = cnt[s] + jnp.where(beats, 1.0, 0.0)
                cnt_ref[kvh] += jnp.concatenate(cnt, axis=0)
    for kvh in range(N_KV_HEADS):
        selneg_t = jnp.where(visible & (cnt_ref[kvh] < float(N_SELECT)), 0.0, MASKED)
        if nb < HEAD_DIM:
            selneg_t = jnp.concatenate([selneg_t, jnp.zeros((HEAD_DIM - nb, tq), F32)], axis=0)
        selneg = jnp.concatenate([selneg_t, selneg_t], axis=0).T
        for pr in range(GQA // 2):
            c0 = kvh * qw + pr * pair_w
            qp = q2[:, c0:c0 + pair_w]
            qaug_ref[kvh, 2 * pr] = jnp.where(lane < HEAD_DIM, qp, selneg).astype(BF16)
            qaug_ref[kvh, 2 * pr + 1] = jnp.where(lane < HEAD_DIM, pltpu.roll(qp, HEAD_DIM, 1), selneg).astype(BF16)


def _cmp_select(z, cmp_kv, q_norm_g, *, bsz, seq, tq):
    t = z.shape[0]
    nq = seq // tq
    nc = seq // CMP_BLOCK
    aug = 2 * HEAD_DIM
    qw = GQA * HEAD_DIM
    kc, vc = cmp_kv[0], cmp_kv[1]
    kv_spec = pl.BlockSpec((N_KV_HEADS, nc, HEAD_DIM), lambda b, i: (b, 0, 0))
    return pl.pallas_call(
        _cmp_select_kernel,
        grid=(bsz, nq),
        in_specs=[pl.BlockSpec((tq, ATTN_WIDTH), lambda b, i: (b * nq + i, Z_Q // ATTN_WIDTH)),
                  kv_spec, kv_spec,
                  pl.BlockSpec((1, ATTN_WIDTH), lambda b, i: (0, 0))],
        out_specs=[pl.BlockSpec((ATTN_WIDTH, tq), lambda b, i: (0, b * nq + i)),
                   pl.BlockSpec((N_KV_HEADS, GQA, tq, aug), lambda b, i: (b, 0, i, 0))],
        out_shape=[jax.ShapeDtypeStruct((ATTN_WIDTH, t), F32),
                   jax.ShapeDtypeStruct((bsz * N_KV_HEADS, GQA, seq, aug), BF16)],
        scratch_shapes=[pltpu.VMEM((N_KV_HEADS, nc // 2, tq), F32), pltpu.VMEM((N_KV_HEADS, nc // 2, tq), F32)],
        name="cmp_select",
    )(z, kc, vc, jnp.tile(q_norm_g, N_HEADS).reshape(1, ATTN_WIDTH))


def _flash_kernel(q_ref, k_ref, vt_ref, o_ref, sa_ref, sb_ref, pa_ref, pb_ref, m_ref, a_ref, acc_ref, *, window):
    tq = q_ref.shape[2]
    tk = vt_ref.shape[3]
    cols = GQA * tq
    q0 = pl.program_id(2) * tq
    qa = q_ref[0].reshape(cols, q_ref.shape[3])
    t = q0 + (lax.broadcasted_iota(jnp.int32, (tk, cols), 1) & (tq - 1))
    key_row = lax.broadcasted_iota(jnp.int32, (tk, cols), 0)
    span = 2 * tk
    lo = jnp.maximum(0, q0 - window + 1) // span if window else 0
    hi = (q0 + tq + span - 1) // span
    mid1 = jnp.maximum(lo, (q0 + tq - 1 - window) // span + 1) if window else lo
    mid2 = q0 // span
    first = 2 * lo
    last = 2 * hi - 1

    def scores(kb):
        return _dot_nt(k_ref[0, pl.ds(pl.multiple_of(kb * tk, tk), tk), :], qa)

    m_ref[...] = jnp.full(m_ref.shape, MASKED, F32)
    a_ref[...] = jnp.ones(a_ref.shape, F32)
    acc_ref[...] = jnp.zeros(acc_ref.shape, F32)
    pb_ref[...] = jnp.zeros(pb_ref.shape, BF16)
    sa_ref[...] = scores(first)

    def tile(kb, s_cur, s_nxt, p_prev, p_cur, mask):
        pv = _dot(vt_ref[0, jnp.maximum(kb - 1, first)], p_prev[...])
        s_nxt[...] = scores(jnp.minimum(kb + 1, last))
        s = s_cur[...]
        if mask == 'causal':
            s = jnp.where(kb * tk + key_row <= t, s, MASKED)
        elif mask == 'window':
            s = jnp.where(kb * tk + key_row > t - window, s, MASKED)
        m_old = m_ref[...]
        m_new = jnp.maximum(m_old, jnp.max(s, axis=0, keepdims=True))
        acc_ref[...] = a_ref[...] * acc_ref[...] + pv
        a_ref[...] = jnp.exp2(m_old - m_new)
        p_cur[...] = jnp.exp2(s - m_new).astype(BF16)
        m_ref[...] = m_new

    def pair_step(pair, mask):
        tile(2 * pair, sa_ref, sb_ref, pb_ref, pa_ref, mask)
        tile(2 * pair + 1, sb_ref, sa_ref, pa_ref, pb_ref, mask)

    def run(begin, end, mask, unroll=1):
        n_main = (end - begin) // unroll

        def main(i, carry):
            for u in range(unroll):
                pair_step(begin + i * unroll + u, mask)
            return carry

        def rest(pair, carry):
            pair_step(pair, mask)
            return carry

        if unroll > 1:
            lax.fori_loop(0, n_main, main, 0)
        lax.fori_loop(begin + n_main * unroll if unroll > 1 else begin, end, rest, 0)

    if window:
        interior = (mid1 - lo == 1) & (mid2 - mid1 == 1) & (hi - mid2 == 1)

        @pl.when(interior)
        def _():
            pair_step(lo, 'window')
            pair_step(lo + 1, None)
            pair_step(lo + 2, 'causal')

        @pl.when(jnp.logical_not(interior))
        def _():
            run(lo, mid1, 'window')
            run(mid1, mid2, None)
            run(mid2, hi, 'causal')
    else:
        run(mid1, mid2, None, unroll=FLASH_UNROLL)
        run(mid2, hi, 'causal')
    acc = a_ref[...] * acc_ref[...] + _dot(vt_ref[0, last], pb_ref[...])
    o_t = acc[:HEAD_DIM] / acc[HEAD_DIM:HEAD_DIM + 1]
    for g in range(GQA):
        o_ref[g * HEAD_DIM:(g + 1) * HEAD_DIM, :] = o_t[:, g * tq:(g + 1) * tq]


def _flash(qaug, k, vt, *, bsz, seq, tq, window, name):
    t = bsz * seq
    nq = seq // tq
    aug = qaug.shape[3]
    tk = vt.shape[3]
    qw = GQA * HEAD_DIM
    cols = GQA * tq
    return pl.pallas_call(
        functools.partial(_flash_kernel, window=window),
        grid=(bsz, N_KV_HEADS, nq),
        in_specs=[pl.BlockSpec((1, GQA, tq, aug), lambda b, h, i: (b * N_KV_HEADS + h, 0, i, 0)),
                  pl.BlockSpec((1, seq, aug), lambda b, h, i: (h, b, 0)),
                  pl.BlockSpec((1, seq // tk, V_ROWS, tk), lambda b, h, i: (h, b, 0, 0))],
        out_specs=pl.BlockSpec((qw, tq), lambda b, h, i: (h, b * nq + i)),
        out_shape=jax.ShapeDtypeStruct((ATTN_WIDTH, t), F32),
        scratch_shapes=[pltpu.VMEM((tk, cols), F32), pltpu.VMEM((tk, cols), F32),
                        pltpu.VMEM((tk, cols), BF16), pltpu.VMEM((tk, cols), BF16),
                        pltpu.VMEM((1, cols), F32), pltpu.VMEM((1, cols), F32),
                        pltpu.VMEM((V_ROWS, cols), F32)],
        name=name,
    )(qaug, k, vt)


def _merge_kernel(x_ref, g0_ref, g1_ref, g2_ref, cb_ref, cc_ref, cx_ref, pc_ref, px_ref, gl_ref, ys_ref,
                  oc_ref, os_ref, ow_ref, cw_ref, wc_ref, wglu_ref, wo_ref, wout_ref, out_ref, *, seq):
    tm = x_ref.shape[0]
    yg = _dot(_gelu_tanh(ys_ref[...]).astype(BF16), wglu_ref[...])
    y_ssm = yg[:, :D_MODEL] * _sigmoid(yg[:, D_MODEL:])
    zc = cc_ref[...] * cx_ref[...]
    keep = jnp.where((pl.program_id(0) * tm) % seq != 0, 1.0, 0.0)
    prev = pc_ref[...] * px_ref[...] * keep
    row = lax.broadcasted_iota(jnp.int32, zc.shape, 0)
    z1 = jnp.where(row >= 1, pltpu.roll(zc, 1, 0), prev[7:8, :])
    z2 = jnp.where(row >= 2, pltpu.roll(zc, 2, 0), jnp.where(row == 1, prev[7:8, :], prev[6:7, :]))
    conv = cw_ref[0:1, :] * z2 + cw_ref[1:2, :] * z1 + cw_ref[2:3, :] * zc
    y_conv = _dot((cb_ref[...] * conv).astype(BF16), wc_ref[...])
    gates_t = _sigmoid(gl_ref[...]).T
    heads = []
    for hd in range(N_HEADS):
        rows = slice(hd * HEAD_DIM, (hd + 1) * HEAD_DIM)
        c = hd * NSA_BRANCHES
        heads.append(oc_ref[rows, :] * gates_t[c:c + 1] + os_ref[rows, :] * gates_t[c + 1:c + 2]
                     + ow_ref[rows, :] * gates_t[c + 2:c + 3])
    o_t = jnp.concatenate(heads, axis=0).astype(BF16)
    y_attn = lax.dot_general(o_t, wo_ref[...], (((0,), (0,)), ((), ())), preferred_element_type=F32)
    mixed =_sigmoid(g0_ref[...]) * y_ssm + _sigmoid(g1_ref[...]) * y_conv + _sigmoid(g2_ref[...]) * y_attn
    out_ref[...] = x_ref[...] + _dot(mixed.astype(BF16), wout_ref[...])


def _merge(x2, z, ys, o_cmp, o_sel, o_win, conv_w, wc, wglu, wo, wout, *, seq, tm):
    t, d = x2.shape
    rb = tm // 8
    zc = lambda width, off: pl.BlockSpec((tm, width), lambda i, o=off // width: (i, o))
    zprev = lambda off: pl.BlockSpec((8, CONV_CH), lambda i, o=off // CONV_CH: (jnp.maximum(i * rb - 1, 0), o))
    row = lambda width: pl.BlockSpec((tm, width), lambda i: (i, 0))
    full = lambda a: pl.BlockSpec(a.shape, lambda i: (0, 0))
    branch_t = pl.BlockSpec((ATTN_WIDTH, tm), lambda i: (0, i))
    return pl.pallas_call(
        functools.partial(_merge_kernel, seq=seq),
        grid=(t // tm,),
        in_specs=[row(d), zc(d, Z_MIX), zc(d, Z_MIX + d), zc(d, Z_MIX + 2 * d),
                  zc(CONV_CH, Z_CB), zc(CONV_CH, Z_CC), zc(CONV_CH, Z_CX), zprev(Z_CC), zprev(Z_CX),
                  zc(GATE_PAD, Z_GATE), row(SSM_WIDTH), branch_t, branch_t, branch_t,
                  full(conv_w), full(wc), full(wglu), full(wo), full(wout)],
        out_specs=row(d),
        out_shape=jax.ShapeDtypeStruct((t, d), F32),
        name="merge",
    )(x2, z, z, z, z, z, z, z, z, z, ys, o_cmp, o_sel, o_win, conv_w, wc, wglu, wo, wout)


def _ffn_kernel(x_ref, g_ref, wg_ref, wu_ref, wd_ref, o_ref, h_ref, acc_ref):
    f = pl.program_id(1)

    @pl.when(f == 0)
    def _():
        h_ref[...] = _rms(x_ref[...], g_ref[...]).astype(BF16)
        acc_ref[...] = jnp.zeros(acc_ref.shape, F32)

    h = h_ref[...]
    gate = _dot(h, wg_ref[...])
    up = _dot(h, wu_ref[...])
    act = (gate * _sigmoid(gate) * up).astype(BF16)
    acc_ref[...] += _dot(act, wd_ref[...])

    @pl.when(f == pl.num_programs(1) - 1)
    def _():
        o_ref[...] = x_ref[...] + acc_ref[...]


def _ffn(x2, g, w_gate_up, w_down, *, tm, tf):
    t, d = x2.shape
    nf = D_FF // tf
    return pl.pallas_call(
        _ffn_kernel,
        grid=(t // tm, nf),
        in_specs=[pl.BlockSpec((tm, d), lambda i, f: (i, 0)),
                  pl.BlockSpec((1, d), lambda i, f: (0, 0)),
                  pl.BlockSpec((d, tf), lambda i, f: (0, f)),
                  pl.BlockSpec((d, tf), lambda i, f: (0, nf + f)),
                  pl.BlockSpec((tf, d), lambda i, f: (f, 0))],
        out_specs=pl.BlockSpec((tm, d), lambda i, f: (i, 0)),
        out_shape=jax.ShapeDtypeStruct((t, d), F32),
        scratch_shapes=[pltpu.VMEM((tm, d), BF16), pltpu.VMEM((tm, d), F32)],
        compiler_params=pltpu.CompilerParams(dimension_semantics=("parallel", "arbitrary")),
        name="ffn",
    )(x2, g.reshape(1, d), w_gate_up, w_gate_up, w_down)


def _pick(n, pref):
    while n % pref:
        pref //= 2
    return pref


def _layer(x2, p, *, bsz, seq):
    t = x2.shape[0]
    z = _inproj(x2, p['mix_norm_g'], p['w_in'], tm=_pick(t, 1024), tn=Z_WIDTH // 3)
    ys = _s5_scan(z, p['s5'], bsz=bsz, seq=seq)
    kaug, vsel, kwin, vwin = _kv_prep(z, p['k_norm_g'], seq=seq, tp=_pick(seq, 512), tk_sel=128, tk_win=128)
    cmp_kv = _compress(z, p['cmp_pe'], p['cmp_w1'], p['cmp_w2'], p['k_norm_g'][0], bsz=bsz, seq=seq)
    o_cmp, qaug = _cmp_select(z, cmp_kv, p['q_norm_g'], bsz=bsz, seq=seq, tq=256)
    tq = _pick(seq, 256)
    o_sel = _flash(qaug, kaug, vsel, bsz=bsz, seq=seq, tq=tq, window=0, name="sel_attn")
    o_win = _flash(qaug, kwin, vwin, bsz=bsz, seq=seq, tq=tq, window=WINDOW, name="win_attn")
    x2 = _merge(x2, z, ys, o_cmp, o_sel, o_win, p['conv_w'], p['conv_w_out'], p['ssm_w_glu'], p['nsa_w_o'],
                p['w_out'], seq=seq, tm=_pick(seq, 256))
    return _ffn(x2, p['ffn_norm_g'], p['ffn_w_gate_up'], p['ffn_w_down'], tm=_pick(t, 1024), tf=256)


def kernel(x, mix_norm_g, w_in, ssm_lam_re, ssm_lam_im, ssm_b_re, ssm_b_im, ssm_c_re, ssm_c_im, ssm_d, ssm_log_dt, ssm_w_glu, conv_w, conv_w_out, q_norm_g, k_norm_g, cmp_pe, cmp_w1, cmp_w2, nsa_w_o, w_out, ffn_norm_g, ffn_w_gate_up, ffn_w_down):
    bsz, seq, d = x.shape
    x2 = x.reshape(bsz * seq, d)
    for i in range(w_in.shape[0]):
        p = dict(
            mix_norm_g=mix_norm_g[i], w_in=_permute_w_in(w_in[i]),
            s5=_s5_tables(ssm_lam_re[i], ssm_lam_im[i], ssm_b_re[i], ssm_b_im[i], ssm_c_re[i], ssm_c_im[i],
                          ssm_d[i], ssm_log_dt[i], seq // S5_CHUNK),
            ssm_w_glu=ssm_w_glu[i].astype(BF16), conv_w=conv_w[i], conv_w_out=conv_w_out[i].astype(BF16),
            q_norm_g=q_norm_g[i], k_norm_g=k_norm_g[i], cmp_pe=cmp_pe[i],
            cmp_w1=cmp_w1[i].astype(BF16), cmp_w2=cmp_w2[i].astype(BF16), nsa_w_o=nsa_w_o[i].astype(BF16),
            w_out=w_out[i].astype(BF16), ffn_norm_g=ffn_norm_g[i],
            ffn_w_gate_up=ffn_w_gate_up[i].astype(BF16), ffn_w_down=ffn_w_down[i].astype(BF16))
        x2 = _layer(x2, p, bsz=bsz, seq=seq)
    return x2.reshape(bsz, seq, d)
```

```python
import functools

import jax
import jax.numpy as jnp
import numpy as np
from jax import lax
from jax.experimental import pallas as pl
from jax.experimental.pallas import tpu as pltpu

F32 = jnp.float32
BF16 = jnp.bfloat16

D_MODEL = 1024
SSM_WIDTH = 512
SSM_GROUP = 16
SSM_GROUPS = SSM_WIDTH // SSM_GROUP
SSM_STATE = 64
CONV_CH = 512
CONV_K = 3
HEAD_DIM = 64
HEAD_SHIFT = 6
N_HEADS = 8
N_KV_HEADS = 2
GQA = N_HEADS // N_KV_HEADS
ATTN_WIDTH = N_HEADS * HEAD_DIM
KV_WIDTH = N_KV_HEADS * HEAD_DIM
CMP_BLOCK = 32
SEL_BLOCK = 64
SEL_SHIFT = 6
N_SELECT = 16
WINDOW = 512
CMP_HIDDEN = 256
FORCE_SCORE = 1e4
NSA_BRANCHES = 3
MIX_BRANCHES = 3
D_FF = 2816
RMS_EPS = 1e-6

LANES = 128
SUBLANES = 8
MASKED = -1e30
LOG2E = 1.4426950408889634
FLASH_UNROLL = 4
S5_CHUNK = 16
S5_TILE_GROUPS = LANES // SSM_GROUP
GATE_PAD = LANES
BF16_SUBLANES = 16
V_ROWS = HEAD_DIM + BF16_SUBLANES

Z_MIX = 0
Z_U = Z_MIX + MIX_BRANCHES * D_MODEL
Z_CB = Z_U + SSM_WIDTH
Z_CC = Z_CB + CONV_CH
Z_CX = Z_CC + CONV_CH
Z_Q = Z_CX + CONV_CH
Z_KC = Z_Q + ATTN_WIDTH
Z_VC = Z_KC + KV_WIDTH
Z_KS = Z_VC + KV_WIDTH
Z_VS = Z_KS + KV_WIDTH
Z_KW = Z_VS + KV_WIDTH
Z_VW = Z_KW + KV_WIDTH
Z_GATE = Z_VW + KV_WIDTH
Z_WIDTH = Z_GATE + GATE_PAD


def _gelu_tanh(x):
    return 0.5 * x * (1.0 + jnp.tanh(np.sqrt(2.0 / np.pi).astype(np.float32) * (x + 0.044715 * (x * x * x))))


def _sigmoid(x):
    return 0.5 * jnp.tanh(0.5 * x) + 0.5


def _rms(x, g):
    return x * lax.rsqrt(jnp.mean(x * x, axis=-1, keepdims=True) + RMS_EPS) * g


def _dot(a, b):
    return jnp.dot(a, b, preferred_element_type=F32)


def _dot_nt(a, b):
    return lax.dot_general(a, b, (((1,), (1,)), ((), ())), preferred_element_type=F32)


def _inproj_kernel(x_ref, g_ref, w_ref, o_ref):
    h = _rms(x_ref[...], g_ref[...]).astype(BF16)
    o_ref[...] = _dot(h, w_ref[...])


def _inproj(x2, g, w, *, tm, tn):
    t, d = x2.shape
    n = w.shape[1]
    return pl.pallas_call(
        _inproj_kernel,
        grid=(n // tn, t // tm),
        in_specs=[pl.BlockSpec((tm, d), lambda j, i: (i, 0)),
                  pl.BlockSpec((1, d), lambda j, i: (0, 0)),
                  pl.BlockSpec((d, tn), lambda j, i: (0, j))],
        out_specs=pl.BlockSpec((tm, tn), lambda j, i: (i, j)),
        out_shape=jax.ShapeDtypeStruct((t, n), F32),
        name="inproj",
    )(x2, g.reshape(1, d), w)


def _permute_w_in(w):
    n_plain = SSM_WIDTH + 3 * CONV_CH + ATTN_WIDTH + 6 * KV_WIDTH
    n_gate = N_HEADS * NSA_BRANCHES
    gate = jnp.pad(w[:, n_plain:n_plain + n_gate], ((0, 0), (0, GATE_PAD - n_gate)))
    return jnp.concatenate([w[:, n_plain + n_gate:], w[:, :n_plain], gate], axis=1).astype(BF16)


def _s5_tables(lam_re, lam_im, b_re, b_im, c_re, c_im, d_skip, log_dt, n_chunks):
    hp = lax.Precision.HIGHEST
    g, p = lam_re.shape
    h, l, gt = SSM_GROUP, S5_CHUNK, S5_TILE_GROUPS
    nt = g // gt
    dt = jnp.exp(log_dt)[:, None]
    ar, ai = lam_re * dt, lam_im * dt

    def powers(k):
        mag = jnp.exp(ar[None] * k[:, None, None])
        ang = ai[None] * k[:, None, None]
        return mag * jnp.cos(ang), mag * jnp.sin(ang)

    pr, pi = powers(jnp.arange(l + 1, dtype=F32))
    nr, ni = pr[1] - 1.0, pi[1]
    den = lam_re * lam_re + lam_im * lam_im
    fr, fi = (nr * lam_re + ni * lam_im) / den, (ni * lam_re - nr * lam_im) / den
    bbr = fr[..., None] * b_re - fi[..., None] * b_im
    bbi = fr[..., None] * b_im + fi[..., None] * b_re
    wr = pr[:l, :, :, None] * bbr[None] - pi[:l, :, :, None] * bbi[None]
    wi = pr[:l, :, :, None] * bbi[None] + pi[:l, :, :, None] * bbr[None]
    kern = (jnp.einsum('gop,tgpi->tgio', c_re, wr, precision=hp)
            - jnp.einsum('gop,tgpi->tgio', c_im, wi, precision=hp))
    r = jnp.arange(l)

    def group_diag(x, row_w, col_w, halves=1):
        rows, hw = x.shape[-2:]
        w = hw // halves
        col = jnp.arange(halves * gt * w)
        src = (col // (gt * w)) * w + col % w
        expand = (jnp.arange(hw)[:, None] == src[None, :]).astype(BF16)
        tiled = jnp.einsum('...w,wv->...v', x.astype(BF16), expand, preferred_element_type=BF16)
        own = (jnp.arange(rows)[:, None] // row_w) % gt == (col[None, :] % (gt * w)) // col_w
        return jnp.where(own, tiled, jnp.zeros((), BF16))

    kblk = group_diag(kern.reshape(l, g * h, h), h, h)
    krev = kblk.reshape(l, nt, LANES, LANES)[::-1].transpose(1, 0, 2, 3).reshape(nt, l * LANES, LANES)
    krev = jnp.concatenate([krev, jnp.zeros((nt, LANES, LANES), krev.dtype)], axis=1)
    t_pair = jnp.concatenate(
        [jnp.concatenate([krev[:, (l - 1 - s) * LANES:(l + 1) * LANES], krev[:, (l - 2 - s) * LANES:l * LANES]], axis=2)
         for s in range(0, l, 2)], axis=1)
    qr, qi = pr[l - 1 - r], pi[l - 1 - r]
    st_re = qr[..., None] * bbr[None] - qi[..., None] * bbi[None]
    st_im = qr[..., None] * bbi[None] + qi[..., None] * bbr[None]
    rows_in = lambda x: group_diag(x.transpose(0, 1, 3, 2).reshape(l, g * h, p), h, p).reshape(l, nt, LANES, gt * p)
    er, ei = pr[1:l + 1][:, :, None, :], pi[1:l + 1][:, :, None, :]

    def rows_out(x):
        x = x.transpose(0, 1, 3, 2).reshape(l // 2, 2, g * p, h).transpose(0, 2, 1, 3).reshape(l // 2, g * p, 2 * h)
        return group_diag(x, p, h, halves=2).reshape(l // 2, nt, gt * p, 2 * LANES)

    n_steps = max(1, int(np.ceil(np.log2(n_chunks))))
    dr, di = powers(l * (2.0 ** jnp.arange(n_steps, dtype=F32)))
    lanes = lambda m: m.reshape(n_steps, nt, gt * p).transpose(1, 0, 2)
    return dict(t_pair=t_pair, s_re=rows_in(st_re), s_im=rows_in(st_im),
                c_re=rows_out(c_re[None] * er - c_im[None] * ei),
                c_im=rows_out(-(c_re[None] * ei + c_im[None] * er)),
                d_re=lanes(dr), d_im=lanes(di), d_skip=d_skip.reshape(1, g * h))


def _s5_kernel(u_ref, tp_ref, sre_ref, sim_ref, cre_ref, cim_ref, dre_ref, dim_ref, dsk_ref, y_ref):
    seq = u_ref.shape[0]
    l = S5_CHUNK
    nc = seq // l
    n_steps = dre_ref.shape[1]
    ns = dre_ref.shape[2]
    us = [u_ref[pl.ds(r, nc, stride=l), :] for r in range(l)]
    ucat = jnp.concatenate(us, axis=1).astype(BF16)
    xr = _dot(ucat, sre_ref[:, 0].reshape(l * LANES, ns))
    xi = _dot(ucat, sim_ref[:, 0].reshape(l * LANES, ns))
    row = lax.broadcasted_iota(jnp.int32, xr.shape, 0)
    for k in range(n_steps):
        s = 1 << k
        if s >= nc:
            break
        dr = dre_ref[0, k:k + 1, :]
        di = dim_ref[0, k:k + 1, :]
        sr = jnp.where(row >= s, pltpu.roll(xr, s, 0), 0.0)
        si = jnp.where(row >= s, pltpu.roll(xi, s, 0), 0.0)
        xr, xi = xr + (dr * sr - di * si), xi + (dr * si + di * sr)
    pr = jnp.where(row >= 1, pltpu.roll(xr, 1, 0), 0.0).astype(BF16)
    pi = jnp.where(row >= 1, pltpu.roll(xi, 1, 0), 0.0).astype(BF16)
    dsk = dsk_ref[...]
    off = 0
    for s in range(0, l, 2):
        rows = (s + 2) * LANES
        y2 = (_dot(ucat[:, :rows], tp_ref[0, off:off + rows, :])
              + _dot(pr, cre_ref[s // 2, 0]) + _dot(pi, cim_ref[s // 2, 0]))
        y_ref[pl.ds(s, nc, stride=l), :] = y2[:, :LANES] + dsk * us[s]
        y_ref[pl.ds(s + 1, nc, stride=l), :] = y2[:, LANES:] + dsk * us[s + 1]
        off += rows


def _s5_scan(z, tabs, *, bsz, seq):
    t = z.shape[0]
    nt = SSM_WIDTH // LANES
    tile3 = lambda a: pl.BlockSpec((1,) + a.shape[1:], lambda j, b: (j, 0, 0))
    step4 = lambda a: pl.BlockSpec((a.shape[0], 1) + a.shape[2:], lambda j, b: (0, j, 0, 0))
    return pl.pallas_call(
        _s5_kernel,
        grid=(nt, bsz),
        in_specs=[pl.BlockSpec((seq, LANES), lambda j, b: (b, Z_U // LANES + j)),
                  tile3(tabs['t_pair']), step4(tabs['s_re']), step4(tabs['s_im']),
                  step4(tabs['c_re']), step4(tabs['c_im']), tile3(tabs['d_re']), tile3(tabs['d_im']),
                  pl.BlockSpec((1, LANES), lambda j, b: (0, j))],
        out_specs=pl.BlockSpec((seq, LANES), lambda j, b: (b, j)),
        out_shape=jax.ShapeDtypeStruct((t, SSM_WIDTH), F32),
        name="s5_scan",
    )(z, tabs['t_pair'], tabs['s_re'], tabs['s_im'], tabs['c_re'], tabs['c_im'], tabs['d_re'], tabs['d_im'],
      tabs['d_skip'])


def _kv_prep_kernel(ks_ref, vs_ref, kw_ref, vw_ref, g_ref, kaug_ref, vsel_ref, kwin_ref, vwin_ref, *, seq):
    tp = ks_ref.shape[0]
    pos = (pl.program_id(0) * tp) % seq + lax.broadcasted_iota(jnp.int32, (tp, HEAD_DIM), 0)
    blk = jnp.right_shift(pos, SEL_SHIFT)
    onehot = jnp.where(lax.broadcasted_iota(jnp.int32, (tp, HEAD_DIM), 1) == blk, 1.0, 0.0).astype(BF16)
    zeros = jnp.zeros((tp, HEAD_DIM), BF16)
    for hh in range(N_KV_HEADS):
        sl = slice(hh * HEAD_DIM, (hh + 1) * HEAD_DIM)
        kn = _rms(ks_ref[:, sl], g_ref[0:1, :]).astype(BF16)
        kaug_ref[hh] = jnp.concatenate([kn, onehot], axis=1)
        kn = _rms(kw_ref[:, sl], g_ref[1:2, :]).astype(BF16)
        kwin_ref[hh] = jnp.concatenate([kn, zeros], axis=1)
    for v_ref, vt_ref in ((vs_ref, vsel_ref), (vw_ref, vwin_ref)):
        tk = vt_ref.shape[3]
        vt = v_ref[...].T.astype(BF16)
        ones = jnp.where(lax.broadcasted_iota(jnp.int32, (V_ROWS - HEAD_DIM, tp), 0) == 0, 1.0, 0.0).astype(BF16)
        for hh in range(N_KV_HEADS):
            vh = jnp.concatenate([vt[hh * HEAD_DIM:(hh + 1) * HEAD_DIM], ones], axis=0)
            for j in range(tp // tk):
                vt_ref[hh, j] = vh[:, j * tk:(j + 1) * tk]


def _kv_prep(z, k_norm_g, *, seq, tp, tk_sel, tk_win):
    t = z.shape[0]
    col = lambda off: pl.BlockSpec((tp, KV_WIDTH), lambda i, o=off // KV_WIDTH: (i, o))
    aug = 2 * HEAD_DIM
    kspec = pl.BlockSpec((N_KV_HEADS, tp, aug), lambda i: (0, i, 0))
    vspec = lambda tk: pl.BlockSpec((N_KV_HEADS, tp // tk, V_ROWS, tk), lambda i: (0, i, 0, 0))
    vshape = lambda tk: jax.ShapeDtypeStruct((N_KV_HEADS, t // tk, V_ROWS, tk), BF16)
    return pl.pallas_call(
        functools.partial(_kv_prep_kernel, seq=seq),
        grid=(t // tp,),
        in_specs=[col(Z_KS), col(Z_VS), col(Z_KW), col(Z_VW), pl.BlockSpec((2, HEAD_DIM), lambda i: (0, 0))],
        out_specs=[kspec, vspec(tk_sel), kspec, vspec(tk_win)],
        out_shape=[jax.ShapeDtypeStruct((N_KV_HEADS, t, aug), BF16), vshape(tk_sel),
                   jax.ShapeDtypeStruct((N_KV_HEADS, t, aug), BF16), vshape(tk_win)],
        name="kv_prep",
    )(z, z, z, z, k_norm_g[1:3])


def _compress_kernel(x_ref, pe_ref, w1_ref, w2_ref, g_ref, o_ref):
    seq = x_ref.shape[0]
    nb = seq // (2 * CMP_BLOCK)
    lane = lax.broadcasted_iota(jnp.int32, (nb, 2 * HEAD_DIM), 1)
    rows = []
    for hh in range(N_KV_HEADS):
        for parity in range(2):
            cols = []
            for s in range(0, CMP_BLOCK, 2):
                a = x_ref[pl.ds(parity * CMP_BLOCK + s, nb, stride=2 * CMP_BLOCK), :]
                b = x_ref[pl.ds(parity * CMP_BLOCK + s + 1, nb, stride=2 * CMP_BLOCK), :]
                if hh == 0:
                    cols.append(jnp.where(lane < HEAD_DIM, a, pltpu.roll(b, HEAD_DIM, 1)))
                else:
                    cols.append(jnp.where(lane < HEAD_DIM, pltpu.roll(a, HEAD_DIM, 1), b))
            rows.append(jnp.concatenate(cols, axis=1))
    x = (jnp.concatenate(rows, axis=0) + pe_ref[0]).astype(BF16)
    hid = _gelu_tanh(_dot(x, w1_ref[0])).astype(BF16)
    y = _dot(hid, w2_ref[0])
    o_ref[0, 0] = jnp.where(pl.program_id(0) == 0, _rms(y, g_ref[...]), y)


def _compress(z, cmp_pe, w1, w2, k_gain, *, bsz, seq):
    nc = seq // CMP_BLOCK
    kdim = CMP_BLOCK * HEAD_DIM
    m = N_KV_HEADS * nc
    out = pl.pallas_call(
        _compress_kernel,
        grid=(2, bsz),
        in_specs=[pl.BlockSpec((seq, KV_WIDTH), lambda w, b: (b, Z_KC // KV_WIDTH + w)),
                  pl.BlockSpec((1, 1, kdim), lambda w, b: (w, 0, 0)),
                  pl.BlockSpec((1, kdim, CMP_HIDDEN), lambda w, b: (w, 0, 0)),
                  pl.BlockSpec((1, CMP_HIDDEN, HEAD_DIM), lambda w, b: (w, 0, 0)),
                  pl.BlockSpec((1, HEAD_DIM), lambda w, b: (0, 0))],
        out_specs=pl.BlockSpec((1, 1, m, HEAD_DIM), lambda w, b: (w, b, 0, 0)),
        out_shape=jax.ShapeDtypeStruct((2, bsz, m, HEAD_DIM), F32),
        name="compress",
    )(z, cmp_pe.reshape(2, 1, kdim), w1, w2, k_gain.reshape(1, HEAD_DIM))
    return out.reshape(2, bsz * N_KV_HEADS, nc, HEAD_DIM)


def _head_gate(gates, kvh, g, branch):
    c0 = g * NSA_BRANCHES + branch
    c1 = (GQA + g) * NSA_BRANCHES + branch
    if isinstance(kvh, int):
        c = c1 if kvh else c0
        return gates[:, c:c + 1]
    return jnp.where(kvh == 0, gates[:, c0:c0 + 1], gates[:, c1:c1 + 1])


def _cmp_select_kernel(q_ref, kc_ref, vc_ref, qg_ref, o_ref, qaug_ref, imp_ref, cnt_ref):
    tq = q_ref.shape[0]
    nc = kc_ref.shape[1]
    nb = nc // 2
    pair_w = 2 * HEAD_DIM
    qw = GQA * HEAD_DIM
    q0 = pl.program_id(1) * tq
    q = q_ref[...]
    x2 = q * q
    hi = x2.astype(BF16)
    lo = (x2 - hi.astype(F32)).astype(BF16)
    wq = q.shape[1]
    seg = jnp.where(jnp.right_shift(lax.broadcasted_iota(jnp.int32, (wq, wq), 0), HEAD_SHIFT)
                    == jnp.right_shift(lax.broadcasted_iota(jnp.int32, (wq, wq), 1), HEAD_SHIFT), 1.0, 0.0).astype(BF16)
    ss = _dot(hi, seg) + _dot(lo, seg)
    qf = q * lax.rsqrt(ss * (1.0 / HEAD_DIM) + RMS_EPS) * qg_ref[...] * (HEAD_DIM ** -0.5)
    qn = qf.astype(BF16)
    q2 = qf * LOG2E

    row = lax.broadcasted_iota(jnp.int32, (nc, tq), 0)
    t = q0 + lax.broadcasted_iota(jnp.int32, (nc, tq), 1)
    blk = jnp.where(row < nb, 2 * row, 2 * (row - nb) + 1)
    valid = (blk + 1) * CMP_BLOCK - 1 <= t
    lane = lax.broadcasted_iota(jnp.int32, (tq, pair_w), 1)
    j = lax.broadcasted_iota(jnp.int32, (nb, tq), 0)
    cur = jnp.right_shift(q0 + lax.broadcasted_iota(jnp.int32, (nb, tq), 1), SEL_SHIFT)
    forced = (j == 0) | (j == cur) | (j == cur - 1)
    visible = j <= cur

    def two_heads(x):
        zx = jnp.zeros_like(x)
        return jnp.concatenate([jnp.concatenate([x, zx], axis=1), jnp.concatenate([zx, x], axis=1)], axis=0)

    pairs = [(kvh, pr) for kvh in range(N_KV_HEADS) for pr in range(GQA // 2)]
    k2 = [two_heads(kc_ref[kvh].astype(BF16)) for kvh in range(N_KV_HEADS)]
    v2 = [two_heads(vc_ref[kvh].astype(BF16)) for kvh in range(N_KV_HEADS)]
    st = [_dot_nt(k2[kvh], qn[:, kvh * qw + pr * pair_w:kvh * qw + (pr + 1) * pair_w]) for kvh, pr in pairs]
    psum = [jnp.zeros((nc, tq), F32) for _ in range(N_KV_HEADS)]
    probs = []
    for (kvh, pr), s2 in zip(pairs, st):
        ph = []
        for hh in range(2):
            sm = jnp.where(valid, s2[hh * nc:(hh + 1) * nc], MASKED)
            m = jnp.max(sm, axis=0, keepdims=True)
            m = jnp.where(m > 0.5 * MASKED, m, 0.0)
            e = jnp.exp(sm - m)
            p = e * (1.0 / jnp.maximum(jnp.sum(e, axis=0, keepdims=True), 1e-30))
            psum[kvh] = psum[kvh] + p
            ph.append(p.astype(BF16))
        probs.append(jnp.concatenate(ph, axis=0))
    imps = []
    for kvh in range(N_KV_HEADS):
        imp = psum[kvh][:nb] + psum[kvh][nb:]
        imp = jnp.where(forced, FORCE_SCORE, jnp.where(visible, imp, -jnp.inf))
        imp_ref[kvh] = imp
        imps.append(imp)
    for n, ((kvh, pr), p2) in enumerate(zip(pairs, probs)):
        o_ref[n * pair_w:(n + 1) * pair_w, :] = lax.dot_general(
            v2[kvh], p2, (((0,), (0,)), ((), ())), preferred_element_type=F32)

    n_vis = jnp.minimum(nb, (q0 + tq - 1) // SEL_BLOCK + 1)
    n_slabs = nb // SUBLANES
    row_in_slab = lax.broadcasted_iota(jnp.int32, (SUBLANES, tq), 0)
    cnt_ref[...] = jnp.zeros(cnt_ref.shape, F32)
    for grp in range(n_slabs):
        @pl.when(grp * SUBLANES < n_vis)
        def _(grp=grp):
            for kvh in range(N_KV_HEADS):
                slabs = [imp_ref[kvh, s * SUBLANES:(s + 1) * SUBLANES, :] for s in range(n_slabs)]
                cnt = [jnp.zeros((SUBLANES, tq), F32) for _ in range(n_slabs)]
                for i in range(grp * SUBLANES, (grp + 1) * SUBLANES):
                    vi = imp_ref[kvh, i:i + 1, :]
                    for s in range(n_slabs):
                        if s > grp:
                            beats = vi >= slabs[s]
                        elif s < grp:
                            beats = vi > slabs[s]
                        else:
                            beats = (vi > slabs[s]) | ((vi == slabs[s]) & (i - s * SUBLANES < row_in_slab))
                        cnt[s] = cnt[s] + jnp.where(beats, 1.0, 0.0)
                cnt_ref[kvh] += jnp.concatenate(cnt, axis=0)
    for kvh in range(N_KV_HEADS):
        selneg_t = jnp.where(visible & (cnt_ref[kvh] < float(N_SELECT)), 0.0, MASKED)
        if nb < HEAD_DIM:
            selneg_t = jnp.concatenate([selneg_t, jnp.zeros((HEAD_DIM - nb, tq), F32)], axis=0)
        selneg = jnp.concatenate([selneg_t, selneg_t], axis=0).T
        for pr in range(GQA // 2):
            c0 = kvh * qw + pr * pair_w
            qp = q2[:, c0:c0 + pair_w]
            qaug_ref[kvh, 2 * pr] = jnp.where(lane < HEAD_DIM, qp, selneg).astype(BF16)
            qaug_ref[kvh, 2 * pr + 1] = jnp.where(lane < HEAD_DIM, pltpu.roll(qp, HEAD_DIM, 1), selneg).astype(BF16)


def _cmp_select(z, cmp_kv, q_norm_g, *, bsz, seq, tq):
    t = z.shape[0]
    nq = seq // tq
    nc = seq // CMP_BLOCK
    aug = 2 * HEAD_DIM
    qw = GQA * HEAD_DIM
    kc, vc = cmp_kv[0], cmp_kv[1]
    kv_spec = pl.BlockSpec((N_KV_HEADS, nc, HEAD_DIM), lambda b, i: (b, 0, 0))
    return pl.pallas_call(
        _cmp_select_kernel,
        grid=(bsz, nq),
        in_specs=[pl.BlockSpec((tq, ATTN_WIDTH), lambda b, i: (b * nq + i, Z_Q // ATTN_WIDTH)),
                  kv_spec, kv_spec,
                  pl.BlockSpec((1, ATTN_WIDTH), lambda b, i: (0, 0))],
        out_specs=[pl.BlockSpec((ATTN_WIDTH, tq), lambda b, i: (0, b * nq + i)),
                   pl.BlockSpec((N_KV_HEADS, GQA, tq, aug), lambda b, i: (b, 0, i, 0))],
        out_shape=[jax.ShapeDtypeStruct((ATTN_WIDTH, t), F32),
                   jax.ShapeDtypeStruct((bsz * N_KV_HEADS, GQA, seq, aug), BF16)],
        scratch_shapes=[pltpu.VMEM((N_KV_HEADS, nc // 2, tq), F32), pltpu.VMEM((N_KV_HEADS, nc // 2, tq), F32)],
        name="cmp_select",
    )(z, kc, vc, jnp.tile(q_norm_g, N_HEADS).reshape(1, ATTN_WIDTH))


def _flash_kernel(q_ref, k_ref, vt_ref, o_ref, sa_ref, sb_ref, pa_ref, pb_ref, m_ref, a_ref, acc_ref, *, window):
    tq = q_ref.shape[2]
    tk = vt_ref.shape[3]
    cols = GQA * tq
    q0 = pl.program_id(2) * tq
    qa = q_ref[0].reshape(cols, q_ref.shape[3])
    t = q0 + (lax.broadcasted_iota(jnp.int32, (tk, cols), 1) & (tq - 1))
    key_row = lax.broadcasted_iota(jnp.int32, (tk, cols), 0)
    span = 2 * tk
    lo = jnp.maximum(0, q0 - window + 1) // span if window else 0
    hi = (q0 + tq + span - 1) // span
    mid1 = jnp.maximum(lo, (q0 + tq - 1 - window) // span + 1) if window else lo
    mid2 = q0 // span
    first = 2 * lo
    last = 2 * hi - 1

    def scores(kb):
        return _dot_nt(k_ref[0, pl.ds(pl.multiple_of(kb * tk, tk), tk), :], qa)

    m_ref[...] = jnp.full(m_ref.shape, MASKED, F32)
    a_ref[...] = jnp.ones(a_ref.shape, F32)
    acc_ref[...] = jnp.zeros(acc_ref.shape, F32)
    pb_ref[...] = jnp.zeros(pb_ref.shape, BF16)
    sa_ref[...] = scores(first)

    def tile(kb, s_cur, s_nxt, p_prev, p_cur, mask):
        pv = _dot(vt_ref[0, jnp.maximum(kb - 1, first)], p_prev[...])
        s_nxt[...] = scores(jnp.minimum(kb + 1, last))
        s = s_cur[...]
        if mask == 'causal':
            s = jnp.where(kb * tk + key_row <= t, s, MASKED)
        elif mask == 'window':
            s = jnp.where(kb * tk + key_row > t - window, s, MASKED)
        m_old = m_ref[...]
        m_new = jnp.maximum(m_old, jnp.max(s, axis=0, keepdims=True))
        acc_ref[...] = a_ref[...] * acc_ref[...] + pv
        a_ref[...] = jnp.exp2(m_old - m_new)
        p_cur[...] = jnp.exp2(s - m_new).astype(BF16)
        m_ref[...] = m_new

    def pair_step(pair, mask):
        tile(2 * pair, sa_ref, sb_ref, pb_ref, pa_ref, mask)
        tile(2 * pair + 1, sb_ref, sa_ref, pa_ref, pb_ref, mask)

    def run(begin, end, mask, unroll=1):
        n_main = (end - begin) // unroll

        def main(i, carry):
            for u in range(unroll):
                pair_step(begin + i * unroll + u, mask)
            return carry

        def rest(pair, carry):
            pair_step(pair, mask)
            return carry

        if unroll > 1:
            lax.fori_loop(0, n_main, main, 0)
        lax.fori_loop(begin + n_main * unroll if unroll > 1 else begin, end, rest, 0)

    if window:
        interior = (mid1 - lo == 1) & (mid2 - mid1 == 1) & (hi - mid2 == 1)

        @pl.when(interior)
        def _():
            pair_step(lo, 'window')
            pair_step(lo + 1, None)
            pair_step(lo + 2, 'causal')

        @pl.when(jnp.logical_not(interior))
        def _():
            run(lo, mid1, 'window')
            run(mid1, mid2, None)
            run(mid2, hi, 'causal')
    else:
        run(mid1, mid2, None, unroll=FLASH_UNROLL)
        run(mid2, hi, 'causal')
    acc = a_ref[...] * acc_ref[...] + _dot(vt_ref[0, last], pb_ref[...])
    o_t = acc[:HEAD_DIM] / acc[HEAD_DIM:HEAD_DIM + 1]
    for g in range(GQA):
        o_ref[g * HEAD_DIM:(g + 1) * HEAD_DIM, :] = o_t[:, g * tq:(g + 1) * tq]


def _flash(qaug, k, vt, *, bsz, seq, tq, window, name):
    t = bsz * seq
    nq = seq // tq
    aug = qaug.shape[3]
    tk = vt.shape[3]
    qw = GQA * HEAD_DIM
    cols = GQA * tq
    return pl.pallas_call(
        functools.partial(_flash_kernel, window=window),
        grid=(bsz, N_KV_HEADS, nq),
        in_specs=[pl.BlockSpec((1, GQA, tq, aug), lambda b, h, i: (b * N_KV_HEADS + h, 0, i, 0)),
                  pl.BlockSpec((1, seq, aug), lambda b, h, i: (h, b, 0)),
                  pl.BlockSpec((1, seq // tk, V_ROWS, tk), lambda b, h, i: (h, b, 0, 0))],
        out_specs=pl.BlockSpec((qw, tq), lambda b, h, i: (h, b * nq + i)),
        out_shape=jax.ShapeDtypeStruct((ATTN_WIDTH, t), F32),
        scratch_shapes=[pltpu.VMEM((tk, cols), F32), pltpu.VMEM((tk, cols), F32),
                        pltpu.VMEM((tk, cols), BF16), pltpu.VMEM((tk, cols), BF16),
                        pltpu.VMEM((1, cols), F32), pltpu.VMEM((1, cols), F32),
                        pltpu.VMEM((V_ROWS, cols), F32)],
        name=name,
    )(qaug, k, vt)


def _merge_kernel(x_ref, g0_ref, g1_ref, g2_ref, cb_ref, cc_ref, cx_ref, pc_ref, px_ref, gl_ref, ys_ref,
                  oc_ref, os_ref, ow_ref, cw_ref, wc_ref, wglu_ref, wo_ref, wout_ref, out_ref, *, seq):
    tm = x_ref.shape[0]
    yg = _dot(_gelu_tanh(ys_ref[...]).astype(BF16), wglu_ref[...])
    y_ssm = yg[:, :D_MODEL] * _sigmoid(yg[:, D_MODEL:])
    zc = cc_ref[...] * cx_ref[...]
    keep = jnp.where((pl.program_id(0) * tm) % seq != 0, 1.0, 0.0)
    prev = pc_ref[...] * px_ref[...] * keep
    row = lax.broadcasted_iota(jnp.int32, zc.shape, 0)
    z1 = jnp.where(row >= 1, pltpu.roll(zc, 1, 0), prev[7:8, :])
    z2 = jnp.where(row >= 2, pltpu.roll(zc, 2, 0), jnp.where(row == 1, prev[7:8, :], prev[6:7, :]))
    conv = cw_ref[0:1, :] * z2 + cw_ref[1:2, :] * z1 + cw_ref[2:3, :] * zc
    y_conv = _dot((cb_ref[...] * conv).astype(BF16), wc_ref[...])
    gates_t = _sigmoid(gl_ref[...]).T
    heads = []
    for hd in range(N_HEADS):
        rows = slice(hd * HEAD_DIM, (hd + 1) * HEAD_DIM)
        c = hd * NSA_BRANCHES
        heads.append(oc_ref[rows, :] * gates_t[c:c + 1] + os_ref[rows, :] * gates_t[c + 1:c + 2]
                     + ow_ref[rows, :] * gates_t[c + 2:c + 3])
    o_t = jnp.concatenate(heads, axis=0).astype(BF16)
    y_attn = lax.dot_general(o_t, wo_ref[...], (((0,), (0,)), ((), ())), preferred_element_type=F32)
    mixed =_sigmoid(g0_ref[...]) * y_ssm + _sigmoid(g1_ref[...]) * y_conv + _sigmoid(g2_ref[...]) * y_attn
    out_ref[...] = x_ref[...] + _dot(mixed.astype(BF16), wout_ref[...])


def _merge(x2, z, ys, o_cmp, o_sel, o_win, conv_w, wc, wglu, wo, wout, *, seq, tm):
    t, d = x2.shape
    rb = tm // 8
    zc = lambda width, off: pl.BlockSpec((tm, width), lambda i, o=off // width: (i, o))
    zprev = lambda off: pl.BlockSpec((8, CONV_CH), lambda i, o=off // CONV_CH: (jnp.maximum(i * rb - 1, 0), o))
    row = lambda width: pl.BlockSpec((tm, width), lambda i: (i, 0))
    full = lambda a: pl.BlockSpec(a.shape, lambda i: (0, 0))
    branch_t = pl.BlockSpec((ATTN_WIDTH, tm), lambda i: (0, i))
    return pl.pallas_call(
        functools.partial(_merge_kernel, seq=seq),
        grid=(t // tm,),
        in_specs=[row(d), zc(d, Z_MIX), zc(d, Z_MIX + d), zc(d, Z_MIX + 2 * d),
                  zc(CONV_CH, Z_CB), zc(CONV_CH, Z_CC), zc(CONV_CH, Z_CX), zprev(Z_CC), zprev(Z_CX),
                  zc(GATE_PAD, Z_GATE), row(SSM_WIDTH), branch_t, branch_t, branch_t,
                  full(conv_w), full(wc), full(wglu), full(wo), full(wout)],
        out_specs=row(d),
        out_shape=jax.ShapeDtypeStruct((t, d), F32),
        name="merge",
    )(x2, z, z, z, z, z, z, z, z, z, ys, o_cmp, o_sel, o_win, conv_w, wc, wglu, wo, wout)


def _ffn_kernel(x_ref, g_ref, wg_ref, wu_ref, wd_ref, o_ref, h_ref, acc_ref):
    f = pl.program_id(1)

    @pl.when(f == 0)
    def _():
        h_ref[...] = _rms(x_ref[...], g_ref[...]).astype(BF16)
        acc_ref[...] = jnp.zeros(acc_ref.shape, F32)

    h = h_ref[...]
    gate = _dot(h, wg_ref[...])
    up = _dot(h, wu_ref[...])
    act = (gate * _sigmoid(gate) * up).astype(BF16)
    acc_ref[...] += _dot(act, wd_ref[...])

    @pl.when(f == pl.num_programs(1) - 1)
    def _():
        o_ref[...] = x_ref[...] + acc_ref[...]


def _ffn(x2, g, w_gate_up, w_down, *, tm, tf):
    t, d = x2.shape
    nf = D_FF // tf
    return pl.pallas_call(
        _ffn_kernel,
        grid=(t // tm, nf),
        in_specs=[pl.BlockSpec((tm, d), lambda i, f: (i, 0)),
                  pl.BlockSpec((1, d), lambda i, f: (0, 0)),
                  pl.BlockSpec((d, tf), lambda i, f: (0, f)),
                  pl.BlockSpec((d, tf), lambda i, f: (0, nf + f)),
                  pl.BlockSpec((tf, d), lambda i, f: (f, 0))],
        out_specs=pl.BlockSpec((tm, d), lambda i, f: (i, 0)),
        out_shape=jax.ShapeDtypeStruct((t, d), F32),
        scratch_shapes=[pltpu.VMEM((tm, d), BF16), pltpu.VMEM((tm, d), F32)],
        compiler_params=pltpu.CompilerParams(dimension_semantics=("parallel", "arbitrary")),
        name="ffn",
    )(x2, g.reshape(1, d), w_gate_up, w_gate_up, w_down)


def _pick(n, pref):
    while n % pref:
        pref //= 2
    return pref


def _layer(x2, p, *, bsz, seq):
    t = x2.shape[0]
    z = _inproj(x2, p['mix_norm_g'], p['w_in'], tm=_pick(t, 1024), tn=Z_WIDTH // 3)
    ys = _s5_scan(z, p['s5'], bsz=bsz, seq=seq)
    kaug, vsel, kwin, vwin = _kv_prep(z, p['k_norm_g'], seq=seq, tp=_pick(seq, 512), tk_sel=128, tk_win=128)
    cmp_kv = _compress(z, p['cmp_pe'], p['cmp_w1'], p['cmp_w2'], p['k_norm_g'][0], bsz=bsz, seq=seq)
    o_cmp, qaug = _cmp_select(z, cmp_kv, p['q_norm_g'], bsz=bsz, seq=seq, tq=256)
    tq = _pick(seq, 256)
    o_sel = _flash(qaug, kaug, vsel, bsz=bsz, seq=seq, tq=tq, window=0, name="sel_attn")
    o_win = _flash(qaug, kwin, vwin, bsz=bsz, seq=seq, tq=tq, window=WINDOW, name="win_attn")
    x2 = _merge(x2, z, ys, o_cmp, o_sel, o_win, p['conv_w'], p['conv_w_out'], p['ssm_w_glu'], p['nsa_w_o'],
                p['w_out'], seq=seq, tm=_pick(seq, 256))
    return _ffn(x2, p['ffn_norm_g'], p['ffn_w_gate_up'], p['ffn_w_down'], tm=_pick(t, 1024), tf=256)


def kernel(x, mix_norm_g, w_in, ssm_lam_re, ssm_lam_im, ssm_b_re, ssm_b_im, ssm_c_re, ssm_c_im, ssm_d, ssm_log_dt, ssm_w_glu, conv_w, conv_w_out, q_norm_g, k_norm_g, cmp_pe, cmp_w1, cmp_w2, nsa_w_o, w_out, ffn_norm_g, ffn_w_gate_up, ffn_w_down):
    bsz, seq, d = x.shape
    x2 = x.reshape(bsz * seq, d)
    for i in range(w_in.shape[0]):
        p = dict(
            mix_norm_g=mix_norm_g[i], w_in=_permute_w_in(w_in[i]),
            s5=_s5_tables(ssm_lam_re[i], ssm_lam_im[i], ssm_b_re[i], ssm_b_im[i], ssm_c_re[i], ssm_c_im[i],
                          ssm_d[i], ssm_log_dt[i], seq // S5_CHUNK),
            ssm_w_glu=ssm_w_glu[i].astype(BF16), conv_w=conv_w[i], conv_w_out=conv_w_out[i].astype(BF16),
            q_norm_g=q_norm_g[i], k_norm_g=k_norm_g[i], cmp_pe=cmp_pe[i],
            cmp_w1=cmp_w1[i].astype(BF16), cmp_w2=cmp_w2[i].astype(BF16), nsa_w_o=nsa_w_o[i].astype(BF16),
            w_out=w_out[i].astype(BF16), ffn_norm_g=ffn_norm_g[i],
            ffn_w_gate_up=ffn_w_gate_up[i].astype(BF16), ffn_w_down=ffn_w_down[i].astype(BF16))
        x2 = _layer(x2, p, bsz=bsz, seq=seq)
    return x2.reshape(bsz, seq, d)
```

```python
import functools

import jax
import jax.numpy as jnp
import numpy as np
from jax import lax
from jax.experimental import pallas as pl
from jax.experimental.pallas import tpu as pltpu

F32 = jnp.float32
BF16 = jnp.bfloat16

D_MODEL = 1024
SSM_WIDTH = 512
SSM_GROUP = 16
SSM_GROUPS = SSM_WIDTH // SSM_GROUP
SSM_STATE = 64
CONV_CH = 512
CONV_K = 3
HEAD_DIM = 64
HEAD_SHIFT = 6
N_HEADS = 8
N_KV_HEADS = 2
GQA = N_HEADS // N_KV_HEADS
ATTN_WIDTH = N_HEADS * HEAD_DIM
KV_WIDTH = N_KV_HEADS * HEAD_DIM
CMP_BLOCK = 32
SEL_BLOCK = 64
SEL_SHIFT = 6
N_SELECT = 16
WINDOW = 512
CMP_HIDDEN = 256
FORCE_SCORE = 1e4
NSA_BRANCHES = 3
MIX_BRANCHES = 3
D_FF = 2816
RMS_EPS = 1e-6

LANES = 128
SUBLANES = 8
MASKED = -1e30
LOG2E = 1.4426950408889634
FLASH_UNROLL = 4
S5_CHUNK = 16
S5_TILE_GROUPS = LANES // SSM_GROUP
GATE_PAD = LANES
BF16_SUBLANES = 16
V_ROWS = HEAD_DIM + BF16_SUBLANES

Z_MIX = 0
Z_U = Z_MIX + MIX_BRANCHES * D_MODEL
Z_CB = Z_U + SSM_WIDTH
Z_CC = Z_CB + CONV_CH
Z_CX = Z_CC + CONV_CH
Z_Q = Z_CX + CONV_CH
Z_KC = Z_Q + ATTN_WIDTH
Z_VC = Z_KC + KV_WIDTH
Z_KS = Z_VC + KV_WIDTH
Z_VS = Z_KS + KV_WIDTH
Z_KW = Z_VS + KV_WIDTH
Z_VW = Z_KW + KV_WIDTH
Z_GATE = Z_VW + KV_WIDTH
Z_WIDTH = Z_GATE + GATE_PAD


def _gelu_tanh(x):
    return 0.5 * x * (1.0 + jnp.tanh(np.sqrt(2.0 / np.pi).astype(np.float32) * (x + 0.044715 * (x * x * x))))


def _sigmoid(x):
    return 0.5 * jnp.tanh(0.5 * x) + 0.5


def _rms(x, g):
    return x * lax.rsqrt(jnp.mean(x * x, axis=-1, keepdims=True) + RMS_EPS) * g


def _dot(a, b):
    return jnp.dot(a, b, preferred_element_type=F32)


def _head_rms(x, g):
    x2 = x * x
    hi = x2.astype(BF16)
    lo = (x2 - hi.astype(F32)).astype(BF16)
    w = x.shape[1]
    seg = jnp.where(jnp.right_shift(lax.broadcasted_iota(jnp.int32, (w, w), 0), HEAD_SHIFT)
                    == jnp.right_shift(lax.broadcasted_iota(jnp.int32, (w, w), 1), HEAD_SHIFT), 1.0, 0.0).astype(BF16)
    ss = _dot(hi, seg) + _dot(lo, seg)
    return x * lax.rsqrt(ss * (1.0 / HEAD_DIM) + RMS_EPS) * g


def _dot_nt(a, b):
    return lax.dot_general(a, b, (((1,), (1,)), ((), ())), preferred_element_type=F32)


def _inproj_kernel(x_ref, g_ref, w_ref, o_ref):
    h = _rms(x_ref[...], g_ref[...]).astype(BF16)
    o_ref[...] = _dot(h, w_ref[...])


def _inproj(x2, g, w, *, tm, tn):
    t, d = x2.shape
    n = w.shape[1]
    return pl.pallas_call(
        _inproj_kernel,
        grid=(n // tn, t // tm),
        in_specs=[pl.BlockSpec((tm, d), lambda j, i: (i, 0)),
                  pl.BlockSpec((1, d), lambda j, i: (0, 0)),
                  pl.BlockSpec((d, tn), lambda j, i: (0, j))],
        out_specs=pl.BlockSpec((tm, tn), lambda j, i: (i, j)),
        out_shape=jax.ShapeDtypeStruct((t, n), F32),
        name="inproj",
    )(x2, g.reshape(1, d), w)


def _permute_w_in(w):
    n_plain = SSM_WIDTH + 3 * CONV_CH + ATTN_WIDTH + 6 * KV_WIDTH
    n_gate = N_HEADS * NSA_BRANCHES
    gate = jnp.pad(w[:, n_plain:n_plain + n_gate], ((0, 0), (0, GATE_PAD - n_gate)))
    return jnp.concatenate([w[:, n_plain + n_gate:], w[:, :n_plain], gate], axis=1).astype(BF16)


def _s5_tables(lam_re, lam_im, b_re, b_im, c_re, c_im, d_skip, log_dt, n_chunks):
    hp = lax.Precision.HIGHEST
    g, p = lam_re.shape
    h, l, gt = SSM_GROUP, S5_CHUNK, S5_TILE_GROUPS
    nt = g // gt
    dt = jnp.exp(log_dt)[:, None]
    ar, ai = lam_re * dt, lam_im * dt

    def powers(k):
        mag = jnp.exp(ar[None] * k[:, None, None])
        ang = ai[None] * k[:, None, None]
        return mag * jnp.cos(ang), mag * jnp.sin(ang)

    pr, pi = powers(jnp.arange(l + 1, dtype=F32))
    nr, ni = pr[1] - 1.0, pi[1]
    den = lam_re * lam_re + lam_im * lam_im
    fr, fi = (nr * lam_re + ni * lam_im) / den, (ni * lam_re - nr * lam_im) / den
    bbr = fr[..., None] * b_re - fi[..., None] * b_im
    bbi = fr[..., None] * b_im + fi[..., None] * b_re
    wr = pr[:l, :, :, None] * bbr[None] - pi[:l, :, :, None] * bbi[None]
    wi = pr[:l, :, :, None] * bbi[None] + pi[:l, :, :, None] * bbr[None]
    kern = (jnp.einsum('gop,tgpi->tgio', c_re, wr, precision=hp)
            - jnp.einsum('gop,tgpi->tgio', c_im, wi, precision=hp))
    r = jnp.arange(l)

    def group_diag(x, row_w, col_w, halves=1):
        rows, hw = x.shape[-2:]
        w = hw // halves
        col = jnp.arange(halves * gt * w)
        src = (col // (gt * w)) * w + col % w
        expand = (jnp.arange(hw)[:, None] == src[None, :]).astype(BF16)
        tiled = jnp.einsum('...w,wv->...v', x.astype(BF16), expand, preferred_element_type=BF16)
        own = (jnp.arange(rows)[:, None] // row_w) % gt == (col[None, :] % (gt * w)) // col_w
        return jnp.where(own, tiled, jnp.zeros((), BF16))

    kblk = group_diag(kern.reshape(l, g * h, h), h, h)
    krev = kblk.reshape(l, nt, LANES, LANES)[::-1].transpose(1, 0, 2, 3).reshape(nt, l * LANES, LANES)
    krev = jnp.concatenate([krev, jnp.zeros((nt, LANES, LANES), krev.dtype)], axis=1)
    t_pair = jnp.concatenate(
        [jnp.concatenate([krev[:, (l - 1 - s) * LANES:(l + 1) * LANES], krev[:, (l - 2 - s) * LANES:l * LANES]], axis=2)
         for s in range(0, l, 2)], axis=1)
    qr, qi = pr[l - 1 - r], pi[l - 1 - r]
    st_re = qr[..., None] * bbr[None] - qi[..., None] * bbi[None]
    st_im = qr[..., None] * bbi[None] + qi[..., None] * bbr[None]
    rows_in = lambda x: group_diag(x.transpose(0, 1, 3, 2).reshape(l, g * h, p), h, p).reshape(l, nt, LANES, gt * p)
    er, ei = pr[1:l + 1][:, :, None, :], pi[1:l + 1][:, :, None, :]

    def rows_out(x):
        x = x.transpose(0, 1, 3, 2).reshape(l // 2, 2, g * p, h).transpose(0, 2, 1, 3).reshape(l // 2, g * p, 2 * h)
        return group_diag(x, p, h, halves=2).reshape(l // 2, nt, gt * p, 2 * LANES)

    n_steps = max(1, int(np.ceil(np.log2(n_chunks))))
    dr, di = powers(l * (2.0 ** jnp.arange(n_steps, dtype=F32)))
    lanes = lambda m: m.reshape(n_steps, nt, gt * p).transpose(1, 0, 2)
    return dict(t_pair=t_pair, s_re=rows_in(st_re), s_im=rows_in(st_im),
                c_re=rows_out(c_re[None] * er - c_im[None] * ei),
                c_im=rows_out(-(c_re[None] * ei + c_im[None] * er)),
                d_re=lanes(dr), d_im=lanes(di), d_skip=d_skip.reshape(1, g * h))


def _s5_kernel(u_ref, tp_ref, sre_ref, sim_ref, cre_ref, cim_ref, dre_ref, dim_ref, dsk_ref, y_ref):
    seq = u_ref.shape[0]
    l = S5_CHUNK
    nc = seq // l
    n_steps = dre_ref.shape[1]
    ns = dre_ref.shape[2]
    us = [u_ref[pl.ds(r, nc, stride=l), :] for r in range(l)]
    ucat = jnp.concatenate(us, axis=1).astype(BF16)
    xr = _dot(ucat, sre_ref[:, 0].reshape(l * LANES, ns))
    xi = _dot(ucat, sim_ref[:, 0].reshape(l * LANES, ns))
    row = lax.broadcasted_iota(jnp.int32, xr.shape, 0)
    for k in range(n_steps):
        s = 1 << k
        if s >= nc:
            break
        dr = dre_ref[0, k:k + 1, :]
        di = dim_ref[0, k:k + 1, :]
        sr = jnp.where(row >= s, pltpu.roll(xr, s, 0), 0.0)
        si = jnp.where(row >= s, pltpu.roll(xi, s, 0), 0.0)
        xr, xi = xr + (dr * sr - di * si), xi + (dr * si + di * sr)
    pr = jnp.where(row >= 1, pltpu.roll(xr, 1, 0), 0.0).astype(BF16)
    pi = jnp.where(row >= 1, pltpu.roll(xi, 1, 0), 0.0).astype(BF16)
    dsk = dsk_ref[...]
    off = 0
    for s in range(0, l, 2):
        rows = (s + 2) * LANES
        y2 = (_dot(ucat[:, :rows], tp_ref[0, off:off + rows, :])
              + _dot(pr, cre_ref[s // 2, 0]) + _dot(pi, cim_ref[s // 2, 0]))
        y_ref[pl.ds(s, nc, stride=l), :] = y2[:, :LANES] + dsk * us[s]
        y_ref[pl.ds(s + 1, nc, stride=l), :] = y2[:, LANES:] + dsk * us[s + 1]
        off += rows


def _s5_scan(z, tabs, *, bsz, seq):
    t = z.shape[0]
    nt = SSM_WIDTH // LANES
    tile3 = lambda a: pl.BlockSpec((1,) + a.shape[1:], lambda j, b: (j, 0, 0))
    step4 = lambda a: pl.BlockSpec((a.shape[0], 1) + a.shape[2:], lambda j, b: (0, j, 0, 0))
    return pl.pallas_call(
        _s5_kernel,
        grid=(nt, bsz),
        in_specs=[pl.BlockSpec((seq, LANES), lambda j, b: (b, Z_U // LANES + j)),
                  tile3(tabs['t_pair']), step4(tabs['s_re']), step4(tabs['s_im']),
                  step4(tabs['c_re']), step4(tabs['c_im']), tile3(tabs['d_re']), tile3(tabs['d_im']),
                  pl.BlockSpec((1, LANES), lambda j, b: (0, j))],
        out_specs=pl.BlockSpec((seq, LANES), lambda j, b: (b, j)),
        out_shape=jax.ShapeDtypeStruct((t, SSM_WIDTH), F32),
        name="s5_scan",
    )(z, tabs['t_pair'], tabs['s_re'], tabs['s_im'], tabs['c_re'], tabs['c_im'], tabs['d_re'], tabs['d_im'],
      tabs['d_skip'])


def _kv_prep_kernel(ks_ref, vs_ref, kw_ref, vw_ref, g_ref, kaug_ref, vsel_ref, kwin_ref, vwin_ref, *, seq):
    tp = ks_ref.shape[0]
    pos = (pl.program_id(0) * tp) % seq + lax.broadcasted_iota(jnp.int32, (tp, HEAD_DIM), 0)
    blk = jnp.right_shift(pos, SEL_SHIFT)
    onehot = jnp.where(lax.broadcasted_iota(jnp.int32, (tp, HEAD_DIM), 1) == blk, 1.0, 0.0).astype(BF16)
    zeros = jnp.zeros((tp, HEAD_DIM), BF16)
    ks = _head_rms(ks_ref[...], g_ref[0:1, :]).astype(BF16)
    kw = _head_rms(kw_ref[...], g_ref[1:2, :]).astype(BF16)
    for hh in range(N_KV_HEADS):
        sl = slice(hh * HEAD_DIM, (hh + 1) * HEAD_DIM)
        kaug_ref[hh] = jnp.concatenate([ks[:, sl], onehot], axis=1)
        kwin_ref[hh] = jnp.concatenate([kw[:, sl], zeros], axis=1)
    for v_ref, vt_ref in ((vs_ref, vsel_ref), (vw_ref, vwin_ref)):
        tk = vt_ref.shape[3]
        vt = v_ref[...].T.astype(BF16)
        ones = jnp.where(lax.broadcasted_iota(jnp.int32, (V_ROWS - HEAD_DIM, tp), 0) == 0, 1.0, 0.0).astype(BF16)
        for hh in range(N_KV_HEADS):
            vh = jnp.concatenate([vt[hh * HEAD_DIM:(hh + 1) * HEAD_DIM], ones], axis=0)
            for j in range(tp // tk):
                vt_ref[hh, j] = vh[:, j * tk:(j + 1) * tk]


def _kv_prep(z, k_norm_g, *, seq, tp, tk_sel, tk_win):
    t = z.shape[0]
    col = lambda off: pl.BlockSpec((tp, KV_WIDTH), lambda i, o=off // KV_WIDTH: (i, o))
    aug = 2 * HEAD_DIM
    kspec = pl.BlockSpec((N_KV_HEADS, tp, aug), lambda i: (0, i, 0))
    vspec = lambda tk: pl.BlockSpec((N_KV_HEADS, tp // tk, V_ROWS, tk), lambda i: (0, i, 0, 0))
    vshape = lambda tk: jax.ShapeDtypeStruct((N_KV_HEADS, t // tk, V_ROWS, tk), BF16)
    return pl.pallas_call(
        functools.partial(_kv_prep_kernel, seq=seq),
        grid=(t // tp,),
        in_specs=[col(Z_KS), col(Z_VS), col(Z_KW), col(Z_VW), pl.BlockSpec((2, KV_WIDTH), lambda i: (0, 0))],
        out_specs=[kspec, vspec(tk_sel), kspec, vspec(tk_win)],
        out_shape=[jax.ShapeDtypeStruct((N_KV_HEADS, t, aug), BF16), vshape(tk_sel),
                   jax.ShapeDtypeStruct((N_KV_HEADS, t, aug), BF16), vshape(tk_win)],
        name="kv_prep",
    )(z, z, z, z, jnp.tile(k_norm_g[1:3], (1, N_KV_HEADS)))


def _compress_kernel(x_ref, pe_ref, w1_ref, w2_ref, g_ref, o_ref):
    seq = x_ref.shape[0]
    nb = seq // (2 * CMP_BLOCK)
    lane = lax.broadcasted_iota(jnp.int32, (nb, 2 * HEAD_DIM), 1)
    rows = []
    for hh in range(N_KV_HEADS):
        for parity in range(2):
            cols = []
            for s in range(0, CMP_BLOCK, 2):
                a = x_ref[pl.ds(parity * CMP_BLOCK + s, nb, stride=2 * CMP_BLOCK), :]
                b = x_ref[pl.ds(parity * CMP_BLOCK + s + 1, nb, stride=2 * CMP_BLOCK), :]
                if hh == 0:
                    cols.append(jnp.where(lane < HEAD_DIM, a, pltpu.roll(b, HEAD_DIM, 1)))
                else:
                    cols.append(jnp.where(lane < HEAD_DIM, pltpu.roll(a, HEAD_DIM, 1), b))
            rows.append(jnp.concatenate(cols, axis=1))
    x = (jnp.concatenate(rows, axis=0) + pe_ref[0]).astype(BF16)
    hid = _gelu_tanh(_dot(x, w1_ref[0])).astype(BF16)
    y = _dot(hid, w2_ref[0])
    o_ref[0, 0] = jnp.where(pl.program_id(0) == 0, _rms(y, g_ref[...]), y)


def _compress(z, cmp_pe, w1, w2, k_gain, *, bsz, seq):
    nc = seq // CMP_BLOCK
    kdim = CMP_BLOCK * HEAD_DIM
    m = N_KV_HEADS * nc
    out = pl.pallas_call(
        _compress_kernel,
        grid=(2, bsz),
        in_specs=[pl.BlockSpec((seq, KV_WIDTH), lambda w, b: (b, Z_KC // KV_WIDTH + w)),
                  pl.BlockSpec((1, 1, kdim), lambda w, b: (w, 0, 0)),
                  pl.BlockSpec((1, kdim, CMP_HIDDEN), lambda w, b: (w, 0, 0)),
                  pl.BlockSpec((1, CMP_HIDDEN, HEAD_DIM), lambda w, b: (w, 0, 0)),
                  pl.BlockSpec((1, HEAD_DIM), lambda w, b: (0, 0))],
        out_specs=pl.BlockSpec((1, 1, m, HEAD_DIM), lambda w, b: (w, b, 0, 0)),
        out_shape=jax.ShapeDtypeStruct((2, bsz, m, HEAD_DIM), F32),
        name="compress",
    )(z, cmp_pe.reshape(2, 1, kdim), w1, w2, k_gain.reshape(1, HEAD_DIM))
    return out.reshape(2, bsz * N_KV_HEADS, nc, HEAD_DIM)


def _head_gate(gates, kvh, g, branch):
    c0 = g * NSA_BRANCHES + branch
    c1 = (GQA + g) * NSA_BRANCHES + branch
    if isinstance(kvh, int):
        c = c1 if kvh else c0
        return gates[:, c:c + 1]
    return jnp.where(kvh == 0, gates[:, c0:c0 + 1], gates[:, c1:c1 + 1])


def _cmp_select_kernel(q_ref, kc_ref, vc_ref, qg_ref, o_ref, qaug_ref, imp_ref, cnt_ref):
    tq = q_ref.shape[0]
    nc = kc_ref.shape[1]
    nb = nc // 2
    pair_w = 2 * HEAD_DIM
    qw = GQA * HEAD_DIM
    q0 = pl.program_id(1) * tq
    qf = _head_rms(q_ref[...], qg_ref[...]) * (HEAD_DIM ** -0.5)
    qn = qf.astype(BF16)
    q2 = qf * LOG2E

    row = lax.broadcasted_iota(jnp.int32, (nc, tq), 0)
    t = q0 + lax.broadcasted_iota(jnp.int32, (nc, tq), 1)
    blk = jnp.where(row < nb, 2 * row, 2 * (row - nb) + 1)
    valid = (blk + 1) * CMP_BLOCK - 1 <= t
    lane = lax.broadcasted_iota(jnp.int32, (tq, pair_w), 1)
    j = lax.broadcasted_iota(jnp.int32, (nb, tq), 0)
    cur = jnp.right_shift(q0 + lax.broadcasted_iota(jnp.int32, (nb, tq), 1), SEL_SHIFT)
    forced = (j == 0) | (j == cur) | (j == cur - 1)
    visible = j <= cur

    def two_heads(x):
        zx = jnp.zeros_like(x)
        return jnp.concatenate([jnp.concatenate([x, zx], axis=1), jnp.concatenate([zx, x], axis=1)], axis=0)

    pairs = [(kvh, pr) for kvh in range(N_KV_HEADS) for pr in range(GQA // 2)]
    k2 = [two_heads(kc_ref[kvh].astype(BF16)) for kvh in range(N_KV_HEADS)]
    v2 = [two_heads(vc_ref[kvh].astype(BF16)) for kvh in range(N_KV_HEADS)]
    st = [_dot_nt(k2[kvh], qn[:, kvh * qw + pr * pair_w:kvh * qw + (pr + 1) * pair_w]) for kvh, pr in pairs]
    psum = [jnp.zeros((nc, tq), F32) for _ in range(N_KV_HEADS)]
    probs = []
    for (kvh, pr), s2 in zip(pairs, st):
        ph = []
        for hh in range(2):
            sm = jnp.where(valid, s2[hh * nc:(hh + 1) * nc], MASKED)
            m = jnp.max(sm, axis=0, keepdims=True)
            m = jnp.where(m > 0.5 * MASKED, m, 0.0)
            e = jnp.exp(sm - m)
            p = e * (1.0 / jnp.maximum(jnp.sum(e, axis=0, keepdims=True), 1e-30))
            psum[kvh] = psum[kvh] + p
            ph.append(p.astype(BF16))
        probs.append(jnp.concatenate(ph, axis=0))
    imps = []
    for kvh in range(N_KV_HEADS):
        imp = psum[kvh][:nb] + psum[kvh][nb:]
        imp = jnp.where(forced, FORCE_SCORE, jnp.where(visible, imp, -jnp.inf))
        imp_ref[kvh] = imp
        imps.append(imp)
    for n, ((kvh, pr), p2) in enumerate(zip(pairs, probs)):
        o_ref[n * pair_w:(n + 1) * pair_w, :] = lax.dot_general(
            v2[kvh], p2, (((0,), (0,)), ((), ())), preferred_element_type=F32).astype(o_ref.dtype)

    n_vis = jnp.minimum(nb, (q0 + tq - 1) // SEL_BLOCK + 1)
    n_slabs = nb // SUBLANES
    row_in_slab = lax.broadcasted_iota(jnp.int32, (SUBLANES, tq), 0)
    cnt_ref[...] = jnp.zeros(cnt_ref.shape, F32)
    for grp in range(n_slabs):
        @pl.when(grp * SUBLANES < n_vis)
        def _(grp=grp):
            for kvh in range(N_KV_HEADS):
                slabs = [imp_ref[kvh, s * SUBLANES:(s + 1) * SUBLANES, :] for s in range(n_slabs)]
                cnt = [jnp.zeros((SUBLANES, tq), F32) for _ in range(n_slabs)]
                for i in range(grp * SUBLANES, (grp + 1) * SUBLANES):
                    vi = imp_ref[kvh, i:i + 1, :]
                    for s in range(n_slabs):
                        if s > grp:
                            beats = vi >= slabs[s]
                        elif s < grp:
                            beats = vi > slabs[s]
                        else:
                            beats = (vi > slabs[s]) | ((vi == slabs[s]) & (i - s * SUBLANES < row_in_slab))
                        cnt[s] = cnt[s] + jnp.where(beats, 1.0, 0.0)
                cnt_ref[kvh] += jnp.concatenate(cnt, axis=0)
    for kvh in range(N_KV_HEADS):
        selneg_t = jnp.where(visible & (cnt_ref[kvh] < float(N_SELECT)), 0.0, MASKED)
        if nb < HEAD_DIM:
            selneg_t = jnp.concatenate([selneg_t, jnp.zeros((HEAD_DIM - nb, tq), F32)], axis=0)
        selneg = jnp.concatenate([selneg_t, selneg_t], axis=0).T
        for pr in range(GQA // 2):
            c0 = kvh * qw + pr * pair_w
            qp = q2[:, c0:c0 + pair_w]
            qaug_ref[kvh, 2 * pr] = jnp.where(lane < HEAD_DIM, qp, selneg).astype(BF16)
            qaug_ref[kvh, 2 * pr + 1] = jnp.where(lane < HEAD_DIM, pltpu.roll(qp, HEAD_DIM, 1), selneg).astype(BF16)


def _cmp_select(z, cmp_kv, q_norm_g, *, bsz, seq, tq):
    t = z.shape[0]
    nq = seq // tq
    nc = seq // CMP_BLOCK
    aug = 2 * HEAD_DIM
    qw = GQA * HEAD_DIM
    kc, vc = cmp_kv[0], cmp_kv[1]
    kv_spec = pl.BlockSpec((N_KV_HEADS, nc, HEAD_DIM), lambda b, i: (b, 0, 0))
    return pl.pallas_call(
        _cmp_select_kernel,
        grid=(bsz, nq),
        in_specs=[pl.BlockSpec((tq, ATTN_WIDTH), lambda b, i: (b * nq + i, Z_Q // ATTN_WIDTH)),
                  kv_spec, kv_spec,
                  pl.BlockSpec((1, ATTN_WIDTH), lambda b, i: (0, 0))],
        out_specs=[pl.BlockSpec((ATTN_WIDTH, tq), lambda b, i: (0, b * nq + i)),
                   pl.BlockSpec((N_KV_HEADS, GQA, tq, aug), lambda b, i: (b, 0, i, 0))],
        out_shape=[jax.ShapeDtypeStruct((ATTN_WIDTH, t), BF16),
                   jax.ShapeDtypeStruct((bsz * N_KV_HEADS, GQA, seq, aug), BF16)],
        scratch_shapes=[pltpu.VMEM((N_KV_HEADS, nc // 2, tq), F32), pltpu.VMEM((N_KV_HEADS, nc // 2, tq), F32)],
        name="cmp_select",
    )(z, kc, vc, jnp.tile(q_norm_g, N_HEADS).reshape(1, ATTN_WIDTH))


def _flash_kernel(q_ref, k_ref, vt_ref, o_ref, sa_ref, sb_ref, pa_ref, pb_ref, m_ref, a_ref, acc_ref, *, window):
    tq = q_ref.shape[2]
    tk = vt_ref.shape[3]
    cols = GQA * tq
    q0 = pl.program_id(2) * tq
    qa = q_ref[0].reshape(cols, q_ref.shape[3])
    t = q0 + (lax.broadcasted_iota(jnp.int32, (tk, cols), 1) & (tq - 1))
    key_row = lax.broadcasted_iota(jnp.int32, (tk, cols), 0)
    span = 2 * tk
    lo = jnp.maximum(0, q0 - window + 1) // span if window else 0
    hi = (q0 + tq + span - 1) // span
    mid1 = jnp.maximum(lo, (q0 + tq - 1 - window) // span + 1) if window else lo
    mid2 = q0 // span
    first = 2 * lo
    last = 2 * hi - 1

    def scores(kb):
        return _dot_nt(k_ref[0, pl.ds(pl.multiple_of(kb * tk, tk), tk), :], qa)

    m_ref[...] = jnp.full(m_ref.shape, MASKED, F32)
    a_ref[...] = jnp.ones(a_ref.shape, F32)
    acc_ref[...] = jnp.zeros(acc_ref.shape, F32)
    pb_ref[...] = jnp.zeros(pb_ref.shape, BF16)
    sa_ref[...] = scores(first)

    def tile(kb, s_cur, s_nxt, p_prev, p_cur, mask):
        pv = _dot(vt_ref[0, jnp.maximum(kb - 1, first)], p_prev[...])
        s_nxt[...] = scores(jnp.minimum(kb + 1, last))
        s = s_cur[...]
        if mask == 'causal':
            s = jnp.where(kb * tk + key_row <= t, s, MASKED)
        elif mask == 'window':
            s = jnp.where(kb * tk + key_row > t - window, s, MASKED)
        m_old = m_ref[...]
        m_new = jnp.maximum(m_old, jnp.max(s, axis=0, keepdims=True))
        acc_ref[...] = a_ref[...] * acc_ref[...] + pv
        a_ref[...] = jnp.exp2(m_old - m_new)
        p_cur[...] = jnp.exp2(s - m_new).astype(BF16)
        m_ref[...] = m_new

    def pair_step(pair, mask):
        tile(2 * pair, sa_ref, sb_ref, pb_ref, pa_ref, mask)
        tile(2 * pair + 1, sb_ref, sa_ref, pa_ref, pb_ref, mask)

    def run(begin, end, mask, unroll=1):
        n_main = (end - begin) // unroll

        def main(i, carry):
            for u in range(unroll):
                pair_step(begin + i * unroll + u, mask)
            return carry

        def rest(pair, carry):
            pair_step(pair, mask)
            return carry

        if unroll > 1:
            lax.fori_loop(0, n_main, main, 0)
        lax.fori_loop(begin + n_main * unroll if unroll > 1 else begin, end, rest, 0)

    if window:
        interior = (mid1 - lo == 1) & (mid2 - mid1 == 1) & (hi - mid2 == 1)

        @pl.when(interior)
        def _():
            pair_step(lo, 'window')
            pair_step(lo + 1, None)
            pair_step(lo + 2, 'causal')

        @pl.when(jnp.logical_not(interior))
        def _():
            run(lo, mid1, 'window')
            run(mid1, mid2, None)
            run(mid2, hi, 'causal')
    else:
        run(mid1, mid2, None, unroll=FLASH_UNROLL)
        run(mid2, hi, 'causal')
    acc = a_ref[...] * acc_ref[...] + _dot(vt_ref[0, last], pb_ref[...])
    o_t = acc[:HEAD_DIM] / acc[HEAD_DIM:HEAD_DIM + 1]
    for g in range(GQA):
        o_ref[g * HEAD_DIM:(g + 1) * HEAD_DIM, :] = o_t[:, g * tq:(g + 1) * tq].astype(o_ref.dtype)


def _flash(qaug, k, vt, *, bsz, seq, tq, window, name):
    t = bsz * seq
    nq = seq // tq
    aug = qaug.shape[3]
    tk = vt.shape[3]
    qw = GQA * HEAD_DIM
    cols = GQA * tq
    return pl.pallas_call(
        functools.partial(_flash_kernel, window=window),
        grid=(bsz, N_KV_HEADS, nq),
        in_specs=[pl.BlockSpec((1, GQA, tq, aug), lambda b, h, i: (b * N_KV_HEADS + h, 0, i, 0)),
                  pl.BlockSpec((1, seq, aug), lambda b, h, i: (h, b, 0)),
                  pl.BlockSpec((1, seq // tk, V_ROWS, tk), lambda b, h, i: (h, b, 0, 0))],
        out_specs=pl.BlockSpec((qw, tq), lambda b, h, i: (h, b * nq + i)),
        out_shape=jax.ShapeDtypeStruct((ATTN_WIDTH, t), BF16),
        scratch_shapes=[pltpu.VMEM((tk, cols), F32), pltpu.VMEM((tk, cols), F32),
                        pltpu.VMEM((tk, cols), BF16), pltpu.VMEM((tk, cols), BF16),
                        pltpu.VMEM((1, cols), F32), pltpu.VMEM((1, cols), F32),
                        pltpu.VMEM((V_ROWS, cols), F32)],
        name=name,
    )(qaug, k, vt)


def _merge_kernel(x_ref, g0_ref, g1_ref, g2_ref, cb_ref, cc_ref, cx_ref, pc_ref, px_ref, gl_ref, ys_ref,
                  oc_ref, os_ref, ow_ref, cw_ref, wc_ref, wglu_ref, wo_ref, wout_ref, out_ref, *, seq):
    tm = x_ref.shape[0]
    yg = _dot(_gelu_tanh(ys_ref[...]).astype(BF16), wglu_ref[...])
    y_ssm = yg[:, :D_MODEL] * _sigmoid(yg[:, D_MODEL:])
    zc = cc_ref[...] * cx_ref[...]
    keep = jnp.where((pl.program_id(0) * tm) % seq != 0, 1.0, 0.0)
    prev = pc_ref[...] * px_ref[...] * keep
    row = lax.broadcasted_iota(jnp.int32, zc.shape, 0)
    z1 = jnp.where(row >= 1, pltpu.roll(zc, 1, 0), prev[7:8, :])
    z2 = jnp.where(row >= 2, pltpu.roll(zc, 2, 0), jnp.where(row == 1, prev[7:8, :], prev[6:7, :]))
    conv = cw_ref[0:1, :] * z2 + cw_ref[1:2, :] * z1 + cw_ref[2:3, :] * zc
    y_conv = _dot((cb_ref[...] * conv).astype(BF16), wc_ref[...])
    gates_t = _sigmoid(gl_ref[...]).T
    heads = []
    for hd in range(N_HEADS):
        rows = slice(hd * HEAD_DIM, (hd + 1) * HEAD_DIM)
        c = hd * NSA_BRANCHES
        heads.append(oc_ref[rows, :] * gates_t[c:c + 1] + os_ref[rows, :] * gates_t[c + 1:c + 2]
                     + ow_ref[rows, :] * gates_t[c + 2:c + 3])
    o_t = jnp.concatenate(heads, axis=0).astype(BF16)
    y_attn = lax.dot_general(o_t, wo_ref[...], (((0,), (0,)), ((), ())), preferred_element_type=F32)
    mixed =_sigmoid(g0_ref[...]) * y_ssm + _sigmoid(g1_ref[...]) * y_conv + _sigmoid(g2_ref[...]) * y_attn
    out_ref[...] = x_ref[...] + _dot(mixed.astype(BF16), wout_ref[...])


def _merge(x2, z, ys, o_cmp, o_sel, o_win, conv_w, wc, wglu, wo, wout, *, seq, tm):
    t, d = x2.shape
    rb = tm // 8
    zc = lambda width, off: pl.BlockSpec((tm, width), lambda i, o=off // width: (i, o))
    zprev = lambda off: pl.BlockSpec((8, CONV_CH), lambda i, o=off // CONV_CH: (jnp.maximum(i * rb - 1, 0), o))
    row = lambda width: pl.BlockSpec((tm, width), lambda i: (i, 0))
    full = lambda a: pl.BlockSpec(a.shape, lambda i: (0, 0))
    branch_t = pl.BlockSpec((ATTN_WIDTH, tm), lambda i: (0, i))
    return pl.pallas_call(
        functools.partial(_merge_kernel, seq=seq),
        grid=(t // tm,),
        in_specs=[row(d), zc(d, Z_MIX), zc(d, Z_MIX + d), zc(d, Z_MIX + 2 * d),
                  zc(CONV_CH, Z_CB), zc(CONV_CH, Z_CC), zc(CONV_CH, Z_CX), zprev(Z_CC), zprev(Z_CX),
                  zc(GATE_PAD, Z_GATE), row(SSM_WIDTH), branch_t, branch_t, branch_t,
                  full(conv_w), full(wc), full(wglu), full(wo), full(wout)],
        out_specs=row(d),
        out_shape=jax.ShapeDtypeStruct((t, d), F32),
        name="merge",
    )(x2, z, z, z, z, z, z, z, z, z, ys, o_cmp, o_sel, o_win, conv_w, wc, wglu, wo, wout)


def _ffn_kernel(x_ref, g_ref, wg_ref, wu_ref, wd_ref, o_ref, h_ref, acc_ref):
    f = pl.program_id(1)

    @pl.when(f == 0)
    def _():
        h_ref[...] = _rms(x_ref[...], g_ref[...]).astype(BF16)
        acc_ref[...] = jnp.zeros(acc_ref.shape, F32)

    h = h_ref[...]
    gate = _dot(h, wg_ref[...])
    up = _dot(h, wu_ref[...])
    act = (gate * _sigmoid(gate) * up).astype(BF16)
    acc_ref[...] += _dot(act, wd_ref[...])

    @pl.when(f == pl.num_programs(1) - 1)
    def _():
        o_ref[...] = x_ref[...] + acc_ref[...]


def _ffn(x2, g, w_gate_up, w_down, *, tm, tf):
    t, d = x2.shape
    nf = D_FF // tf
    return pl.pallas_call(
        _ffn_kernel,
        grid=(t // tm, nf),
        in_specs=[pl.BlockSpec((tm, d), lambda i, f: (i, 0)),
                  pl.BlockSpec((1, d), lambda i, f: (0, 0)),
                  pl.BlockSpec((d, tf), lambda i, f: (0, f)),
                  pl.BlockSpec((d, tf), lambda i, f: (0, nf + f)),
                  pl.BlockSpec((tf, d), lambda i, f: (f, 0))],
        out_specs=pl.BlockSpec((tm, d), lambda i, f: (i, 0)),
        out_shape=jax.ShapeDtypeStruct((t, d), F32),
        scratch_shapes=[pltpu.VMEM((tm, d), BF16), pltpu.VMEM((tm, d), F32)],
        compiler_params=pltpu.CompilerParams(dimension_semantics=("parallel", "arbitrary")),
        name="ffn",
    )(x2, g.reshape(1, d), w_gate_up, w_gate_up, w_down)


def _pick(n, pref):
    while n % pref:
        pref //= 2
    return pref


def _layer(x2, p, *, bsz, seq):
    t = x2.shape[0]
    z = _inproj(x2, p['mix_norm_g'], p['w_in'], tm=_pick(t, 1024), tn=Z_WIDTH // 3)
    ys = _s5_scan(z, p['s5'], bsz=bsz, seq=seq)
    kaug, vsel, kwin, vwin = _kv_prep(z, p['k_norm_g'], seq=seq, tp=_pick(seq, 512), tk_sel=128, tk_win=128)
    cmp_kv = _compress(z, p['cmp_pe'], p['cmp_w1'], p['cmp_w2'], p['k_norm_g'][0], bsz=bsz, seq=seq)
    o_cmp, qaug = _cmp_select(z, cmp_kv, p['q_norm_g'], bsz=bsz, seq=seq, tq=256)
    tq = _pick(seq, 256)
    o_sel = _flash(qaug, kaug, vsel, bsz=bsz, seq=seq, tq=tq, window=0, name="sel_attn")
    o_win = _flash(qaug, kwin, vwin, bsz=bsz, seq=seq, tq=tq, window=WINDOW, name="win_attn")
    x2 = _merge(x2, z, ys, o_cmp, o_sel, o_win, p['conv_w'], p['conv_w_out'], p['ssm_w_glu'], p['nsa_w_o'],
                p['w_out'], seq=seq, tm=_pick(seq, 256))
    return _ffn(x2, p['ffn_norm_g'], p['ffn_w_gate_up'], p['ffn_w_down'], tm=_pick(t, 1024), tf=256)


def kernel(x, mix_norm_g, w_in, ssm_lam_re, ssm_lam_im, ssm_b_re, ssm_b_im, ssm_c_re, ssm_c_im, ssm_d, ssm_log_dt, ssm_w_glu, conv_w, conv_w_out, q_norm_g, k_norm_g, cmp_pe, cmp_w1, cmp_w2, nsa_w_o, w_out, ffn_norm_g, ffn_w_gate_up, ffn_w_down):
    bsz, seq, d = x.shape
    x2 = x.reshape(bsz * seq, d)
    for i in range(w_in.shape[0]):
        p = dict(
            mix_norm_g=mix_norm_g[i], w_in=_permute_w_in(w_in[i]),
            s5=_s5_tables(ssm_lam_re[i], ssm_lam_im[i], ssm_b_re[i], ssm_b_im[i], ssm_c_re[i], ssm_c_im[i],
                          ssm_d[i], ssm_log_dt[i], seq // S5_CHUNK),
            ssm_w_glu=ssm_w_glu[i].astype(BF16), conv_w=conv_w[i], conv_w_out=conv_w_out[i].astype(BF16),
            q_norm_g=q_norm_g[i], k_norm_g=k_norm_g[i], cmp_pe=cmp_pe[i],
            cmp_w1=cmp_w1[i].astype(BF16), cmp_w2=cmp_w2[i].astype(BF16), nsa_w_o=nsa_w_o[i].astype(BF16),
            w_out=w_out[i].astype(BF16), ffn_norm_g=ffn_norm_g[i],
            ffn_w_gate_up=ffn_w_gate_up[i].astype(BF16), ffn_w_down=ffn_w_down[i].astype(BF16))
        x2 = _layer(x2, p, bsz=bsz, seq=seq)
    return x2.reshape(bsz, seq, d)
```

```python
import functools

import jax
import jax.numpy as jnp
import numpy as np
from jax import lax
from jax.experimental import pallas as pl
from jax.experimental.pallas import tpu as pltpu

F32 = jnp.float32
BF16 = jnp.bfloat16

D_MODEL = 1024
SSM_WIDTH = 512
SSM_GROUP = 16
SSM_GROUPS = SSM_WIDTH // SSM_GROUP
SSM_STATE = 64
CONV_CH = 512
CONV_K = 3
HEAD_DIM = 64
HEAD_SHIFT = HEAD_DIM.bit_length() - 1
N_HEADS = 8
N_KV_HEADS = 2
GQA = N_HEADS // N_KV_HEADS
ATTN_WIDTH = N_HEADS * HEAD_DIM
KV_WIDTH = N_KV_HEADS * HEAD_DIM
CMP_BLOCK = 32
SEL_BLOCK = 64
SEL_SHIFT = SEL_BLOCK.bit_length() - 1
N_SELECT = 16
WINDOW = 512
CMP_HIDDEN = 256
FORCE_SCORE = 1e4
NSA_BRANCHES = 3
MIX_BRANCHES = 3
D_FF = 2816
RMS_EPS = 1e-6

LANES = 128
SUBLANES = 8
MXU_WIDTH = 256
MASKED = -1e30
LOG2E = 1.4426950408889634
FLASH_UNROLL = 4
S5_CHUNK = 16
S5_TILE_GROUPS = LANES // SSM_GROUP
GATE_PAD = MXU_WIDTH
BF16_SUBLANES = 16
V_ROWS = HEAD_DIM + BF16_SUBLANES

Z_MIX = 0
Z_U = Z_MIX + MIX_BRANCHES * D_MODEL
Z_CB = Z_U + SSM_WIDTH
Z_CC = Z_CB + CONV_CH
Z_CX = Z_CC + CONV_CH
Z_Q = Z_CX + CONV_CH
Z_KC = Z_Q + ATTN_WIDTH
Z_VC = Z_KC + KV_WIDTH
Z_KS = Z_VC + KV_WIDTH
Z_VS = Z_KS + KV_WIDTH
Z_KW = Z_VS + KV_WIDTH
Z_VW = Z_KW + KV_WIDTH
Z_GATE = Z_VW + KV_WIDTH
Z_WIDTH = Z_GATE + GATE_PAD


def _gelu_tanh(x):
    return 0.5 * x * (1.0 + jnp.tanh(np.sqrt(2.0 / np.pi).astype(np.float32) * (x + 0.044715 * (x * x * x))))


def _sigmoid(x):
    return 0.5 * jnp.tanh(0.5 * x) + 0.5


def _rms(x, g):
    return x * lax.rsqrt(jnp.mean(x * x, axis=-1, keepdims=True) + RMS_EPS) * g


def _dot(a, b):
    return jnp.dot(a, b, preferred_element_type=F32)


def _head_rms(x, g):
    x2 = x * x
    hi = x2.astype(BF16)
    lo = (x2 - hi.astype(F32)).astype(BF16)
    w = x.shape[1]
    seg = jnp.where(jnp.right_shift(lax.broadcasted_iota(jnp.int32, (w, w), 0), HEAD_SHIFT)
                    == jnp.right_shift(lax.broadcasted_iota(jnp.int32, (w, w), 1), HEAD_SHIFT), 1.0, 0.0).astype(BF16)
    ss = _dot(hi, seg) + _dot(lo, seg)
    return x * lax.rsqrt(ss * (1.0 / HEAD_DIM) + RMS_EPS) * g


def _dot_nt(a, b):
    return lax.dot_general(a, b, (((1,), (1,)), ((), ())), preferred_element_type=F32)


def _inproj_kernel(x_ref, g_ref, w_ref, o_ref):
    h = _rms(x_ref[...], g_ref[...]).astype(BF16)
    o_ref[...] = _dot(h, w_ref[...])


def _inproj(x2, g, w, *, tm, tn):
    t, d = x2.shape
    n = w.shape[1]
    return pl.pallas_call(
        _inproj_kernel,
        grid=(n // tn, t // tm),
        in_specs=[pl.BlockSpec((tm, d), lambda j, i: (i, 0)),
                  pl.BlockSpec((1, d), lambda j, i: (0, 0)),
                  pl.BlockSpec((d, tn), lambda j, i: (0, j))],
        out_specs=pl.BlockSpec((tm, tn), lambda j, i: (i, j)),
        out_shape=jax.ShapeDtypeStruct((t, n), F32),
        name="inproj",
    )(x2, g.reshape(1, d), w)


def _permute_w_in(w):
    n_plain = SSM_WIDTH + 3 * CONV_CH + ATTN_WIDTH + 6 * KV_WIDTH
    n_gate = N_HEADS * NSA_BRANCHES
    gate = jnp.pad(w[:, n_plain:n_plain + n_gate], ((0, 0), (0, GATE_PAD - n_gate)))
    return jnp.concatenate([w[:, n_plain + n_gate:], w[:, :n_plain], gate], axis=1).astype(BF16)


def _s5_tables(lam_re, lam_im, b_re, b_im, c_re, c_im, d_skip, log_dt, n_chunks):
    hp = lax.Precision.HIGHEST
    g, p = lam_re.shape
    h, l, gt = SSM_GROUP, S5_CHUNK, S5_TILE_GROUPS
    nt = g // gt
    dt = jnp.exp(log_dt)[:, None]
    ar, ai = lam_re * dt, lam_im * dt

    def powers(k):
        mag = jnp.exp(ar[None] * k[:, None, None])
        ang = ai[None] * k[:, None, None]
        return mag * jnp.cos(ang), mag * jnp.sin(ang)

    pr, pi = powers(jnp.arange(l + 1, dtype=F32))
    nr, ni = pr[1] - 1.0, pi[1]
    den = lam_re * lam_re + lam_im * lam_im
    fr, fi = (nr * lam_re + ni * lam_im) / den, (ni * lam_re - nr * lam_im) / den
    bbr = fr[..., None] * b_re - fi[..., None] * b_im
    bbi = fr[..., None] * b_im + fi[..., None] * b_re
    wr = pr[:l, :, :, None] * bbr[None] - pi[:l, :, :, None] * bbi[None]
    wi = pr[:l, :, :, None] * bbi[None] + pi[:l, :, :, None] * bbr[None]
    kern = (jnp.einsum('gop,tgpi->tgio', c_re, wr, precision=hp)
            - jnp.einsum('gop,tgpi->tgio', c_im, wi, precision=hp))
    r = jnp.arange(l)

    def group_diag(x, row_w, col_w, halves=1):
        rows, hw = x.shape[-2:]
        w = hw // halves
        col = jnp.arange(halves * gt * w)
        src = (col // (gt * w)) * w + col % w
        expand = (jnp.arange(hw)[:, None] == src[None, :]).astype(BF16)
        tiled = jnp.einsum('...w,wv->...v', x.astype(BF16), expand, preferred_element_type=BF16)
        own = (jnp.arange(rows)[:, None] // row_w) % gt == (col[None, :] % (gt * w)) // col_w
        return jnp.where(own, tiled, jnp.zeros((), BF16))

    kblk = group_diag(kern.reshape(l, g * h, h), h, h)
    krev = kblk.reshape(l, nt, LANES, LANES)[::-1].transpose(1, 0, 2, 3).reshape(nt, l * LANES, LANES)
    krev = jnp.concatenate([krev, jnp.zeros((nt, LANES, LANES), krev.dtype)], axis=1)
    t_pair = jnp.concatenate(
        [jnp.concatenate([krev[:, (l - 1 - s) * LANES:(l + 1) * LANES], krev[:, (l - 2 - s) * LANES:l * LANES]], axis=2)
         for s in range(0, l, 2)], axis=1)
    qr, qi = pr[l - 1 - r], pi[l - 1 - r]
    st_re = qr[..., None] * bbr[None] - qi[..., None] * bbi[None]
    st_im = qr[..., None] * bbi[None] + qi[..., None] * bbr[None]
    rows_in = lambda x: group_diag(x.transpose(0, 1, 3, 2).reshape(l, g * h, p), h, p).reshape(l, nt, LANES, gt * p)
    er, ei = pr[1:l + 1][:, :, None, :], pi[1:l + 1][:, :, None, :]

    def rows_out(x):
        x = x.transpose(0, 1, 3, 2).reshape(l // 2, 2, g * p, h).transpose(0, 2, 1, 3).reshape(l // 2, g * p, 2 * h)
        return group_diag(x, p, h, halves=2).reshape(l // 2, nt, gt * p, 2 * LANES)

    n_steps = max(1, int(np.ceil(np.log2(n_chunks))))
    dr, di = powers(l * (2.0 ** jnp.arange(n_steps, dtype=F32)))
    lanes = lambda m: m.reshape(n_steps, nt, gt * p).transpose(1, 0, 2)
    return dict(t_pair=t_pair, s_re=rows_in(st_re), s_im=rows_in(st_im),
                c_re=rows_out(c_re[None] * er - c_im[None] * ei),
                c_im=rows_out(-(c_re[None] * ei + c_im[None] * er)),
                d_re=lanes(dr), d_im=lanes(di), d_skip=d_skip.reshape(1, g * h))


def _s5_kernel(u_ref, tp_ref, sre_ref, sim_ref, cre_ref, cim_ref, dre_ref, dim_ref, dsk_ref, y_ref):
    seq = u_ref.shape[0]
    l = S5_CHUNK
    nc = seq // l
    n_steps = dre_ref.shape[1]
    ns = dre_ref.shape[2]
    us = [u_ref[pl.ds(r, nc, stride=l), :] for r in range(l)]
    ucat = jnp.concatenate(us, axis=1).astype(BF16)
    xr = _dot(ucat, sre_ref[:, 0].reshape(l * LANES, ns))
    xi = _dot(ucat, sim_ref[:, 0].reshape(l * LANES, ns))
    row = lax.broadcasted_iota(jnp.int32, xr.shape, 0)
    for k in range(n_steps):
        s = 1 << k
        if s >= nc:
            break
        dr = dre_ref[0, k:k + 1, :]
        di = dim_ref[0, k:k + 1, :]
        sr = jnp.where(row >= s, pltpu.roll(xr, s, 0), 0.0)
        si = jnp.where(row >= s, pltpu.roll(xi, s, 0), 0.0)
        xr, xi = xr + (dr * sr - di * si), xi + (dr * si + di * sr)
    pr = jnp.where(row >= 1, pltpu.roll(xr, 1, 0), 0.0).astype(BF16)
    pi = jnp.where(row >= 1, pltpu.roll(xi, 1, 0), 0.0).astype(BF16)
    dsk = dsk_ref[...]
    off = 0
    for s in range(0, l, 2):
        rows = (s + 2) * LANES
        y2 = (_dot(ucat[:, :rows], tp_ref[0, off:off + rows, :])
              + _dot(pr, cre_ref[s // 2, 0]) + _dot(pi, cim_ref[s // 2, 0]))
        y_ref[pl.ds(s, nc, stride=l), :] = y2[:, :LANES] + dsk * us[s]
        y_ref[pl.ds(s + 1, nc, stride=l), :] = y2[:, LANES:] + dsk * us[s + 1]
        off += rows


def _s5_scan(z, tabs, *, bsz, seq):
    t = z.shape[0]
    nt = SSM_WIDTH // LANES
    tile3 = lambda a: pl.BlockSpec((1,) + a.shape[1:], lambda j, b: (j, 0, 0))
    step4 = lambda a: pl.BlockSpec((a.shape[0], 1) + a.shape[2:], lambda j, b: (0, j, 0, 0))
    return pl.pallas_call(
        _s5_kernel,
        grid=(nt, bsz),
        in_specs=[pl.BlockSpec((seq, LANES), lambda j, b: (b, Z_U // LANES + j)),
                  tile3(tabs['t_pair']), step4(tabs['s_re']), step4(tabs['s_im']),
                  step4(tabs['c_re']), step4(tabs['c_im']), tile3(tabs['d_re']), tile3(tabs['d_im']),
                  pl.BlockSpec((1, LANES), lambda j, b: (0, j))],
        out_specs=pl.BlockSpec((seq, LANES), lambda j, b: (b, j)),
        out_shape=jax.ShapeDtypeStruct((t, SSM_WIDTH), F32),
        name="s5_scan",
    )(z, tabs['t_pair'], tabs['s_re'], tabs['s_im'], tabs['c_re'], tabs['c_im'], tabs['d_re'], tabs['d_im'],
      tabs['d_skip'])


def _kv_prep_kernel(ks_ref, vs_ref, kw_ref, vw_ref, g_ref, kaug_ref, vsel_ref, kwin_ref, vwin_ref, *, seq):
    tp = ks_ref.shape[0]
    pos = (pl.program_id(0) * tp) % seq + lax.broadcasted_iota(jnp.int32, (tp, HEAD_DIM), 0)
    blk = jnp.right_shift(pos, SEL_SHIFT)
    onehot = jnp.where(lax.broadcasted_iota(jnp.int32, (tp, HEAD_DIM), 1) == blk, 1.0, 0.0).astype(BF16)
    zeros = jnp.zeros((tp, HEAD_DIM), BF16)
    ks = _head_rms(ks_ref[...], g_ref[0:1, :]).astype(BF16)
    kw = _head_rms(kw_ref[...], g_ref[1:2, :]).astype(BF16)
    for hh in range(N_KV_HEADS):
        sl = slice(hh * HEAD_DIM, (hh + 1) * HEAD_DIM)
        kaug_ref[hh] = jnp.concatenate([ks[:, sl], onehot], axis=1)
        kwin_ref[hh] = jnp.concatenate([kw[:, sl], zeros], axis=1)
    for v_ref, vt_ref in ((vs_ref, vsel_ref), (vw_ref, vwin_ref)):
        tk = vt_ref.shape[3]
        vt = v_ref[...].T.astype(BF16)
        ones = jnp.where(lax.broadcasted_iota(jnp.int32, (V_ROWS - HEAD_DIM, tp), 0) == 0, 1.0, 0.0).astype(BF16)
        for hh in range(N_KV_HEADS):
            vh = jnp.concatenate([vt[hh * HEAD_DIM:(hh + 1) * HEAD_DIM], ones], axis=0)
            for j in range(tp // tk):
                vt_ref[hh, j] = vh[:, j * tk:(j + 1) * tk]


def _kv_prep(z, k_norm_g, *, seq, tp, tk_sel, tk_win):
    t = z.shape[0]
    col = lambda off: pl.BlockSpec((tp, KV_WIDTH), lambda i, o=off // KV_WIDTH: (i, o))
    aug = 2 * HEAD_DIM
    kspec = pl.BlockSpec((N_KV_HEADS, tp, aug), lambda i: (0, i, 0))
    vspec = lambda tk: pl.BlockSpec((N_KV_HEADS, tp // tk, V_ROWS, tk), lambda i: (0, i, 0, 0))
    vshape = lambda tk: jax.ShapeDtypeStruct((N_KV_HEADS, t // tk, V_ROWS, tk), BF16)
    return pl.pallas_call(
        functools.partial(_kv_prep_kernel, seq=seq),
        grid=(t // tp,),
        in_specs=[col(Z_KS), col(Z_VS), col(Z_KW), col(Z_VW), pl.BlockSpec((2, KV_WIDTH), lambda i: (0, 0))],
        out_specs=[kspec, vspec(tk_sel), kspec, vspec(tk_win)],
        out_shape=[jax.ShapeDtypeStruct((N_KV_HEADS, t, aug), BF16), vshape(tk_sel),
                   jax.ShapeDtypeStruct((N_KV_HEADS, t, aug), BF16), vshape(tk_win)],
        name="kv_prep",
    )(z, z, z, z, jnp.tile(k_norm_g[1:3], (1, N_KV_HEADS)))


def _compress_kernel(x_ref, pe_ref, w1_ref, w2_ref, g_ref, o_ref):
    seq = x_ref.shape[0]
    nb = seq // (2 * CMP_BLOCK)
    lane = lax.broadcasted_iota(jnp.int32, (nb, 2 * HEAD_DIM), 1)
    rows = []
    for hh in range(N_KV_HEADS):
        for parity in range(2):
            cols = []
            for s in range(0, CMP_BLOCK, 2):
                a = x_ref[pl.ds(parity * CMP_BLOCK + s, nb, stride=2 * CMP_BLOCK), :]
                b = x_ref[pl.ds(parity * CMP_BLOCK + s + 1, nb, stride=2 * CMP_BLOCK), :]
                if hh == 0:
                    cols.append(jnp.where(lane < HEAD_DIM, a, pltpu.roll(b, HEAD_DIM, 1)))
                else:
                    cols.append(jnp.where(lane < HEAD_DIM, pltpu.roll(a, HEAD_DIM, 1), b))
            rows.append(jnp.concatenate(cols, axis=1))
    x = (jnp.concatenate(rows, axis=0) + pe_ref[0]).astype(BF16)
    hid = _gelu_tanh(_dot(x, w1_ref[0])).astype(BF16)
    y = _dot(hid, w2_ref[0])
    o_ref[0, 0] = jnp.where(pl.program_id(0) == 0, _rms(y, g_ref[...]), y)


def _compress(z, cmp_pe, w1, w2, k_gain, *, bsz, seq):
    nc = seq // CMP_BLOCK
    kdim = CMP_BLOCK * HEAD_DIM
    m = N_KV_HEADS * nc
    out = pl.pallas_call(
        _compress_kernel,
        grid=(2, bsz),
        in_specs=[pl.BlockSpec((seq, KV_WIDTH), lambda w, b: (b, Z_KC // KV_WIDTH + w)),
                  pl.BlockSpec((1, 1, kdim), lambda w, b: (w, 0, 0)),
                  pl.BlockSpec((1, kdim, CMP_HIDDEN), lambda w, b: (w, 0, 0)),
                  pl.BlockSpec((1, CMP_HIDDEN, HEAD_DIM), lambda w, b: (w, 0, 0)),
                  pl.BlockSpec((1, HEAD_DIM), lambda w, b: (0, 0))],
        out_specs=pl.BlockSpec((1, 1, m, HEAD_DIM), lambda w, b: (w, b, 0, 0)),
        out_shape=jax.ShapeDtypeStruct((2, bsz, m, HEAD_DIM), F32),
        name="compress",
    )(z, cmp_pe.reshape(2, 1, kdim), w1, w2, k_gain.reshape(1, HEAD_DIM))
    return out.reshape(2, bsz * N_KV_HEADS, nc, HEAD_DIM)


def _cmp_select_kernel(q_ref, kc_ref, vc_ref, qg_ref, o_ref, qaug_ref, imp_ref, cnt_ref):
    tq = q_ref.shape[0]
    nc = kc_ref.shape[1]
    nb = nc // 2
    pair_w = 2 * HEAD_DIM
    qw = GQA * HEAD_DIM
    q0 = pl.program_id(1) * tq
    qf = _head_rms(q_ref[...], qg_ref[...]) * (HEAD_DIM ** -0.5)
    qn = qf.astype(BF16)
    q2 = qf * LOG2E

    row = lax.broadcasted_iota(jnp.int32, (nc, tq), 0)
    t = q0 + lax.broadcasted_iota(jnp.int32, (nc, tq), 1)
    blk = jnp.where(row < nb, 2 * row, 2 * (row - nb) + 1)
    valid = (blk + 1) * CMP_BLOCK - 1 <= t
    lane = lax.broadcasted_iota(jnp.int32, (tq, pair_w), 1)
    j = lax.broadcasted_iota(jnp.int32, (nb, tq), 0)
    cur = jnp.right_shift(q0 + lax.broadcasted_iota(jnp.int32, (nb, tq), 1), SEL_SHIFT)
    forced = (j == 0) | (j == cur) | (j == cur - 1)
    visible = j <= cur

    def two_heads(x):
        zx = jnp.zeros_like(x)
        return jnp.concatenate([jnp.concatenate([x, zx], axis=1), jnp.concatenate([zx, x], axis=1)], axis=0)

    pairs = [(kvh, pr) for kvh in range(N_KV_HEADS) for pr in range(GQA // 2)]
    k2 = [two_heads(kc_ref[kvh].astype(BF16)) for kvh in range(N_KV_HEADS)]
    v2 = [two_heads(vc_ref[kvh].astype(BF16)) for kvh in range(N_KV_HEADS)]
    st = [_dot_nt(k2[kvh], qn[:, kvh * qw + pr * pair_w:kvh * qw + (pr + 1) * pair_w]) for kvh, pr in pairs]
    psum = [jnp.zeros((nc, tq), F32) for _ in range(N_KV_HEADS)]
    probs = []
    for (kvh, pr), s2 in zip(pairs, st):
        ph = []
        for hh in range(2):
            sm = jnp.where(valid, s2[hh * nc:(hh + 1) * nc], MASKED)
            m = jnp.max(sm, axis=0, keepdims=True)
            m = jnp.where(m > 0.5 * MASKED, m, 0.0)
            e = jnp.exp(sm - m)
            p = e * (1.0 / jnp.maximum(jnp.sum(e, axis=0, keepdims=True), 1e-30))
            psum[kvh] = psum[kvh] + p
            ph.append(p.astype(BF16))
        probs.append(jnp.concatenate(ph, axis=0))
    imps = []
    for kvh in range(N_KV_HEADS):
        imp = psum[kvh][:nb] + psum[kvh][nb:]
        imp = jnp.where(forced, FORCE_SCORE, jnp.where(visible, imp, -jnp.inf))
        imp_ref[kvh] = imp
        imps.append(imp)
    for n, ((kvh, pr), p2) in enumerate(zip(pairs, probs)):
        o_ref[n * pair_w:(n + 1) * pair_w, :] = lax.dot_general(
            v2[kvh], p2, (((0,), (0,)), ((), ())), preferred_element_type=F32).astype(o_ref.dtype)

    n_vis = jnp.minimum(nb, (q0 + tq - 1) // SEL_BLOCK + 1)
    n_slabs = nb // SUBLANES
    row_in_slab = lax.broadcasted_iota(jnp.int32, (SUBLANES, tq), 0)
    cnt_ref[...] = jnp.zeros(cnt_ref.shape, F32)
    for grp in range(n_slabs):
        @pl.when(grp * SUBLANES < n_vis)
        def _(grp=grp):
            for kvh in range(N_KV_HEADS):
                slabs = [imp_ref[kvh, s * SUBLANES:(s + 1) * SUBLANES, :] for s in range(n_slabs)]
                cnt = [jnp.zeros((SUBLANES, tq), F32) for _ in range(n_slabs)]
                for i in range(grp * SUBLANES, (grp + 1) * SUBLANES):
                    vi = imp_ref[kvh, i:i + 1, :]
                    for s in range(n_slabs):
                        if s > grp:
                            beats = vi >= slabs[s]
                        elif s < grp:
                            beats = vi > slabs[s]
                        else:
                            beats = (vi > slabs[s]) | ((vi == slabs[s]) & (i - s * SUBLANES < row_in_slab))
                        cnt[s] = cnt[s] + jnp.where(beats, 1.0, 0.0)
                cnt_ref[kvh] += jnp.concatenate(cnt, axis=0)
    for kvh in range(N_KV_HEADS):
        selneg_t = jnp.where(visible & (cnt_ref[kvh] < float(N_SELECT)), 0.0, MASKED)
        if nb < HEAD_DIM:
            selneg_t = jnp.concatenate([selneg_t, jnp.zeros((HEAD_DIM - nb, tq), F32)], axis=0)
        selneg = jnp.concatenate([selneg_t, selneg_t], axis=0).T
        for pr in range(GQA // 2):
            c0 = kvh * qw + pr * pair_w
            qp = q2[:, c0:c0 + pair_w]
            qaug_ref[kvh, 2 * pr] = jnp.where(lane < HEAD_DIM, qp, selneg).astype(BF16)
            qaug_ref[kvh, 2 * pr + 1] = jnp.where(lane < HEAD_DIM, pltpu.roll(qp, HEAD_DIM, 1), selneg).astype(BF16)


def _cmp_select(z, cmp_kv, q_norm_g, *, bsz, seq, tq):
    t = z.shape[0]
    nq = seq // tq
    nc = seq // CMP_BLOCK
    aug = 2 * HEAD_DIM
    qw = GQA * HEAD_DIM
    kc, vc = cmp_kv[0], cmp_kv[1]
    kv_spec = pl.BlockSpec((N_KV_HEADS, nc, HEAD_DIM), lambda b, i: (b, 0, 0))
    return pl.pallas_call(
        _cmp_select_kernel,
        grid=(bsz, nq),
        in_specs=[pl.BlockSpec((tq, ATTN_WIDTH), lambda b, i: (b * nq + i, Z_Q // ATTN_WIDTH)),
                  kv_spec, kv_spec,
                  pl.BlockSpec((1, ATTN_WIDTH), lambda b, i: (0, 0))],
        out_specs=[pl.BlockSpec((ATTN_WIDTH, tq), lambda b, i: (0, b * nq + i)),
                   pl.BlockSpec((N_KV_HEADS, GQA, tq, aug), lambda b, i: (b, 0, i, 0))],
        out_shape=[jax.ShapeDtypeStruct((ATTN_WIDTH, t), BF16),
                   jax.ShapeDtypeStruct((bsz * N_KV_HEADS, GQA, seq, aug), BF16)],
        scratch_shapes=[pltpu.VMEM((N_KV_HEADS, nc // 2, tq), F32), pltpu.VMEM((N_KV_HEADS, nc // 2, tq), F32)],
        name="cmp_select",
    )(z, kc, vc, jnp.tile(q_norm_g, N_HEADS).reshape(1, ATTN_WIDTH))


def _flash_kernel(q_ref, k_ref, vt_ref, o_ref, sa_ref, sb_ref, pa_ref, pb_ref, m_ref, a_ref, acc_ref, *, window):
    tq = q_ref.shape[2]
    tk = vt_ref.shape[3]
    cols = GQA * tq
    q0 = pl.program_id(2) * tq
    qa = q_ref[0].reshape(cols, q_ref.shape[3])
    t = q0 + (lax.broadcasted_iota(jnp.int32, (tk, cols), 1) & (tq - 1))
    key_row = lax.broadcasted_iota(jnp.int32, (tk, cols), 0)
    span = 2 * tk
    lo = jnp.maximum(0, q0 - window + 1) // span if window else 0
    hi = (q0 + tq + span - 1) // span
    mid1 = jnp.maximum(lo, (q0 + tq - 1 - window) // span + 1) if window else lo
    mid2 = q0 // span
    first = 2 * lo
    last = 2 * hi - 1

    def scores(kb):
        return _dot_nt(k_ref[0, pl.ds(pl.multiple_of(kb * tk, tk), tk), :], qa)

    m_ref[...] = jnp.full(m_ref.shape, MASKED, F32)
    a_ref[...] = jnp.ones(a_ref.shape, F32)
    acc_ref[...] = jnp.zeros(acc_ref.shape, F32)
    pb_ref[...] = jnp.zeros(pb_ref.shape, BF16)
    sa_ref[...] = scores(first)

    def tile(kb, s_cur, s_nxt, p_prev, p_cur, mask):
        pv = _dot(vt_ref[0, jnp.maximum(kb - 1, first)], p_prev[...])
        s_nxt[...] = scores(jnp.minimum(kb + 1, last))
        s = s_cur[...]
        if mask == 'causal':
            s = jnp.where(kb * tk + key_row <= t, s, MASKED)
        elif mask == 'window':
            s = jnp.where(kb * tk + key_row > t - window, s, MASKED)
        m_old = m_ref[...]
        m_new = jnp.maximum(m_old, jnp.max(s, axis=0, keepdims=True))
        acc_ref[...] = a_ref[...] * acc_ref[...] + pv
        a_ref[...] = jnp.exp2(m_old - m_new)
        p_cur[...] = jnp.exp2(s - m_new).astype(BF16)
        m_ref[...] = m_new

    def pair_step(pair, mask):
        tile(2 * pair, sa_ref, sb_ref, pb_ref, pa_ref, mask)
        tile(2 * pair + 1, sb_ref, sa_ref, pa_ref, pb_ref, mask)

    def run(begin, end, mask, unroll=1):
        n_main = (end - begin) // unroll

        def main(i, carry):
            for u in range(unroll):
                pair_step(begin + i * unroll + u, mask)
            return carry

        def rest(pair, carry):
            pair_step(pair, mask)
            return carry

        if unroll > 1:
            lax.fori_loop(0, n_main, main, 0)
        lax.fori_loop(begin + n_main * unroll if unroll > 1 else begin, end, rest, 0)

    if window:
        interior = (mid1 - lo == 1) & (mid2 - mid1 == 1) & (hi - mid2 == 1)

        @pl.when(interior)
        def _():
            pair_step(lo, 'window')
            pair_step(lo + 1, None)
            pair_step(lo + 2, 'causal')

        @pl.when(jnp.logical_not(interior))
        def _():
            run(lo, mid1, 'window')
            run(mid1, mid2, None)
            run(mid2, hi, 'causal')
    else:
        run(mid1, mid2, None, unroll=FLASH_UNROLL)
        run(mid2, hi, 'causal')
    acc = a_ref[...] * acc_ref[...] + _dot(vt_ref[0, last], pb_ref[...])
    o_t = acc[:HEAD_DIM] / acc[HEAD_DIM:HEAD_DIM + 1]
    for g in range(GQA):
        o_ref[g * HEAD_DIM:(g + 1) * HEAD_DIM, :] = o_t[:, g * tq:(g + 1) * tq].astype(o_ref.dtype)


def _flash(qaug, k, vt, *, bsz, seq, tq, window, name):
    t = bsz * seq
    nq = seq // tq
    aug = qaug.shape[3]
    tk = vt.shape[3]
    qw = GQA * HEAD_DIM
    cols = GQA * tq
    return pl.pallas_call(
        functools.partial(_flash_kernel, window=window),
        grid=(bsz, N_KV_HEADS, nq),
        in_specs=[pl.BlockSpec((1, GQA, tq, aug), lambda b, h, i: (b * N_KV_HEADS + h, 0, i, 0)),
                  pl.BlockSpec((1, seq, aug), lambda b, h, i: (h, b, 0)),
                  pl.BlockSpec((1, seq // tk, V_ROWS, tk), lambda b, h, i: (h, b, 0, 0))],
        out_specs=pl.BlockSpec((qw, tq), lambda b, h, i: (h, b * nq + i)),
        out_shape=jax.ShapeDtypeStruct((ATTN_WIDTH, t), BF16),
        scratch_shapes=[pltpu.VMEM((tk, cols), F32), pltpu.VMEM((tk, cols), F32),
                        pltpu.VMEM((tk, cols), BF16), pltpu.VMEM((tk, cols), BF16),
                        pltpu.VMEM((1, cols), F32), pltpu.VMEM((1, cols), F32),
                        pltpu.VMEM((V_ROWS, cols), F32)],
        name=name,
    )(qaug, k, vt)


def _merge_kernel(x_ref, g0_ref, g1_ref, g2_ref, cb_ref, cc_ref, cx_ref, pc_ref, px_ref, gl_ref, ys_ref,
                  oc_ref, os_ref, ow_ref, cw_ref, wc_ref, wglu_ref, wo_ref, wout_ref, out_ref, *, seq):
    tm = x_ref.shape[0]
    yg = _dot(_gelu_tanh(ys_ref[...]).astype(BF16), wglu_ref[...])
    y_ssm = yg[:, :D_MODEL] * _sigmoid(yg[:, D_MODEL:])
    zc = cc_ref[...] * cx_ref[...]
    keep = jnp.where((pl.program_id(0) * tm) % seq != 0, 1.0, 0.0)
    prev = pc_ref[...] * px_ref[...] * keep
    row = lax.broadcasted_iota(jnp.int32, zc.shape, 0)
    z1 = jnp.where(row >= 1, pltpu.roll(zc, 1, 0), prev[7:8, :])
    z2 = jnp.where(row >= 2, pltpu.roll(zc, 2, 0), jnp.where(row == 1, prev[7:8, :], prev[6:7, :]))
    conv = cw_ref[0:1, :] * z2 + cw_ref[1:2, :] * z1 + cw_ref[2:3, :] * zc
    y_conv = _dot((cb_ref[...] * conv).astype(BF16), wc_ref[...])
    gates_t = _sigmoid(gl_ref[...]).T
    heads = []
    for hd in range(N_HEADS):
        rows = slice(hd * HEAD_DIM, (hd + 1) * HEAD_DIM)
        c = hd * NSA_BRANCHES
        heads.append(oc_ref[rows, :] * gates_t[c:c + 1] + os_ref[rows, :] * gates_t[c + 1:c + 2]
                     + ow_ref[rows, :] * gates_t[c + 2:c + 3])
    o_t = jnp.concatenate(heads, axis=0).astype(BF16)
    y_attn = lax.dot_general(o_t, wo_ref[...], (((0,), (0,)), ((), ())), preferred_element_type=F32)
    mixed =_sigmoid(g0_ref[...]) * y_ssm + _sigmoid(g1_ref[...]) * y_conv + _sigmoid(g2_ref[...]) * y_attn
    out_ref[...] = x_ref[...] + _dot(mixed.astype(BF16), wout_ref[...])


def _merge(x2, z, ys, o_cmp, o_sel, o_win, conv_w, wc, wglu, wo, wout, *, seq, tm):
    t, d = x2.shape
    rb = tm // 8
    zc = lambda width, off: pl.BlockSpec((tm, width), lambda i, o=off // width: (i, o))
    zprev = lambda off: pl.BlockSpec((8, CONV_CH), lambda i, o=off // CONV_CH: (jnp.maximum(i * rb - 1, 0), o))
    row = lambda width: pl.BlockSpec((tm, width), lambda i: (i, 0))
    full = lambda a: pl.BlockSpec(a.shape, lambda i: (0, 0))
    branch_t = pl.BlockSpec((ATTN_WIDTH, tm), lambda i: (0, i))
    return pl.pallas_call(
        functools.partial(_merge_kernel, seq=seq),
        grid=(t // tm,),
        in_specs=[row(d), zc(d, Z_MIX), zc(d, Z_MIX + d), zc(d, Z_MIX + 2 * d),
                  zc(CONV_CH, Z_CB), zc(CONV_CH, Z_CC), zc(CONV_CH, Z_CX), zprev(Z_CC), zprev(Z_CX),
                  zc(GATE_PAD, Z_GATE), row(SSM_WIDTH), branch_t, branch_t, branch_t,
                  full(conv_w), full(wc), full(wglu), full(wo), full(wout)],
        out_specs=row(d),
        out_shape=jax.ShapeDtypeStruct((t, d), F32),
        name="merge",
    )(x2, z, z, z, z, z, z, z, z, z, ys, o_cmp, o_sel, o_win, conv_w, wc, wglu, wo, wout)


def _ffn_kernel(x_ref, g_ref, wg_ref, wu_ref, wd_ref, o_ref, h_ref, acc_ref):
    f = pl.program_id(1)

    @pl.when(f == 0)
    def _():
        h_ref[...] = _rms(x_ref[...], g_ref[...]).astype(BF16)
        acc_ref[...] = jnp.zeros(acc_ref.shape, F32)

    h = h_ref[...]
    gate = _dot(h, wg_ref[...])
    up = _dot(h, wu_ref[...])
    act = (gate * _sigmoid(gate) * up).astype(BF16)
    acc_ref[...] += _dot(act, wd_ref[...])

    @pl.when(f == pl.num_programs(1) - 1)
    def _():
        o_ref[...] = x_ref[...] + acc_ref[...]


def _ffn(x2, g, w_gate_up, w_down, *, tm, tf):
    t, d = x2.shape
    nf = D_FF // tf
    return pl.pallas_call(
        _ffn_kernel,
        grid=(t // tm, nf),
        in_specs=[pl.BlockSpec((tm, d), lambda i, f: (i, 0)),
                  pl.BlockSpec((1, d), lambda i, f: (0, 0)),
                  pl.BlockSpec((d, tf), lambda i, f: (0, f)),
                  pl.BlockSpec((d, tf), lambda i, f: (0, nf + f)),
                  pl.BlockSpec((tf, d), lambda i, f: (f, 0))],
        out_specs=pl.BlockSpec((tm, d), lambda i, f: (i, 0)),
        out_shape=jax.ShapeDtypeStruct((t, d), F32),
        scratch_shapes=[pltpu.VMEM((tm, d), BF16), pltpu.VMEM((tm, d), F32)],
        compiler_params=pltpu.CompilerParams(dimension_semantics=("parallel", "arbitrary")),
        name="ffn",
    )(x2, g.reshape(1, d), w_gate_up, w_gate_up, w_down)


def _pick(n, pref):
    while n % pref:
        pref //= 2
    return pref


def _tiles(t, seq):
    return dict(
        inproj_tm=_pick(t, 512), inproj_tn=Z_WIDTH // 2,
        kv_tp=_pick(seq, 512),
        key_tile=LANES,
        cmp_tq=_pick(seq, 256), attn_tq=_pick(seq, 256),
        merge_tm=_pick(seq, 256),
        ffn_tm=_pick(t, 1024), ffn_tf=MXU_WIDTH)


def _layer(x2, p, *, bsz, seq):
    ts = _tiles(x2.shape[0], seq)
    z = _inproj(x2, p['mix_norm_g'], p['w_in'], tm=ts['inproj_tm'], tn=ts['inproj_tn'])
    ys = _s5_scan(z, p['s5'], bsz=bsz, seq=seq)
    kaug, vsel, kwin, vwin = _kv_prep(z, p['k_norm_g'], seq=seq, tp=ts['kv_tp'], tk_sel=ts['key_tile'],
                                      tk_win=ts['key_tile'])
    cmp_kv = _compress(z, p['cmp_pe'], p['cmp_w1'], p['cmp_w2'], p['k_norm_g'][0], bsz=bsz, seq=seq)
    o_cmp, qaug = _cmp_select(z, cmp_kv, p['q_norm_g'], bsz=bsz, seq=seq, tq=ts['cmp_tq'])
    o_sel = _flash(qaug, kaug, vsel, bsz=bsz, seq=seq, tq=ts['attn_tq'], window=0, name="sel_attn")
    o_win = _flash(qaug, kwin, vwin, bsz=bsz, seq=seq, tq=ts['attn_tq'], window=WINDOW, name="win_attn")
    x2 = _merge(x2, z, ys, o_cmp, o_sel, o_win, p['conv_w'], p['conv_w_out'], p['ssm_w_glu'], p['nsa_w_o'],
                p['w_out'], seq=seq, tm=ts['merge_tm'])
    return _ffn(x2, p['ffn_norm_g'], p['ffn_w_gate_up'], p['ffn_w_down'], tm=ts['ffn_tm'], tf=ts['ffn_tf'])


def kernel(x, mix_norm_g, w_in, ssm_lam_re, ssm_lam_im, ssm_b_re, ssm_b_im, ssm_c_re, ssm_c_im, ssm_d, ssm_log_dt, ssm_w_glu, conv_w, conv_w_out, q_norm_g, k_norm_g, cmp_pe, cmp_w1, cmp_w2, nsa_w_o, w_out, ffn_norm_g, ffn_w_gate_up, ffn_w_down):
    bsz, seq, d = x.shape
    assert d == D_MODEL, d
    assert seq % (SUBLANES * SEL_BLOCK) == 0 and seq // SEL_BLOCK <= HEAD_DIM, seq
    assert seq % WINDOW == 0 and seq % (S5_CHUNK * SUBLANES) == 0, seq
    x2 = x.reshape(bsz * seq, d)
    for i in range(w_in.shape[0]):
        p = dict(
            mix_norm_g=mix_norm_g[i], w_in=_permute_w_in(w_in[i]),
            s5=_s5_tables(ssm_lam_re[i], ssm_lam_im[i], ssm_b_re[i], ssm_b_im[i], ssm_c_re[i], ssm_c_im[i],
                          ssm_d[i], ssm_log_dt[i], seq // S5_CHUNK),
            ssm_w_glu=ssm_w_glu[i].astype(BF16), conv_w=conv_w[i], conv_w_out=conv_w_out[i].astype(BF16),
            q_norm_g=q_norm_g[i], k_norm_g=k_norm_g[i], cmp_pe=cmp_pe[i],
            cmp_w1=cmp_w1[i].astype(BF16), cmp_w2=cmp_w2[i].astype(BF16), nsa_w_o=nsa_w_o[i].astype(BF16),
            w_out=w_out[i].astype(BF16), ffn_norm_g=ffn_norm_g[i],
            ffn_w_gate_up=ffn_w_gate_up[i].astype(BF16), ffn_w_down=ffn_w_down[i].astype(BF16))
        x2 = _layer(x2, p, bsz=bsz, seq=seq)
    return x2.reshape(bsz, seq, d)
```

```python
import functools

import jax
import jax.numpy as jnp
import numpy as np
from jax import lax
from jax.experimental import pallas as pl
from jax.experimental.pallas import tpu as pltpu

F32 = jnp.float32
BF16 = jnp.bfloat16

D_MODEL = 1024
SSM_WIDTH = 512
SSM_GROUP = 16
SSM_GROUPS = SSM_WIDTH // SSM_GROUP
SSM_STATE = 64
CONV_CH = 512
CONV_K = 3
HEAD_DIM = 64
HEAD_SHIFT = HEAD_DIM.bit_length() - 1
N_HEADS = 8
N_KV_HEADS = 2
GQA = N_HEADS // N_KV_HEADS
ATTN_WIDTH = N_HEADS * HEAD_DIM
KV_WIDTH = N_KV_HEADS * HEAD_DIM
CMP_BLOCK = 32
SEL_BLOCK = 64
SEL_SHIFT = SEL_BLOCK.bit_length() - 1
N_SELECT = 16
WINDOW = 512
CMP_HIDDEN = 256
FORCE_SCORE = 1e4
NSA_BRANCHES = 3
MIX_BRANCHES = 3
D_FF = 2816
RMS_EPS = 1e-6

LANES = 128
SUBLANES = 8
MXU_WIDTH = 256
MASKED = -1e30
LOG2E = 1.4426950408889634
FLASH_UNROLL = 4
S5_CHUNK = 16
S5_TILE_GROUPS = LANES // SSM_GROUP
GATE_PAD = MXU_WIDTH
BF16_SUBLANES = 16
V_ROWS = HEAD_DIM + BF16_SUBLANES

Z_MIX = 0
Z_CB = Z_MIX + MIX_BRANCHES * D_MODEL
Z_CC = Z_CB + CONV_CH
Z_CX = Z_CC + CONV_CH
Z_U = Z_CX + CONV_CH
Z_Q = Z_U + SSM_WIDTH
Z_KC = Z_Q + ATTN_WIDTH
Z_VC = Z_KC + KV_WIDTH
Z_KS = Z_VC + KV_WIDTH
Z_VS = Z_KS + KV_WIDTH
Z_KW = Z_VS + KV_WIDTH
Z_VW = Z_KW + KV_WIDTH
Z_GATE = Z_VW + KV_WIDTH
Z_WIDTH = Z_GATE + GATE_PAD


def _gelu_tanh(x):
    return 0.5 * x * (1.0 + jnp.tanh(np.sqrt(2.0 / np.pi).astype(np.float32) * (x + 0.044715 * (x * x * x))))


def _sigmoid(x):
    return 0.5 * jnp.tanh(0.5 * x) + 0.5


def _rms(x, g):
    return x * lax.rsqrt(jnp.mean(x * x, axis=-1, keepdims=True) + RMS_EPS) * g


def _dot(a, b):
    return jnp.dot(a, b, preferred_element_type=F32)


def _head_rms(x, g):
    x2 = x * x
    hi = x2.astype(BF16)
    lo = (x2 - hi.astype(F32)).astype(BF16)
    w = x.shape[1]
    seg = jnp.where(jnp.right_shift(lax.broadcasted_iota(jnp.int32, (w, w), 0), HEAD_SHIFT)
                    == jnp.right_shift(lax.broadcasted_iota(jnp.int32, (w, w), 1), HEAD_SHIFT), 1.0, 0.0).astype(BF16)
    ss = _dot(hi, seg) + _dot(lo, seg)
    return x * lax.rsqrt(ss * (1.0 / HEAD_DIM) + RMS_EPS) * g


def _dot_nt(a, b):
    return lax.dot_general(a, b, (((1,), (1,)), ((), ())), preferred_element_type=F32)


def _inproj_kernel(x_ref, g_ref, w_ref, o_ref):
    h = _rms(x_ref[...], g_ref[...]).astype(BF16)
    o_ref[...] = _dot(h, w_ref[...])


def _inproj(x2, g, w, *, tm, tn):
    t, d = x2.shape
    n = w.shape[1]
    return pl.pallas_call(
        _inproj_kernel,
        grid=(n // tn, t // tm),
        in_specs=[pl.BlockSpec((tm, d), lambda j, i: (i, 0)),
                  pl.BlockSpec((1, d), lambda j, i: (0, 0)),
                  pl.BlockSpec((d, tn), lambda j, i: (0, j))],
        out_specs=pl.BlockSpec((tm, tn), lambda j, i: (i, j)),
        out_shape=jax.ShapeDtypeStruct((t, n), F32),
        name="inproj",
    )(x2, g.reshape(1, d), w)


def _permute_w_in(w):
    n_plain = SSM_WIDTH + 3 * CONV_CH + ATTN_WIDTH + 6 * KV_WIDTH
    n_gate = N_HEADS * NSA_BRANCHES
    gate = jnp.pad(w[:, n_plain:n_plain + n_gate], ((0, 0), (0, GATE_PAD - n_gate)))
    n_conv = SSM_WIDTH + 3 * CONV_CH
    return jnp.concatenate([w[:, n_plain + n_gate:], w[:, SSM_WIDTH:n_conv], w[:, :SSM_WIDTH], w[:, n_conv:n_plain],
                            gate], axis=1).astype(BF16)


def _s5_tables(lam_re, lam_im, b_re, b_im, c_re, c_im, d_skip, log_dt, n_chunks):
    hp = lax.Precision.HIGHEST
    g, p = lam_re.shape
    h, l, gt = SSM_GROUP, S5_CHUNK, S5_TILE_GROUPS
    nt = g // gt
    dt = jnp.exp(log_dt)[:, None]
    ar, ai = lam_re * dt, lam_im * dt

    def powers(k):
        mag = jnp.exp(ar[None] * k[:, None, None])
        ang = ai[None] * k[:, None, None]
        return mag * jnp.cos(ang), mag * jnp.sin(ang)

    pr, pi = powers(jnp.arange(l + 1, dtype=F32))
    nr, ni = pr[1] - 1.0, pi[1]
    den = lam_re * lam_re + lam_im * lam_im
    fr, fi = (nr * lam_re + ni * lam_im) / den, (ni * lam_re - nr * lam_im) / den
    bbr = fr[..., None] * b_re - fi[..., None] * b_im
    bbi = fr[..., None] * b_im + fi[..., None] * b_re
    wr = pr[:l, :, :, None] * bbr[None] - pi[:l, :, :, None] * bbi[None]
    wi = pr[:l, :, :, None] * bbi[None] + pi[:l, :, :, None] * bbr[None]
    kern = (jnp.einsum('gop,tgpi->tgio', c_re, wr, precision=hp)
            - jnp.einsum('gop,tgpi->tgio', c_im, wi, precision=hp))
    r = jnp.arange(l)

    def group_diag(x, row_w, col_w, halves=1):
        rows, hw = x.shape[-2:]
        w = hw // halves
        col = jnp.arange(halves * gt * w)
        src = (col // (gt * w)) * w + col % w
        expand = (jnp.arange(hw)[:, None] == src[None, :]).astype(BF16)
        tiled = jnp.einsum('...w,wv->...v', x.astype(BF16), expand, preferred_element_type=BF16)
        own = (jnp.arange(rows)[:, None] // row_w) % gt == (col[None, :] % (gt * w)) // col_w
        return jnp.where(own, tiled, jnp.zeros((), BF16))

    kblk = group_diag(kern.reshape(l, g * h, h), h, h)
    krev = kblk.reshape(l, nt, LANES, LANES)[::-1].transpose(1, 0, 2, 3).reshape(nt, l * LANES, LANES)
    krev = jnp.concatenate([krev, jnp.zeros((nt, LANES, LANES), krev.dtype)], axis=1)
    t_pair = jnp.concatenate(
        [jnp.concatenate([krev[:, (l - 1 - s) * LANES:(l + 1) * LANES], krev[:, (l - 2 - s) * LANES:l * LANES]], axis=2)
         for s in range(0, l, 2)], axis=1)
    qr, qi = pr[l - 1 - r], pi[l - 1 - r]
    st_re = qr[..., None] * bbr[None] - qi[..., None] * bbi[None]
    st_im = qr[..., None] * bbi[None] + qi[..., None] * bbr[None]
    rows_in = lambda x: group_diag(x.transpose(0, 1, 3, 2).reshape(l, g * h, p), h, p).reshape(l, nt, LANES, gt * p)
    er, ei = pr[1:l + 1][:, :, None, :], pi[1:l + 1][:, :, None, :]

    def rows_out(x):
        x = x.transpose(0, 1, 3, 2).reshape(l // 2, 2, g * p, h).transpose(0, 2, 1, 3).reshape(l // 2, g * p, 2 * h)
        return group_diag(x, p, h, halves=2).reshape(l // 2, nt, gt * p, 2 * LANES)

    n_steps = max(1, int(np.ceil(np.log2(n_chunks))))
    dr, di = powers(l * (2.0 ** jnp.arange(n_steps, dtype=F32)))
    lanes = lambda m: m.reshape(n_steps, nt, gt * p).transpose(1, 0, 2)
    return dict(t_pair=t_pair, s_re=rows_in(st_re), s_im=rows_in(st_im),
                c_re=rows_out(c_re[None] * er - c_im[None] * ei),
                c_im=rows_out(-(c_re[None] * ei + c_im[None] * er)),
                d_re=lanes(dr), d_im=lanes(di), d_skip=d_skip.reshape(1, g * h))


def _s5_kernel(u_ref, tp_ref, sre_ref, sim_ref, cre_ref, cim_ref, dre_ref, dim_ref, dsk_ref, y_ref):
    seq = u_ref.shape[0]
    l = S5_CHUNK
    nc = seq // l
    n_steps = dre_ref.shape[1]
    ns = dre_ref.shape[2]
    us = [u_ref[pl.ds(r, nc, stride=l), :] for r in range(l)]
    ucat = jnp.concatenate(us, axis=1).astype(BF16)
    xr = _dot(ucat, sre_ref[:, 0].reshape(l * LANES, ns))
    xi = _dot(ucat, sim_ref[:, 0].reshape(l * LANES, ns))
    row = lax.broadcasted_iota(jnp.int32, xr.shape, 0)
    for k in range(n_steps):
        s = 1 << k
        if s >= nc:
            break
        dr = dre_ref[0, k:k + 1, :]
        di = dim_ref[0, k:k + 1, :]
        sr = jnp.where(row >= s, pltpu.roll(xr, s, 0), 0.0)
        si = jnp.where(row >= s, pltpu.roll(xi, s, 0), 0.0)
        xr, xi = xr + (dr * sr - di * si), xi + (dr * si + di * sr)
    pr = jnp.where(row >= 1, pltpu.roll(xr, 1, 0), 0.0).astype(BF16)
    pi = jnp.where(row >= 1, pltpu.roll(xi, 1, 0), 0.0).astype(BF16)
    dsk = dsk_ref[...]
    off = 0
    for s in range(0, l, 2):
        rows = (s + 2) * LANES
        y2 = (_dot(ucat[:, :rows], tp_ref[0, off:off + rows, :])
              + _dot(pr, cre_ref[s // 2, 0]) + _dot(pi, cim_ref[s // 2, 0]))
        y_ref[pl.ds(s, nc, stride=l), :] = y2[:, :LANES] + dsk * us[s]
        y_ref[pl.ds(s + 1, nc, stride=l), :] = y2[:, LANES:] + dsk * us[s + 1]
        off += rows


def _s5_scan(z, tabs, *, bsz, seq):
    t = z.shape[0]
    nt = SSM_WIDTH // LANES
    tile3 = lambda a: pl.BlockSpec((1,) + a.shape[1:], lambda j, b: (j, 0, 0))
    step4 = lambda a: pl.BlockSpec((a.shape[0], 1) + a.shape[2:], lambda j, b: (0, j, 0, 0))
    return pl.pallas_call(
        _s5_kernel,
        grid=(nt, bsz),
        in_specs=[pl.BlockSpec((seq, LANES), lambda j, b: (b, Z_U // LANES + j)),
                  tile3(tabs['t_pair']), step4(tabs['s_re']), step4(tabs['s_im']),
                  step4(tabs['c_re']), step4(tabs['c_im']), tile3(tabs['d_re']), tile3(tabs['d_im']),
                  pl.BlockSpec((1, LANES), lambda j, b: (0, j))],
        out_specs=pl.BlockSpec((seq, LANES), lambda j, b: (b, j)),
        out_shape=jax.ShapeDtypeStruct((t, SSM_WIDTH), F32),
        name="s5_scan",
    )(z, tabs['t_pair'], tabs['s_re'], tabs['s_im'], tabs['c_re'], tabs['c_im'], tabs['d_re'], tabs['d_im'],
      tabs['d_skip'])


def _kv_prep_kernel(ks_ref, vs_ref, kw_ref, vw_ref, g_ref, kaug_ref, vsel_ref, kwin_ref, vwin_ref, *, seq):
    tp = ks_ref.shape[0]
    pos = (pl.program_id(0) * tp) % seq + lax.broadcasted_iota(jnp.int32, (tp, HEAD_DIM), 0)
    blk = jnp.right_shift(pos, SEL_SHIFT)
    onehot = jnp.where(lax.broadcasted_iota(jnp.int32, (tp, HEAD_DIM), 1) == blk, 1.0, 0.0).astype(BF16)
    zeros = jnp.zeros((tp, HEAD_DIM), BF16)
    ks = _head_rms(ks_ref[...], g_ref[0:1, :]).astype(BF16)
    kw = _head_rms(kw_ref[...], g_ref[1:2, :]).astype(BF16)
    for hh in range(N_KV_HEADS):
        sl = slice(hh * HEAD_DIM, (hh + 1) * HEAD_DIM)
        kaug_ref[hh] = jnp.concatenate([ks[:, sl], onehot], axis=1)
        kwin_ref[hh] = jnp.concatenate([kw[:, sl], zeros], axis=1)
    for v_ref, vt_ref in ((vs_ref, vsel_ref), (vw_ref, vwin_ref)):
        tk = vt_ref.shape[3]
        vt = v_ref[...].T.astype(BF16)
        ones = jnp.where(lax.broadcasted_iota(jnp.int32, (V_ROWS - HEAD_DIM, tp), 0) == 0, 1.0, 0.0).astype(BF16)
        for hh in range(N_KV_HEADS):
            vh = jnp.concatenate([vt[hh * HEAD_DIM:(hh + 1) * HEAD_DIM], ones], axis=0)
            for j in range(tp // tk):
                vt_ref[hh, j] = vh[:, j * tk:(j + 1) * tk]


def _kv_prep(z, k_norm_g, *, seq, tp, tk_sel, tk_win):
    t = z.shape[0]
    col = lambda off: pl.BlockSpec((tp, KV_WIDTH), lambda i, o=off // KV_WIDTH: (i, o))
    aug = 2 * HEAD_DIM
    kspec = pl.BlockSpec((N_KV_HEADS, tp, aug), lambda i: (0, i, 0))
    vspec = lambda tk: pl.BlockSpec((N_KV_HEADS, tp // tk, V_ROWS, tk), lambda i: (0, i, 0, 0))
    vshape = lambda tk: jax.ShapeDtypeStruct((N_KV_HEADS, t // tk, V_ROWS, tk), BF16)
    return pl.pallas_call(
        functools.partial(_kv_prep_kernel, seq=seq),
        grid=(t // tp,),
        in_specs=[col(Z_KS), col(Z_VS), col(Z_KW), col(Z_VW), pl.BlockSpec((2, KV_WIDTH), lambda i: (0, 0))],
        out_specs=[kspec, vspec(tk_sel), kspec, vspec(tk_win)],
        out_shape=[jax.ShapeDtypeStruct((N_KV_HEADS, t, aug), BF16), vshape(tk_sel),
                   jax.ShapeDtypeStruct((N_KV_HEADS, t, aug), BF16), vshape(tk_win)],
        name="kv_prep",
    )(z, z, z, z, jnp.tile(k_norm_g[1:3], (1, N_KV_HEADS)))


def _compress_kernel(x_ref, pe_ref, w1_ref, w2_ref, g_ref, o_ref):
    seq = x_ref.shape[0]
    nb = seq // (2 * CMP_BLOCK)
    lane = lax.broadcasted_iota(jnp.int32, (nb, 2 * HEAD_DIM), 1)
    rows = []
    for hh in range(N_KV_HEADS):
        for parity in range(2):
            cols = []
            for s in range(0, CMP_BLOCK, 2):
                a = x_ref[pl.ds(parity * CMP_BLOCK + s, nb, stride=2 * CMP_BLOCK), :]
                b = x_ref[pl.ds(parity * CMP_BLOCK + s + 1, nb, stride=2 * CMP_BLOCK), :]
                if hh == 0:
                    cols.append(jnp.where(lane < HEAD_DIM, a, pltpu.roll(b, HEAD_DIM, 1)))
                else:
                    cols.append(jnp.where(lane < HEAD_DIM, pltpu.roll(a, HEAD_DIM, 1), b))
            rows.append(jnp.concatenate(cols, axis=1))
    x = (jnp.concatenate(rows, axis=0) + pe_ref[0]).astype(BF16)
    hid = _gelu_tanh(_dot(x, w1_ref[0])).astype(BF16)
    y = _dot(hid, w2_ref[0])
    o_ref[0, 0] = jnp.where(pl.program_id(0) == 0, _rms(y, g_ref[...]), y)


def _compress(z, cmp_pe, w1, w2, k_gain, *, bsz, seq):
    nc = seq // CMP_BLOCK
    kdim = CMP_BLOCK * HEAD_DIM
    m = N_KV_HEADS * nc
    out = pl.pallas_call(
        _compress_kernel,
        grid=(2, bsz),
        in_specs=[pl.BlockSpec((seq, KV_WIDTH), lambda w, b: (b, Z_KC // KV_WIDTH + w)),
                  pl.BlockSpec((1, 1, kdim), lambda w, b: (w, 0, 0)),
                  pl.BlockSpec((1, kdim, CMP_HIDDEN), lambda w, b: (w, 0, 0)),
                  pl.BlockSpec((1, CMP_HIDDEN, HEAD_DIM), lambda w, b: (w, 0, 0)),
                  pl.BlockSpec((1, HEAD_DIM), lambda w, b: (0, 0))],
        out_specs=pl.BlockSpec((1, 1, m, HEAD_DIM), lambda w, b: (w, b, 0, 0)),
        out_shape=jax.ShapeDtypeStruct((2, bsz, m, HEAD_DIM), F32),
        name="compress",
    )(z, cmp_pe.reshape(2, 1, kdim), w1, w2, k_gain.reshape(1, HEAD_DIM))
    return out.reshape(2, bsz * N_KV_HEADS, nc, HEAD_DIM)


def _cmp_select_kernel(q_ref, kc_ref, vc_ref, qg_ref, o_ref, qaug_ref, imp_ref, cnt_ref):
    tq = q_ref.shape[0]
    nc = kc_ref.shape[1]
    nb = nc // 2
    pair_w = 2 * HEAD_DIM
    qw = GQA * HEAD_DIM
    q0 = pl.program_id(1) * tq
    qf = _head_rms(q_ref[...], qg_ref[...]) * (HEAD_DIM ** -0.5)
    qn = qf.astype(BF16)
    q2 = qf * LOG2E

    row = lax.broadcasted_iota(jnp.int32, (nc, tq), 0)
    t = q0 + lax.broadcasted_iota(jnp.int32, (nc, tq), 1)
    blk = jnp.where(row < nb, 2 * row, 2 * (row - nb) + 1)
    valid = (blk + 1) * CMP_BLOCK - 1 <= t
    lane = lax.broadcasted_iota(jnp.int32, (tq, pair_w), 1)
    j = lax.broadcasted_iota(jnp.int32, (nb, tq), 0)
    cur = jnp.right_shift(q0 + lax.broadcasted_iota(jnp.int32, (nb, tq), 1), SEL_SHIFT)
    forced = (j == 0) | (j == cur) | (j == cur - 1)
    visible = j <= cur

    def two_heads(x):
        zx = jnp.zeros_like(x)
        return jnp.concatenate([jnp.concatenate([x, zx], axis=1), jnp.concatenate([zx, x], axis=1)], axis=0)

    pairs = [(kvh, pr) for kvh in range(N_KV_HEADS) for pr in range(GQA // 2)]
    k2 = [two_heads(kc_ref[kvh].astype(BF16)) for kvh in range(N_KV_HEADS)]
    v2 = [two_heads(vc_ref[kvh].astype(BF16)) for kvh in range(N_KV_HEADS)]
    st = [_dot_nt(k2[kvh], qn[:, kvh * qw + pr * pair_w:kvh * qw + (pr + 1) * pair_w]) for kvh, pr in pairs]
    psum = [jnp.zeros((nc, tq), F32) for _ in range(N_KV_HEADS)]
    probs = []
    for (kvh, pr), s2 in zip(pairs, st):
        ph = []
        for hh in range(2):
            sm = jnp.where(valid, s2[hh * nc:(hh + 1) * nc], MASKED)
            m = jnp.max(sm, axis=0, keepdims=True)
            m = jnp.where(m > 0.5 * MASKED, m, 0.0)
            e = jnp.exp(sm - m)
            p = e * (1.0 / jnp.maximum(jnp.sum(e, axis=0, keepdims=True), 1e-30))
            psum[kvh] = psum[kvh] + p
            ph.append(p.astype(BF16))
        probs.append(jnp.concatenate(ph, axis=0))
    imps = []
    for kvh in range(N_KV_HEADS):
        imp = psum[kvh][:nb] + psum[kvh][nb:]
        imp = jnp.where(forced, FORCE_SCORE, jnp.where(visible, imp, -jnp.inf))
        imp_ref[kvh] = imp
        imps.append(imp)
    for n, ((kvh, pr), p2) in enumerate(zip(pairs, probs)):
        o_ref[n * pair_w:(n + 1) * pair_w, :] = lax.dot_general(
            v2[kvh], p2, (((0,), (0,)), ((), ())), preferred_element_type=F32).astype(o_ref.dtype)

    n_vis = jnp.minimum(nb, (q0 + tq - 1) // SEL_BLOCK + 1)
    n_slabs = nb // SUBLANES
    row_in_slab = lax.broadcasted_iota(jnp.int32, (SUBLANES, tq), 0)
    cnt_ref[...] = jnp.zeros(cnt_ref.shape, F32)
    for grp in range(n_slabs):
        @pl.when(grp * SUBLANES < n_vis)
        def _(grp=grp):
            for kvh in range(N_KV_HEADS):
                slabs = [imp_ref[kvh, s * SUBLANES:(s + 1) * SUBLANES, :] for s in range(n_slabs)]
                cnt = [jnp.zeros((SUBLANES, tq), F32) for _ in range(n_slabs)]
                for i in range(grp * SUBLANES, (grp + 1) * SUBLANES):
                    vi = imp_ref[kvh, i:i + 1, :]
                    for s in range(n_slabs):
                        if s > grp:
                            beats = vi >= slabs[s]
                        elif s < grp:
                            beats = vi > slabs[s]
                        else:
                            beats = (vi > slabs[s]) | ((vi == slabs[s]) & (i - s * SUBLANES < row_in_slab))
                        cnt[s] = cnt[s] + jnp.where(beats, 1.0, 0.0)
                cnt_ref[kvh] += jnp.concatenate(cnt, axis=0)
    for kvh in range(N_KV_HEADS):
        selneg_t = jnp.where(visible & (cnt_ref[kvh] < float(N_SELECT)), 0.0, MASKED)
        if nb < HEAD_DIM:
            selneg_t = jnp.concatenate([selneg_t, jnp.zeros((HEAD_DIM - nb, tq), F32)], axis=0)
        selneg = jnp.concatenate([selneg_t, selneg_t], axis=0).T
        for pr in range(GQA // 2):
            c0 = kvh * qw + pr * pair_w
            qp = q2[:, c0:c0 + pair_w]
            qaug_ref[kvh, 2 * pr] = jnp.where(lane < HEAD_DIM, qp, selneg).astype(BF16)
            qaug_ref[kvh, 2 * pr + 1] = jnp.where(lane < HEAD_DIM, pltpu.roll(qp, HEAD_DIM, 1), selneg).astype(BF16)


def _cmp_select(z, cmp_kv, q_norm_g, *, bsz, seq, tq):
    t = z.shape[0]
    nq = seq // tq
    nc = seq // CMP_BLOCK
    aug = 2 * HEAD_DIM
    qw = GQA * HEAD_DIM
    kc, vc = cmp_kv[0], cmp_kv[1]
    kv_spec = pl.BlockSpec((N_KV_HEADS, nc, HEAD_DIM), lambda b, i: (b, 0, 0))
    return pl.pallas_call(
        _cmp_select_kernel,
        grid=(bsz, nq),
        in_specs=[pl.BlockSpec((tq, ATTN_WIDTH), lambda b, i: (b * nq + i, Z_Q // ATTN_WIDTH)),
                  kv_spec, kv_spec,
                  pl.BlockSpec((1, ATTN_WIDTH), lambda b, i: (0, 0))],
        out_specs=[pl.BlockSpec((ATTN_WIDTH, tq), lambda b, i: (0, b * nq + i)),
                   pl.BlockSpec((N_KV_HEADS, GQA, tq, aug), lambda b, i: (b, 0, i, 0))],
        out_shape=[jax.ShapeDtypeStruct((ATTN_WIDTH, t), BF16),
                   jax.ShapeDtypeStruct((bsz * N_KV_HEADS, GQA, seq, aug), BF16)],
        scratch_shapes=[pltpu.VMEM((N_KV_HEADS, nc // 2, tq), F32), pltpu.VMEM((N_KV_HEADS, nc // 2, tq), F32)],
        name="cmp_select",
    )(z, kc, vc, jnp.tile(q_norm_g, N_HEADS).reshape(1, ATTN_WIDTH))


def _flash_kernel(q_ref, k_ref, vt_ref, o_ref, sa_ref, sb_ref, pa_ref, pb_ref, m_ref, a_ref, acc_ref, *, window):
    tq = q_ref.shape[2]
    tk = vt_ref.shape[3]
    cols = GQA * tq
    q0 = pl.program_id(2) * tq
    qa = q_ref[0].reshape(cols, q_ref.shape[3])
    t = q0 + (lax.broadcasted_iota(jnp.int32, (tk, cols), 1) & (tq - 1))
    key_row = lax.broadcasted_iota(jnp.int32, (tk, cols), 0)
    span = 2 * tk
    lo = jnp.maximum(0, q0 - window + 1) // span if window else 0
    hi = (q0 + tq + span - 1) // span
    mid1 = jnp.maximum(lo, (q0 + tq - 1 - window) // span + 1) if window else lo
    mid2 = q0 // span
    first = 2 * lo
    last = 2 * hi - 1

    def scores(kb):
        return _dot_nt(k_ref[0, pl.ds(pl.multiple_of(kb * tk, tk), tk), :], qa)

    m_ref[...] = jnp.full(m_ref.shape, MASKED, F32)
    a_ref[...] = jnp.ones(a_ref.shape, F32)
    acc_ref[...] = jnp.zeros(acc_ref.shape, F32)
    pb_ref[...] = jnp.zeros(pb_ref.shape, BF16)
    sa_ref[...] = scores(first)

    def tile(kb, s_cur, s_nxt, p_prev, p_cur, mask):
        pv = _dot(vt_ref[0, jnp.maximum(kb - 1, first)], p_prev[...])
        s_nxt[...] = scores(jnp.minimum(kb + 1, last))
        s = s_cur[...]
        if mask == 'causal':
            s = jnp.where(kb * tk + key_row <= t, s, MASKED)
        elif mask == 'window':
            s = jnp.where(kb * tk + key_row > t - window, s, MASKED)
        m_old = m_ref[...]
        m_new = jnp.maximum(m_old, jnp.max(s, axis=0, keepdims=True))
        acc_ref[...] = a_ref[...] * acc_ref[...] + pv
        a_ref[...] = jnp.exp2(m_old - m_new)
        p_cur[...] = jnp.exp2(s - m_new).astype(BF16)
        m_ref[...] = m_new

    def pair_step(pair, mask):
        tile(2 * pair, sa_ref, sb_ref, pb_ref, pa_ref, mask)
        tile(2 * pair + 1, sb_ref, sa_ref, pa_ref, pb_ref, mask)

    def run(begin, end, mask, unroll=1):
        n_main = (end - begin) // unroll

        def main(i, carry):
            for u in range(unroll):
                pair_step(begin + i * unroll + u, mask)
            return carry

        def rest(pair, carry):
            pair_step(pair, mask)
            return carry

        if unroll > 1:
            lax.fori_loop(0, n_main, main, 0)
        lax.fori_loop(begin + n_main * unroll if unroll > 1 else begin, end, rest, 0)

    if window:
        interior = (mid1 - lo == 1) & (mid2 - mid1 == 1) & (hi - mid2 == 1)

        @pl.when(interior)
        def _():
            pair_step(lo, 'window')
            pair_step(lo + 1, None)
            pair_step(lo + 2, 'causal')

        @pl.when(jnp.logical_not(interior))
        def _():
            run(lo, mid1, 'window')
            run(mid1, mid2, None)
            run(mid2, hi, 'causal')
    else:
        run(mid1, mid2, None, unroll=FLASH_UNROLL)
        run(mid2, hi, 'causal')
    acc = a_ref[...] * acc_ref[...] + _dot(vt_ref[0, last], pb_ref[...])
    o_t = acc[:HEAD_DIM] / acc[HEAD_DIM:HEAD_DIM + 1]
    for g in range(GQA):
        o_ref[g * HEAD_DIM:(g + 1) * HEAD_DIM, :] = o_t[:, g * tq:(g + 1) * tq].astype(o_ref.dtype)


def _flash(qaug, k, vt, *, bsz, seq, tq, window, name):
    t = bsz * seq
    nq = seq // tq
    aug = qaug.shape[3]
    tk = vt.shape[3]
    qw = GQA * HEAD_DIM
    cols = GQA * tq
    return pl.pallas_call(
        functools.partial(_flash_kernel, window=window),
        grid=(bsz, N_KV_HEADS, nq),
        in_specs=[pl.BlockSpec((1, GQA, tq, aug), lambda b, h, i: (b * N_KV_HEADS + h, 0, i, 0)),
                  pl.BlockSpec((1, seq, aug), lambda b, h, i: (h, b, 0)),
                  pl.BlockSpec((1, seq // tk, V_ROWS, tk), lambda b, h, i: (h, b, 0, 0))],
        out_specs=pl.BlockSpec((qw, tq), lambda b, h, i: (h, b * nq + i)),
        out_shape=jax.ShapeDtypeStruct((ATTN_WIDTH, t), BF16),
        scratch_shapes=[pltpu.VMEM((tk, cols), F32), pltpu.VMEM((tk, cols), F32),
                        pltpu.VMEM((tk, cols), BF16), pltpu.VMEM((tk, cols), BF16),
                        pltpu.VMEM((1, cols), F32), pltpu.VMEM((1, cols), F32),
                        pltpu.VMEM((V_ROWS, cols), F32)],
        name=name,
    )(qaug, k, vt)


def _merge_kernel(x_ref, mix_ref, c_ref, cprev_ref, gl_ref, ys_ref, oc_ref, os_ref, ow_ref,
                  cw_ref, wc_ref, wglu_ref, wo_ref, wout_ref, out_ref, *, seq):
    tm = x_ref.shape[0]
    yg = _dot(_gelu_tanh(ys_ref[...]).astype(BF16), wglu_ref[...])
    y_ssm = yg[:, :D_MODEL] * _sigmoid(yg[:, D_MODEL:])
    cb = c_ref[:, :CONV_CH]
    zc = c_ref[:, CONV_CH:2 * CONV_CH] * c_ref[:, 2 * CONV_CH:]
    keep = jnp.where((pl.program_id(0) * tm) % seq != 0, 1.0, 0.0)
    prev = cprev_ref[:, CONV_CH:2 * CONV_CH] * cprev_ref[:, 2 * CONV_CH:] * keep
    row = lax.broadcasted_iota(jnp.int32, zc.shape, 0)
    z1 = jnp.where(row >= 1, pltpu.roll(zc, 1, 0), prev[7:8, :])
    z2 = jnp.where(row >= 2, pltpu.roll(zc, 2, 0), jnp.where(row == 1, prev[7:8, :], prev[6:7, :]))
    conv = cw_ref[0:1, :] * z2 + cw_ref[1:2, :] * z1 + cw_ref[2:3, :] * zc
    y_conv = _dot((cb * conv).astype(BF16), wc_ref[...])
    gates_t = _sigmoid(gl_ref[...]).T
    heads = []
    for hd in range(N_HEADS):
        rows = slice(hd * HEAD_DIM, (hd + 1) * HEAD_DIM)
        c = hd * NSA_BRANCHES
        heads.append(oc_ref[rows, :] * gates_t[c:c + 1] + os_ref[rows, :] * gates_t[c + 1:c + 2]
                     + ow_ref[rows, :] * gates_t[c + 2:c + 3])
    o_t = jnp.concatenate(heads, axis=0).astype(BF16)
    y_attn = lax.dot_general(o_t, wo_ref[...], (((0,), (0,)), ((), ())), preferred_element_type=F32)
    mixed = (_sigmoid(mix_ref[:, :D_MODEL]) * y_ssm + _sigmoid(mix_ref[:, D_MODEL:2 * D_MODEL]) * y_conv
             + _sigmoid(mix_ref[:, 2 * D_MODEL:]) * y_attn)
    out_ref[...] = x_ref[...] + _dot(mixed.astype(BF16), wout_ref[...])


def _merge(x2, z, ys, o_cmp, o_sel, o_win, conv_w, wc, wglu, wo, wout, *, seq, tm):
    t, d = x2.shape
    rb = tm // SUBLANES
    conv_w3 = 3 * CONV_CH
    zc = lambda width, off: pl.BlockSpec((tm, width), lambda i, o=off // width: (i, o))
    row = lambda width: pl.BlockSpec((tm, width), lambda i: (i, 0))
    full = lambda a: pl.BlockSpec(a.shape, lambda i: (0, 0))
    branch_t = pl.BlockSpec((ATTN_WIDTH, tm), lambda i: (0, i))
    return pl.pallas_call(
        functools.partial(_merge_kernel, seq=seq),
        grid=(t // tm,),
        in_specs=[row(d), zc(MIX_BRANCHES * d, Z_MIX), zc(conv_w3, Z_CB),
                  pl.BlockSpec((SUBLANES, conv_w3), lambda i: (jnp.maximum(i * rb - 1, 0), Z_CB // conv_w3)),
                  zc(GATE_PAD, Z_GATE), row(SSM_WIDTH), branch_t, branch_t, branch_t,
                  full(conv_w), full(wc), full(wglu), full(wo), full(wout)],
        out_specs=row(d),
        out_shape=jax.ShapeDtypeStruct((t, d), F32),
        name="merge",
    )(x2, z, z, z, z, ys, o_cmp, o_sel, o_win, conv_w, wc, wglu, wo, wout)


def _ffn_kernel(x_ref, g_ref, wg_ref, wu_ref, wd_ref, o_ref, h_ref, acc_ref):
    f = pl.program_id(1)

    @pl.when(f == 0)
    def _():
        h_ref[...] = _rms(x_ref[...], g_ref[...]).astype(BF16)
        acc_ref[...] = jnp.zeros(acc_ref.shape, F32)

    h = h_ref[...]
    gate = _dot(h, wg_ref[...])
    up = _dot(h, wu_ref[...])
    act = (gate * _sigmoid(gate) * up).astype(BF16)
    acc_ref[...] += _dot(act, wd_ref[...])

    @pl.when(f == pl.num_programs(1) - 1)
    def _():
        o_ref[...] = x_ref[...] + acc_ref[...]


def _ffn(x2, g, w_gate_up, w_down, *, tm, tf):
    t, d = x2.shape
    nf = D_FF // tf
    return pl.pallas_call(
        _ffn_kernel,
        grid=(t // tm, nf),
        in_specs=[pl.BlockSpec((tm, d), lambda i, f: (i, 0)),
                  pl.BlockSpec((1, d), lambda i, f: (0, 0)),
                  pl.BlockSpec((d, tf), lambda i, f: (0, f)),
                  pl.BlockSpec((d, tf), lambda i, f: (0, nf + f)),
                  pl.BlockSpec((tf, d), lambda i, f: (f, 0))],
        out_specs=pl.BlockSpec((tm, d), lambda i, f: (i, 0)),
        out_shape=jax.ShapeDtypeStruct((t, d), F32),
        scratch_shapes=[pltpu.VMEM((tm, d), BF16), pltpu.VMEM((tm, d), F32)],
        compiler_params=pltpu.CompilerParams(dimension_semantics=("parallel", "arbitrary")),
        name="ffn",
    )(x2, g.reshape(1, d), w_gate_up, w_gate_up, w_down)


def _pick(n, pref):
    while n % pref:
        pref //= 2
    return pref


def _tiles(t, seq):
    return dict(
        inproj_tm=_pick(t, 512), inproj_tn=Z_WIDTH // 2,
        kv_tp=_pick(seq, 512),
        key_tile=LANES,
        cmp_tq=_pick(seq, 256), attn_tq=_pick(seq, 256),
        merge_tm=_pick(seq, 256),
        ffn_tm=_pick(t, 1024), ffn_tf=MXU_WIDTH)


def _layer(x2, p, *, bsz, seq):
    ts = _tiles(x2.shape[0], seq)
    z = _inproj(x2, p['mix_norm_g'], p['w_in'], tm=ts['inproj_tm'], tn=ts['inproj_tn'])
    ys = _s5_scan(z, p['s5'], bsz=bsz, seq=seq)
    kaug, vsel, kwin, vwin = _kv_prep(z, p['k_norm_g'], seq=seq, tp=ts['kv_tp'], tk_sel=ts['key_tile'],
                                      tk_win=ts['key_tile'])
    cmp_kv = _compress(z, p['cmp_pe'], p['cmp_w1'], p['cmp_w2'], p['k_norm_g'][0], bsz=bsz, seq=seq)
    o_cmp, qaug = _cmp_select(z, cmp_kv, p['q_norm_g'], bsz=bsz, seq=seq, tq=ts['cmp_tq'])
    o_sel = _flash(qaug, kaug, vsel, bsz=bsz, seq=seq, tq=ts['attn_tq'], window=0, name="sel_attn")
    o_win = _flash(qaug, kwin, vwin, bsz=bsz, seq=seq, tq=ts['attn_tq'], window=WINDOW, name="win_attn")
    x2 = _merge(x2, z, ys, o_cmp, o_sel, o_win, p['conv_w'], p['conv_w_out'], p['ssm_w_glu'], p['nsa_w_o'],
                p['w_out'], seq=seq, tm=ts['merge_tm'])
    return _ffn(x2, p['ffn_norm_g'], p['ffn_w_gate_up'], p['ffn_w_down'], tm=ts['ffn_tm'], tf=ts['ffn_tf'])


def kernel(x, mix_norm_g, w_in, ssm_lam_re, ssm_lam_im, ssm_b_re, ssm_b_im, ssm_c_re, ssm_c_im, ssm_d, ssm_log_dt, ssm_w_glu, conv_w, conv_w_out, q_norm_g, k_norm_g, cmp_pe, cmp_w1, cmp_w2, nsa_w_o, w_out, ffn_norm_g, ffn_w_gate_up, ffn_w_down):
    bsz, seq, d = x.shape
    assert d == D_MODEL, d
    assert seq % (SUBLANES * SEL_BLOCK) == 0 and seq // SEL_BLOCK <= HEAD_DIM, seq
    assert seq % WINDOW == 0 and seq % (S5_CHUNK * SUBLANES) == 0, seq
    x2 = x.reshape(bsz * seq, d)
    for i in range(w_in.shape[0]):
        p = dict(
            mix_norm_g=mix_norm_g[i], w_in=_permute_w_in(w_in[i]),
            s5=_s5_tables(ssm_lam_re[i], ssm_lam_im[i], ssm_b_re[i], ssm_b_im[i], ssm_c_re[i], ssm_c_im[i],
                          ssm_d[i], ssm_log_dt[i], seq // S5_CHUNK),
            ssm_w_glu=ssm_w_glu[i].astype(BF16), conv_w=conv_w[i], conv_w_out=conv_w_out[i].astype(BF16),
            q_norm_g=q_norm_g[i], k_norm_g=k_norm_g[i], cmp_pe=cmp_pe[i],
            cmp_w1=cmp_w1[i].astype(BF16), cmp_w2=cmp_w2[i].astype(BF16), nsa_w_o=nsa_w_o[i].astype(BF16),
            w_out=w_out[i].astype(BF16), ffn_norm_g=ffn_norm_g[i],
            ffn_w_gate_up=ffn_w_gate_up[i].astype(BF16), ffn_w_down=ffn_w_down[i].astype(BF16))
        x2 = _layer(x2, p, bsz=bsz, seq=seq)
    return x2.reshape(bsz, seq, d)
```

```python
import functools

import jax
import jax.numpy as jnp
import numpy as np
from jax import lax
from jax.experimental import pallas as pl
from jax.experimental.pallas import tpu as pltpu

F32 = jnp.float32
BF16 = jnp.bfloat16

D_MODEL = 1024
SSM_WIDTH = 512
SSM_GROUP = 16
CONV_CH = 512
HEAD_DIM = 64
N_HEADS = 8
N_KV_HEADS = 2
GQA = N_HEADS // N_KV_HEADS
ATTN_WIDTH = N_HEADS * HEAD_DIM
KV_WIDTH = N_KV_HEADS * HEAD_DIM
CMP_BLOCK = 32
SEL_BLOCK = 64
SEL_SHIFT = SEL_BLOCK.bit_length() - 1
N_SELECT = 16
WINDOW = 512
CMP_HIDDEN = 256
FORCE_SCORE = 1e4
NSA_BRANCHES = 3
MIX_BRANCHES = 3
D_FF = 2816
RMS_EPS = 1e-6

LANES = 128
SUBLANES = 8
MXU_WIDTH = 256
MASKED = -1e30
LOG2E = 1.4426950408889634
FLASH_UNROLL = 4
S5_CHUNK = 16
S5_TILE_GROUPS = LANES // SSM_GROUP
GATE_PAD = MXU_WIDTH
BF16_SUBLANES = 16
V_ROWS = HEAD_DIM + BF16_SUBLANES

Z_MIX = 0
Z_U = Z_MIX + MIX_BRANCHES * D_MODEL
Z_CB = Z_U + SSM_WIDTH
Z_CC = Z_CB + CONV_CH
Z_CX = Z_CC + CONV_CH
Z_Q = Z_CX + CONV_CH
Z_KC = Z_Q + ATTN_WIDTH
Z_VC = Z_KC + KV_WIDTH
Z_KS = Z_VC + KV_WIDTH
Z_VS = Z_KS + KV_WIDTH
Z_KW = Z_VS + KV_WIDTH
Z_VW = Z_KW + KV_WIDTH
Z_GATE = Z_VW + KV_WIDTH
Z_WIDTH = Z_GATE + GATE_PAD


def _gelu_tanh(x):
    return 0.5 * x * (1.0 + jnp.tanh(np.sqrt(2.0 / np.pi).astype(np.float32) * (x + 0.044715 * (x * x * x))))


def _sigmoid(x):
    return 0.5 * jnp.tanh(0.5 * x) + 0.5


def _rms(x, g):
    return x * lax.rsqrt(jnp.mean(x * x, axis=-1, keepdims=True) + RMS_EPS) * g


def _dot(a, b):
    return jnp.dot(a, b, preferred_element_type=F32)


def _segment_ones(width):
    seg = jnp.arange(width) // HEAD_DIM
    return (seg[:, None] == seg[None, :]).astype(BF16)


def _head_rms(x, g, seg):
    x2 = x * x
    hi = x2.astype(BF16)
    lo = (x2 - hi.astype(F32)).astype(BF16)
    ss = _dot(hi, seg) + _dot(lo, seg)
    return x * lax.rsqrt(ss * (1.0 / HEAD_DIM) + RMS_EPS) * g


def _dot_nt(a, b):
    return lax.dot_general(a, b, (((1,), (1,)), ((), ())), preferred_element_type=F32)


def _inproj_kernel(x_ref, g_ref, w_ref, o_ref):
    h = _rms(x_ref[...], g_ref[...]).astype(BF16)
    o_ref[...] = _dot(h, w_ref[...])


def _inproj(x2, g, w, *, tm, tn):
    t, d = x2.shape
    n = w.shape[1]
    return pl.pallas_call(
        _inproj_kernel,
        grid=(n // tn, t // tm),
        in_specs=[pl.BlockSpec((tm, d), lambda j, i: (i, 0)),
                  pl.BlockSpec((1, d), lambda j, i: (0, 0)),
                  pl.BlockSpec((d, tn), lambda j, i: (0, j))],
        out_specs=pl.BlockSpec((tm, tn), lambda j, i: (i, j)),
        out_shape=jax.ShapeDtypeStruct((t, n), F32),
        name="inproj",
    )(x2, g.reshape(1, d), w)


def _permute_w_in(w):
    n_plain = SSM_WIDTH + 3 * CONV_CH + ATTN_WIDTH + 6 * KV_WIDTH
    n_gate = N_HEADS * NSA_BRANCHES
    gate = jnp.pad(w[:, n_plain:n_plain + n_gate], ((0, 0), (0, GATE_PAD - n_gate)))
    return jnp.concatenate([w[:, n_plain + n_gate:], w[:, :n_plain], gate], axis=1).astype(BF16)


def _s5_tables(lam_re, lam_im, b_re, b_im, c_re, c_im, d_skip, log_dt, n_chunks):
    hp = lax.Precision.HIGHEST
    g, p = lam_re.shape
    h, l, gt = SSM_GROUP, S5_CHUNK, S5_TILE_GROUPS
    nt = g // gt
    dt = jnp.exp(log_dt)[:, None]
    ar, ai = lam_re * dt, lam_im * dt

    def powers(k):
        mag = jnp.exp(ar[None] * k[:, None, None])
        ang = ai[None] * k[:, None, None]
        return mag * jnp.cos(ang), mag * jnp.sin(ang)

    pr, pi = powers(jnp.arange(l + 1, dtype=F32))
    nr, ni = pr[1] - 1.0, pi[1]
    den = lam_re * lam_re + lam_im * lam_im
    fr, fi = (nr * lam_re + ni * lam_im) / den, (ni * lam_re - nr * lam_im) / den
    bbr = fr[..., None] * b_re - fi[..., None] * b_im
    bbi = fr[..., None] * b_im + fi[..., None] * b_re
    wr = pr[:l, :, :, None] * bbr[None] - pi[:l, :, :, None] * bbi[None]
    wi = pr[:l, :, :, None] * bbi[None] + pi[:l, :, :, None] * bbr[None]
    kern = (jnp.einsum('gop,tgpi->tgio', c_re, wr, precision=hp)
            - jnp.einsum('gop,tgpi->tgio', c_im, wi, precision=hp))
    r = jnp.arange(l)

    def group_diag(x, row_w, col_w, halves=1):
        rows, hw = x.shape[-2:]
        w = hw // halves
        col = jnp.arange(halves * gt * w)
        src = (col // (gt * w)) * w + col % w
        expand = (jnp.arange(hw)[:, None] == src[None, :]).astype(BF16)
        tiled = jnp.einsum('...w,wv->...v', x.astype(BF16), expand, preferred_element_type=BF16)
        own = (jnp.arange(rows)[:, None] // row_w) % gt == (col[None, :] % (gt * w)) // col_w
        return jnp.where(own, tiled, jnp.zeros((), BF16))

    kblk = group_diag(kern.reshape(l, g * h, h), h, h)
    krev = kblk.reshape(l, nt, LANES, LANES)[::-1].transpose(1, 0, 2, 3).reshape(nt, l * LANES, LANES)
    krev = jnp.concatenate([krev, jnp.zeros((nt, LANES, LANES), krev.dtype)], axis=1)
    t_pair = jnp.concatenate(
        [jnp.concatenate([krev[:, (l - 1 - s) * LANES:(l + 1) * LANES], krev[:, (l - 2 - s) * LANES:l * LANES]], axis=2)
         for s in range(0, l, 2)], axis=1)
    qr, qi = pr[l - 1 - r], pi[l - 1 - r]
    st_re = qr[..., None] * bbr[None] - qi[..., None] * bbi[None]
    st_im = qr[..., None] * bbi[None] + qi[..., None] * bbr[None]
    rows_in = lambda x: group_diag(x.transpose(0, 1, 3, 2).reshape(l, g * h, p), h, p).reshape(l, nt, LANES, gt * p)
    er, ei = pr[1:l + 1][:, :, None, :], pi[1:l + 1][:, :, None, :]

    def rows_out(x):
        x = x.transpose(0, 1, 3, 2).reshape(l // 2, 2, g * p, h).transpose(0, 2, 1, 3).reshape(l // 2, g * p, 2 * h)
        return group_diag(x, p, h, halves=2).reshape(l // 2, nt, gt * p, 2 * LANES)

    n_steps = max(1, int(np.ceil(np.log2(n_chunks))))
    dr, di = powers(l * (2.0 ** jnp.arange(n_steps, dtype=F32)))
    lanes = lambda m: m.reshape(n_steps, nt, gt * p).transpose(1, 0, 2)
    return dict(t_pair=t_pair, s_re=rows_in(st_re), s_im=rows_in(st_im),
                c_re=rows_out(c_re[None] * er - c_im[None] * ei),
                c_im=rows_out(-(c_re[None] * ei + c_im[None] * er)),
                d_re=lanes(dr), d_im=lanes(di), d_skip=d_skip.reshape(1, g * h))


def _s5_kernel(u_ref, tp_ref, sre_ref, sim_ref, cre_ref, cim_ref, dre_ref, dim_ref, dsk_ref, y_ref):
    seq = u_ref.shape[0]
    l = S5_CHUNK
    nc = seq // l
    n_steps = dre_ref.shape[1]
    ns = dre_ref.shape[2]
    us = [u_ref[pl.ds(r, nc, stride=l), :] for r in range(l)]
    ucat = jnp.concatenate(us, axis=1).astype(BF16)
    xr = _dot(ucat, sre_ref[:, 0].reshape(l * LANES, ns))
    xi = _dot(ucat, sim_ref[:, 0].reshape(l * LANES, ns))
    row = lax.broadcasted_iota(jnp.int32, xr.shape, 0)
    for k in range(n_steps):
        s = 1 << k
        if s >= nc:
            break
        dr = dre_ref[0, k:k + 1, :]
        di = dim_ref[0, k:k + 1, :]
        sr = jnp.where(row >= s, pltpu.roll(xr, s, 0), 0.0)
        si = jnp.where(row >= s, pltpu.roll(xi, s, 0), 0.0)
        xr, xi = xr + (dr * sr - di * si), xi + (dr * si + di * sr)
    pr = jnp.where(row >= 1, pltpu.roll(xr, 1, 0), 0.0).astype(BF16)
    pi = jnp.where(row >= 1, pltpu.roll(xi, 1, 0), 0.0).astype(BF16)
    dsk = dsk_ref[...]
    off = 0
    for s in range(0, l, 2):
        rows = (s + 2) * LANES
        y2 = (_dot(ucat[:, :rows], tp_ref[0, off:off + rows, :])
              + _dot(pr, cre_ref[s // 2, 0]) + _dot(pi, cim_ref[s // 2, 0]))
        y_ref[pl.ds(s, nc, stride=l), :] = y2[:, :LANES] + dsk * us[s]
        y_ref[pl.ds(s + 1, nc, stride=l), :] = y2[:, LANES:] + dsk * us[s + 1]
        off += rows


def _s5_scan(z, tabs, *, bsz, seq):
    t = z.shape[0]
    nt = SSM_WIDTH // LANES
    tile3 = lambda a: pl.BlockSpec((1,) + a.shape[1:], lambda j, b: (j, 0, 0))
    step4 = lambda a: pl.BlockSpec((a.shape[0], 1) + a.shape[2:], lambda j, b: (0, j, 0, 0))
    return pl.pallas_call(
        _s5_kernel,
        grid=(nt, bsz),
        in_specs=[pl.BlockSpec((seq, LANES), lambda j, b: (b, Z_U // LANES + j)),
                  tile3(tabs['t_pair']), step4(tabs['s_re']), step4(tabs['s_im']),
                  step4(tabs['c_re']), step4(tabs['c_im']), tile3(tabs['d_re']), tile3(tabs['d_im']),
                  pl.BlockSpec((1, LANES), lambda j, b: (0, j))],
        out_specs=pl.BlockSpec((seq, LANES), lambda j, b: (b, j)),
        out_shape=jax.ShapeDtypeStruct((t, SSM_WIDTH), F32),
        name="s5_scan",
    )(z, tabs['t_pair'], tabs['s_re'], tabs['s_im'], tabs['c_re'], tabs['c_im'], tabs['d_re'], tabs['d_im'],
      tabs['d_skip'])


def _kv_prep_kernel(ks_ref, vs_ref, kw_ref, vw_ref, g_ref, seg_ref, kaug_ref, vsel_ref, kwin_ref, vwin_ref, *, seq):
    tp = ks_ref.shape[0]
    pos = (pl.program_id(0) * tp) % seq + lax.broadcasted_iota(jnp.int32, (tp, HEAD_DIM), 0)
    blk = jnp.right_shift(pos, SEL_SHIFT)
    onehot = jnp.where(lax.broadcasted_iota(jnp.int32, (tp, HEAD_DIM), 1) == blk, 1.0, 0.0).astype(BF16)
    zeros = jnp.zeros((tp, HEAD_DIM), BF16)
    ks = _head_rms(ks_ref[...], g_ref[0:1, :], seg_ref[...]).astype(BF16)
    kw = _head_rms(kw_ref[...], g_ref[1:2, :], seg_ref[...]).astype(BF16)
    for hh in range(N_KV_HEADS):
        sl = slice(hh * HEAD_DIM, (hh + 1) * HEAD_DIM)
        kaug_ref[hh] = jnp.concatenate([ks[:, sl], onehot], axis=1)
        kwin_ref[hh] = jnp.concatenate([kw[:, sl], zeros], axis=1)
    for v_ref, vt_ref in ((vs_ref, vsel_ref), (vw_ref, vwin_ref)):
        tk = vt_ref.shape[3]
        vt = v_ref[...].T.astype(BF16)
        ones = jnp.where(lax.broadcasted_iota(jnp.int32, (V_ROWS - HEAD_DIM, tp), 0) == 0, 1.0, 0.0).astype(BF16)
        for hh in range(N_KV_HEADS):
            vh = jnp.concatenate([vt[hh * HEAD_DIM:(hh + 1) * HEAD_DIM], ones], axis=0)
            for j in range(tp // tk):
                vt_ref[hh, j] = vh[:, j * tk:(j + 1) * tk]


def _kv_prep(z, k_norm_g, *, seq, tp, tk_sel, tk_win):
    t = z.shape[0]
    col = lambda off: pl.BlockSpec((tp, KV_WIDTH), lambda i, o=off // KV_WIDTH: (i, o))
    aug = 2 * HEAD_DIM
    kspec = pl.BlockSpec((N_KV_HEADS, tp, aug), lambda i: (0, i, 0))
    vspec = lambda tk: pl.BlockSpec((N_KV_HEADS, tp // tk, V_ROWS, tk), lambda i: (0, i, 0, 0))
    vshape = lambda tk: jax.ShapeDtypeStruct((N_KV_HEADS, t // tk, V_ROWS, tk), BF16)
    return pl.pallas_call(
        functools.partial(_kv_prep_kernel, seq=seq),
        grid=(t // tp,),
        in_specs=[col(Z_KS), col(Z_VS), col(Z_KW), col(Z_VW), pl.BlockSpec((2, KV_WIDTH), lambda i: (0, 0)),
                  pl.BlockSpec((KV_WIDTH, KV_WIDTH), lambda i: (0, 0))],
        out_specs=[kspec, vspec(tk_sel), kspec, vspec(tk_win)],
        out_shape=[jax.ShapeDtypeStruct((N_KV_HEADS, t, aug), BF16), vshape(tk_sel),
                   jax.ShapeDtypeStruct((N_KV_HEADS, t, aug), BF16), vshape(tk_win)],
        name="kv_prep",
    )(z, z, z, z, jnp.tile(k_norm_g[1:3], (1, N_KV_HEADS)), _segment_ones(KV_WIDTH))


def _compress_kernel(x_ref, pe_ref, w1_ref, w2_ref, g_ref, o_ref):
    seq = x_ref.shape[0]
    nb = seq // (2 * CMP_BLOCK)
    lane = lax.broadcasted_iota(jnp.int32, (nb, 2 * HEAD_DIM), 1)
    rows = []
    for hh in range(N_KV_HEADS):
        for parity in range(2):
            cols = []
            for s in range(0, CMP_BLOCK, 2):
                a = x_ref[pl.ds(parity * CMP_BLOCK + s, nb, stride=2 * CMP_BLOCK), :]
                b = x_ref[pl.ds(parity * CMP_BLOCK + s + 1, nb, stride=2 * CMP_BLOCK), :]
                if hh == 0:
                    cols.append(jnp.where(lane < HEAD_DIM, a, pltpu.roll(b, HEAD_DIM, 1)))
                else:
                    cols.append(jnp.where(lane < HEAD_DIM, pltpu.roll(a, HEAD_DIM, 1), b))
            rows.append(jnp.concatenate(cols, axis=1))
    x = (jnp.concatenate(rows, axis=0) + pe_ref[0]).astype(BF16)
    hid = _gelu_tanh(_dot(x, w1_ref[0])).astype(BF16)
    y = _dot(hid, w2_ref[0])
    o_ref[0, 0] = jnp.where(pl.program_id(0) == 0, _rms(y, g_ref[...]), y)


def _compress(z, cmp_pe, w1, w2, k_gain, *, bsz, seq):
    nc = seq // CMP_BLOCK
    kdim = CMP_BLOCK * HEAD_DIM
    m = N_KV_HEADS * nc
    out = pl.pallas_call(
        _compress_kernel,
        grid=(2, bsz),
        in_specs=[pl.BlockSpec((seq, KV_WIDTH), lambda w, b: (b, Z_KC // KV_WIDTH + w)),
                  pl.BlockSpec((1, 1, kdim), lambda w, b: (w, 0, 0)),
                  pl.BlockSpec((1, kdim, CMP_HIDDEN), lambda w, b: (w, 0, 0)),
                  pl.BlockSpec((1, CMP_HIDDEN, HEAD_DIM), lambda w, b: (w, 0, 0)),
                  pl.BlockSpec((1, HEAD_DIM), lambda w, b: (0, 0))],
        out_specs=pl.BlockSpec((1, 1, m, HEAD_DIM), lambda w, b: (w, b, 0, 0)),
        out_shape=jax.ShapeDtypeStruct((2, bsz, m, HEAD_DIM), F32),
        name="compress",
    )(z, cmp_pe.reshape(2, 1, kdim), w1, w2, k_gain.reshape(1, HEAD_DIM))
    return out.reshape(2, bsz * N_KV_HEADS, nc, HEAD_DIM)


def _cmp_select_kernel(q_ref, kc_ref, vc_ref, qg_ref, seg_ref, o_ref, qaug_ref, imp_ref, cnt_ref):
    tq = q_ref.shape[0]
    nc = kc_ref.shape[1]
    nb = nc // 2
    pair_w = 2 * HEAD_DIM
    qw = GQA * HEAD_DIM
    q0 = pl.program_id(1) * tq
    qf = _head_rms(q_ref[...], qg_ref[...], seg_ref[...]) * (HEAD_DIM ** -0.5)
    qn = qf.astype(BF16)
    q2 = qf * LOG2E

    row = lax.broadcasted_iota(jnp.int32, (nc, tq), 0)
    t = q0 + lax.broadcasted_iota(jnp.int32, (nc, tq), 1)
    blk = jnp.where(row < nb, 2 * row, 2 * (row - nb) + 1)
    valid = (blk + 1) * CMP_BLOCK - 1 <= t
    lane = lax.broadcasted_iota(jnp.int32, (tq, pair_w), 1)
    j = lax.broadcasted_iota(jnp.int32, (nb, tq), 0)
    cur = jnp.right_shift(q0 + lax.broadcasted_iota(jnp.int32, (nb, tq), 1), SEL_SHIFT)
    forced = (j == 0) | (j == cur) | (j == cur - 1)
    visible = j <= cur

    def two_heads(x):
        zx = jnp.zeros_like(x)
        return jnp.concatenate([jnp.concatenate([x, zx], axis=1), jnp.concatenate([zx, x], axis=1)], axis=0)

    pairs = [(kvh, pr) for kvh in range(N_KV_HEADS) for pr in range(GQA // 2)]
    k2 = [two_heads(kc_ref[kvh].astype(BF16)) for kvh in range(N_KV_HEADS)]
    v2 = [two_heads(vc_ref[kvh].astype(BF16)) for kvh in range(N_KV_HEADS)]
    st = [_dot_nt(k2[kvh], qn[:, kvh * qw + pr * pair_w:kvh * qw + (pr + 1) * pair_w]) for kvh, pr in pairs]
    psum = [jnp.zeros((nc, tq), F32) for _ in range(N_KV_HEADS)]
    probs = []
    for (kvh, pr), s2 in zip(pairs, st):
        ph = []
        for hh in range(2):
            sm = jnp.where(valid, s2[hh * nc:(hh + 1) * nc], MASKED)
            m = jnp.max(sm, axis=0, keepdims=True)
            m = jnp.where(m > 0.5 * MASKED, m, 0.0)
            e = jnp.exp(sm - m)
            p = e * (1.0 / jnp.maximum(jnp.sum(e, axis=0, keepdims=True), 1e-30))
            psum[kvh] = psum[kvh] + p
            ph.append(p.astype(BF16))
        probs.append(jnp.concatenate(ph, axis=0))
    imps = []
    for kvh in range(N_KV_HEADS):
        imp = psum[kvh][:nb] + psum[kvh][nb:]
        imp = jnp.where(forced, FORCE_SCORE, jnp.where(visible, imp, -jnp.inf))
        imp_ref[kvh] = imp
        imps.append(imp)
    for n, ((kvh, pr), p2) in enumerate(zip(pairs, probs)):
        o_ref[n * pair_w:(n + 1) * pair_w, :] = lax.dot_general(
            v2[kvh], p2, (((0,), (0,)), ((), ())), preferred_element_type=F32).astype(o_ref.dtype)

    n_vis = jnp.minimum(nb, (q0 + tq - 1) // SEL_BLOCK + 1)
    n_slabs = nb // SUBLANES
    row_in_slab = lax.broadcasted_iota(jnp.int32, (SUBLANES, tq), 0)
    cnt_ref[...] = jnp.zeros(cnt_ref.shape, F32)
    for grp in range(n_slabs):
        @pl.when(grp * SUBLANES < n_vis)
        def _(grp=grp):
            for kvh in range(N_KV_HEADS):
                slabs = [imp_ref[kvh, s * SUBLANES:(s + 1) * SUBLANES, :] for s in range(n_slabs)]
                cnt = [jnp.zeros((SUBLANES, tq), F32) for _ in range(n_slabs)]
                for i in range(grp * SUBLANES, (grp + 1) * SUBLANES):
                    vi = imp_ref[kvh, i:i + 1, :]
                    for s in range(n_slabs):
                        if s > grp:
                            beats = vi >= slabs[s]
                        elif s < grp:
                            beats = vi > slabs[s]
                        else:
                            beats = (vi > slabs[s]) | ((vi == slabs[s]) & (i - s * SUBLANES < row_in_slab))
                        cnt[s] = cnt[s] + jnp.where(beats, 1.0, 0.0)
                cnt_ref[kvh] += jnp.concatenate(cnt, axis=0)
    for kvh in range(N_KV_HEADS):
        selneg_t = jnp.where(visible & (cnt_ref[kvh] < float(N_SELECT)), 0.0, MASKED)
        if nb < HEAD_DIM:
            selneg_t = jnp.concatenate([selneg_t, jnp.zeros((HEAD_DIM - nb, tq), F32)], axis=0)
        selneg = jnp.concatenate([selneg_t, selneg_t], axis=0).T
        for pr in range(GQA // 2):
            c0 = kvh * qw + pr * pair_w
            qp = q2[:, c0:c0 + pair_w]
            qaug_ref[kvh, 2 * pr] = jnp.where(lane < HEAD_DIM, qp, selneg).astype(BF16)
            qaug_ref[kvh, 2 * pr + 1] = jnp.where(lane < HEAD_DIM, pltpu.roll(qp, HEAD_DIM, 1), selneg).astype(BF16)


def _cmp_select(z, cmp_kv, q_norm_g, *, bsz, seq, tq):
    t = z.shape[0]
    nq = seq // tq
    nc = seq // CMP_BLOCK
    aug = 2 * HEAD_DIM
    qw = GQA * HEAD_DIM
    kc, vc = cmp_kv[0], cmp_kv[1]
    kv_spec = pl.BlockSpec((N_KV_HEADS, nc, HEAD_DIM), lambda b, i: (b, 0, 0))
    return pl.pallas_call(
        _cmp_select_kernel,
        grid=(bsz, nq),
        in_specs=[pl.BlockSpec((tq, ATTN_WIDTH), lambda b, i: (b * nq + i, Z_Q // ATTN_WIDTH)),
                  kv_spec, kv_spec,
                  pl.BlockSpec((1, ATTN_WIDTH), lambda b, i: (0, 0)),
                  pl.BlockSpec((ATTN_WIDTH, ATTN_WIDTH), lambda b, i: (0, 0))],
        out_specs=[pl.BlockSpec((ATTN_WIDTH, tq), lambda b, i: (0, b * nq + i)),
                   pl.BlockSpec((N_KV_HEADS, GQA, tq, aug), lambda b, i: (b, 0, i, 0))],
        out_shape=[jax.ShapeDtypeStruct((ATTN_WIDTH, t), BF16),
                   jax.ShapeDtypeStruct((bsz * N_KV_HEADS, GQA, seq, aug), BF16)],
        scratch_shapes=[pltpu.VMEM((N_KV_HEADS, nc // 2, tq), F32), pltpu.VMEM((N_KV_HEADS, nc // 2, tq), F32)],
        name="cmp_select",
    )(z, kc, vc, jnp.tile(q_norm_g, N_HEADS).reshape(1, ATTN_WIDTH), _segment_ones(ATTN_WIDTH))


def _flash_kernel(q_ref, k_ref, vt_ref, o_ref, sa_ref, sb_ref, pa_ref, pb_ref, m_ref, a_ref, acc_ref, *, window):
    tq = q_ref.shape[2]
    tk = vt_ref.shape[3]
    cols = GQA * tq
    q0 = pl.program_id(2) * tq
    qa = q_ref[0].reshape(cols, q_ref.shape[3])
    t = q0 + (lax.broadcasted_iota(jnp.int32, (tk, cols), 1) & (tq - 1))
    key_row = lax.broadcasted_iota(jnp.int32, (tk, cols), 0)
    span = 2 * tk
    lo = jnp.maximum(0, q0 - window + 1) // span if window else 0
    hi = (q0 + tq + span - 1) // span
    mid1 = jnp.maximum(lo, (q0 + tq - 1 - window) // span + 1) if window else lo
    mid2 = q0 // span
    first = 2 * lo
    last = 2 * hi - 1

    def scores(kb):
        return _dot_nt(k_ref[0, pl.ds(pl.multiple_of(kb * tk, tk), tk), :], qa)

    m_ref[...] = jnp.full(m_ref.shape, MASKED, F32)
    a_ref[...] = jnp.ones(a_ref.shape, F32)
    acc_ref[...] = jnp.zeros(acc_ref.shape, F32)
    pb_ref[...] = jnp.zeros(pb_ref.shape, BF16)
    sa_ref[...] = scores(first)

    def tile(kb, s_cur, s_nxt, p_prev, p_cur, mask):
        pv = _dot(vt_ref[0, jnp.maximum(kb - 1, first)], p_prev[...])
        s_nxt[...] = scores(jnp.minimum(kb + 1, last))
        s = s_cur[...]
        if mask == 'causal':
            s = jnp.where(kb * tk + key_row <= t, s, MASKED)
        elif mask == 'window':
            s = jnp.where(kb * tk + key_row > t - window, s, MASKED)
        m_old = m_ref[...]
        m_new = jnp.maximum(m_old, jnp.max(s, axis=0, keepdims=True))
        acc_ref[...] = a_ref[...] * acc_ref[...] + pv
        a_ref[...] = jnp.exp2(m_old - m_new)
        p_cur[...] = jnp.exp2(s - m_new).astype(BF16)
        m_ref[...] = m_new

    def pair_step(pair, mask):
        tile(2 * pair, sa_ref, sb_ref, pb_ref, pa_ref, mask)
        tile(2 * pair + 1, sb_ref, sa_ref, pa_ref, pb_ref, mask)

    def run(begin, end, mask, unroll=1):
        n_main = (end - begin) // unroll

        def main(i, carry):
            for u in range(unroll):
                pair_step(begin + i * unroll + u, mask)
            return carry

        def rest(pair, carry):
            pair_step(pair, mask)
            return carry

        if unroll > 1:
            lax.fori_loop(0, n_main, main, 0)
        lax.fori_loop(begin + n_main * unroll if unroll > 1 else begin, end, rest, 0)

    if window:
        interior = (mid1 - lo == 1) & (mid2 - mid1 == 1) & (hi - mid2 == 1)

        @pl.when(interior)
        def _():
            pair_step(lo, 'window')
            pair_step(lo + 1, None)
            pair_step(lo + 2, 'causal')

        @pl.when(jnp.logical_not(interior))
        def _():
            run(lo, mid1, 'window')
            run(mid1, mid2, None)
            run(mid2, hi, 'causal')
    else:
        run(mid1, mid2, None, unroll=FLASH_UNROLL)
        run(mid2, hi, 'causal')
    acc = a_ref[...] * acc_ref[...] + _dot(vt_ref[0, last], pb_ref[...])
    o_t = acc[:HEAD_DIM] / acc[HEAD_DIM:HEAD_DIM + 1]
    for g in range(GQA):
        o_ref[g * HEAD_DIM:(g + 1) * HEAD_DIM, :] = o_t[:, g * tq:(g + 1) * tq].astype(o_ref.dtype)


def _flash(qaug, k, vt, *, bsz, seq, tq, window, name):
    t = bsz * seq
    nq = seq // tq
    aug = qaug.shape[3]
    tk = vt.shape[3]
    qw = GQA * HEAD_DIM
    cols = GQA * tq
    return pl.pallas_call(
        functools.partial(_flash_kernel, window=window),
        grid=(bsz, N_KV_HEADS, nq),
        in_specs=[pl.BlockSpec((1, GQA, tq, aug), lambda b, h, i: (b * N_KV_HEADS + h, 0, i, 0)),
                  pl.BlockSpec((1, seq, aug), lambda b, h, i: (h, b, 0)),
                  pl.BlockSpec((1, seq // tk, V_ROWS, tk), lambda b, h, i: (h, b, 0, 0))],
        out_specs=pl.BlockSpec((qw, tq), lambda b, h, i: (h, b * nq + i)),
        out_shape=jax.ShapeDtypeStruct((ATTN_WIDTH, t), BF16),
        scratch_shapes=[pltpu.VMEM((tk, cols), F32), pltpu.VMEM((tk, cols), F32),
                        pltpu.VMEM((tk, cols), BF16), pltpu.VMEM((tk, cols), BF16),
                        pltpu.VMEM((1, cols), F32), pltpu.VMEM((1, cols), F32),
                        pltpu.VMEM((V_ROWS, cols), F32)],
        name=name,
    )(qaug, k, vt)


def _merge_kernel(x_ref, g0_ref, g1_ref, g2_ref, cb_ref, cc_ref, cx_ref, pc_ref, px_ref, gl_ref, ys_ref,
                  oc_ref, os_ref, ow_ref, cw_ref, wc_ref, wglu_ref, wo_ref, wout_ref, out_ref, *, seq):
    tm = x_ref.shape[0]
    yg = _dot(_gelu_tanh(ys_ref[...]).astype(BF16), wglu_ref[...])
    y_ssm = yg[:, :D_MODEL] * _sigmoid(yg[:, D_MODEL:])
    zc = cc_ref[...] * cx_ref[...]
    keep = jnp.where((pl.program_id(0) * tm) % seq != 0, 1.0, 0.0)
    prev = pc_ref[...] * px_ref[...] * keep
    row = lax.broadcasted_iota(jnp.int32, zc.shape, 0)
    z1 = jnp.where(row >= 1, pltpu.roll(zc, 1, 0), prev[7:8, :])
    z2 = jnp.where(row >= 2, pltpu.roll(zc, 2, 0), jnp.where(row == 1, prev[7:8, :], prev[6:7, :]))
    conv = cw_ref[0:1, :] * z2 + cw_ref[1:2, :] * z1 + cw_ref[2:3, :] * zc
    y_conv = _dot((cb_ref[...] * conv).astype(BF16), wc_ref[...])
    gates_t = _sigmoid(gl_ref[...]).T
    heads = []
    for hd in range(N_HEADS):
        rows = slice(hd * HEAD_DIM, (hd + 1) * HEAD_DIM)
        c = hd * NSA_BRANCHES
        heads.append(oc_ref[rows, :] * gates_t[c:c + 1] + os_ref[rows, :] * gates_t[c + 1:c + 2]
                     + ow_ref[rows, :] * gates_t[c + 2:c + 3])
    o_t = jnp.concatenate(heads, axis=0).astype(BF16)
    y_attn = lax.dot_general(o_t, wo_ref[...], (((0,), (0,)), ((), ())), preferred_element_type=F32)
    mixed =_sigmoid(g0_ref[...]) * y_ssm + _sigmoid(g1_ref[...]) * y_conv + _sigmoid(g2_ref[...]) * y_attn
    out_ref[...] = x_ref[...] + _dot(mixed.astype(BF16), wout_ref[...])


def _merge(x2, z, ys, o_cmp, o_sel, o_win, conv_w, wc, wglu, wo, wout, *, seq, tm):
    t, d = x2.shape
    rb = tm // 8
    zc = lambda width, off: pl.BlockSpec((tm, width), lambda i, o=off // width: (i, o))
    zprev = lambda off: pl.BlockSpec((8, CONV_CH), lambda i, o=off // CONV_CH: (jnp.maximum(i * rb - 1, 0), o))
    row = lambda width: pl.BlockSpec((tm, width), lambda i: (i, 0))
    full = lambda a: pl.BlockSpec(a.shape, lambda i: (0, 0))
    branch_t = pl.BlockSpec((ATTN_WIDTH, tm), lambda i: (0, i))
    return pl.pallas_call(
        functools.partial(_merge_kernel, seq=seq),
        grid=(t // tm,),
        in_specs=[row(d), zc(d, Z_MIX), zc(d, Z_MIX + d), zc(d, Z_MIX + 2 * d),
                  zc(CONV_CH, Z_CB), zc(CONV_CH, Z_CC), zc(CONV_CH, Z_CX), zprev(Z_CC), zprev(Z_CX),
                  zc(GATE_PAD, Z_GATE), row(SSM_WIDTH), branch_t, branch_t, branch_t,
                  full(conv_w), full(wc), full(wglu), full(wo), full(wout)],
        out_specs=row(d),
        out_shape=jax.ShapeDtypeStruct((t, d), F32),
        name="merge",
    )(x2, z, z, z, z, z, z, z, z, z, ys, o_cmp, o_sel, o_win, conv_w, wc, wglu, wo, wout)


def _ffn_kernel(x_ref, g_ref, wg_ref, wu_ref, wd_ref, o_ref, h_ref, acc_ref):
    f = pl.program_id(1)

    @pl.when(f == 0)
    def _():
        h_ref[...] = _rms(x_ref[...], g_ref[...]).astype(BF16)
        acc_ref[...] = jnp.zeros(acc_ref.shape, F32)

    h = h_ref[...]
    gate = _dot(h, wg_ref[...])
    up = _dot(h, wu_ref[...])
    act = (gate * _sigmoid(gate) * up).astype(BF16)
    acc_ref[...] += _dot(act, wd_ref[...])

    @pl.when(f == pl.num_programs(1) - 1)
    def _():
        o_ref[...] = x_ref[...] + acc_ref[...]


def _ffn(x2, g, w_gate_up, w_down, *, tm, tf):
    t, d = x2.shape
    nf = D_FF // tf
    return pl.pallas_call(
        _ffn_kernel,
        grid=(t // tm, nf),
        in_specs=[pl.BlockSpec((tm, d), lambda i, f: (i, 0)),
                  pl.BlockSpec((1, d), lambda i, f: (0, 0)),
                  pl.BlockSpec((d, tf), lambda i, f: (0, f)),
                  pl.BlockSpec((d, tf), lambda i, f: (0, nf + f)),
                  pl.BlockSpec((tf, d), lambda i, f: (f, 0))],
        out_specs=pl.BlockSpec((tm, d), lambda i, f: (i, 0)),
        out_shape=jax.ShapeDtypeStruct((t, d), F32),
        scratch_shapes=[pltpu.VMEM((tm, d), BF16), pltpu.VMEM((tm, d), F32)],
        compiler_params=pltpu.CompilerParams(dimension_semantics=("parallel", "arbitrary")),
        name="ffn",
    )(x2, g.reshape(1, d), w_gate_up, w_gate_up, w_down)


def _pick(n, pref):
    while n % pref:
        pref //= 2
    return pref


def _tiles(t, seq):
    return dict(
        inproj_tm=_pick(t, 512), inproj_tn=Z_WIDTH // 2,
        kv_tp=_pick(seq, 512),
        key_tile=LANES,
        cmp_tq=_pick(seq, 256), attn_tq=_pick(seq, 256),
        merge_tm=_pick(seq, 256),
        ffn_tm=_pick(t, 1024), ffn_tf=MXU_WIDTH)


def _layer(x2, p, *, bsz, seq):
    ts = _tiles(x2.shape[0], seq)
    z = _inproj(x2, p['mix_norm_g'], p['w_in'], tm=ts['inproj_tm'], tn=ts['inproj_tn'])
    ys = _s5_scan(z, p['s5'], bsz=bsz, seq=seq)
    kaug, vsel, kwin, vwin = _kv_prep(z, p['k_norm_g'], seq=seq, tp=ts['kv_tp'], tk_sel=ts['key_tile'],
                                      tk_win=ts['key_tile'])
    cmp_kv = _compress(z, p['cmp_pe'], p['cmp_w1'], p['cmp_w2'], p['k_norm_g'][0], bsz=bsz, seq=seq)
    o_cmp, qaug = _cmp_select(z, cmp_kv, p['q_norm_g'], bsz=bsz, seq=seq, tq=ts['cmp_tq'])
    o_sel = _flash(qaug, kaug, vsel, bsz=bsz, seq=seq, tq=ts['attn_tq'], window=0, name="sel_attn")
    o_win = _flash(qaug, kwin, vwin, bsz=bsz, seq=seq, tq=ts['attn_tq'], window=WINDOW, name="win_attn")
    x2 = _merge(x2, z, ys, o_cmp, o_sel, o_win, p['conv_w'], p['conv_w_out'], p['ssm_w_glu'], p['nsa_w_o'],
                p['w_out'], seq=seq, tm=ts['merge_tm'])
    return _ffn(x2, p['ffn_norm_g'], p['ffn_w_gate_up'], p['ffn_w_down'], tm=ts['ffn_tm'], tf=ts['ffn_tf'])


def kernel(x, mix_norm_g, w_in, ssm_lam_re, ssm_lam_im, ssm_b_re, ssm_b_im, ssm_c_re, ssm_c_im, ssm_d, ssm_log_dt, ssm_w_glu, conv_w, conv_w_out, q_norm_g, k_norm_g, cmp_pe, cmp_w1, cmp_w2, nsa_w_o, w_out, ffn_norm_g, ffn_w_gate_up, ffn_w_down):
    bsz, seq, d = x.shape
    assert d == D_MODEL, d
    assert seq % (SUBLANES * SEL_BLOCK) == 0 and seq // SEL_BLOCK <= HEAD_DIM, seq
    assert seq % WINDOW == 0 and seq % (S5_CHUNK * SUBLANES) == 0, seq
    x2 = x.reshape(bsz * seq, d)
    for i in range(w_in.shape[0]):
        p = dict(
            mix_norm_g=mix_norm_g[i], w_in=_permute_w_in(w_in[i]),
            s5=_s5_tables(ssm_lam_re[i], ssm_lam_im[i], ssm_b_re[i], ssm_b_im[i], ssm_c_re[i], ssm_c_im[i],
                          ssm_d[i], ssm_log_dt[i], seq // S5_CHUNK),
            ssm_w_glu=ssm_w_glu[i].astype(BF16), conv_w=conv_w[i], conv_w_out=conv_w_out[i].astype(BF16),
            q_norm_g=q_norm_g[i], k_norm_g=k_norm_g[i], cmp_pe=cmp_pe[i],
            cmp_w1=cmp_w1[i].astype(BF16), cmp_w2=cmp_w2[i].astype(BF16), nsa_w_o=nsa_w_o[i].astype(BF16),
            w_out=w_out[i].astype(BF16), ffn_norm_g=ffn_norm_g[i],
            ffn_w_gate_up=ffn_w_gate_up[i].astype(BF16), ffn_w_down=ffn_w_down[i].astype(BF16))
        x2 = _layer(x2, p, bsz=bsz, seq=seq)
    return x2.reshape(bsz, seq, d)
```

```python
import functools

import jax
import jax.numpy as jnp
import numpy as np
from jax import lax
from jax.experimental import pallas as pl
from jax.experimental.pallas import tpu as pltpu

F32 = jnp.float32
BF16 = jnp.bfloat16

D_MODEL = 1024
SSM_WIDTH = 512
SSM_GROUP = 16
CONV_CH = 512
HEAD_DIM = 64
N_HEADS = 8
N_KV_HEADS = 2
GQA = N_HEADS // N_KV_HEADS
ATTN_WIDTH = N_HEADS * HEAD_DIM
KV_WIDTH = N_KV_HEADS * HEAD_DIM
CMP_BLOCK = 32
SEL_BLOCK = 64
SEL_SHIFT = SEL_BLOCK.bit_length() - 1
N_SELECT = 16
WINDOW = 512
CMP_HIDDEN = 256
FORCE_SCORE = 1e4
NSA_BRANCHES = 3
MIX_BRANCHES = 3
D_FF = 2816
RMS_EPS = 1e-6

LANES = 128
SUBLANES = 8
MXU_WIDTH = 256
MASKED = -1e30
LOG2E = 1.4426950408889634
MERGE_STREAM_BUFFERS = 2
FLASH_UNROLL = 4
S5_CHUNK = 16
S5_TILE_GROUPS = LANES // SSM_GROUP
GATE_PAD = MXU_WIDTH
BF16_SUBLANES = 16
V_ROWS = HEAD_DIM + BF16_SUBLANES

Z_MIX = 0
Z_U = Z_MIX + MIX_BRANCHES * D_MODEL
Z_CB = Z_U + SSM_WIDTH
Z_CC = Z_CB + CONV_CH
Z_CX = Z_CC + CONV_CH
Z_Q = Z_CX + CONV_CH
Z_KC = Z_Q + ATTN_WIDTH
Z_VC = Z_KC + KV_WIDTH
Z_KS = Z_VC + KV_WIDTH
Z_VS = Z_KS + KV_WIDTH
Z_KW = Z_VS + KV_WIDTH
Z_VW = Z_KW + KV_WIDTH
Z_GATE = Z_VW + KV_WIDTH
Z_WIDTH = Z_GATE + GATE_PAD


def _gelu_tanh(x):
    return 0.5 * x * (1.0 + jnp.tanh(np.sqrt(2.0 / np.pi).astype(np.float32) * (x + 0.044715 * (x * x * x))))


def _sigmoid(x):
    return 0.5 * jnp.tanh(0.5 * x) + 0.5


def _rms(x, g):
    return x * lax.rsqrt(jnp.mean(x * x, axis=-1, keepdims=True) + RMS_EPS) * g


def _dot(a, b):
    return jnp.dot(a, b, preferred_element_type=F32)


def _segment_ones(width):
    seg = jnp.arange(width) // HEAD_DIM
    return (seg[:, None] == seg[None, :]).astype(BF16)


def _head_rms(x, g, seg):
    x2 = x * x
    hi = x2.astype(BF16)
    lo = (x2 - hi.astype(F32)).astype(BF16)
    ss = _dot(hi, seg) + _dot(lo, seg)
    return x * lax.rsqrt(ss * (1.0 / HEAD_DIM) + RMS_EPS) * g


def _dot_nt(a, b):
    return lax.dot_general(a, b, (((1,), (1,)), ((), ())), preferred_element_type=F32)


def _inproj_kernel(x_ref, g_ref, w_ref, o_ref):
    h = _rms(x_ref[...], g_ref[...]).astype(BF16)
    o_ref[...] = _dot(h, w_ref[...])


def _inproj(x2, g, w, *, tm, tn):
    t, d = x2.shape
    n = w.shape[1]
    return pl.pallas_call(
        _inproj_kernel,
        grid=(n // tn, t // tm),
        in_specs=[pl.BlockSpec((tm, d), lambda j, i: (i, 0)),
                  pl.BlockSpec((1, d), lambda j, i: (0, 0)),
                  pl.BlockSpec((d, tn), lambda j, i: (0, j))],
        out_specs=pl.BlockSpec((tm, tn), lambda j, i: (i, j)),
        out_shape=jax.ShapeDtypeStruct((t, n), F32),
        name="inproj",
    )(x2, g.reshape(1, d), w)


def _permute_w_in(w):
    n_plain = SSM_WIDTH + 3 * CONV_CH + ATTN_WIDTH + 6 * KV_WIDTH
    n_gate = N_HEADS * NSA_BRANCHES
    gate = jnp.pad(w[:, n_plain:n_plain + n_gate], ((0, 0), (0, GATE_PAD - n_gate)))
    return jnp.concatenate([w[:, n_plain + n_gate:], w[:, :n_plain], gate], axis=1).astype(BF16)


def _s5_tables(lam_re, lam_im, b_re, b_im, c_re, c_im, d_skip, log_dt, n_chunks):
    hp = lax.Precision.HIGHEST
    g, p = lam_re.shape
    h, l, gt = SSM_GROUP, S5_CHUNK, S5_TILE_GROUPS
    nt = g // gt
    dt = jnp.exp(log_dt)[:, None]
    ar, ai = lam_re * dt, lam_im * dt

    def powers(k):
        mag = jnp.exp(ar[None] * k[:, None, None])
        ang = ai[None] * k[:, None, None]
        return mag * jnp.cos(ang), mag * jnp.sin(ang)

    pr, pi = powers(jnp.arange(l + 1, dtype=F32))
    nr, ni = pr[1] - 1.0, pi[1]
    den = lam_re * lam_re + lam_im * lam_im
    fr, fi = (nr * lam_re + ni * lam_im) / den, (ni * lam_re - nr * lam_im) / den
    bbr = fr[..., None] * b_re - fi[..., None] * b_im
    bbi = fr[..., None] * b_im + fi[..., None] * b_re
    wr = pr[:l, :, :, None] * bbr[None] - pi[:l, :, :, None] * bbi[None]
    wi = pr[:l, :, :, None] * bbi[None] + pi[:l, :, :, None] * bbr[None]
    kern = (jnp.einsum('gop,tgpi->tgio', c_re, wr, precision=hp)
            - jnp.einsum('gop,tgpi->tgio', c_im, wi, precision=hp))
    r = jnp.arange(l)

    def group_diag(x, row_w, col_w, halves=1):
        rows, hw = x.shape[-2:]
        w = hw // halves
        col = jnp.arange(halves * gt * w)
        src = (col // (gt * w)) * w + col % w
        expand = (jnp.arange(hw)[:, None] == src[None, :]).astype(BF16)
        tiled = jnp.einsum('...w,wv->...v', x.astype(BF16), expand, preferred_element_type=BF16)
        own = (jnp.arange(rows)[:, None] // row_w) % gt == (col[None, :] % (gt * w)) // col_w
        return jnp.where(own, tiled, jnp.zeros((), BF16))

    kblk = group_diag(kern.reshape(l, g * h, h), h, h)
    krev = kblk.reshape(l, nt, LANES, LANES)[::-1].transpose(1, 0, 2, 3).reshape(nt, l * LANES, LANES)
    krev = jnp.concatenate([krev, jnp.zeros((nt, LANES, LANES), krev.dtype)], axis=1)
    t_pair = jnp.concatenate(
        [jnp.concatenate([krev[:, (l - 1 - s) * LANES:(l + 1) * LANES], krev[:, (l - 2 - s) * LANES:l * LANES]], axis=2)
         for s in range(0, l, 2)], axis=1)
    qr, qi = pr[l - 1 - r], pi[l - 1 - r]
    st_re = qr[..., None] * bbr[None] - qi[..., None] * bbi[None]
    st_im = qr[..., None] * bbi[None] + qi[..., None] * bbr[None]
    rows_in = lambda x: group_diag(x.transpose(0, 1, 3, 2).reshape(l, g * h, p), h, p).reshape(l, nt, LANES, gt * p)
    er, ei = pr[1:l + 1][:, :, None, :], pi[1:l + 1][:, :, None, :]

    def rows_out(x):
        x = x.transpose(0, 1, 3, 2).reshape(l // 2, 2, g * p, h).transpose(0, 2, 1, 3).reshape(l // 2, g * p, 2 * h)
        return group_diag(x, p, h, halves=2).reshape(l // 2, nt, gt * p, 2 * LANES)

    n_steps = max(1, int(np.ceil(np.log2(n_chunks))))
    dr, di = powers(l * (2.0 ** jnp.arange(n_steps, dtype=F32)))
    lanes = lambda m: m.reshape(n_steps, nt, gt * p).transpose(1, 0, 2)
    return dict(t_pair=t_pair, s_re=rows_in(st_re), s_im=rows_in(st_im),
                c_re=rows_out(c_re[None] * er - c_im[None] * ei),
                c_im=rows_out(-(c_re[None] * ei + c_im[None] * er)),
                d_re=lanes(dr), d_im=lanes(di), d_skip=d_skip.reshape(1, g * h))


def _s5_kernel(u_ref, tp_ref, sre_ref, sim_ref, cre_ref, cim_ref, dre_ref, dim_ref, dsk_ref, y_ref):
    seq = u_ref.shape[0]
    l = S5_CHUNK
    nc = seq // l
    n_steps = dre_ref.shape[1]
    ns = dre_ref.shape[2]
    us = [u_ref[pl.ds(r, nc, stride=l), :] for r in range(l)]
    ucat = jnp.concatenate(us, axis=1).astype(BF16)
    xr = _dot(ucat, sre_ref[:, 0].reshape(l * LANES, ns))
    xi = _dot(ucat, sim_ref[:, 0].reshape(l * LANES, ns))
    row = lax.broadcasted_iota(jnp.int32, xr.shape, 0)
    for k in range(n_steps):
        s = 1 << k
        if s >= nc:
            break
        dr = dre_ref[0, k:k + 1, :]
        di = dim_ref[0, k:k + 1, :]
        sr = jnp.where(row >= s, pltpu.roll(xr, s, 0), 0.0)
        si = jnp.where(row >= s, pltpu.roll(xi, s, 0), 0.0)
        xr, xi = xr + (dr * sr - di * si), xi + (dr * si + di * sr)
    pr = jnp.where(row >= 1, pltpu.roll(xr, 1, 0), 0.0).astype(BF16)
    pi = jnp.where(row >= 1, pltpu.roll(xi, 1, 0), 0.0).astype(BF16)
    dsk = dsk_ref[...]
    off = 0
    for s in range(0, l, 2):
        rows = (s + 2) * LANES
        y2 = (_dot(ucat[:, :rows], tp_ref[0, off:off + rows, :])
              + _dot(pr, cre_ref[s // 2, 0]) + _dot(pi, cim_ref[s // 2, 0]))
        y_ref[pl.ds(s, nc, stride=l), :] = y2[:, :LANES] + dsk * us[s]
        y_ref[pl.ds(s + 1, nc, stride=l), :] = y2[:, LANES:] + dsk * us[s + 1]
        off += rows


def _s5_scan(z, tabs, *, bsz, seq):
    t = z.shape[0]
    nt = SSM_WIDTH // LANES
    tile3 = lambda a: pl.BlockSpec((1,) + a.shape[1:], lambda j, b: (j, 0, 0))
    step4 = lambda a: pl.BlockSpec((a.shape[0], 1) + a.shape[2:], lambda j, b: (0, j, 0, 0))
    return pl.pallas_call(
        _s5_kernel,
        grid=(nt, bsz),
        in_specs=[pl.BlockSpec((seq, LANES), lambda j, b: (b, Z_U // LANES + j)),
                  tile3(tabs['t_pair']), step4(tabs['s_re']), step4(tabs['s_im']),
                  step4(tabs['c_re']), step4(tabs['c_im']), tile3(tabs['d_re']), tile3(tabs['d_im']),
                  pl.BlockSpec((1, LANES), lambda j, b: (0, j))],
        out_specs=pl.BlockSpec((seq, LANES), lambda j, b: (b, j)),
        out_shape=jax.ShapeDtypeStruct((t, SSM_WIDTH), F32),
        name="s5_scan",
    )(z, tabs['t_pair'], tabs['s_re'], tabs['s_im'], tabs['c_re'], tabs['c_im'], tabs['d_re'], tabs['d_im'],
      tabs['d_skip'])


def _kv_prep_kernel(ks_ref, vs_ref, kw_ref, vw_ref, g_ref, seg_ref, kaug_ref, vsel_ref, kwin_ref, vwin_ref, *, seq):
    tp = ks_ref.shape[0]
    pos = (pl.program_id(0) * tp) % seq + lax.broadcasted_iota(jnp.int32, (tp, HEAD_DIM), 0)
    blk = jnp.right_shift(pos, SEL_SHIFT)
    onehot = jnp.where(lax.broadcasted_iota(jnp.int32, (tp, HEAD_DIM), 1) == blk, 1.0, 0.0).astype(BF16)
    zeros = jnp.zeros((tp, HEAD_DIM), BF16)
    ks = _head_rms(ks_ref[...], g_ref[0:1, :], seg_ref[...]).astype(BF16)
    kw = _head_rms(kw_ref[...], g_ref[1:2, :], seg_ref[...]).astype(BF16)
    for hh in range(N_KV_HEADS):
        sl = slice(hh * HEAD_DIM, (hh + 1) * HEAD_DIM)
        kaug_ref[hh] = jnp.concatenate([ks[:, sl], onehot], axis=1)
        kwin_ref[hh] = jnp.concatenate([kw[:, sl], zeros], axis=1)
    for v_ref, vt_ref in ((vs_ref, vsel_ref), (vw_ref, vwin_ref)):
        tk = vt_ref.shape[3]
        vt = v_ref[...].T.astype(BF16)
        ones = jnp.where(lax.broadcasted_iota(jnp.int32, (V_ROWS - HEAD_DIM, tp), 0) == 0, 1.0, 0.0).astype(BF16)
        for hh in range(N_KV_HEADS):
            vh = jnp.concatenate([vt[hh * HEAD_DIM:(hh + 1) * HEAD_DIM], ones], axis=0)
            for j in range(tp // tk):
                vt_ref[hh, j] = vh[:, j * tk:(j + 1) * tk]


def _kv_prep(z, k_norm_g, *, seq, tp, tk_sel, tk_win):
    t = z.shape[0]
    col = lambda off: pl.BlockSpec((tp, KV_WIDTH), lambda i, o=off // KV_WIDTH: (i, o))
    aug = 2 * HEAD_DIM
    kspec = pl.BlockSpec((N_KV_HEADS, tp, aug), lambda i: (0, i, 0))
    vspec = lambda tk: pl.BlockSpec((N_KV_HEADS, tp // tk, V_ROWS, tk), lambda i: (0, i, 0, 0))
    vshape = lambda tk: jax.ShapeDtypeStruct((N_KV_HEADS, t // tk, V_ROWS, tk), BF16)
    return pl.pallas_call(
        functools.partial(_kv_prep_kernel, seq=seq),
        grid=(t // tp,),
        in_specs=[col(Z_KS), col(Z_VS), col(Z_KW), col(Z_VW), pl.BlockSpec((2, KV_WIDTH), lambda i: (0, 0)),
                  pl.BlockSpec((KV_WIDTH, KV_WIDTH), lambda i: (0, 0))],
        out_specs=[kspec, vspec(tk_sel), kspec, vspec(tk_win)],
        out_shape=[jax.ShapeDtypeStruct((N_KV_HEADS, t, aug), BF16), vshape(tk_sel),
                   jax.ShapeDtypeStruct((N_KV_HEADS, t, aug), BF16), vshape(tk_win)],
        name="kv_prep",
    )(z, z, z, z, jnp.tile(k_norm_g[1:3], (1, N_KV_HEADS)), _segment_ones(KV_WIDTH))


def _compress_kernel(x_ref, pe_ref, w1_ref, w2_ref, g_ref, o_ref):
    seq = x_ref.shape[0]
    nb = seq // (2 * CMP_BLOCK)
    lane = lax.broadcasted_iota(jnp.int32, (nb, 2 * HEAD_DIM), 1)
    rows = []
    for hh in range(N_KV_HEADS):
        for parity in range(2):
            cols = []
            for s in range(0, CMP_BLOCK, 2):
                a = x_ref[pl.ds(parity * CMP_BLOCK + s, nb, stride=2 * CMP_BLOCK), :]
                b = x_ref[pl.ds(parity * CMP_BLOCK + s + 1, nb, stride=2 * CMP_BLOCK), :]
                if hh == 0:
                    cols.append(jnp.where(lane < HEAD_DIM, a, pltpu.roll(b, HEAD_DIM, 1)))
                else:
                    cols.append(jnp.where(lane < HEAD_DIM, pltpu.roll(a, HEAD_DIM, 1), b))
            rows.append(jnp.concatenate(cols, axis=1))
    x = (jnp.concatenate(rows, axis=0) + pe_ref[0]).astype(BF16)
    hid = _gelu_tanh(_dot(x, w1_ref[0])).astype(BF16)
    y = _dot(hid, w2_ref[0])
    o_ref[0, 0] = jnp.where(pl.program_id(0) == 0, _rms(y, g_ref[...]), y)


def _compress(z, cmp_pe, w1, w2, k_gain, *, bsz, seq):
    nc = seq // CMP_BLOCK
    kdim = CMP_BLOCK * HEAD_DIM
    m = N_KV_HEADS * nc
    out = pl.pallas_call(
        _compress_kernel,
        grid=(2, bsz),
        in_specs=[pl.BlockSpec((seq, KV_WIDTH), lambda w, b: (b, Z_KC // KV_WIDTH + w)),
                  pl.BlockSpec((1, 1, kdim), lambda w, b: (w, 0, 0)),
                  pl.BlockSpec((1, kdim, CMP_HIDDEN), lambda w, b: (w, 0, 0)),
                  pl.BlockSpec((1, CMP_HIDDEN, HEAD_DIM), lambda w, b: (w, 0, 0)),
                  pl.BlockSpec((1, HEAD_DIM), lambda w, b: (0, 0))],
        out_specs=pl.BlockSpec((1, 1, m, HEAD_DIM), lambda w, b: (w, b, 0, 0)),
        out_shape=jax.ShapeDtypeStruct((2, bsz, m, HEAD_DIM), F32),
        name="compress",
    )(z, cmp_pe.reshape(2, 1, kdim), w1, w2, k_gain.reshape(1, HEAD_DIM))
    return out.reshape(2, bsz * N_KV_HEADS, nc, HEAD_DIM)


def _cmp_select_kernel(q_ref, kc_ref, vc_ref, qg_ref, seg_ref, o_ref, qaug_ref, imp_ref, cnt_ref):
    tq = q_ref.shape[0]
    nc = kc_ref.shape[1]
    nb = nc // 2
    pair_w = 2 * HEAD_DIM
    qw = GQA * HEAD_DIM
    q0 = pl.program_id(1) * tq
    qf = _head_rms(q_ref[...], qg_ref[...], seg_ref[...]) * (HEAD_DIM ** -0.5)
    qn = qf.astype(BF16)
    q2 = qf * LOG2E

    row = lax.broadcasted_iota(jnp.int32, (nc, tq), 0)
    t = q0 + lax.broadcasted_iota(jnp.int32, (nc, tq), 1)
    blk = jnp.where(row < nb, 2 * row, 2 * (row - nb) + 1)
    valid = (blk + 1) * CMP_BLOCK - 1 <= t
    lane = lax.broadcasted_iota(jnp.int32, (tq, pair_w), 1)
    j = lax.broadcasted_iota(jnp.int32, (nb, tq), 0)
    cur = jnp.right_shift(q0 + lax.broadcasted_iota(jnp.int32, (nb, tq), 1), SEL_SHIFT)
    forced = (j == 0) | (j == cur) | (j == cur - 1)
    visible = j <= cur

    def two_heads(x):
        zx = jnp.zeros_like(x)
        return jnp.concatenate([jnp.concatenate([x, zx], axis=1), jnp.concatenate([zx, x], axis=1)], axis=0)

    pairs = [(kvh, pr) for kvh in range(N_KV_HEADS) for pr in range(GQA // 2)]
    k2 = [two_heads(kc_ref[kvh].astype(BF16)) for kvh in range(N_KV_HEADS)]
    v2 = [two_heads(vc_ref[kvh].astype(BF16)) for kvh in range(N_KV_HEADS)]
    st = [_dot_nt(k2[kvh], qn[:, kvh * qw + pr * pair_w:kvh * qw + (pr + 1) * pair_w]) for kvh, pr in pairs]
    psum = [jnp.zeros((nc, tq), F32) for _ in range(N_KV_HEADS)]
    probs = []
    for (kvh, pr), s2 in zip(pairs, st):
        ph = []
        for hh in range(2):
            sm = jnp.where(valid, s2[hh * nc:(hh + 1) * nc], MASKED)
            m = jnp.max(sm, axis=0, keepdims=True)
            m = jnp.where(m > 0.5 * MASKED, m, 0.0)
            e = jnp.exp(sm - m)
            p = e * (1.0 / jnp.maximum(jnp.sum(e, axis=0, keepdims=True), 1e-30))
            psum[kvh] = psum[kvh] + p
            ph.append(p.astype(BF16))
        probs.append(jnp.concatenate(ph, axis=0))
    imps = []
    for kvh in range(N_KV_HEADS):
        imp = psum[kvh][:nb] + psum[kvh][nb:]
        imp = jnp.where(forced, FORCE_SCORE, jnp.where(visible, imp, -jnp.inf))
        imp_ref[kvh] = imp
        imps.append(imp)
    for n, ((kvh, pr), p2) in enumerate(zip(pairs, probs)):
        o_ref[n * pair_w:(n + 1) * pair_w, :] = lax.dot_general(
            v2[kvh], p2, (((0,), (0,)), ((), ())), preferred_element_type=F32).astype(o_ref.dtype)

    n_vis = jnp.minimum(nb, (q0 + tq - 1) // SEL_BLOCK + 1)
    n_slabs = nb // SUBLANES
    row_in_slab = lax.broadcasted_iota(jnp.int32, (SUBLANES, tq), 0)
    cnt_ref[...] = jnp.zeros(cnt_ref.shape, F32)
    for grp in range(n_slabs):
        @pl.when(grp * SUBLANES < n_vis)
        def _(grp=grp):
            for kvh in range(N_KV_HEADS):
                slabs = [imp_ref[kvh, s * SUBLANES:(s + 1) * SUBLANES, :] for s in range(n_slabs)]
                cnt = [jnp.zeros((SUBLANES, tq), F32) for _ in range(n_slabs)]
                for i in range(grp * SUBLANES, (grp + 1) * SUBLANES):
                    vi = imp_ref[kvh, i:i + 1, :]
                    for s in range(n_slabs):
                        if s > grp:
                            beats = vi >= slabs[s]
                        elif s < grp:
                            beats = vi > slabs[s]
                        else:
                            beats = (vi > slabs[s]) | ((vi == slabs[s]) & (i - s * SUBLANES < row_in_slab))
                        cnt[s] = cnt[s] + jnp.where(beats, 1.0, 0.0)
                cnt_ref[kvh] += jnp.concatenate(cnt, axis=0)
    for kvh in range(N_KV_HEADS):
        selneg_t = jnp.where(visible & (cnt_ref[kvh] < float(N_SELECT)), 0.0, MASKED)
        if nb < HEAD_DIM:
            selneg_t = jnp.concatenate([selneg_t, jnp.zeros((HEAD_DIM - nb, tq), F32)], axis=0)
        selneg = jnp.concatenate([selneg_t, selneg_t], axis=0).T
        for pr in range(GQA // 2):
            c0 = kvh * qw + pr * pair_w
            qp = q2[:, c0:c0 + pair_w]
            qaug_ref[kvh, 2 * pr] = jnp.where(lane < HEAD_DIM, qp, selneg).astype(BF16)
            qaug_ref[kvh, 2 * pr + 1] = jnp.where(lane < HEAD_DIM, pltpu.roll(qp, HEAD_DIM, 1), selneg).astype(BF16)


def _cmp_select(z, cmp_kv, q_norm_g, *, bsz, seq, tq):
    t = z.shape[0]
    nq = seq // tq
    nc = seq // CMP_BLOCK
    aug = 2 * HEAD_DIM
    qw = GQA * HEAD_DIM
    kc, vc = cmp_kv[0], cmp_kv[1]
    kv_spec = pl.BlockSpec((N_KV_HEADS, nc, HEAD_DIM), lambda b, i: (b, 0, 0))
    return pl.pallas_call(
        _cmp_select_kernel,
        grid=(bsz, nq),
        in_specs=[pl.BlockSpec((tq, ATTN_WIDTH), lambda b, i: (b * nq + i, Z_Q // ATTN_WIDTH)),
                  kv_spec, kv_spec,
                  pl.BlockSpec((1, ATTN_WIDTH), lambda b, i: (0, 0)),
                  pl.BlockSpec((ATTN_WIDTH, ATTN_WIDTH), lambda b, i: (0, 0))],
        out_specs=[pl.BlockSpec((ATTN_WIDTH, tq), lambda b, i: (0, b * nq + i)),
                   pl.BlockSpec((N_KV_HEADS, GQA, tq, aug), lambda b, i: (b, 0, i, 0))],
        out_shape=[jax.ShapeDtypeStruct((ATTN_WIDTH, t), BF16),
                   jax.ShapeDtypeStruct((bsz * N_KV_HEADS, GQA, seq, aug), BF16)],
        scratch_shapes=[pltpu.VMEM((N_KV_HEADS, nc // 2, tq), F32), pltpu.VMEM((N_KV_HEADS, nc // 2, tq), F32)],
        name="cmp_select",
    )(z, kc, vc, jnp.tile(q_norm_g, N_HEADS).reshape(1, ATTN_WIDTH), _segment_ones(ATTN_WIDTH))


def _flash_kernel(q_ref, k_ref, vt_ref, o_ref, sa_ref, sb_ref, pa_ref, pb_ref, m_ref, a_ref, acc_ref, *, window):
    tq = q_ref.shape[2]
    tk = vt_ref.shape[3]
    cols = GQA * tq
    q0 = pl.program_id(2) * tq
    qa = q_ref[0].reshape(cols, q_ref.shape[3])
    t = q0 + (lax.broadcasted_iota(jnp.int32, (tk, cols), 1) & (tq - 1))
    key_row = lax.broadcasted_iota(jnp.int32, (tk, cols), 0)
    span = 2 * tk
    lo = jnp.maximum(0, q0 - window + 1) // span if window else 0
    hi = (q0 + tq + span - 1) // span
    mid1 = jnp.maximum(lo, (q0 + tq - 1 - window) // span + 1) if window else lo
    mid2 = q0 // span
    first = 2 * lo
    last = 2 * hi - 1

    def scores(kb):
        return _dot_nt(k_ref[0, pl.ds(pl.multiple_of(kb * tk, tk), tk), :], qa)

    m_ref[...] = jnp.full(m_ref.shape, MASKED, F32)
    a_ref[...] = jnp.ones(a_ref.shape, F32)
    acc_ref[...] = jnp.zeros(acc_ref.shape, F32)
    pb_ref[...] = jnp.zeros(pb_ref.shape, BF16)
    sa_ref[...] = scores(first)

    def tile(kb, s_cur, s_nxt, p_prev, p_cur, mask):
        pv = _dot(vt_ref[0, jnp.maximum(kb - 1, first)], p_prev[...])
        s_nxt[...] = scores(jnp.minimum(kb + 1, last))
        s = s_cur[...]
        if mask == 'causal':
            s = jnp.where(kb * tk + key_row <= t, s, MASKED)
        elif mask == 'window':
            s = jnp.where(kb * tk + key_row > t - window, s, MASKED)
        m_old = m_ref[...]
        m_new = jnp.maximum(m_old, jnp.max(s, axis=0, keepdims=True))
        acc_ref[...] = a_ref[...] * acc_ref[...] + pv
        a_ref[...] = jnp.exp2(m_old - m_new)
        p_cur[...] = jnp.exp2(s - m_new).astype(BF16)
        m_ref[...] = m_new

    def pair_step(pair, mask):
        tile(2 * pair, sa_ref, sb_ref, pb_ref, pa_ref, mask)
        tile(2 * pair + 1, sb_ref, sa_ref, pa_ref, pb_ref, mask)

    def run(begin, end, mask, unroll=1):
        n_main = (end - begin) // unroll

        def main(i, carry):
            for u in range(unroll):
                pair_step(begin + i * unroll + u, mask)
            return carry

        def rest(pair, carry):
            pair_step(pair, mask)
            return carry

        if unroll > 1:
            lax.fori_loop(0, n_main, main, 0)
        lax.fori_loop(begin + n_main * unroll if unroll > 1 else begin, end, rest, 0)

    if window:
        interior = (mid1 - lo == 1) & (mid2 - mid1 == 1) & (hi - mid2 == 1)

        @pl.when(interior)
        def _():
            pair_step(lo, 'window')
            pair_step(lo + 1, None)
            pair_step(lo + 2, 'causal')

        @pl.when(jnp.logical_not(interior))
        def _():
            run(lo, mid1, 'window')
            run(mid1, mid2, None)
            run(mid2, hi, 'causal')
    else:
        run(mid1, mid2, None, unroll=FLASH_UNROLL)
        run(mid2, hi, 'causal')
    acc = a_ref[...] * acc_ref[...] + _dot(vt_ref[0, last], pb_ref[...])
    o_t = acc[:HEAD_DIM] / acc[HEAD_DIM:HEAD_DIM + 1]
    for g in range(GQA):
        o_ref[g * HEAD_DIM:(g + 1) * HEAD_DIM, :] = o_t[:, g * tq:(g + 1) * tq].astype(o_ref.dtype)


def _flash(qaug, k, vt, *, bsz, seq, tq, window, name):
    t = bsz * seq
    nq = seq // tq
    aug = qaug.shape[3]
    tk = vt.shape[3]
    qw = GQA * HEAD_DIM
    cols = GQA * tq
    return pl.pallas_call(
        functools.partial(_flash_kernel, window=window),
        grid=(bsz, N_KV_HEADS, nq),
        in_specs=[pl.BlockSpec((1, GQA, tq, aug), lambda b, h, i: (b * N_KV_HEADS + h, 0, i, 0)),
                  pl.BlockSpec((1, seq, aug), lambda b, h, i: (h, b, 0)),
                  pl.BlockSpec((1, seq // tk, V_ROWS, tk), lambda b, h, i: (h, b, 0, 0))],
        out_specs=pl.BlockSpec((qw, tq), lambda b, h, i: (h, b * nq + i)),
        out_shape=jax.ShapeDtypeStruct((ATTN_WIDTH, t), BF16),
        scratch_shapes=[pltpu.VMEM((tk, cols), F32), pltpu.VMEM((tk, cols), F32),
                        pltpu.VMEM((tk, cols), BF16), pltpu.VMEM((tk, cols), BF16),
                        pltpu.VMEM((1, cols), F32), pltpu.VMEM((1, cols), F32),
                        pltpu.VMEM((V_ROWS, cols), F32)],
        name=name,
    )(qaug, k, vt)


def _merge_kernel(x_ref, g0_ref, g1_ref, g2_ref, cb_ref, cc_ref, cx_ref, pc_ref, px_ref, gl_ref, ys_ref,
                  oc_ref, os_ref, ow_ref, cw_ref, wc_ref, wglu_ref, wo_ref, wout_ref, out_ref, *, seq):
    tm = x_ref.shape[0]
    yg = _dot(_gelu_tanh(ys_ref[...]).astype(BF16), wglu_ref[...])
    y_ssm = yg[:, :D_MODEL] * _sigmoid(yg[:, D_MODEL:])
    zc = cc_ref[...] * cx_ref[...]
    keep = jnp.where((pl.program_id(0) * tm) % seq != 0, 1.0, 0.0)
    prev = pc_ref[...] * px_ref[...] * keep
    row = lax.broadcasted_iota(jnp.int32, zc.shape, 0)
    z1 = jnp.where(row >= 1, pltpu.roll(zc, 1, 0), prev[7:8, :])
    z2 = jnp.where(row >= 2, pltpu.roll(zc, 2, 0), jnp.where(row == 1, prev[7:8, :], prev[6:7, :]))
    conv = cw_ref[0:1, :] * z2 + cw_ref[1:2, :] * z1 + cw_ref[2:3, :] * zc
    y_conv = _dot((cb_ref[...] * conv).astype(BF16), wc_ref[...])
    gates_t = _sigmoid(gl_ref[...]).T
    heads = []
    for hd in range(N_HEADS):
        rows = slice(hd * HEAD_DIM, (hd + 1) * HEAD_DIM)
        c = hd * NSA_BRANCHES
        heads.append(oc_ref[rows, :] * gates_t[c:c + 1] + os_ref[rows, :] * gates_t[c + 1:c + 2]
                     + ow_ref[rows, :] * gates_t[c + 2:c + 3])
    o_t = jnp.concatenate(heads, axis=0).astype(BF16)
    y_attn = lax.dot_general(o_t, wo_ref[...], (((0,), (0,)), ((), ())), preferred_element_type=F32)
    mixed =_sigmoid(g0_ref[...]) * y_ssm + _sigmoid(g1_ref[...]) * y_conv + _sigmoid(g2_ref[...]) * y_attn
    out_ref[...] = x_ref[...] + _dot(mixed.astype(BF16), wout_ref[...])


def _merge(x2, z, ys, o_cmp, o_sel, o_win, conv_w, wc, wglu, wo, wout, *, seq, tm):
    t, d = x2.shape
    rb = tm // 8
    deep = pl.Buffered(MERGE_STREAM_BUFFERS)
    zc = lambda width, off: pl.BlockSpec((tm, width), lambda i, o=off // width: (i, o), pipeline_mode=deep)
    zprev = lambda off: pl.BlockSpec((8, CONV_CH), lambda i, o=off // CONV_CH: (jnp.maximum(i * rb - 1, 0), o))
    row = lambda width, **kw: pl.BlockSpec((tm, width), lambda i: (i, 0), **kw)
    full = lambda a: pl.BlockSpec(a.shape, lambda i: (0, 0), pipeline_mode=pl.Buffered(1))
    branch_t = pl.BlockSpec((ATTN_WIDTH, tm), lambda i: (0, i), pipeline_mode=deep)
    return pl.pallas_call(
        functools.partial(_merge_kernel, seq=seq),
        grid=(t // tm,),
        in_specs=[row(d, pipeline_mode=deep), zc(d, Z_MIX), zc(d, Z_MIX + d), zc(d, Z_MIX + 2 * d),
                  zc(CONV_CH, Z_CB), zc(CONV_CH, Z_CC), zc(CONV_CH, Z_CX), zprev(Z_CC), zprev(Z_CX),
                  zc(GATE_PAD, Z_GATE), row(SSM_WIDTH, pipeline_mode=deep), branch_t, branch_t, branch_t,
                  full(conv_w), full(wc), full(wglu), full(wo), full(wout)],
        out_specs=row(d),
        out_shape=jax.ShapeDtypeStruct((t, d), F32),
        name="merge",
    )(x2, z, z, z, z, z, z, z, z, z, ys, o_cmp, o_sel, o_win, conv_w, wc, wglu, wo, wout)


def _ffn_kernel(x_ref, g_ref, wg_ref, wu_ref, wd_ref, o_ref, h_ref, acc_ref):
    f = pl.program_id(1)

    @pl.when(f == 0)
    def _():
        h_ref[...] = _rms(x_ref[...], g_ref[...]).astype(BF16)
        acc_ref[...] = jnp.zeros(acc_ref.shape, F32)

    h = h_ref[...]
    gate = _dot(h, wg_ref[...])
    up = _dot(h, wu_ref[...])
    act = (gate * _sigmoid(gate) * up).astype(BF16)
    acc_ref[...] += _dot(act, wd_ref[...])

    @pl.when(f == pl.num_programs(1) - 1)
    def _():
        o_ref[...] = x_ref[...] + acc_ref[...]


def _ffn(x2, g, w_gate_up, w_down, *, tm, tf):
    t, d = x2.shape
    nf = D_FF // tf
    return pl.pallas_call(
        _ffn_kernel,
        grid=(t // tm, nf),
        in_specs=[pl.BlockSpec((tm, d), lambda i, f: (i, 0)),
                  pl.BlockSpec((1, d), lambda i, f: (0, 0)),
                  pl.BlockSpec((d, tf), lambda i, f: (0, f)),
                  pl.BlockSpec((d, tf), lambda i, f: (0, nf + f)),
                  pl.BlockSpec((tf, d), lambda i, f: (f, 0))],
        out_specs=pl.BlockSpec((tm, d), lambda i, f: (i, 0)),
        out_shape=jax.ShapeDtypeStruct((t, d), F32),
        scratch_shapes=[pltpu.VMEM((tm, d), BF16), pltpu.VMEM((tm, d), F32)],
        compiler_params=pltpu.CompilerParams(dimension_semantics=("parallel", "arbitrary")),
        name="ffn",
    )(x2, g.reshape(1, d), w_gate_up, w_gate_up, w_down)


def _pick(n, pref):
    while n % pref:
        pref //= 2
    return pref


def _tiles(t, seq):
    return dict(
        inproj_tm=_pick(t, 512), inproj_tn=Z_WIDTH // 2,
        kv_tp=_pick(seq, 512),
        key_tile=LANES,
        cmp_tq=_pick(seq, 256), attn_tq=_pick(seq, 256),
        merge_tm=_pick(seq, 512),
        ffn_tm=_pick(t, 1024), ffn_tf=MXU_WIDTH)


def _layer(x2, p, *, bsz, seq):
    ts = _tiles(x2.shape[0], seq)
    z = _inproj(x2, p['mix_norm_g'], p['w_in'], tm=ts['inproj_tm'], tn=ts['inproj_tn'])
    ys = _s5_scan(z, p['s5'], bsz=bsz, seq=seq)
    kaug, vsel, kwin, vwin = _kv_prep(z, p['k_norm_g'], seq=seq, tp=ts['kv_tp'], tk_sel=ts['key_tile'],
                                      tk_win=ts['key_tile'])
    cmp_kv = _compress(z, p['cmp_pe'], p['cmp_w1'], p['cmp_w2'], p['k_norm_g'][0], bsz=bsz, seq=seq)
    o_cmp, qaug = _cmp_select(z, cmp_kv, p['q_norm_g'], bsz=bsz, seq=seq, tq=ts['cmp_tq'])
    o_sel = _flash(qaug, kaug, vsel, bsz=bsz, seq=seq, tq=ts['attn_tq'], window=0, name="sel_attn")
    o_win = _flash(qaug, kwin, vwin, bsz=bsz, seq=seq, tq=ts['attn_tq'], window=WINDOW, name="win_attn")
    x2 = _merge(x2, z, ys, o_cmp, o_sel, o_win, p['conv_w'], p['conv_w_out'], p['ssm_w_glu'], p['nsa_w_o'],
                p['w_out'], seq=seq, tm=ts['merge_tm'])
    return _ffn(x2, p['ffn_norm_g'], p['ffn_w_gate_up'], p['ffn_w_down'], tm=ts['ffn_tm'], tf=ts['ffn_tf'])


def kernel(x, mix_norm_g, w_in, ssm_lam_re, ssm_lam_im, ssm_b_re, ssm_b_im, ssm_c_re, ssm_c_im, ssm_d, ssm_log_dt, ssm_w_glu, conv_w, conv_w_out, q_norm_g, k_norm_g, cmp_pe, cmp_w1, cmp_w2, nsa_w_o, w_out, ffn_norm_g, ffn_w_gate_up, ffn_w_down):
    bsz, seq, d = x.shape
    assert d == D_MODEL, d
    assert seq % (SUBLANES * SEL_BLOCK) == 0 and seq // SEL_BLOCK <= HEAD_DIM, seq
    assert seq % WINDOW == 0 and seq % (S5_CHUNK * SUBLANES) == 0, seq
    x2 = x.reshape(bsz * seq, d)
    for i in range(w_in.shape[0]):
        p = dict(
            mix_norm_g=mix_norm_g[i], w_in=_permute_w_in(w_in[i]),
            s5=_s5_tables(ssm_lam_re[i], ssm_lam_im[i], ssm_b_re[i], ssm_b_im[i], ssm_c_re[i], ssm_c_im[i],
                          ssm_d[i], ssm_log_dt[i], seq // S5_CHUNK),
            ssm_w_glu=ssm_w_glu[i].astype(BF16), conv_w=conv_w[i], conv_w_out=conv_w_out[i].astype(BF16),
            q_norm_g=q_norm_g[i], k_norm_g=k_norm_g[i], cmp_pe=cmp_pe[i],
            cmp_w1=cmp_w1[i].astype(BF16), cmp_w2=cmp_w2[i].astype(BF16), nsa_w_o=nsa_w_o[i].astype(BF16),
            w_out=w_out[i].astype(BF16), ffn_norm_g=ffn_norm_g[i],
            ffn_w_gate_up=ffn_w_gate_up[i].astype(BF16), ffn_w_down=ffn_w_down[i].astype(BF16))
        x2 = _layer(x2, p, bsz=bsz, seq=seq)
    return x2.reshape(bsz, seq, d)
```
